```python
import math
import jax
import jax.numpy as jnp
from jax import lax
import numpy as np

D_MODEL = 1024
BATCH = 2
SEQ = 8192
DEPTH = 2

GRID_W = 64
CTX_LEN = 256

NA_HEADS = 8
HEAD_DIM = 64
NA_W = NA_HEADS * HEAD_DIM
NA_WIN_R = 8
NA_WIN_C = 16
ROPE_THETA = 10000.0
HY_W = 256
HY_ORDER = 2
HY_SHORT = 3
HY_BANDS = 16
HY_EMB = 1 + 2 * HY_BANDS
HY_FO = 64
HY_DECAY_TARGET = 1e-2
HY_FAST_DECAY = 0.3
HY_SLOW_DECAY = 1.5
POOL_W = 256
POOL_SIZES = (2, 4, 8, 16)
POOL_GROUP = POOL_W // len(POOL_SIZES)

MIX_W = NA_W + HY_W + POOL_W
IN_W = 3 * NA_W + (HY_ORDER + 1) * HY_W + POOL_W

N_EXPERTS = 16
EC_FACTOR = 2
D_EXPERT = 2048

EPS = 1e-6

kernel_name = 'hybrid_natten_hyena_pool_ecmoe_dit'


def rmsnorm(x, w):
    xf = x.astype(jnp.float32)
    y = xf * lax.rsqrt(jnp.mean(xf * xf, axis=-1, keepdims=True) + EPS)
    return (y * w.astype(jnp.float32)).astype(x.dtype)


def modulate(h, shift, scale):
    return h * (1.0 + scale) + shift


def split_heads(u):
    return u.reshape(u.shape[:-1] + (NA_HEADS, HEAD_DIM))


def axial_rope(n_tokens):
    t = jnp.arange(n_tokens, dtype=jnp.int32)
    pos = jnp.stack([t // GRID_W, t % GRID_W], axis=-1).astype(jnp.float32)
    nf = HEAD_DIM // 4
    inv = ROPE_THETA ** (-jnp.arange(nf, dtype=jnp.float32) / nf)
    ang = pos[:, :, None] * inv
    return jnp.cos(ang), jnp.sin(ang)


def apply_rope(x, cos, sin):
    B, N, H, dh = x.shape
    xr = x.reshape(B, N, H, 2, 2, dh // 4)
    a, b = xr[..., 0, :], xr[..., 1, :]
    c = cos[None, :, None].astype(x.dtype)
    s = sin[None, :, None].astype(x.dtype)
    return jnp.stack([a * c - b * s, b * c + a * s], axis=-2).reshape(B, N, H, dh)


def dense_attention(q, k, v):
    s = jnp.einsum('bqhd,bkhd->bhqk', q, k).astype(jnp.float32) * (q.shape[-1] ** -0.5)
    p = jax.nn.softmax(s, axis=-1).astype(v.dtype)
    return jnp.einsum('bhqk,bkhd->bqhd', p, v)


def neighborhood_attention(q, k, v, q_plain, kc, vc, rpb):
    B, N, H, dh = q.shape
    rows = N // GRID_W
    wr = min(NA_WIN_R, rows)
    scale = dh ** -0.5
    grid = lambda a: a.reshape(B, rows, GRID_W, H, dh)
    qg, qpg, kg, vg = grid(q), grid(q_plain), grid(k), grid(v)
    col = jnp.arange(GRID_W)
    col_start = jnp.clip(col - NA_WIN_C // 2, 0, GRID_W - NA_WIN_C)
    col_idx = col_start[:, None] + jnp.arange(NA_WIN_C)[None, :]
    col_bias_idx = col_idx - col[:, None] + (NA_WIN_C - 1)
    rpb_f = rpb.astype(jnp.float32)

    def row_block(r):
        rs = jnp.clip(r - NA_WIN_R // 2, 0, rows - wr)
        q_r = lax.dynamic_index_in_dim(qg, r, axis=1, keepdims=False)
        qp_r = lax.dynamic_index_in_dim(qpg, r, axis=1, keepdims=False)
        k_win = lax.dynamic_slice_in_dim(kg, rs, wr, axis=1)[:, :, col_idx]
        v_win = lax.dynamic_slice_in_dim(vg, rs, wr, axis=1)[:, :, col_idx]
        s_lat = jnp.einsum('bqhd,biqjhd->bhqij', q_r, k_win).astype(jnp.float32) * scale
        row_bias_idx = rs + jnp.arange(wr) - r + (NA_WIN_R - 1)
        bias = rpb_f[:, row_bias_idx[:, None, None], col_bias_idx[None]].transpose(0, 2, 1, 3)
        s_lat = (s_lat + bias).reshape(B, H, GRID_W, wr * NA_WIN_C)
        s_ctx = jnp.einsum('bqhd,bkhd->bhqk', qp_r, kc).astype(jnp.float32) * scale
        p = jax.nn.softmax(jnp.concatenate([s_lat, s_ctx], axis=-1), axis=-1).astype(v.dtype)
        p_lat = p[..., :wr * NA_WIN_C].reshape(B, H, GRID_W, wr, NA_WIN_C)
        p_ctx = p[..., wr * NA_WIN_C:]
        return (jnp.einsum('bhqij,biqjhd->bqhd', p_lat, v_win)
                + jnp.einsum('bhqk,bkhd->bqhd', p_ctx, vc))

    out = lax.map(row_block, jnp.arange(rows))
    return out.transpose(1, 0, 2, 3, 4).reshape(B, N, H * dh)


def short_conv(u, w, b):
    L = u.shape[1]
    pad = HY_SHORT // 2
    up = jnp.pad(u, ((0, 0), (pad, pad), (0, 0)))
    y = b
    for j in range(HY_SHORT):
        y = y + up[:, j:j + L] * w[j]
    return y


def hyena_filter(L, w1, b1, w2, b2, w3, freq):
    f32 = jnp.float32
    t = jnp.linspace(0.0, 1.0, L, dtype=f32)[:, None]
    w = (2.0 * math.pi / L) * jnp.arange(L, dtype=f32)[:, None]
    f = jnp.linspace(1e-4, HY_BANDS - 1, HY_BANDS, dtype=f32)[None, :]
    z = jnp.concatenate([t, jnp.cos(f * w), -jnp.sin(f * w)], axis=-1)
    fr = freq.astype(f32)
    h = jnp.sin(fr * (z @ w1.astype(f32) + b1.astype(f32)))
    h = jnp.sin(fr * (h @ w2.astype(f32) + b2.astype(f32)))
    h = (h @ w3.astype(f32)).reshape(L, HY_ORDER, 2, HY_W)
    deltas = jnp.abs(jnp.linspace(math.log(HY_DECAY_TARGET) / HY_SLOW_DECAY,
                                  math.log(HY_DECAY_TARGET) / HY_FAST_DECAY, HY_W, dtype=f32))
    h = h * jnp.exp(-t * deltas)[:, None, None, :]
    hf, hb = h[:, :, 0], h[:, :, 1]
    k2 = jnp.concatenate([hf, jnp.zeros_like(hf[:1]), hb[:0:-1]], axis=0)
    k2 = k2 * lax.rsqrt(jnp.sum(k2 * k2, axis=0, keepdims=True) + EPS)
    return jnp.fft.rfft(k2, axis=0)


def fft_conv(u, kf):
    L = u.shape[1]
    U = jnp.fft.rfft(u.astype(jnp.float32), n=2 * L, axis=1)
    return jnp.fft.irfft(U * kf[None], n=2 * L, axis=1)[:, :L].astype(u.dtype)


def hyena_mixer(u, kf, conv_w, conv_b, skip):
    u = short_conv(u, conv_w, conv_b)
    x1, x2, v = jnp.split(u, 3, axis=-1)
    z = v
    for o, gate in enumerate((x1, x2)):
        z = gate * (fft_conv(z, kf[:, o]) + z * skip[o])
    return z


def pool_mixer(u, pool_w, pool_scale):
    B, L, _ = u.shape
    uf = u.astype(jnp.float32)
    csum = jnp.pad(jnp.cumsum(uf, axis=1), ((0, 0), (1, 0), (0, 0)))
    t = jnp.arange(L)
    outs = []
    for g, w in enumerate(POOL_SIZES):
        lo = jnp.clip(t - w // 2, 0, L)
        hi = jnp.clip(t - w // 2 + w, 0, L)
        sl = slice(g * POOL_GROUP, (g + 1) * POOL_GROUP)
        mean = (csum[:, hi, sl] - csum[:, lo, sl]) / (hi - lo).astype(jnp.float32)[:, None]
        outs.append(jnp.einsum('blc,cd->bld', (mean - uf[..., sl]).astype(u.dtype), pool_w[g]))
    return jnp.concatenate(outs, axis=-1) * pool_scale


def expert_choice_moe(h, w_router, w_gate, w_up, w_down):
    B, N, _ = h.shape
    cap = EC_FACTOR * N // N_EXPERTS
    aff = jax.nn.softmax(jnp.einsum('bnd,de->bne', h, w_router).astype(jnp.float32), axis=-1)
    gate, idx = lax.top_k(aff.transpose(0, 2, 1), cap)
    bidx = jnp.arange(B)[:, None, None]
    xg = h[bidx, idx]
    a = jnp.einsum('becd,edf->becf', xg, w_gate)
    b = jnp.einsum('becd,edf->becf', xg, w_up)
    y = jnp.einsum('becf,efd->becd', jax.nn.silu(a) * b, w_down)
    y = (y * gate[..., None].astype(y.dtype)).astype(h.dtype)
    return jnp.zeros_like(h).at[bidx, idx].add(y)


def trunk_layer(xl, xc, c, c_ctx, rope_cos, rope_sin, p, last):
    B, N, _ = xl.shape
    mod_l = (jax.nn.silu(c) @ p['w_mod'] + p['b_mod'])[:, None, :]
    mod_c = jax.nn.silu(c_ctx) @ p['w_mod'] + p['b_mod']
    sh1, sc1, g1, sh2, sc2, g2 = jnp.split(mod_l, 6, axis=-1)
    csh1, csc1, cg1, csh2, csc2, cg2 = jnp.split(mod_c, 6, axis=-1)
    w_in = p['w_in']
    hy_cols = slice(3 * NA_W, 3 * NA_W + 3 * HY_W)
    pool_cols = slice(3 * NA_W + 3 * HY_W, IN_W)

    hc = modulate(rmsnorm(xc, p['norm1_w']), csh1, csc1)
    if last:
        uc = hc @ w_in[:, NA_W:3 * NA_W]
        kc = rmsnorm(split_heads(uc[..., :NA_W]), p['k_norm_w'])
        vc = split_heads(uc[..., NA_W:])
        xc_new = None
    else:
        uc = hc @ w_in
        qc = rmsnorm(split_heads(uc[..., :NA_W]), p['q_norm_w'])
        kc = rmsnorm(split_heads(uc[..., NA_W:2 * NA_W]), p['k_norm_w'])
        vc = split_heads(uc[..., 2 * NA_W:3 * NA_W])
        ctx_kf = hyena_filter(xc.shape[1], p['hy_w1'], p['hy_b1'], p['hy_w2'], p['hy_b2'], p['hy_w3'], p['hy_freq'])
        mix_c = jnp.concatenate([
            dense_attention(qc, kc, vc).reshape(B, xc.shape[1], NA_W),
            hyena_mixer(uc[..., hy_cols], ctx_kf, p['hy_conv_w'], p['hy_conv_b'], p['hy_skip']),
            pool_mixer(uc[..., pool_cols], p['pool_w'], p['pool_scale'])], axis=-1)
        xc_new = xc + cg1 * (mix_c @ p['w_out'])
        h2c = modulate(rmsnorm(xc_new, p['norm2_w']), csh2, csc2)
        xc_new = xc_new + cg2 * expert_choice_moe(h2c, p['w_router'], p['w_gate'], p['w_up'], p['w_down'])

    hl = modulate(rmsnorm(xl, p['norm1_w']), sh1, sc1)
    ul = hl @ w_in
    ql = rmsnorm(split_heads(ul[..., :NA_W]), p['q_norm_w'])
    kl = apply_rope(rmsnorm(split_heads(ul[..., NA_W:2 * NA_W]), p['k_norm_w']), rope_cos, rope_sin)
    vl = split_heads(ul[..., 2 * NA_W:3 * NA_W])
    o_na = neighborhood_attention(apply_rope(ql, rope_cos, rope_sin), kl, vl, ql, kc, vc, p['na_rpb'])
    lat_kf = hyena_filter(N, p['hy_w1'], p['hy_b1'], p['hy_w2'], p['hy_b2'], p['hy_w3'], p['hy_freq'])
    mix_l = jnp.concatenate([
        o_na,
        hyena_mixer(ul[..., hy_cols], lat_kf, p['hy_conv_w'], p['hy_conv_b'], p['hy_skip']),
        pool_mixer(ul[..., pool_cols], p['pool_w'], p['pool_scale'])], axis=-1)
    xl_new = xl + g1 * (mix_l @ p['w_out'])
    h2l = modulate(rmsnorm(xl_new, p['norm2_w']), sh2, sc2)
    xl_new = xl_new + g2 * expert_choice_moe(h2l, p['w_router'], p['w_gate'], p['w_up'], p['w_down'])
    return xl_new, xc_new


def setup_inputs(seed: int = 0) -> dict:
    key = jax.random.key(seed)
    ks = iter(jax.random.split(key, 40))
    D = D_MODEL

    def nrm(shape, scale):
        return jax.random.normal(next(ks), shape, jnp.float32) * scale

    return {
        'x': nrm((BATCH, SEQ, D), 1.0),
        'c': nrm((BATCH, D), 1.0),
        'ctx': nrm((BATCH, CTX_LEN, D), 1.0),
        'c_ctx': nrm((D,), 1.0),
        'w_mod': nrm((DEPTH, D, 6 * D), 0.5 * D ** -0.5),
        'b_mod': nrm((DEPTH, 6 * D), 0.02),
        'norm1_w': 1.0 + nrm((DEPTH, D), 0.02),
        'norm2_w': 1.0 + nrm((DEPTH, D), 0.02),
        'w_in': nrm((DEPTH, D, IN_W), D ** -0.5),
        'w_out': nrm((DEPTH, MIX_W, D), MIX_W ** -0.5),
        'q_norm_w': 1.0 + nrm((DEPTH, HEAD_DIM), 0.02),
        'k_norm_w': 1.0 + nrm((DEPTH, HEAD_DIM), 0.02),
        'na_rpb': nrm((DEPTH, NA_HEADS, 2 * NA_WIN_R - 1, 2 * NA_WIN_C - 1), 0.1),
        'hy_conv_w': nrm((DEPTH, HY_SHORT, 3 * HY_W), HY_SHORT ** -0.5),
        'hy_conv_b': nrm((DEPTH, 3 * HY_W), 0.02),
        'hy_w1': nrm((DEPTH, HY_EMB, HY_FO), HY_EMB ** -0.5),
        'hy_b1': nrm((DEPTH, HY_FO), 0.02),
        'hy_w2': nrm((DEPTH, HY_FO, HY_FO), HY_FO ** -0.5),
        'hy_b2': nrm((DEPTH, HY_FO), 0.02),
        'hy_w3': nrm((DEPTH, HY_FO, HY_ORDER * 2 * HY_W), HY_FO ** -0.5),
        'hy_freq': 1.0 + nrm((DEPTH, HY_FO), 0.1),
        'hy_skip': nrm((DEPTH, HY_ORDER, HY_W), 0.5),
        'pool_w': nrm((DEPTH, len(POOL_SIZES), POOL_GROUP, POOL_GROUP), POOL_GROUP ** -0.5),
        'pool_scale': 1.0 + nrm((DEPTH, POOL_W), 0.02),
        'w_router': nrm((DEPTH, D, N_EXPERTS), D ** -0.5),
        'w_gate': nrm((DEPTH, N_EXPERTS, D, D_EXPERT), D ** -0.5),
        'w_up': nrm((DEPTH, N_EXPERTS, D, D_EXPERT), D ** -0.5),
        'w_down': nrm((DEPTH, N_EXPERTS, D_EXPERT, D), D_EXPERT ** -0.5),
    }


def reference(x, c, ctx, c_ctx, w_mod, b_mod, norm1_w, norm2_w, w_in, w_out, q_norm_w, k_norm_w, na_rpb,
              hy_conv_w, hy_conv_b, hy_w1, hy_b1, hy_w2, hy_b2, hy_w3, hy_freq, hy_skip,
              pool_w, pool_scale, w_router, w_gate, w_up, w_down):
    rope_cos, rope_sin = axial_rope(x.shape[1])
    xl, xc = x, ctx
    for i in range(DEPTH):
        p = {
            'w_mod': w_mod[i], 'b_mod': b_mod[i], 'norm1_w': norm1_w[i], 'norm2_w': norm2_w[i],
            'w_in': w_in[i], 'w_out': w_out[i], 'q_norm_w': q_norm_w[i], 'k_norm_w': k_norm_w[i],
            'na_rpb': na_rpb[i], 'hy_conv_w': hy_conv_w[i], 'hy_conv_b': hy_conv_b[i],
            'hy_w1': hy_w1[i], 'hy_b1': hy_b1[i], 'hy_w2': hy_w2[i], 'hy_b2': hy_b2[i], 'hy_w3': hy_w3[i],
            'hy_freq': hy_freq[i], 'hy_skip': hy_skip[i], 'pool_w': pool_w[i], 'pool_scale': pool_scale[i],
            'w_router': w_router[i], 'w_gate': w_gate[i], 'w_up': w_up[i], 'w_down': w_down[i],
        }
        xl, xc = trunk_layer(xl, xc, c, c_ctx, rope_cos, rope_sin, p, i == DEPTH - 1)
    return xl
```

```python
import functools
import math

import jax
import jax.numpy as jnp
import numpy as np
from jax import lax
from jax.experimental import pallas as pl
from jax.experimental.pallas import tpu as pltpu

F32 = jnp.float32
BF16 = jnp.bfloat16

D_MODEL = 1024
GRID_W = 64
NA_HEADS = 8
HEAD_DIM = 64
NA_W = NA_HEADS * HEAD_DIM
NA_WIN_R = 8
NA_WIN_C = 16
ROPE_THETA = 10000.0
HY_W = 256
HY_ORDER = 2
HY_SHORT = 3
HY_BANDS = 16
HY_DECAY_TARGET = 1e-2
HY_FAST_DECAY = 0.3
HY_SLOW_DECAY = 1.5
POOL_W = 256
POOL_SIZES = (2, 4, 8, 16)
POOL_GROUP = POOL_W // len(POOL_SIZES)
MIX_W = NA_W + HY_W + POOL_W
IN_W = 3 * NA_W + (HY_ORDER + 1) * HY_W + POOL_W
N_EXPERTS = 16
EC_FACTOR = 2
D_EXPERT = 2048
EPS = 1e-6

LANES = 128
NEG_BIG = -1e30
VMEM_LIMIT = 56 * 1024 * 1024

NA_Q_ROWS = 4
NA_K_ROWS = NA_Q_ROWS + NA_WIN_R - 1


def _cparams(sem):
    return pltpu.CompilerParams(dimension_semantics=sem, vmem_limit_bytes=VMEM_LIMIT)


def _mod_kernel(c_ref, w_ref, b_ref, o_ref):
    c = c_ref[...]
    s = c * jax.nn.sigmoid(c)
    o_ref[...] = jnp.dot(s, w_ref[...], preferred_element_type=F32,
                         precision=lax.Precision.HIGHEST) + b_ref[...]


def _modulation(cc, w_mod, b_mod):
    rows, d = cc.shape
    n = w_mod.shape[1]
    tn = 1536
    return pl.pallas_call(
        _mod_kernel,
        grid=(n // tn,),
        in_specs=[pl.BlockSpec((rows, d), lambda j: (0, 0)),
                  pl.BlockSpec((d, tn), lambda j: (0, j)),
                  pl.BlockSpec((1, tn), lambda j: (0, j))],
        out_specs=pl.BlockSpec((rows, tn), lambda j: (0, j)),
        out_shape=jax.ShapeDtypeStruct((rows, n), F32),
        compiler_params=_cparams(("arbitrary",)),
        name="modulation",
    )(cc, w_mod, b_mod.reshape(1, n))


def _head_norm(xs, w128, seg):
    sq = xs * xs
    hi = sq.astype(BF16)
    lo = (sq - hi.astype(F32)).astype(BF16)
    m = (jnp.dot(hi, seg, preferred_element_type=F32) + jnp.dot(lo, seg, preferred_element_type=F32))
    return xs * lax.rsqrt(m + EPS) * w128


def _rope(xs, c, s_signed):
    lane = lax.broadcasted_iota(jnp.int32, xs.shape, 1)
    partner = jnp.where((lane & 16) == 0, pltpu.roll(xs, LANES - 16, 1), pltpu.roll(xs, 16, 1))
    return xs * c + partner * s_signed


def _inproj_kernel(x_ref, sh_ref, sc_ref, nw_ref, w_ref, qw_ref, kw_ref, cos_ref, sin_ref, seg_ref,
                   *out_refs, rope):
    if rope:
        qr_ref, qp_ref, k_ref, v_ref, hy_ref, pool_ref = out_refs
    else:
        qp_ref, k_ref, v_ref, hy_ref, pool_ref = out_refs
    x = x_ref[...]
    ms = jnp.mean(x * x, axis=-1, keepdims=True)
    h = x * lax.rsqrt(ms + EPS) * nw_ref[...]
    h = h * (1.0 + sc_ref[0]) + sh_ref[0]
    u = jnp.dot(h.astype(BF16), w_ref[...], preferred_element_type=F32)
    seg = seg_ref[...]
    qk_scale = HEAD_DIM ** -0.5
    for ch in range(NA_W // LANES):
        sl = slice(ch * LANES, (ch + 1) * LANES)
        qn = _head_norm(u[:, sl], qw_ref[...], seg)
        kn = _head_norm(u[:, NA_W + ch * LANES:NA_W + (ch + 1) * LANES], kw_ref[...], seg)
        qp_ref[:, sl] = (qn * qk_scale).astype(BF16)
        if rope:
            c = cos_ref[...]
            s = sin_ref[...]
            qr_ref[:, sl] = (_rope(qn, c, s) * qk_scale).astype(BF16)
            k_ref[:, sl] = _rope(kn, c, s).astype(BF16)
        else:
            k_ref[:, sl] = kn.astype(BF16)
    v_ref[...] = u[:, 2 * NA_W:3 * NA_W].astype(BF16)
    hy_ref[...] = u[:, 3 * NA_W:3 * NA_W + 3 * HY_W]
    pool_ref[...] = u[:, 3 * NA_W + 3 * HY_W:]


def _in_projection(x2, shift, scale, norm_w, w_in_bf, q_norm_w, k_norm_w, cos128, sin128, seg, *,
                   rows_per_mod, seq_len, tm, rope):
    m, d = x2.shape
    tiles_per_mod = rows_per_mod // tm
    tiles_per_seq = seq_len // tm
    row = lambda i: (i, 0)
    const = lambda i: (0, 0)
    modi = lambda i: (i // tiles_per_mod, 0, 0)
    pos = lambda i: (i % tiles_per_seq, 0)
    outs = []
    if rope:
        outs.append((NA_W, BF16))
    outs += [(NA_W, BF16), (NA_W, BF16), (NA_W, BF16), (3 * HY_W, F32), (POOL_W, F32)]
    return pl.pallas_call(
        functools.partial(_inproj_kernel, rope=rope),
        grid=(m // tm,),
        in_specs=[pl.BlockSpec((tm, d), row),
                  pl.BlockSpec((1, 1, d), modi), pl.BlockSpec((1, 1, d), modi),
                  pl.BlockSpec((1, d), const),
                  pl.BlockSpec((d, IN_W), const),
                  pl.BlockSpec((1, LANES), const), pl.BlockSpec((1, LANES), const),
                  pl.BlockSpec((tm, LANES), pos), pl.BlockSpec((tm, LANES), pos),
                  pl.BlockSpec((LANES, LANES), const)],
        out_specs=[pl.BlockSpec((tm, w), row) for w, _ in outs],
        out_shape=[jax.ShapeDtypeStruct((m, w), dt) for w, dt in outs],
        compiler_params=_cparams(("arbitrary",)),
        name="in_projection_rope" if rope else "in_projection_ctx",
    )(x2, shift, scale, norm_w.reshape(1, d), w_in_bf,
      jnp.tile(q_norm_w, 2).reshape(1, LANES), jnp.tile(k_norm_w, 2).reshape(1, LANES),
      cos128, sin128, seg)


def _softmax_pv(s_list, v_list):
    mx = functools.reduce(jnp.maximum, [jnp.max(s, axis=-1, keepdims=True) for s in s_list])
    ps = [jnp.exp(s - mx) for s in s_list]
    l = functools.reduce(jnp.add, [jnp.sum(p, axis=-1, keepdims=True) for p in ps])
    o = functools.reduce(jnp.add, [jnp.dot(p.astype(BF16), v, preferred_element_type=F32)
                                   for p, v in zip(ps, v_list)])
    return o / l


_NT = (((1,), (1,)), ((), ()))


def _na_kernel(qr_ref, qp_ref, k_ref, v_ref, kc_ref, vc_ref, bias_ref, o_ref, *, rows):
    t = pl.program_id(2)
    ks = jnp.clip(t * NA_Q_ROWS - NA_WIN_R // 2, 0, rows - NA_K_ROWS)
    start = pl.multiple_of(ks * GRID_W, GRID_W)
    k = k_ref[0, pl.ds(start, NA_K_ROWS * GRID_W), :]
    v = v_ref[0, pl.ds(start, NA_K_ROWS * GRID_W), :]
    q = qr_ref[0]
    qp = qp_ref[0]
    kc = kc_ref[0]
    vc = vc_ref[0]
    lane = lax.broadcasted_iota(jnp.int32, q.shape, 1)
    outs = []
    for hh in range(2):
        keep = (lane < HEAD_DIM) if hh == 0 else (lane >= HEAD_DIM)
        zero = jnp.zeros_like(q)
        s_lat = lax.dot_general(jnp.where(keep, q, zero), k, _NT, preferred_element_type=F32) + bias_ref[0, hh]
        s_ctx = lax.dot_general(jnp.where(keep, qp, zero), kc, _NT, preferred_element_type=F32)
        outs.append(_softmax_pv([s_lat, s_ctx], [v, vc]))
    o_ref[0] = jnp.where(lane < HEAD_DIM, outs[0], outs[1]).astype(BF16)


def _neighborhood_attention(qr, qp, k, v, kc, vc, bias):
    b, n, _ = qr.shape
    lc = kc.shape[1]
    rows = n // GRID_W
    n_tiles = rows // NA_Q_ROWS
    tq = NA_Q_ROWS * GRID_W
    tk = NA_K_ROWS * GRID_W
    qmap = lambda bi, hp, t: (bi, t, hp)
    smap = lambda bi, hp, t: (bi, 0, hp)
    bmap = lambda bi, hp, t: (jnp.where(t == 0, 0, jnp.where(t == n_tiles - 1, 2, 1)), hp, 0, 0)
    return pl.pallas_call(
        functools.partial(_na_kernel, rows=rows),
        grid=(b, NA_W // LANES, n_tiles),
        in_specs=[pl.BlockSpec((1, tq, LANES), qmap), pl.BlockSpec((1, tq, LANES), qmap),
                  pl.BlockSpec((1, n, LANES), smap), pl.BlockSpec((1, n, LANES), smap),
                  pl.BlockSpec((1, lc, LANES), smap), pl.BlockSpec((1, lc, LANES), smap),
                  pl.BlockSpec((1, 2, tq, tk), bmap)],
        out_specs=pl.BlockSpec((1, tq, LANES), qmap),
        out_shape=jax.ShapeDtypeStruct((b, n, NA_W), BF16),
        compiler_params=_cparams(("arbitrary", "arbitrary", "arbitrary")),
        name="neighborhood_attention",
    )(qr, qp, k, v, kc, vc, bias)


def _na_bias_tables(rpb, rows):
    n_tiles = rows // NA_Q_ROWS
    tabs = []
    for t in (0, 1, n_tiles - 1):
        r0 = t * NA_Q_ROWS
        ks = int(np.clip(r0 - NA_WIN_R // 2, 0, rows - NA_K_ROWS))
        r = r0 + np.arange(NA_Q_ROWS)
        rs = np.clip(r - NA_WIN_R // 2, 0, rows - NA_WIN_R)
        krow = ks + np.arange(NA_K_ROWS)
        ok_r = (krow[None, :] >= rs[:, None]) & (krow[None, :] < rs[:, None] + NA_WIN_R)
        dr = np.clip(krow[None, :] - r[:, None] + NA_WIN_R - 1, 0, 2 * NA_WIN_R - 2)
        qc = np.arange(GRID_W)
        cs = np.clip(qc - NA_WIN_C // 2, 0, GRID_W - NA_WIN_C)
        kcol = np.arange(GRID_W)
        ok_c = (kcol[None, :] >= cs[:, None]) & (kcol[None, :] < cs[:, None] + NA_WIN_C)
        dc = np.clip(kcol[None, :] - qc[:, None] + NA_WIN_C - 1, 0, 2 * NA_WIN_C - 2)
        ok = ok_r[:, None, :, None] & ok_c[None, :, None, :]
        g = rpb.astype(F32)[:, dr[:, None, :, None], dc[None, :, None, :]]
        g = jnp.where(ok[None], g, NEG_BIG)
        tabs.append(g.reshape(NA_HEADS, NA_Q_ROWS * GRID_W, NA_K_ROWS * GRID_W))
    return jnp.stack(tabs)


def _dense_attn_kernel(q_ref, k_ref, v_ref, o_ref):
    q = q_ref[0]
    k = k_ref[0]
    v = v_ref[0]
    lane = lax.broadcasted_iota(jnp.int32, q.shape, 1)
    outs = []
    for hh in range(2):
        keep = (lane < HEAD_DIM) if hh == 0 else (lane >= HEAD_DIM)
        s = lax.dot_general(jnp.where(keep, q, jnp.zeros_like(q)), k, _NT, preferred_element_type=F32)
        outs.append(_softmax_pv([s], [v]))
    o_ref[0] = jnp.where(lane < HEAD_DIM, outs[0], outs[1]).astype(BF16)


def _dense_attention(q, k, v):
    b, l, _ = q.shape
    spec = pl.BlockSpec((1, l, LANES), lambda bi, hp: (bi, 0, hp))
    return pl.pallas_call(
        _dense_attn_kernel,
        grid=(b, NA_W // LANES),
        in_specs=[spec, spec, spec],
        out_specs=spec,
        out_shape=jax.ShapeDtypeStruct((b, l, NA_W), BF16),
        compiler_params=_cparams(("arbitrary", "arbitrary")),
        name="context_attention",
    )(q, k, v)


POOL_HALO = max(POOL_SIZES) // 2


def _pool_kernel(x_ref, w_ref, sc_ref, o_ref, xpad, *, seq_len, chunk):
    zeros = jnp.zeros((POOL_HALO, POOL_W), F32)
    xpad[pl.ds(0, POOL_HALO), :] = zeros
    xpad[pl.ds(seq_len + POOL_HALO, POOL_HALO), :] = zeros
    xpad[pl.ds(POOL_HALO, seq_len), :] = x_ref[0]
    span = chunk + 2 * POOL_HALO
    w = w_ref[...]
    sc = sc_ref[...]

    def body(c, carry):
        base = pl.multiple_of(c * chunk, chunk)
        xs = xpad[pl.ds(base, span), :]
        s2 = xs + pltpu.roll(xs, 1, 0)
        s4 = pltpu.roll(s2, 1, 0) + pltpu.roll(s2, span - 1, 0)
        s8 = pltpu.roll(s4, 2, 0) + pltpu.roll(s4, span - 2, 0)
        s16 = pltpu.roll(s8, 4, 0) + pltpu.roll(s8, span - 4, 0)
        mid = slice(POOL_HALO, POOL_HALO + chunk)
        grp = lax.broadcasted_iota(jnp.int32, (chunk, POOL_W), 1) // POOL_GROUP
        tpos = base + lax.broadcasted_iota(jnp.int32, (chunk, POOL_W), 0)
        half = jnp.left_shift(1, grp)
        cnt = (jnp.minimum(tpos + half, seq_len) - jnp.maximum(tpos - half, 0)).astype(F32)
        ssum = jnp.where(grp == 0, s2[mid], jnp.where(grp == 1, s4[mid], jnp.where(grp == 2, s8[mid], s16[mid])))
        diff = ssum / cnt - xs[mid]
        y = jnp.dot(diff.astype(BF16), w, preferred_element_type=F32) * sc
        o_ref[0, pl.ds(base, chunk), :] = y.astype(BF16)
        return carry

    lax.fori_loop(0, seq_len // chunk, body, 0)


def _pool_mixer(u, w_blockdiag_bf, pool_scale):
    b, l, _ = u.shape
    chunk = min(l, 512)
    return pl.pallas_call(
        functools.partial(_pool_kernel, seq_len=l, chunk=chunk),
        grid=(b,),
        in_specs=[pl.BlockSpec((1, l, POOL_W), lambda bi: (bi, 0, 0)),
                  pl.BlockSpec((POOL_W, POOL_W), lambda bi: (0, 0)),
                  pl.BlockSpec((1, POOL_W), lambda bi: (0, 0))],
        out_specs=pl.BlockSpec((1, l, POOL_W), lambda bi: (bi, 0, 0)),
        out_shape=jax.ShapeDtypeStruct((b, l, POOL_W), BF16),
        scratch_shapes=[pltpu.VMEM((l + 2 * POOL_HALO, POOL_W), F32)],
        compiler_params=_cparams(("arbitrary",)),
        name="pool_mixer",
    )(u, w_blockdiag_bf, pool_scale.reshape(1, POOL_W))


def _outproj_kernel(na_ref, hy_ref, pl_ref, w_ref, x_ref, g1_ref, nw_ref, sh_ref, sc_ref, wr_ref,
                    xn_ref, h2_ref, aff_ref):
    w = w_ref
    acc = jnp.dot(na_ref[...], w[pl.ds(0, NA_W), :], preferred_element_type=F32)
    acc += jnp.dot(hy_ref[...], w[pl.ds(NA_W, HY_W), :], preferred_element_type=F32)
    acc += jnp.dot(pl_ref[...], w[pl.ds(NA_W + HY_W, POOL_W), :], preferred_element_type=F32)
    xn = x_ref[...] + g1_ref[0] * acc
    xn_ref[...] = xn
    ms = jnp.mean(xn * xn, axis=-1, keepdims=True)
    h2 = xn * lax.rsqrt(ms + EPS) * nw_ref[...]
    h2 = h2 * (1.0 + sc_ref[0]) + sh_ref[0]
    h2_ref[...] = h2.astype(BF16)
    logits = lax.dot_general(wr_ref[...], h2, _NT, preferred_element_type=F32,
                             precision=lax.Precision.HIGHEST)
    mx = jnp.max(logits, axis=0, keepdims=True)
    p = jnp.exp(logits - mx)
    aff_ref[0] = p / jnp.sum(p, axis=0, keepdims=True)


def _out_projection(na, hy, pool, w_out_bf, x2, g1, norm_w, shift, scale, w_router_t, *,
                    rows_per_mod, seq_len, tm):
    m, d = x2.shape
    tiles_per_mod = rows_per_mod // tm
    tiles_per_seq = seq_len // tm
    row = lambda i: (i, 0)
    const = lambda i: (0, 0)
    modi = lambda i: (i // tiles_per_mod, 0, 0)
    return pl.pallas_call(
        _outproj_kernel,
        grid=(m // tm,),
        in_specs=[pl.BlockSpec((tm, NA_W), row), pl.BlockSpec((tm, HY_W), row), pl.BlockSpec((tm, POOL_W), row),
                  pl.BlockSpec((MIX_W, d), const),
                  pl.BlockSpec((tm, d), row),
                  pl.BlockSpec((1, 1, d), modi),
                  pl.BlockSpec((1, d), const),
                  pl.BlockSpec((1, 1, d), modi), pl.BlockSpec((1, 1, d), modi),
                  pl.BlockSpec((N_EXPERTS, d), const)],
        out_specs=[pl.BlockSpec((tm, d), row), pl.BlockSpec((tm, d), row),
                   pl.BlockSpec((1, N_EXPERTS, tm), lambda i: (i // tiles_per_seq, 0, i % tiles_per_seq))],
        out_shape=[jax.ShapeDtypeStruct((m, d), F32), jax.ShapeDtypeStruct((m, d), BF16),
                   jax.ShapeDtypeStruct((m // seq_len, N_EXPERTS, seq_len), F32)],
        compiler_params=_cparams(("arbitrary",)),
        name="out_projection_router",
    )(na, hy, pool, w_out_bf, x2, g1, norm_w.reshape(1, d), shift, scale, w_router_t)


def _ffn_kernel(x_ref, wg_ref, wu_ref, wd_ref, o_ref, *, row_chunk):
    f = pl.program_id(1)
    wg = wg_ref[0].astype(BF16)
    wu = wu_ref[0].astype(BF16)
    wd = wd_ref[0].astype(BF16)
    m = x_ref.shape[1]
    for r in range(m // row_chunk):
        rs = pl.ds(r * row_chunk, row_chunk)
        x = x_ref[0, rs, :]
        a = jnp.dot(x, wg, preferred_element_type=F32)
        b = jnp.dot(x, wu, preferred_element_type=F32)
        h = (a * jax.nn.sigmoid(a) * b).astype(BF16)
        y = jnp.dot(h, wd, preferred_element_type=F32)

        @pl.when(f == 0)
        def _():
            o_ref[0, rs, :] = y

        @pl.when(f != 0)
        def _():
            o_ref[0, rs, :] += y


def _expert_ffn(xg, w_gate, w_up, w_down):
    e, m, d = xg.shape
    tf = 512
    row_chunk = min(m, 512)
    return pl.pallas_call(
        functools.partial(_ffn_kernel, row_chunk=row_chunk),
        grid=(e, D_EXPERT // tf),
        in_specs=[pl.BlockSpec((1, m, d), lambda ei, f: (ei, 0, 0)),
                  pl.BlockSpec((1, d, tf), lambda ei, f: (ei, 0, f)),
                  pl.BlockSpec((1, d, tf), lambda ei, f: (ei, 0, f)),
                  pl.BlockSpec((1, tf, d), lambda ei, f: (ei, f, 0))],
        out_specs=pl.BlockSpec((1, m, d), lambda ei, f: (ei, 0, 0)),
        out_shape=jax.ShapeDtypeStruct((e, m, d), F32),
        compiler_params=_cparams(("arbitrary", "arbitrary")),
        name="expert_ffn",
    )(xg, w_gate, w_up, w_down)


def _expert_choice_moe(h2, aff, w_gate, w_up, w_down):
    b, n, d = h2.shape
    cap = EC_FACTOR * n // N_EXPERTS
    gate, idx = lax.top_k(aff, cap)
    bidx = jnp.arange(b)[:, None, None]
    xg = h2[bidx, idx]
    xg = xg.transpose(1, 0, 2, 3).reshape(N_EXPERTS, b * cap, d)
    y = _expert_ffn(xg, w_gate, w_up, w_down)
    y = y.reshape(N_EXPERTS, b, cap, d).transpose(1, 0, 2, 3) * gate[..., None]
    return jnp.zeros((b, n, d), F32).at[bidx, idx].add(y)


def _short_conv(u, w, b):
    l = u.shape[1]
    pad = HY_SHORT // 2
    up = jnp.pad(u, ((0, 0), (pad, pad), (0, 0)))
    y = b
    for j in range(HY_SHORT):
        y = y + up[:, j:j + l] * w[j]
    return y


def _hyena_filter(l, w1, b1, w2, b2, w3, freq):
    t = jnp.linspace(0.0, 1.0, l, dtype=F32)[:, None]
    w = (2.0 * math.pi / l) * jnp.arange(l, dtype=F32)[:, None]
    f = jnp.linspace(1e-4, HY_BANDS - 1, HY_BANDS, dtype=F32)[None, :]
    z = jnp.concatenate([t, jnp.cos(f * w), -jnp.sin(f * w)], axis=-1)
    hp = lax.Precision.HIGHEST
    h = jnp.sin(freq * (jnp.dot(z, w1, precision=hp) + b1))
    h = jnp.sin(freq * (jnp.dot(h, w2, precision=hp) + b2))
    h = jnp.dot(h, w3, precision=hp).reshape(l, HY_ORDER, 2, HY_W)
    deltas = jnp.abs(jnp.linspace(math.log(HY_DECAY_TARGET) / HY_SLOW_DECAY,
                                  math.log(HY_DECAY_TARGET) / HY_FAST_DECAY, HY_W, dtype=F32))
    h = h * jnp.exp(-t * deltas)[:, None, None, :]
    hf, hb = h[:, :, 0], h[:, :, 1]
    k2 = jnp.concatenate([hf, jnp.zeros_like(hf[:1]), hb[:0:-1]], axis=0)
    k2 = k2 * lax.rsqrt(jnp.sum(k2 * k2, axis=0, keepdims=True) + EPS)
    return jnp.fft.rfft(k2, axis=0)


def _fft_conv(u, kf):
    l = u.shape[1]
    uf = jnp.fft.rfft(u, n=2 * l, axis=1)
    return jnp.fft.irfft(uf * kf[None], n=2 * l, axis=1)[:, :l]


def _hyena_mixer(u, kf, conv_w, conv_b, skip):
    u = _short_conv(u, conv_w, conv_b)
    x1, x2, v = jnp.split(u, 3, axis=-1)
    z = v
    for o, gate in enumerate((x1, x2)):
        z = gate * (_fft_conv(z, kf[:, o]) + z * skip[o])
    return z


def _rope_tables(n):
    t = jnp.arange(n, dtype=jnp.int32)
    pos = jnp.stack([t // GRID_W, t % GRID_W], axis=-1).astype(F32)
    nf = HEAD_DIM // 4
    inv = ROPE_THETA ** (-jnp.arange(nf, dtype=F32) / nf)
    ang = pos[:, :, None] * inv
    cos, sin = jnp.cos(ang), jnp.sin(ang)
    c64 = jnp.concatenate([cos[:, 0], cos[:, 0], cos[:, 1], cos[:, 1]], axis=-1)
    s64 = jnp.concatenate([-sin[:, 0], sin[:, 0], -sin[:, 1], sin[:, 1]], axis=-1)
    return jnp.tile(c64, (1, 2)), jnp.tile(s64, (1, 2))


def _block_diag(pool_w):
    g, c, _ = pool_w.shape
    out = jnp.zeros((g * c, g * c), pool_w.dtype)
    for i in range(g):
        out = out.at[i * c:(i + 1) * c, i * c:(i + 1) * c].set(pool_w[i])
    return out


def kernel(x, c, ctx, c_ctx, w_mod, b_mod, norm1_w, norm2_w, w_in, w_out, q_norm_w, k_norm_w, na_rpb,
           hy_conv_w, hy_conv_b, hy_w1, hy_b1, hy_w2, hy_b2, hy_w3, hy_freq, hy_skip,
           pool_w, pool_scale, w_router, w_gate, w_up, w_down):
    b, n, d = x.shape
    lc = ctx.shape[1]
    depth = w_mod.shape[0]
    rows = n // GRID_W
    cos128, sin128 = _rope_tables(n)
    seg = jnp.asarray(np.kron(np.eye(2), np.full((HEAD_DIM, HEAD_DIM), 1.0 / HEAD_DIM)), BF16)
    cc = jnp.zeros((8, d), F32).at[:b].set(c).at[b].set(c_ctx)

    xl = x.reshape(b * n, d)
    xc = ctx.reshape(b * lc, d)
    for i in range(depth):
        last = i == depth - 1
        mod = _modulation(cc, w_mod[i], b_mod[i])
        ml = mod[:b].reshape(b, 1, 6, d)
        mc = mod[b].reshape(1, 1, 6, d)
        sh1, sc1, g1, sh2, sc2, g2 = [ml[:, :, j] for j in range(6)]
        csh1, csc1, cg1, csh2, csc2, cg2 = [mc[:, :, j] for j in range(6)]
        w_in_bf = w_in[i].astype(BF16)
        w_out_bf = w_out[i].astype(BF16)
        w_router_t = w_router[i].T
        pool_bd = _block_diag(pool_w[i]).astype(BF16)
        hy_args = (hy_w1[i], hy_b1[i], hy_w2[i], hy_b2[i], hy_w3[i], hy_freq[i])

        qc, kc, vc, hyc, poolc = _in_projection(
            xc, csh1, csc1, norm1_w[i], w_in_bf, q_norm_w[i], k_norm_w[i], cos128, sin128, seg,
            rows_per_mod=b * lc, seq_len=lc, tm=lc, rope=False)
        kc = kc.reshape(b, lc, NA_W)
        vc = vc.reshape(b, lc, NA_W)
        if not last:
            na_c = _dense_attention(qc.reshape(b, lc, NA_W), kc, vc).reshape(b * lc, NA_W)
            ctx_kf = _hyena_filter(lc, *hy_args)
            hy_c = _hyena_mixer(hyc.reshape(b, lc, 3 * HY_W), ctx_kf, hy_conv_w[i], hy_conv_b[i], hy_skip[i])
            pool_c = _pool_mixer(poolc.reshape(b, lc, POOL_W), pool_bd, pool_scale[i])
            xc1, h2c, affc = _out_projection(
                na_c, hy_c.reshape(b * lc, HY_W).astype(BF16), pool_c.reshape(b * lc, POOL_W), w_out_bf,
                xc, cg1, norm2_w[i], csh2, csc2, w_router_t, rows_per_mod=b * lc, seq_len=lc, tm=lc)
            moe_c = _expert_choice_moe(h2c.reshape(b, lc, d), affc, w_gate[i], w_up[i], w_down[i])
            xc_new = xc1 + (cg2 * moe_c).reshape(b * lc, d)

        qr, qp, kl, vl, hyl, pooll = _in_projection(
            xl, sh1, sc1, norm1_w[i], w_in_bf, q_norm_w[i], k_norm_w[i], cos128, sin128, seg,
            rows_per_mod=n, seq_len=n, tm=512, rope=True)
        bias = _na_bias_tables(na_rpb[i], rows)
        o_na = _neighborhood_attention(qr.reshape(b, n, NA_W), qp.reshape(b, n, NA_W), kl.reshape(b, n, NA_W),
                                       vl.reshape(b, n, NA_W), kc, vc, bias)
        lat_kf = _hyena_filter(n, *hy_args)
        hy_l = _hyena_mixer(hyl.reshape(b, n, 3 * HY_W), lat_kf, hy_conv_w[i], hy_conv_b[i], hy_skip[i])
        pool_l = _pool_mixer(pooll.reshape(b, n, POOL_W), pool_bd, pool_scale[i])
        xl1, h2l, affl = _out_projection(
            o_na.reshape(b * n, NA_W), hy_l.reshape(b * n, HY_W).astype(BF16), pool_l.reshape(b * n, POOL_W),
            w_out_bf, xl, g1, norm2_w[i], sh2, sc2, w_router_t, rows_per_mod=n, seq_len=n, tm=512)
        moe_l = _expert_choice_moe(h2l.reshape(b, n, d), affl, w_gate[i], w_up[i], w_down[i])
        xl = xl1 + (g2 * moe_l).reshape(b * n, d)
        if not last:
            xc = xc_new
    return xl.reshape(b, n, d)
```

```python
import functools
import math

import jax
import jax.numpy as jnp
import numpy as np
from jax import lax
from jax.experimental import pallas as pl
from jax.experimental.pallas import tpu as pltpu

F32 = jnp.float32
BF16 = jnp.bfloat16

D_MODEL = 1024
GRID_W = 64
NA_HEADS = 8
HEAD_DIM = 64
NA_W = NA_HEADS * HEAD_DIM
NA_WIN_R = 8
NA_WIN_C = 16
ROPE_THETA = 10000.0
HY_W = 256
HY_ORDER = 2
HY_SHORT = 3
HY_BANDS = 16
HY_DECAY_TARGET = 1e-2
HY_FAST_DECAY = 0.3
HY_SLOW_DECAY = 1.5
POOL_W = 256
POOL_SIZES = (2, 4, 8, 16)
POOL_GROUP = POOL_W // len(POOL_SIZES)
MIX_W = NA_W + HY_W + POOL_W
IN_W = 3 * NA_W + (HY_ORDER + 1) * HY_W + POOL_W
N_EXPERTS = 16
EC_FACTOR = 2
D_EXPERT = 2048
EPS = 1e-6

LANES = 128
NEG_BIG = -1e30
VMEM_LIMIT = 56 * 1024 * 1024

NA_Q_ROWS = 4
NA_K_ROWS = NA_Q_ROWS + NA_WIN_R - 1


def _cparams(sem):
    return pltpu.CompilerParams(dimension_semantics=sem, vmem_limit_bytes=VMEM_LIMIT)


def _mod_kernel(c_ref, w_ref, b_ref, o_ref):
    c = c_ref[...]
    s = c * jax.nn.sigmoid(c)
    o_ref[...] = jnp.dot(s, w_ref[...], preferred_element_type=F32,
                         precision=lax.Precision.HIGHEST) + b_ref[...]


def _modulation(cc, w_mod, b_mod):
    rows, d = cc.shape
    n = w_mod.shape[1]
    tn = 1536
    return pl.pallas_call(
        _mod_kernel,
        grid=(n // tn,),
        in_specs=[pl.BlockSpec((rows, d), lambda j: (0, 0)),
                  pl.BlockSpec((d, tn), lambda j: (0, j)),
                  pl.BlockSpec((1, tn), lambda j: (0, j))],
        out_specs=pl.BlockSpec((rows, tn), lambda j: (0, j)),
        out_shape=jax.ShapeDtypeStruct((rows, n), F32),
        compiler_params=_cparams(("arbitrary",)),
        name="modulation",
    )(cc, w_mod, b_mod.reshape(1, n))


def _head_norm(xs, w128, seg):
    sq = xs * xs
    hi = sq.astype(BF16)
    lo = (sq - hi.astype(F32)).astype(BF16)
    m = (jnp.dot(hi, seg, preferred_element_type=F32) + jnp.dot(lo, seg, preferred_element_type=F32))
    return xs * lax.rsqrt(m + EPS) * w128


def _rope(xs, c, s_signed):
    lane = lax.broadcasted_iota(jnp.int32, xs.shape, 1)
    partner = jnp.where((lane & 16) == 0, pltpu.roll(xs, LANES - 16, 1), pltpu.roll(xs, 16, 1))
    return xs * c + partner * s_signed


def _inproj_kernel(x_ref, sh_ref, sc_ref, nw_ref, w_ref, qw_ref, kw_ref, cos_ref, sin_ref, seg_ref,
                   *out_refs, rope):
    if rope:
        qr_ref, qp_ref, k_ref, v_ref, hy_ref, pool_ref = out_refs
    else:
        qp_ref, k_ref, v_ref, hy_ref, pool_ref = out_refs
    x = x_ref[...]
    ms = jnp.mean(x * x, axis=-1, keepdims=True)
    h = x * lax.rsqrt(ms + EPS) * nw_ref[...]
    h = h * (1.0 + sc_ref[0]) + sh_ref[0]
    u = jnp.dot(h.astype(BF16), w_ref[...], preferred_element_type=F32)
    seg = seg_ref[...]
    qk_scale = HEAD_DIM ** -0.5
    for ch in range(NA_W // LANES):
        sl = slice(ch * LANES, (ch + 1) * LANES)
        qn = _head_norm(u[:, sl], qw_ref[...], seg)
        kn = _head_norm(u[:, NA_W + ch * LANES:NA_W + (ch + 1) * LANES], kw_ref[...], seg)
        qp_ref[:, sl] = (qn * qk_scale).astype(BF16)
        if rope:
            c = cos_ref[...]
            s = sin_ref[...]
            qr_ref[:, sl] = (_rope(qn, c, s) * qk_scale).astype(BF16)
            k_ref[:, sl] = _rope(kn, c, s).astype(BF16)
        else:
            k_ref[:, sl] = kn.astype(BF16)
    v_ref[...] = u[:, 2 * NA_W:3 * NA_W].astype(BF16)
    hy_ref[...] = u[:, 3 * NA_W:3 * NA_W + 3 * HY_W]
    pool_ref[...] = u[:, 3 * NA_W + 3 * HY_W:]


def _in_projection(x2, shift, scale, norm_w, w_in_bf, q_norm_w, k_norm_w, cos128, sin128, seg, *,
                   rows_per_mod, seq_len, tm, rope):
    m, d = x2.shape
    tiles_per_mod = rows_per_mod // tm
    tiles_per_seq = seq_len // tm
    row = lambda i: (i, 0)
    const = lambda i: (0, 0)
    modi = lambda i: (i // tiles_per_mod, 0, 0)
    pos = lambda i: (i % tiles_per_seq, 0)
    outs = []
    if rope:
        outs.append((NA_W, BF16))
    outs += [(NA_W, BF16), (NA_W, BF16), (NA_W, BF16), (3 * HY_W, F32), (POOL_W, F32)]
    return pl.pallas_call(
        functools.partial(_inproj_kernel, rope=rope),
        grid=(m // tm,),
        in_specs=[pl.BlockSpec((tm, d), row),
                  pl.BlockSpec((1, 1, d), modi), pl.BlockSpec((1, 1, d), modi),
                  pl.BlockSpec((1, d), const),
                  pl.BlockSpec((d, IN_W), const),
                  pl.BlockSpec((1, LANES), const), pl.BlockSpec((1, LANES), const),
                  pl.BlockSpec((tm, LANES), pos), pl.BlockSpec((tm, LANES), pos),
                  pl.BlockSpec((LANES, LANES), const)],
        out_specs=[pl.BlockSpec((tm, w), row) for w, _ in outs],
        out_shape=[jax.ShapeDtypeStruct((m, w), dt) for w, dt in outs],
        compiler_params=_cparams(("arbitrary",)),
        name="in_projection_rope" if rope else "in_projection_ctx",
    )(x2, shift, scale, norm_w.reshape(1, d), w_in_bf,
      jnp.tile(q_norm_w, 2).reshape(1, LANES), jnp.tile(k_norm_w, 2).reshape(1, LANES),
      cos128, sin128, seg)


def _softmax_pv(s_list, v_list):
    mx = functools.reduce(jnp.maximum, [jnp.max(s, axis=-1, keepdims=True) for s in s_list])
    ps = [jnp.exp(s - mx) for s in s_list]
    l = functools.reduce(jnp.add, [jnp.sum(p, axis=-1, keepdims=True) for p in ps])
    o = functools.reduce(jnp.add, [jnp.dot(p.astype(BF16), v, preferred_element_type=F32)
                                   for p, v in zip(ps, v_list)])
    return o / l


_NT = (((1,), (1,)), ((), ()))


def _na_kernel(qr_ref, qp_ref, k_ref, v_ref, kc_ref, vc_ref, bias_ref, o_ref, *, rows):
    t = pl.program_id(2)
    ks = jnp.clip(t * NA_Q_ROWS - NA_WIN_R // 2, 0, rows - NA_K_ROWS)
    start = pl.multiple_of(ks * GRID_W, GRID_W)
    k = k_ref[0, pl.ds(start, NA_K_ROWS * GRID_W), :]
    v = v_ref[0, pl.ds(start, NA_K_ROWS * GRID_W), :]
    q = qr_ref[0]
    qp = qp_ref[0]
    kc = kc_ref[0]
    vc = vc_ref[0]
    lane = lax.broadcasted_iota(jnp.int32, q.shape, 1)
    outs = []
    for hh in range(2):
        keep = (lane < HEAD_DIM) if hh == 0 else (lane >= HEAD_DIM)
        zero = jnp.zeros_like(q)
        s_lat = lax.dot_general(jnp.where(keep, q, zero), k, _NT, preferred_element_type=F32) + bias_ref[0, hh]
        s_ctx = lax.dot_general(jnp.where(keep, qp, zero), kc, _NT, preferred_element_type=F32)
        outs.append(_softmax_pv([s_lat, s_ctx], [v, vc]))
    o_ref[0] = jnp.where(lane < HEAD_DIM, outs[0], outs[1]).astype(BF16)


def _neighborhood_attention(qr, qp, k, v, kc, vc, bias):
    b, n, _ = qr.shape
    lc = kc.shape[1]
    rows = n // GRID_W
    n_tiles = rows // NA_Q_ROWS
    tq = NA_Q_ROWS * GRID_W
    tk = NA_K_ROWS * GRID_W
    qmap = lambda bi, hp, t: (bi, t, hp)
    smap = lambda bi, hp, t: (bi, 0, hp)
    bmap = lambda bi, hp, t: (jnp.where(t == 0, 0, jnp.where(t == n_tiles - 1, 2, 1)), hp, 0, 0)
    return pl.pallas_call(
        functools.partial(_na_kernel, rows=rows),
        grid=(b, NA_W // LANES, n_tiles),
        in_specs=[pl.BlockSpec((1, tq, LANES), qmap), pl.BlockSpec((1, tq, LANES), qmap),
                  pl.BlockSpec((1, n, LANES), smap), pl.BlockSpec((1, n, LANES), smap),
                  pl.BlockSpec((1, lc, LANES), smap), pl.BlockSpec((1, lc, LANES), smap),
                  pl.BlockSpec((1, 2, tq, tk), bmap)],
        out_specs=pl.BlockSpec((1, tq, LANES), qmap),
        out_shape=jax.ShapeDtypeStruct((b, n, NA_W), BF16),
        compiler_params=_cparams(("arbitrary", "arbitrary", "arbitrary")),
        name="neighborhood_attention",
    )(qr, qp, k, v, kc, vc, bias)


def _na_bias_tables(rpb, rows):
    n_tiles = rows // NA_Q_ROWS
    n_dr, n_dc = 2 * NA_WIN_R - 1, 2 * NA_WIN_C - 1
    qc = np.arange(GRID_W)
    cs = np.clip(qc - NA_WIN_C // 2, 0, GRID_W - NA_WIN_C)
    kcol = np.arange(GRID_W)
    ok_c = (kcol[None, :] >= cs[:, None]) & (kcol[None, :] < cs[:, None] + NA_WIN_C)
    dc = kcol[None, :] - qc[:, None] + NA_WIN_C - 1
    col_sel = (dc[:, :, None] == np.arange(n_dc)) & ok_c[:, :, None]
    row_sel, masks = [], []
    for t in (0, 1, n_tiles - 1):
        r0 = t * NA_Q_ROWS
        ks = int(np.clip(r0 - NA_WIN_R // 2, 0, rows - NA_K_ROWS))
        r = r0 + np.arange(NA_Q_ROWS)
        rs = np.clip(r - NA_WIN_R // 2, 0, rows - NA_WIN_R)
        krow = ks + np.arange(NA_K_ROWS)
        ok_r = (krow[None, :] >= rs[:, None]) & (krow[None, :] < rs[:, None] + NA_WIN_R)
        dr = krow[None, :] - r[:, None] + NA_WIN_R - 1
        row_sel.append((dr[:, :, None] == np.arange(n_dr)) & ok_r[:, :, None])
        ok = ok_r[:, None, :, None] & ok_c[None, :, None, :]
        masks.append(np.where(ok, 0.0, NEG_BIG))
    tab = jnp.einsum("hrd,tijr,qkd->thiqjk", rpb.astype(F32), jnp.asarray(np.stack(row_sel), F32),
                     jnp.asarray(col_sel, F32), precision=lax.Precision.HIGHEST)
    tab = tab + jnp.asarray(np.stack(masks), F32)[:, None]
    return tab.reshape(3, NA_HEADS, NA_Q_ROWS * GRID_W, NA_K_ROWS * GRID_W)


def _dense_attn_kernel(q_ref, k_ref, v_ref, o_ref):
    q = q_ref[0]
    k = k_ref[0]
    v = v_ref[0]
    lane = lax.broadcasted_iota(jnp.int32, q.shape, 1)
    outs = []
    for hh in range(2):
        keep = (lane < HEAD_DIM) if hh == 0 else (lane >= HEAD_DIM)
        s = lax.dot_general(jnp.where(keep, q, jnp.zeros_like(q)), k, _NT, preferred_element_type=F32)
        outs.append(_softmax_pv([s], [v]))
    o_ref[0] = jnp.where(lane < HEAD_DIM, outs[0], outs[1]).astype(BF16)


def _dense_attention(q, k, v):
    b, l, _ = q.shape
    spec = pl.BlockSpec((1, l, LANES), lambda bi, hp: (bi, 0, hp))
    return pl.pallas_call(
        _dense_attn_kernel,
        grid=(b, NA_W // LANES),
        in_specs=[spec, spec, spec],
        out_specs=spec,
        out_shape=jax.ShapeDtypeStruct((b, l, NA_W), BF16),
        compiler_params=_cparams(("arbitrary", "arbitrary")),
        name="context_attention",
    )(q, k, v)


POOL_HALO = max(POOL_SIZES) // 2


def _pool_kernel(x_ref, w_ref, sc_ref, o_ref, xpad, *, seq_len, chunk):
    zeros = jnp.zeros((POOL_HALO, POOL_W), F32)
    xpad[pl.ds(0, POOL_HALO), :] = zeros
    xpad[pl.ds(seq_len + POOL_HALO, POOL_HALO), :] = zeros
    xpad[pl.ds(POOL_HALO, seq_len), :] = x_ref[0]
    span = chunk + 2 * POOL_HALO
    w = w_ref[...]
    sc = sc_ref[...]

    def body(c, carry):
        base = pl.multiple_of(c * chunk, chunk)
        xs = xpad[pl.ds(base, span), :]
        s2 = xs + pltpu.roll(xs, 1, 0)
        s4 = pltpu.roll(s2, 1, 0) + pltpu.roll(s2, span - 1, 0)
        s8 = pltpu.roll(s4, 2, 0) + pltpu.roll(s4, span - 2, 0)
        s16 = pltpu.roll(s8, 4, 0) + pltpu.roll(s8, span - 4, 0)
        mid = slice(POOL_HALO, POOL_HALO + chunk)
        grp = lax.broadcasted_iota(jnp.int32, (chunk, POOL_W), 1) // POOL_GROUP
        tpos = base + lax.broadcasted_iota(jnp.int32, (chunk, POOL_W), 0)
        half = jnp.left_shift(1, grp)
        cnt = (jnp.minimum(tpos + half, seq_len) - jnp.maximum(tpos - half, 0)).astype(F32)
        ssum = jnp.where(grp == 0, s2[mid], jnp.where(grp == 1, s4[mid], jnp.where(grp == 2, s8[mid], s16[mid])))
        diff = ssum / cnt - xs[mid]
        y = jnp.dot(diff.astype(BF16), w, preferred_element_type=F32) * sc
        o_ref[0, pl.ds(base, chunk), :] = y.astype(BF16)
        return carry

    lax.fori_loop(0, seq_len // chunk, body, 0)


def _pool_mixer(u, w_blockdiag_bf, pool_scale):
    b, l, _ = u.shape
    chunk = min(l, 512)
    return pl.pallas_call(
        functools.partial(_pool_kernel, seq_len=l, chunk=chunk),
        grid=(b,),
        in_specs=[pl.BlockSpec((1, l, POOL_W), lambda bi: (bi, 0, 0)),
                  pl.BlockSpec((POOL_W, POOL_W), lambda bi: (0, 0)),
                  pl.BlockSpec((1, POOL_W), lambda bi: (0, 0))],
        out_specs=pl.BlockSpec((1, l, POOL_W), lambda bi: (bi, 0, 0)),
        out_shape=jax.ShapeDtypeStruct((b, l, POOL_W), BF16),
        scratch_shapes=[pltpu.VMEM((l + 2 * POOL_HALO, POOL_W), F32)],
        compiler_params=_cparams(("arbitrary",)),
        name="pool_mixer",
    )(u, w_blockdiag_bf, pool_scale.reshape(1, POOL_W))


def _outproj_kernel(na_ref, hy_ref, pl_ref, w_ref, x_ref, g1_ref, nw_ref, sh_ref, sc_ref, wr_ref,
                    xn_ref, h2_ref, aff_ref):
    w = w_ref
    acc = jnp.dot(na_ref[...], w[pl.ds(0, NA_W), :], preferred_element_type=F32)
    acc += jnp.dot(hy_ref[...], w[pl.ds(NA_W, HY_W), :], preferred_element_type=F32)
    acc += jnp.dot(pl_ref[...], w[pl.ds(NA_W + HY_W, POOL_W), :], preferred_element_type=F32)
    xn = x_ref[...] + g1_ref[0] * acc
    xn_ref[...] = xn
    ms = jnp.mean(xn * xn, axis=-1, keepdims=True)
    h2 = xn * lax.rsqrt(ms + EPS) * nw_ref[...]
    h2 = h2 * (1.0 + sc_ref[0]) + sh_ref[0]
    h2_ref[...] = h2.astype(BF16)
    logits = lax.dot_general(wr_ref[...], h2, _NT, preferred_element_type=F32,
                             precision=lax.Precision.HIGHEST)
    mx = jnp.max(logits, axis=0, keepdims=True)
    p = jnp.exp(logits - mx)
    aff_ref[0] = p / jnp.sum(p, axis=0, keepdims=True)


def _out_projection(na, hy, pool, w_out_bf, x2, g1, norm_w, shift, scale, w_router_t, *,
                    rows_per_mod, seq_len, tm):
    m, d = x2.shape
    tiles_per_mod = rows_per_mod // tm
    tiles_per_seq = seq_len // tm
    row = lambda i: (i, 0)
    const = lambda i: (0, 0)
    modi = lambda i: (i // tiles_per_mod, 0, 0)
    return pl.pallas_call(
        _outproj_kernel,
        grid=(m // tm,),
        in_specs=[pl.BlockSpec((tm, NA_W), row), pl.BlockSpec((tm, HY_W), row), pl.BlockSpec((tm, POOL_W), row),
                  pl.BlockSpec((MIX_W, d), const),
                  pl.BlockSpec((tm, d), row),
                  pl.BlockSpec((1, 1, d), modi),
                  pl.BlockSpec((1, d), const),
                  pl.BlockSpec((1, 1, d), modi), pl.BlockSpec((1, 1, d), modi),
                  pl.BlockSpec((N_EXPERTS, d), const)],
        out_specs=[pl.BlockSpec((tm, d), row), pl.BlockSpec((tm, d), row),
                   pl.BlockSpec((1, N_EXPERTS, tm), lambda i: (i // tiles_per_seq, 0, i % tiles_per_seq))],
        out_shape=[jax.ShapeDtypeStruct((m, d), F32), jax.ShapeDtypeStruct((m, d), BF16),
                   jax.ShapeDtypeStruct((m // seq_len, N_EXPERTS, seq_len), F32)],
        compiler_params=_cparams(("arbitrary",)),
        name="out_projection_router",
    )(na, hy, pool, w_out_bf, x2, g1, norm_w.reshape(1, d), shift, scale, w_router_t)


def _ffn_kernel(x_ref, wg_ref, wu_ref, wd_ref, o_ref, *, row_chunk):
    f = pl.program_id(1)
    wg = wg_ref[0].astype(BF16)
    wu = wu_ref[0].astype(BF16)
    wd = wd_ref[0].astype(BF16)
    m = x_ref.shape[1]
    for r in range(m // row_chunk):
        rs = pl.ds(r * row_chunk, row_chunk)
        x = x_ref[0, rs, :]
        a = jnp.dot(x, wg, preferred_element_type=F32)
        b = jnp.dot(x, wu, preferred_element_type=F32)
        h = (a * jax.nn.sigmoid(a) * b).astype(BF16)
        y = jnp.dot(h, wd, preferred_element_type=F32)

        @pl.when(f == 0)
        def _():
            o_ref[0, rs, :] = y

        @pl.when(f != 0)
        def _():
            o_ref[0, rs, :] += y


def _expert_ffn(xg, w_gate, w_up, w_down):
    e, m, d = xg.shape
    tf = 512
    row_chunk = min(m, 512)
    return pl.pallas_call(
        functools.partial(_ffn_kernel, row_chunk=row_chunk),
        grid=(e, D_EXPERT // tf),
        in_specs=[pl.BlockSpec((1, m, d), lambda ei, f: (ei, 0, 0)),
                  pl.BlockSpec((1, d, tf), lambda ei, f: (ei, 0, f)),
                  pl.BlockSpec((1, d, tf), lambda ei, f: (ei, 0, f)),
                  pl.BlockSpec((1, tf, d), lambda ei, f: (ei, f, 0))],
        out_specs=pl.BlockSpec((1, m, d), lambda ei, f: (ei, 0, 0)),
        out_shape=jax.ShapeDtypeStruct((e, m, d), F32),
        compiler_params=_cparams(("arbitrary", "arbitrary")),
        name="expert_ffn",
    )(xg, w_gate, w_up, w_down)


def _prefix_count(m, n):
    lane = lax.broadcasted_iota(jnp.int32, m.shape, 1)
    c = m
    s = 1
    while s < n:
        c = c + jnp.where(lane >= s, pltpu.roll(c, s, 1), 0)
        s *= 2
    return c


def _topk_kernel(aff_ref, idx_ref, gate_ref, *, cap):
    a = aff_ref[...]
    rows, n = a.shape

    def search(i, t):
        cand = t | jnp.left_shift(jnp.int32(1), 30 - i)
        cnt = jnp.sum((a >= pltpu.bitcast(cand, F32)).astype(jnp.int32), axis=1, keepdims=True)
        return jnp.where(cnt >= cap, cand, t)

    thr_bits = lax.fori_loop(0, 31, search, jnp.zeros((rows, 1), jnp.int32))
    thr = pltpu.bitcast(thr_bits, F32)
    gt = a >= pltpu.bitcast(thr_bits + 1, F32)
    eq = (a >= thr) & jnp.logical_not(gt)
    need = cap - jnp.sum(gt.astype(jnp.int32), axis=1, keepdims=True)
    eq_i = eq.astype(jnp.int32)
    sel = gt | (eq & (_prefix_count(eq_i, n) - eq_i < need))
    sel_i = sel.astype(jnp.int32)
    lane = lax.broadcasted_iota(jnp.int32, a.shape, 1)
    dist = jnp.where(sel, lane + 1 - _prefix_count(sel_i, n), 0)
    tok = jnp.where(sel, lane, -1)
    val = a
    s = 1
    while s < n:
        tok_in = pltpu.roll(tok, n - s, 1)
        dist_in = pltpu.roll(dist, n - s, 1)
        val_in = pltpu.roll(val, n - s, 1)
        take = (lane < n - s) & (tok_in >= 0) & ((dist_in & s) != 0)
        stay = (tok >= 0) & ((dist & s) == 0)
        tok = jnp.where(take, tok_in, jnp.where(stay, tok, -1))
        dist = jnp.where(take, dist_in, jnp.where(stay, dist, 0))
        val = jnp.where(take, val_in, val)
        s *= 2
    idx_ref[...] = tok[:, :cap]
    gate_ref[...] = val[:, :cap]


def _topk_select(aff, cap):
    b, e, n = aff.shape
    assert n & (n - 1) == 0
    rows = b * e
    idx, gate = pl.pallas_call(
        functools.partial(_topk_kernel, cap=cap),
        grid=(1,),
        in_specs=[pl.BlockSpec((rows, n), lambda i: (0, 0))],
        out_specs=[pl.BlockSpec((rows, cap), lambda i: (0, 0)), pl.BlockSpec((rows, cap), lambda i: (0, 0))],
        out_shape=[jax.ShapeDtypeStruct((rows, cap), jnp.int32), jax.ShapeDtypeStruct((rows, cap), F32)],
        compiler_params=_cparams(("arbitrary",)),
        name="expert_choice_topk",
    )(aff.reshape(rows, n))
    return idx.reshape(b, e, cap), gate.reshape(b, e, cap)


def _expert_choice_moe(h2, aff, w_gate, w_up, w_down):
    b, n, d = h2.shape
    cap = EC_FACTOR * n // N_EXPERTS
    idx, gate = _topk_select(aff, cap)
    bidx = jnp.arange(b)[:, None, None]
    xg = h2[bidx, idx]
    xg = xg.transpose(1, 0, 2, 3).reshape(N_EXPERTS, b * cap, d)
    y = _expert_ffn(xg, w_gate, w_up, w_down)
    y = y.reshape(N_EXPERTS, b, cap, d).transpose(1, 0, 2, 3) * gate[..., None]
    return jnp.zeros((b, n, d), F32).at[bidx, idx].add(y)


HY_R = 128
HY_COLS = 4096
HY_KB = 8


def _dft_consts(r):
    k = np.arange(r)
    ang = 2.0 * np.pi * np.outer(k, k) / r
    fre, fim = np.cos(ang), -np.sin(ang)
    k2 = np.arange(r)
    tang = 2.0 * np.pi * np.outer(k2, k2) / (r * r)
    tw = np.stack([np.cos(tang), -np.sin(tang)])
    fwd = np.block([[fre, -fim], [fim, fre]])
    inv = np.block([[fre, fim], [-fim, fre]])
    return fre, fim, tw, fwd, inv


def _hy_prep_kernel(x_ref, prev_ref, next_ref, w_ref, b_ref, x1_ref, x2_ref, v_ref, *, n_tiles):
    t = pl.program_id(1)
    x = x_ref[0]
    tm = x.shape[0]
    row = lax.broadcasted_iota(jnp.int32, x.shape, 0)
    prev = jnp.where(t == 0, 0.0, prev_ref[0, pl.ds(7, 1), :])
    nxt = jnp.where(t == n_tiles - 1, 0.0, next_ref[0, pl.ds(0, 1), :])
    up = jnp.where(row == 0, prev, pltpu.roll(x, 1, 0))
    dn = jnp.where(row == tm - 1, nxt, pltpu.roll(x, tm - 1, 0))
    y = b_ref[...] + up * w_ref[pl.ds(0, 1), :] + x * w_ref[pl.ds(1, 1), :] + dn * w_ref[pl.ds(2, 1), :]
    x1_ref[0] = y[:, :HY_W].astype(BF16)
    x2_ref[0] = y[:, HY_W:2 * HY_W].astype(BF16)
    v_ref[0] = y[:, 2 * HY_W:].astype(BF16)


def _hyena_prep(u, conv_w, conv_b):
    b, l, c = u.shape
    tm = min(l, 1024)
    n_tiles = l // tm
    g = tm // 8
    n_groups = l // 8
    out = jax.ShapeDtypeStruct((b, l, HY_W), BF16)
    ospec = pl.BlockSpec((1, tm, HY_W), lambda bi, t: (bi, t, 0))
    return pl.pallas_call(
        functools.partial(_hy_prep_kernel, n_tiles=n_tiles),
        grid=(b, n_tiles),
        in_specs=[pl.BlockSpec((1, tm, c), lambda bi, t: (bi, t, 0)),
                  pl.BlockSpec((1, 8, c), lambda bi, t: (bi, jnp.maximum(t * g - 1, 0), 0)),
                  pl.BlockSpec((1, 8, c), lambda bi, t: (bi, jnp.minimum((t + 1) * g, n_groups - 1), 0)),
                  pl.BlockSpec((HY_SHORT, c), lambda bi, t: (0, 0)),
                  pl.BlockSpec((1, c), lambda bi, t: (0, 0))],
        out_specs=[ospec, ospec, ospec],
        out_shape=[out, out, out],
        compiler_params=_cparams(("arbitrary", "arbitrary")),
        name="hyena_prep",
    )(u, u, u, conv_w, conv_b.reshape(1, c))


def _hy_filter_kernel(z_ref, w1_ref, b1_ref, w2_ref, b2_ref, w3_ref, fr_ref, dl_ref, k_ref, ss_ref, *, seq_len):
    i = pl.program_id(0)
    hp = lax.Precision.HIGHEST
    z = z_ref[...]
    tm = z.shape[0]
    fr = fr_ref[...]
    h = jnp.sin(fr * (jnp.dot(z, w1_ref[...], preferred_element_type=F32, precision=hp) + b1_ref[...]))
    h = jnp.sin(fr * (jnp.dot(h, w2_ref[...], preferred_element_type=F32, precision=hp) + b2_ref[...]))
    h = jnp.dot(h, w3_ref[...], preferred_element_type=F32, precision=hp)
    dec = jnp.exp(-z[:, 0:1] * dl_ref[...])
    r = i * tm + lax.broadcasted_iota(jnp.int32, (tm, HY_W), 0)
    parts = []
    for o in range(HY_ORDER):
        hf = h[:, (2 * o) * HY_W:(2 * o + 1) * HY_W]
        hb = h[:, (2 * o + 1) * HY_W:(2 * o + 2) * HY_W]
        parts.append(jnp.where(r < seq_len, hf, jnp.where(r > seq_len, hb, 0.0)) * dec)
    k = jnp.concatenate(parts, axis=1)
    k_ref[...] = k
    ss = jnp.sum(k * k, axis=0, keepdims=True)

    @pl.when(i == 0)
    def _():
        ss_ref[...] = ss

    @pl.when(i != 0)
    def _():
        ss_ref[...] += ss


def _hyena_filter_taps(l, w1, b1, w2, b2, w3, freq):
    t = jnp.linspace(0.0, 1.0, l, dtype=F32)[:, None]
    w = (2.0 * math.pi / l) * jnp.arange(l, dtype=F32)[:, None]
    f = jnp.linspace(1e-4, HY_BANDS - 1, HY_BANDS, dtype=F32)[None, :]
    z = jnp.concatenate([t, jnp.cos(f * w), -jnp.sin(f * w)], axis=-1)
    z2 = jnp.concatenate([z, z[:1], z[:0:-1]], axis=0)
    emb = z.shape[1]
    z2 = jnp.pad(z2, ((0, 0), (0, LANES - emb)))
    w1p = jnp.pad(w1, ((0, LANES - emb), (0, 0)))
    deltas = jnp.abs(jnp.linspace(math.log(HY_DECAY_TARGET) / HY_SLOW_DECAY,
                                  math.log(HY_DECAY_TARGET) / HY_FAST_DECAY, HY_W, dtype=F32))
    fo = w2.shape[0]
    nout = w3.shape[1]
    tm = min(2 * l, 1024)
    const = lambda i: (0, 0)
    return pl.pallas_call(
        functools.partial(_hy_filter_kernel, seq_len=l),
        grid=(2 * l // tm,),
        in_specs=[pl.BlockSpec((tm, LANES), lambda i: (i, 0)),
                  pl.BlockSpec((LANES, fo), const), pl.BlockSpec((1, fo), const),
                  pl.BlockSpec((fo, fo), const), pl.BlockSpec((1, fo), const),
                  pl.BlockSpec((fo, nout), const), pl.BlockSpec((1, fo), const),
                  pl.BlockSpec((1, HY_W), const)],
        out_specs=[pl.BlockSpec((tm, HY_ORDER * HY_W), lambda i: (i, 0)),
                   pl.BlockSpec((1, HY_ORDER * HY_W), const)],
        out_shape=[jax.ShapeDtypeStruct((2 * l, HY_ORDER * HY_W), F32),
                   jax.ShapeDtypeStruct((1, HY_ORDER * HY_W), F32)],
        compiler_params=_cparams(("arbitrary",)),
        name="hyena_filter_taps",
    )(z2, w1p, b1.reshape(1, fo), w2, b2.reshape(1, fo), w3, freq.reshape(1, fo), deltas.reshape(1, HY_W))


def _left_matmul_kernel(f_ref, x_ref, o_ref):
    o_ref[0] = jnp.dot(f_ref[...], x_ref[0].astype(BF16), preferred_element_type=F32).astype(o_ref.dtype)


def _left_matmul(f_bf, x3, name):
    b, k, c = x3.shape
    m = f_bf.shape[0]
    return pl.pallas_call(
        _left_matmul_kernel,
        grid=(b, c // HY_COLS),
        in_specs=[pl.BlockSpec((m, k), lambda bi, j: (0, 0)),
                  pl.BlockSpec((1, k, HY_COLS), lambda bi, j: (bi, 0, j))],
        out_specs=pl.BlockSpec((1, m, HY_COLS), lambda bi, j: (bi, 0, j)),
        out_shape=jax.ShapeDtypeStruct((b, m, c), BF16),
        compiler_params=_cparams(("arbitrary", "arbitrary")),
        name=name,
    )(f_bf, x3)


def _twiddle_cols(tw_ref, k1):
    lane = lax.broadcasted_iota(jnp.int32, (HY_R, HY_R), 1)
    pick = lane == k1
    twr = jnp.sum(jnp.where(pick, tw_ref[0], 0.0), axis=1, keepdims=True)
    twi = jnp.sum(jnp.where(pick, tw_ref[1], 0.0), axis=1, keepdims=True)
    return twr, twi


def _hy_spectrum_kernel(a_ref, tw_ref, fwd_ref, ss_ref, kf_ref):
    kb = pl.program_id(0)
    scale = lax.rsqrt(ss_ref[...] + EPS) * (1.0 / (HY_R * HY_R))

    def body(kk, carry):
        twr, twi = _twiddle_cols(tw_ref, kb * HY_KB + kk)
        are = a_ref[0, 0, kk].astype(F32)
        aim = a_ref[0, 1, kk].astype(F32)
        bst = jnp.concatenate([are * twr - aim * twi, are * twi + aim * twr], axis=0).astype(BF16)
        x = jnp.dot(fwd_ref[...], bst, preferred_element_type=F32)
        kf_ref[0, kk] = x[:HY_R] * scale
        kf_ref[1, kk] = x[HY_R:] * scale
        return carry

    lax.fori_loop(0, HY_KB, body, 0)


def _hy_spectrum(a5, tw, fwd_bf, ss):
    nch = a5.shape[-1]
    return pl.pallas_call(
        _hy_spectrum_kernel,
        grid=(HY_R // HY_KB,),
        in_specs=[pl.BlockSpec((1, 2, HY_KB, HY_R, nch), lambda kb: (0, 0, kb, 0, 0)),
                  pl.BlockSpec((2, HY_R, HY_R), lambda kb: (0, 0, 0)),
                  pl.BlockSpec((2 * HY_R, 2 * HY_R), lambda kb: (0, 0)),
                  pl.BlockSpec((1, nch), lambda kb: (0, 0))],
        out_specs=pl.BlockSpec((2, HY_KB, HY_R, nch), lambda kb: (0, kb, 0, 0)),
        out_shape=jax.ShapeDtypeStruct((2, HY_R, HY_R, nch), F32),
        compiler_params=_cparams(("arbitrary",)),
        name="hyena_filter_spectrum",
    )(a5, tw, fwd_bf, ss)


def _hy_mid_kernel(a_ref, kf_ref, tw_ref, fwd_ref, inv_ref, z_ref):
    kb = pl.program_id(0)

    def body(kk, carry):
        twr, twi = _twiddle_cols(tw_ref, kb * HY_KB + kk)
        are = a_ref[0, 0, kk].astype(F32)
        aim = a_ref[0, 1, kk].astype(F32)
        bst = jnp.concatenate([are * twr - aim * twi, are * twi + aim * twr], axis=0).astype(BF16)
        x = jnp.dot(fwd_ref[...], bst, preferred_element_type=F32)
        xre, xim = x[:HY_R], x[HY_R:]
        kre, kim = kf_ref[0, kk], kf_ref[1, kk]
        yst = jnp.concatenate([xre * kre - xim * kim, xre * kim + xim * kre], axis=0).astype(BF16)
        zz = jnp.dot(inv_ref[...], yst, preferred_element_type=F32)
        zre, zim = zz[:HY_R], zz[HY_R:]
        z_ref[0, 0, kk] = (zre * twr + zim * twi).astype(BF16)
        z_ref[0, 1, kk] = (zim * twr - zre * twi).astype(BF16)
        return carry

    lax.fori_loop(0, HY_KB, body, 0)


def _hy_mid(a5, kf, order, tw, fwd_bf, inv_bf):
    b = a5.shape[0]
    blk = (1, 2, HY_KB, HY_R, HY_W)
    return pl.pallas_call(
        _hy_mid_kernel,
        grid=(HY_R // HY_KB, b),
        in_specs=[pl.BlockSpec(blk, lambda kb, bi: (bi, 0, kb, 0, 0)),
                  pl.BlockSpec((2, HY_KB, HY_R, HY_W), lambda kb, bi: (0, kb, 0, order)),
                  pl.BlockSpec((2, HY_R, HY_R), lambda kb, bi: (0, 0, 0)),
                  pl.BlockSpec((2 * HY_R, 2 * HY_R), lambda kb, bi: (0, 0)),
                  pl.BlockSpec((2 * HY_R, 2 * HY_R), lambda kb, bi: (0, 0))],
        out_specs=pl.BlockSpec(blk, lambda kb, bi: (bi, 0, kb, 0, 0)),
        out_shape=jax.ShapeDtypeStruct(a5.shape, BF16),
        compiler_params=_cparams(("arbitrary", "arbitrary")),
        name="hyena_spectral_product",
    )(a5, kf, tw, fwd_bf, inv_bf)


def _hy_out_kernel(f_ref, z_ref, g_ref, w_ref, sk_ref, o_ref):
    y = jnp.dot(f_ref[...], z_ref[0], preferred_element_type=F32)
    w = w_ref[0].astype(F32)
    o_ref[0] = (g_ref[0].astype(F32) * (y + w * sk_ref[...])).astype(BF16)


def _hy_out(f_bf, z3, gate3, w3, skip_row):
    b, m2, c = z3.shape
    m = f_bf.shape[0]
    dspec = pl.BlockSpec((1, m, HY_COLS), lambda bi, j: (bi, 0, j))
    return pl.pallas_call(
        _hy_out_kernel,
        grid=(b, c // HY_COLS),
        in_specs=[pl.BlockSpec((m, m2), lambda bi, j: (0, 0)),
                  pl.BlockSpec((1, m2, HY_COLS), lambda bi, j: (bi, 0, j)),
                  dspec, dspec,
                  pl.BlockSpec((1, HY_COLS), lambda bi, j: (0, 0))],
        out_specs=dspec,
        out_shape=jax.ShapeDtypeStruct((b, m, c), BF16),
        compiler_params=_cparams(("arbitrary", "arbitrary")),
        name="hyena_inverse_gate",
    )(f_bf, z3, gate3, w3, skip_row)


def _hyena_long(u, conv_w, conv_b, skip, filt_args):
    b, l, _ = u.shape
    assert 2 * l == HY_R * HY_R
    fre, fim, tw, fwd, inv = _dft_consts(HY_R)
    half = HY_R // 2
    as_bf = lambda m: jnp.asarray(m, F32).astype(BF16)
    f_a_data = as_bf(np.concatenate([fre[:, :half], fim[:, :half]], axis=0))
    f_a_filt = as_bf(np.concatenate([fre, fim], axis=0))
    f_c = as_bf(np.concatenate([fre[:half], fim[:half]], axis=1))
    tw_j = jnp.asarray(tw, F32)
    fwd_bf = as_bf(fwd)
    inv_bf = as_bf(inv)

    taps, ss = _hyena_filter_taps(l, *filt_args)
    nch = HY_ORDER * HY_W
    ak = _left_matmul(f_a_filt, taps.reshape(1, HY_R, HY_R * nch), "hyena_filter_dft_a")
    kf = _hy_spectrum(ak.reshape(1, 2, HY_R, HY_R, nch), tw_j, fwd_bf, ss)

    x1, x2, v = _hyena_prep(u, conv_w, conv_b)
    z = v
    for o, gate in enumerate((x1, x2)):
        z3 = z.reshape(b, half, HY_R * HY_W)
        a = _left_matmul(f_a_data, z3, "hyena_dft_a")
        zs = _hy_mid(a.reshape(b, 2, HY_R, HY_R, HY_W), kf, o, tw_j, fwd_bf, inv_bf)
        skip_row = jnp.tile(skip[o], HY_COLS // HY_W).reshape(1, HY_COLS)
        z = _hy_out(f_c, zs.reshape(b, 2 * HY_R, HY_R * HY_W), gate.reshape(b, half, HY_R * HY_W), z3,
                    skip_row).reshape(b, l, HY_W)
    return z


def _hy_ctx_kernel(x1_ref, x2_ref, v_ref, taps_ref, ss_ref, fa_ref, fc_ref, sk_ref, o_ref, *, seq_len):
    n2 = 2 * seq_len
    fa = fa_ref[...]
    scale = lax.rsqrt(ss_ref[...] + EPS) * (1.0 / n2)
    kf = jnp.dot(fa, taps_ref[...].astype(BF16), preferred_element_type=F32) * scale
    z = v_ref[0].astype(F32)
    for o, g_ref in enumerate((x1_ref, x2_ref)):
        x = jnp.dot(fa[:, :seq_len], z.astype(BF16), preferred_element_type=F32)
        xre, xim = x[:n2], x[n2:]
        kre = kf[:n2, o * HY_W:(o + 1) * HY_W]
        kim = kf[n2:, o * HY_W:(o + 1) * HY_W]
        yst = jnp.concatenate([xre * kre - xim * kim, xre * kim + xim * kre], axis=0).astype(BF16)
        y = jnp.dot(fc_ref[...], yst, preferred_element_type=F32)
        z = g_ref[0].astype(F32) * (y + z * sk_ref[pl.ds(o, 1), :])
    o_ref[0] = z.astype(BF16)


def _hyena_short(u, conv_w, conv_b, skip, filt_args):
    b, l, _ = u.shape
    n2 = 2 * l
    k = np.arange(n2)
    ang = 2.0 * np.pi * np.outer(k, k) / n2
    fre, fim = np.cos(ang), -np.sin(ang)
    fa = jnp.asarray(np.concatenate([fre, fim], axis=0), F32).astype(BF16)
    fc = jnp.asarray(np.concatenate([fre[:l], fim[:l]], axis=1), F32).astype(BF16)
    taps, ss = _hyena_filter_taps(l, *filt_args)
    x1, x2, v = _hyena_prep(u, conv_w, conv_b)
    nch = HY_ORDER * HY_W
    dspec = pl.BlockSpec((1, l, HY_W), lambda bi: (bi, 0, 0))
    const = lambda bi: (0, 0)
    return pl.pallas_call(
        functools.partial(_hy_ctx_kernel, seq_len=l),
        grid=(b,),
        in_specs=[dspec, dspec, dspec,
                  pl.BlockSpec((n2, nch), const), pl.BlockSpec((1, nch), const),
                  pl.BlockSpec((2 * n2, n2), const), pl.BlockSpec((l, 2 * n2), const),
                  pl.BlockSpec((HY_ORDER, HY_W), const)],
        out_specs=dspec,
        out_shape=jax.ShapeDtypeStruct((b, l, HY_W), BF16),
        compiler_params=_cparams(("arbitrary",)),
        name="hyena_context",
    )(x1, x2, v, taps, ss, fa, fc, skip)


def _rope_tables(n):
    t = jnp.arange(n, dtype=jnp.int32)
    pos = jnp.stack([t // GRID_W, t % GRID_W], axis=-1).astype(F32)
    nf = HEAD_DIM // 4
    inv = ROPE_THETA ** (-jnp.arange(nf, dtype=F32) / nf)
    ang = pos[:, :, None] * inv
    cos, sin = jnp.cos(ang), jnp.sin(ang)
    c64 = jnp.concatenate([cos[:, 0], cos[:, 0], cos[:, 1], cos[:, 1]], axis=-1)
    s64 = jnp.concatenate([-sin[:, 0], sin[:, 0], -sin[:, 1], sin[:, 1]], axis=-1)
    return jnp.tile(c64, (1, 2)), jnp.tile(s64, (1, 2))


def _block_diag(pool_w):
    g, c, _ = pool_w.shape
    out = jnp.zeros((g * c, g * c), pool_w.dtype)
    for i in range(g):
        out = out.at[i * c:(i + 1) * c, i * c:(i + 1) * c].set(pool_w[i])
    return out


def kernel(x, c, ctx, c_ctx, w_mod, b_mod, norm1_w, norm2_w, w_in, w_out, q_norm_w, k_norm_w, na_rpb,
           hy_conv_w, hy_conv_b, hy_w1, hy_b1, hy_w2, hy_b2, hy_w3, hy_freq, hy_skip,
           pool_w, pool_scale, w_router, w_gate, w_up, w_down):
    b, n, d = x.shape
    lc = ctx.shape[1]
    depth = w_mod.shape[0]
    rows = n // GRID_W
    cos128, sin128 = _rope_tables(n)
    seg = jnp.asarray(np.kron(np.eye(2), np.full((HEAD_DIM, HEAD_DIM), 1.0 / HEAD_DIM)), BF16)
    cc = jnp.zeros((8, d), F32).at[:b].set(c).at[b].set(c_ctx)

    xl = x.reshape(b * n, d)
    xc = ctx.reshape(b * lc, d)
    for i in range(depth):
        last = i == depth - 1
        mod = _modulation(cc, w_mod[i], b_mod[i])
        ml = mod[:b].reshape(b, 1, 6, d)
        mc = mod[b].reshape(1, 1, 6, d)
        sh1, sc1, g1, sh2, sc2, g2 = [ml[:, :, j] for j in range(6)]
        csh1, csc1, cg1, csh2, csc2, cg2 = [mc[:, :, j] for j in range(6)]
        w_in_bf = w_in[i].astype(BF16)
        w_out_bf = w_out[i].astype(BF16)
        w_router_t = w_router[i].T
        pool_bd = _block_diag(pool_w[i]).astype(BF16)
        hy_args = (hy_w1[i], hy_b1[i], hy_w2[i], hy_b2[i], hy_w3[i], hy_freq[i])

        qc, kc, vc, hyc, poolc = _in_projection(
            xc, csh1, csc1, norm1_w[i], w_in_bf, q_norm_w[i], k_norm_w[i], cos128, sin128, seg,
            rows_per_mod=b * lc, seq_len=lc, tm=lc, rope=False)
        kc = kc.reshape(b, lc, NA_W)
        vc = vc.reshape(b, lc, NA_W)
        if not last:
            na_c = _dense_attention(qc.reshape(b, lc, NA_W), kc, vc).reshape(b * lc, NA_W)
            hy_c = _hyena_short(hyc.reshape(b, lc, 3 * HY_W), hy_conv_w[i], hy_conv_b[i], hy_skip[i], hy_args)
            pool_c = _pool_mixer(poolc.reshape(b, lc, POOL_W), pool_bd, pool_scale[i])
            xc1, h2c, affc = _out_projection(
                na_c, hy_c.reshape(b * lc, HY_W).astype(BF16), pool_c.reshape(b * lc, POOL_W), w_out_bf,
                xc, cg1, norm2_w[i], csh2, csc2, w_router_t, rows_per_mod=b * lc, seq_len=lc, tm=lc)
            moe_c = _expert_choice_moe(h2c.reshape(b, lc, d), affc, w_gate[i], w_up[i], w_down[i])
            xc_new = xc1 + (cg2 * moe_c).reshape(b * lc, d)

        qr, qp, kl, vl, hyl, pooll = _in_projection(
            xl, sh1, sc1, norm1_w[i], w_in_bf, q_norm_w[i], k_norm_w[i], cos128, sin128, seg,
            rows_per_mod=n, seq_len=n, tm=512, rope=True)
        bias = _na_bias_tables(na_rpb[i], rows)
        o_na = _neighborhood_attention(qr.reshape(b, n, NA_W), qp.reshape(b, n, NA_W), kl.reshape(b, n, NA_W),
                                       vl.reshape(b, n, NA_W), kc, vc, bias)
        hy_l = _hyena_long(hyl.reshape(b, n, 3 * HY_W), hy_conv_w[i], hy_conv_b[i], hy_skip[i], hy_args)
        pool_l = _pool_mixer(pooll.reshape(b, n, POOL_W), pool_bd, pool_scale[i])
        xl1, h2l, affl = _out_projection(
            o_na.reshape(b * n, NA_W), hy_l.reshape(b * n, HY_W).astype(BF16), pool_l.reshape(b * n, POOL_W),
            w_out_bf, xl, g1, norm2_w[i], sh2, sc2, w_router_t, rows_per_mod=n, seq_len=n, tm=512)
        moe_l = _expert_choice_moe(h2l.reshape(b, n, d), affl, w_gate[i], w_up[i], w_down[i])
        xl = xl1 + (g2 * moe_l).reshape(b * n, d)
        if not last:
            xc = xc_new
    return xl.reshape(b, n, d)
```

```python
import functools
import math

import jax
import jax.numpy as jnp
import numpy as np
from jax import lax
from jax.experimental import pallas as pl
from jax.experimental.pallas import tpu as pltpu

F32 = jnp.float32
BF16 = jnp.bfloat16

D_MODEL = 1024
GRID_W = 64
NA_HEADS = 8
HEAD_DIM = 64
NA_W = NA_HEADS * HEAD_DIM
NA_WIN_R = 8
NA_WIN_C = 16
ROPE_THETA = 10000.0
HY_W = 256
HY_ORDER = 2
HY_SHORT = 3
HY_BANDS = 16
HY_DECAY_TARGET = 1e-2
HY_FAST_DECAY = 0.3
HY_SLOW_DECAY = 1.5
POOL_W = 256
POOL_SIZES = (2, 4, 8, 16)
POOL_GROUP = POOL_W // len(POOL_SIZES)
MIX_W = NA_W + HY_W + POOL_W
IN_W = 3 * NA_W + (HY_ORDER + 1) * HY_W + POOL_W
N_EXPERTS = 16
EC_FACTOR = 2
D_EXPERT = 2048
EPS = 1e-6

LANES = 128
NEG_BIG = -1e30
VMEM_LIMIT = 56 * 1024 * 1024

NA_Q_ROWS = 4
NA_K_ROWS = NA_Q_ROWS + NA_WIN_R - 1


def _cparams(sem):
    return pltpu.CompilerParams(dimension_semantics=sem, vmem_limit_bytes=VMEM_LIMIT)


def _mod_kernel(c_ref, w_ref, b_ref, o_ref):
    c = c_ref[...]
    s = c * jax.nn.sigmoid(c)
    o_ref[...] = jnp.dot(s, w_ref[...], preferred_element_type=F32,
                         precision=lax.Precision.HIGHEST) + b_ref[...]


def _modulation(cc, w_mod, b_mod):
    rows, d = cc.shape
    n = w_mod.shape[1]
    tn = 1536
    return pl.pallas_call(
        _mod_kernel,
        grid=(n // tn,),
        in_specs=[pl.BlockSpec((rows, d), lambda j: (0, 0)),
                  pl.BlockSpec((d, tn), lambda j: (0, j)),
                  pl.BlockSpec((1, tn), lambda j: (0, j))],
        out_specs=pl.BlockSpec((rows, tn), lambda j: (0, j)),
        out_shape=jax.ShapeDtypeStruct((rows, n), F32),
        compiler_params=_cparams(("arbitrary",)),
        name="modulation",
    )(cc, w_mod, b_mod.reshape(1, n))


def _head_norm(xs, w128, seg):
    sq = xs * xs
    hi = sq.astype(BF16)
    lo = (sq - hi.astype(F32)).astype(BF16)
    m = (jnp.dot(hi, seg, preferred_element_type=F32) + jnp.dot(lo, seg, preferred_element_type=F32))
    return xs * lax.rsqrt(m + EPS) * w128


def _rope(xs, c, s_signed):
    lane = lax.broadcasted_iota(jnp.int32, xs.shape, 1)
    partner = jnp.where((lane & 16) == 0, pltpu.roll(xs, LANES - 16, 1), pltpu.roll(xs, 16, 1))
    return xs * c + partner * s_signed


def _inproj_kernel(x_ref, sh_ref, sc_ref, nw_ref, w_ref, qw_ref, kw_ref, cos_ref, sin_ref, seg_ref,
                   *out_refs, rope):
    if rope:
        qr_ref, qp_ref, k_ref, v_ref, hy_ref, pool_ref = out_refs
    else:
        qp_ref, k_ref, v_ref, hy_ref, pool_ref = out_refs
    x = x_ref[...]
    ms = jnp.mean(x * x, axis=-1, keepdims=True)
    h = x * lax.rsqrt(ms + EPS) * nw_ref[...]
    h = h * (1.0 + sc_ref[0]) + sh_ref[0]
    u = jnp.dot(h.astype(BF16), w_ref[...], preferred_element_type=F32)
    seg = seg_ref[...]
    qk_scale = HEAD_DIM ** -0.5
    for ch in range(NA_W // LANES):
        sl = slice(ch * LANES, (ch + 1) * LANES)
        qn = _head_norm(u[:, sl], qw_ref[...], seg)
        kn = _head_norm(u[:, NA_W + ch * LANES:NA_W + (ch + 1) * LANES], kw_ref[...], seg)
        qp_ref[:, sl] = (qn * qk_scale).astype(BF16)
        if rope:
            c = cos_ref[...]
            s = sin_ref[...]
            qr_ref[:, sl] = (_rope(qn, c, s) * qk_scale).astype(BF16)
            k_ref[:, sl] = _rope(kn, c, s).astype(BF16)
        else:
            k_ref[:, sl] = kn.astype(BF16)
    v_ref[...] = u[:, 2 * NA_W:3 * NA_W].astype(BF16)
    hy_ref[...] = u[:, 3 * NA_W:3 * NA_W + 3 * HY_W]
    pool_ref[...] = u[:, 3 * NA_W + 3 * HY_W:]


def _in_projection(x2, shift, scale, norm_w, w_in_bf, q_norm_w, k_norm_w, cos128, sin128, seg, *,
                   rows_per_mod, seq_len, tm, rope):
    m, d = x2.shape
    tiles_per_mod = rows_per_mod // tm
    tiles_per_seq = seq_len // tm
    row = lambda i: (i, 0)
    const = lambda i: (0, 0)
    modi = lambda i: (i // tiles_per_mod, 0, 0)
    pos = lambda i: (i % tiles_per_seq, 0)
    outs = []
    if rope:
        outs.append((NA_W, BF16))
    outs += [(NA_W, BF16), (NA_W, BF16), (NA_W, BF16), (3 * HY_W, F32), (POOL_W, F32)]
    return pl.pallas_call(
        functools.partial(_inproj_kernel, rope=rope),
        grid=(m // tm,),
        in_specs=[pl.BlockSpec((tm, d), row),
                  pl.BlockSpec((1, 1, d), modi), pl.BlockSpec((1, 1, d), modi),
                  pl.BlockSpec((1, d), const),
                  pl.BlockSpec((d, IN_W), const),
                  pl.BlockSpec((1, LANES), const), pl.BlockSpec((1, LANES), const),
                  pl.BlockSpec((tm, LANES), pos), pl.BlockSpec((tm, LANES), pos),
                  pl.BlockSpec((LANES, LANES), const)],
        out_specs=[pl.BlockSpec((tm, w), row) for w, _ in outs],
        out_shape=[jax.ShapeDtypeStruct((m, w), dt) for w, dt in outs],
        compiler_params=_cparams(("arbitrary",)),
        name="in_projection_rope" if rope else "in_projection_ctx",
    )(x2, shift, scale, norm_w.reshape(1, d), w_in_bf,
      jnp.tile(q_norm_w, 2).reshape(1, LANES), jnp.tile(k_norm_w, 2).reshape(1, LANES),
      cos128, sin128, seg)


def _softmax_pv(s_list, v_list):
    mx = functools.reduce(jnp.maximum, [jnp.max(s, axis=-1, keepdims=True) for s in s_list])
    ps = [jnp.exp(s - mx) for s in s_list]
    l = functools.reduce(jnp.add, [jnp.sum(p, axis=-1, keepdims=True) for p in ps])
    o = functools.reduce(jnp.add, [jnp.dot(p.astype(BF16), v, preferred_element_type=F32)
                                   for p, v in zip(ps, v_list)])
    return o / l


_NT = (((1,), (1,)), ((), ()))


def _na_kernel(qr_ref, qp_ref, k_ref, v_ref, kc_ref, vc_ref, bias_ref, o_ref, *, rows):
    t = pl.program_id(2)
    ks = jnp.clip(t * NA_Q_ROWS - NA_WIN_R // 2, 0, rows - NA_K_ROWS)
    start = pl.multiple_of(ks * GRID_W, GRID_W)
    k = k_ref[0, pl.ds(start, NA_K_ROWS * GRID_W), :]
    v = v_ref[0, pl.ds(start, NA_K_ROWS * GRID_W), :]
    q = qr_ref[0]
    qp = qp_ref[0]
    kc = kc_ref[0]
    vc = vc_ref[0]
    lane = lax.broadcasted_iota(jnp.int32, q.shape, 1)
    outs = []
    for hh in range(2):
        keep = (lane < HEAD_DIM) if hh == 0 else (lane >= HEAD_DIM)
        zero = jnp.zeros_like(q)
        s_lat = lax.dot_general(jnp.where(keep, q, zero), k, _NT, preferred_element_type=F32) + bias_ref[0, hh]
        s_ctx = lax.dot_general(jnp.where(keep, qp, zero), kc, _NT, preferred_element_type=F32)
        outs.append(_softmax_pv([s_lat, s_ctx], [v, vc]))
    o_ref[0] = jnp.where(lane < HEAD_DIM, outs[0], outs[1]).astype(BF16)


def _neighborhood_attention(qr, qp, k, v, kc, vc, bias):
    b, n, _ = qr.shape
    lc = kc.shape[1]
    rows = n // GRID_W
    n_tiles = rows // NA_Q_ROWS
    tq = NA_Q_ROWS * GRID_W
    tk = NA_K_ROWS * GRID_W
    qmap = lambda bi, hp, t: (bi, t, hp)
    smap = lambda bi, hp, t: (bi, 0, hp)
    bmap = lambda bi, hp, t: (jnp.where(t == 0, 0, jnp.where(t == n_tiles - 1, 2, 1)), hp, 0, 0)
    return pl.pallas_call(
        functools.partial(_na_kernel, rows=rows),
        grid=(b, NA_W // LANES, n_tiles),
        in_specs=[pl.BlockSpec((1, tq, LANES), qmap), pl.BlockSpec((1, tq, LANES), qmap),
                  pl.BlockSpec((1, n, LANES), smap), pl.BlockSpec((1, n, LANES), smap),
                  pl.BlockSpec((1, lc, LANES), smap), pl.BlockSpec((1, lc, LANES), smap),
                  pl.BlockSpec((1, 2, tq, tk), bmap)],
        out_specs=pl.BlockSpec((1, tq, LANES), qmap),
        out_shape=jax.ShapeDtypeStruct((b, n, NA_W), BF16),
        compiler_params=_cparams(("arbitrary", "arbitrary", "arbitrary")),
        name="neighborhood_attention",
    )(qr, qp, k, v, kc, vc, bias)


def _na_bias_tables(rpb, rows):
    n_tiles = rows // NA_Q_ROWS
    n_dr, n_dc = 2 * NA_WIN_R - 1, 2 * NA_WIN_C - 1
    qc = np.arange(GRID_W)
    cs = np.clip(qc - NA_WIN_C // 2, 0, GRID_W - NA_WIN_C)
    kcol = np.arange(GRID_W)
    ok_c = (kcol[None, :] >= cs[:, None]) & (kcol[None, :] < cs[:, None] + NA_WIN_C)
    dc = kcol[None, :] - qc[:, None] + NA_WIN_C - 1
    col_sel = (dc[:, :, None] == np.arange(n_dc)) & ok_c[:, :, None]
    row_sel, masks = [], []
    for t in (0, 1, n_tiles - 1):
        r0 = t * NA_Q_ROWS
        ks = int(np.clip(r0 - NA_WIN_R // 2, 0, rows - NA_K_ROWS))
        r = r0 + np.arange(NA_Q_ROWS)
        rs = np.clip(r - NA_WIN_R // 2, 0, rows - NA_WIN_R)
        krow = ks + np.arange(NA_K_ROWS)
        ok_r = (krow[None, :] >= rs[:, None]) & (krow[None, :] < rs[:, None] + NA_WIN_R)
        dr = krow[None, :] - r[:, None] + NA_WIN_R - 1
        row_sel.append((dr[:, :, None] == np.arange(n_dr)) & ok_r[:, :, None])
        ok = ok_r[:, None, :, None] & ok_c[None, :, None, :]
        masks.append(np.where(ok, 0.0, NEG_BIG))
    tab = jnp.einsum("hrd,tijr,qkd->thiqjk", rpb.astype(F32), jnp.asarray(np.stack(row_sel), F32),
                     jnp.asarray(col_sel, F32), precision=lax.Precision.HIGHEST)
    tab = tab + jnp.asarray(np.stack(masks), F32)[:, None]
    return tab.reshape(3, NA_HEADS, NA_Q_ROWS * GRID_W, NA_K_ROWS * GRID_W)


def _dense_attn_kernel(q_ref, k_ref, v_ref, o_ref):
    q = q_ref[0]
    k = k_ref[0]
    v = v_ref[0]
    lane = lax.broadcasted_iota(jnp.int32, q.shape, 1)
    outs = []
    for hh in range(2):
        keep = (lane < HEAD_DIM) if hh == 0 else (lane >= HEAD_DIM)
        s = lax.dot_general(jnp.where(keep, q, jnp.zeros_like(q)), k, _NT, preferred_element_type=F32)
        outs.append(_softmax_pv([s], [v]))
    o_ref[0] = jnp.where(lane < HEAD_DIM, outs[0], outs[1]).astype(BF16)


def _dense_attention(q, k, v):
    b, l, _ = q.shape
    spec = pl.BlockSpec((1, l, LANES), lambda bi, hp: (bi, 0, hp))
    return pl.pallas_call(
        _dense_attn_kernel,
        grid=(b, NA_W // LANES),
        in_specs=[spec, spec, spec],
        out_specs=spec,
        out_shape=jax.ShapeDtypeStruct((b, l, NA_W), BF16),
        compiler_params=_cparams(("arbitrary", "arbitrary")),
        name="context_attention",
    )(q, k, v)


POOL_HALO = max(POOL_SIZES) // 2


def _pool_kernel(x_ref, w_ref, sc_ref, o_ref, xpad, *, seq_len, chunk):
    zeros = jnp.zeros((POOL_HALO, POOL_W), F32)
    xpad[pl.ds(0, POOL_HALO), :] = zeros
    xpad[pl.ds(seq_len + POOL_HALO, POOL_HALO), :] = zeros
    xpad[pl.ds(POOL_HALO, seq_len), :] = x_ref[0]
    span = chunk + 2 * POOL_HALO
    w = w_ref[...]
    sc = sc_ref[...]

    def body(c, carry):
        base = pl.multiple_of(c * chunk, chunk)
        xs = xpad[pl.ds(base, span), :]
        s2 = xs + pltpu.roll(xs, 1, 0)
        s4 = pltpu.roll(s2, 1, 0) + pltpu.roll(s2, span - 1, 0)
        s8 = pltpu.roll(s4, 2, 0) + pltpu.roll(s4, span - 2, 0)
        s16 = pltpu.roll(s8, 4, 0) + pltpu.roll(s8, span - 4, 0)
        mid = slice(POOL_HALO, POOL_HALO + chunk)
        grp = lax.broadcasted_iota(jnp.int32, (chunk, POOL_W), 1) // POOL_GROUP
        tpos = base + lax.broadcasted_iota(jnp.int32, (chunk, POOL_W), 0)
        half = jnp.left_shift(1, grp)
        cnt = (jnp.minimum(tpos + half, seq_len) - jnp.maximum(tpos - half, 0)).astype(F32)
        ssum = jnp.where(grp == 0, s2[mid], jnp.where(grp == 1, s4[mid], jnp.where(grp == 2, s8[mid], s16[mid])))
        diff = ssum / cnt - xs[mid]
        y = jnp.dot(diff.astype(BF16), w, preferred_element_type=F32) * sc
        o_ref[0, pl.ds(base, chunk), :] = y.astype(BF16)
        return carry

    lax.fori_loop(0, seq_len // chunk, body, 0)


def _pool_mixer(u, w_blockdiag_bf, pool_scale):
    b, l, _ = u.shape
    chunk = min(l, 512)
    return pl.pallas_call(
        functools.partial(_pool_kernel, seq_len=l, chunk=chunk),
        grid=(b,),
        in_specs=[pl.BlockSpec((1, l, POOL_W), lambda bi: (bi, 0, 0)),
                  pl.BlockSpec((POOL_W, POOL_W), lambda bi: (0, 0)),
                  pl.BlockSpec((1, POOL_W), lambda bi: (0, 0))],
        out_specs=pl.BlockSpec((1, l, POOL_W), lambda bi: (bi, 0, 0)),
        out_shape=jax.ShapeDtypeStruct((b, l, POOL_W), BF16),
        scratch_shapes=[pltpu.VMEM((l + 2 * POOL_HALO, POOL_W), F32)],
        compiler_params=_cparams(("arbitrary",)),
        name="pool_mixer",
    )(u, w_blockdiag_bf, pool_scale.reshape(1, POOL_W))


def _outproj_kernel(na_ref, hy_ref, pl_ref, w_ref, x_ref, g1_ref, nw_ref, sh_ref, sc_ref, wr_ref,
                    xn_ref, h2_ref, aff_ref):
    w = w_ref
    acc = jnp.dot(na_ref[...], w[pl.ds(0, NA_W), :], preferred_element_type=F32)
    acc += jnp.dot(hy_ref[...], w[pl.ds(NA_W, HY_W), :], preferred_element_type=F32)
    acc += jnp.dot(pl_ref[...], w[pl.ds(NA_W + HY_W, POOL_W), :], preferred_element_type=F32)
    xn = x_ref[...] + g1_ref[0] * acc
    xn_ref[...] = xn
    ms = jnp.mean(xn * xn, axis=-1, keepdims=True)
    h2 = xn * lax.rsqrt(ms + EPS) * nw_ref[...]
    h2 = h2 * (1.0 + sc_ref[0]) + sh_ref[0]
    h2_ref[...] = h2
    logits = lax.dot_general(wr_ref[...], h2, _NT, preferred_element_type=F32,
                             precision=lax.Precision.HIGHEST)
    mx = jnp.max(logits, axis=0, keepdims=True)
    p = jnp.exp(logits - mx)
    aff_ref[0] = p / jnp.sum(p, axis=0, keepdims=True)


def _out_projection(na, hy, pool, w_out_bf, x2, g1, norm_w, shift, scale, w_router_t, *,
                    rows_per_mod, seq_len, tm):
    m, d = x2.shape
    tiles_per_mod = rows_per_mod // tm
    tiles_per_seq = seq_len // tm
    row = lambda i: (i, 0)
    const = lambda i: (0, 0)
    modi = lambda i: (i // tiles_per_mod, 0, 0)
    return pl.pallas_call(
        _outproj_kernel,
        grid=(m // tm,),
        in_specs=[pl.BlockSpec((tm, NA_W), row), pl.BlockSpec((tm, HY_W), row), pl.BlockSpec((tm, POOL_W), row),
                  pl.BlockSpec((MIX_W, d), const),
                  pl.BlockSpec((tm, d), row),
                  pl.BlockSpec((1, 1, d), modi),
                  pl.BlockSpec((1, d), const),
                  pl.BlockSpec((1, 1, d), modi), pl.BlockSpec((1, 1, d), modi),
                  pl.BlockSpec((N_EXPERTS, d), const)],
        out_specs=[pl.BlockSpec((tm, d), row), pl.BlockSpec((tm, d), row),
                   pl.BlockSpec((1, N_EXPERTS, tm), lambda i: (i // tiles_per_seq, 0, i % tiles_per_seq))],
        out_shape=[jax.ShapeDtypeStruct((m, d), F32), jax.ShapeDtypeStruct((m, d), F32),
                   jax.ShapeDtypeStruct((m // seq_len, N_EXPERTS, seq_len), F32)],
        compiler_params=_cparams(("arbitrary",)),
        name="out_projection_router",
    )(na, hy, pool, w_out_bf, x2, g1, norm_w.reshape(1, d), shift, scale, w_router_t)


FFN_TF = 256
FFN_ROW_CHUNKS = 4


def _ffn_kernel(*refs, n_experts, sources, n_rows):
    ns = len(sources)
    idx_cur = refs[0:ns]
    idx_nxt = refs[ns:2 * ns]
    src = refs[2 * ns:3 * ns]
    gate_ref, wg_ref, wu_ref, wd_ref, o_ref, stage, xb, acc, sem = refs[3 * ns:]
    e = pl.program_id(0)
    f = pl.program_id(1)
    n_f = pl.num_programs(1)

    def row_copy(s, row, r):
        return pltpu.make_async_copy(src[s].at[pl.ds(row, 1)], stage.at[pl.ds(sources[s][0] + r, 1)], sem.at[0])

    def gather(idx_refs):
        for s in range(ns):
            def body(r, carry, s=s):
                row_copy(s, idx_refs[s][0, 0, r], r).start()
                return carry
            lax.fori_loop(0, sources[s][1], body, 0, unroll=8)

    def gather_wait():
        for s in range(ns):
            def body(r, carry, s=s):
                row_copy(s, 0, r).wait()
                return carry
            lax.fori_loop(0, sources[s][1], body, 0, unroll=8)

    @pl.when(f == 0)
    def _():
        @pl.when(e == 0)
        def _():
            gather(idx_cur)

        gather_wait()
        xb[...] = stage[...].astype(BF16)

        @pl.when(e + 1 < n_experts)
        def _():
            gather(idx_nxt)

    wg = wg_ref[0].astype(BF16)
    wu = wu_ref[0].astype(BF16)
    wd = wd_ref[0].astype(BF16)
    rc = n_rows // FFN_ROW_CHUNKS
    for r in range(FFN_ROW_CHUNKS):
        rs = pl.ds(r * rc, rc)
        x = xb[rs, :]
        a = jnp.dot(x, wg, preferred_element_type=F32)
        b = jnp.dot(x, wu, preferred_element_type=F32)
        h = (a * jax.nn.sigmoid(a) * b).astype(BF16)
        y = jnp.dot(h, wd, preferred_element_type=F32)

        @pl.when(f == 0)
        def _():
            acc[rs, :] = y

        @pl.when(f != 0)
        def _():
            acc[rs, :] += y

    @pl.when(f == n_f - 1)
    def _():
        o_ref[0] = (acc[...] * gate_ref[0]).astype(BF16)


def _expert_ffn(rows_idx, tokens, gate, w_gate, w_up, w_down):
    ns = len(tokens)
    e, d = w_gate.shape[0], w_gate.shape[1]
    counts = [int(ri.shape[2]) for ri in rows_idx]
    offs = [int(sum(counts[:s])) for s in range(ns)]
    r_total = sum(counts)
    assert r_total % (FFN_ROW_CHUNKS * 16) == 0
    smem = lambda cnt, nxt: pl.BlockSpec(
        (1, 1, cnt), (lambda ei, f: (jnp.minimum(ei + 1, e - 1), 0, 0)) if nxt else (lambda ei, f: (ei, 0, 0)),
        memory_space=pltpu.SMEM)
    in_specs = ([smem(cnt, False) for cnt in counts] + [smem(cnt, True) for cnt in counts]
                + [pl.BlockSpec(memory_space=pl.ANY) for _ in range(ns)]
                + [pl.BlockSpec((1, r_total, 1), lambda ei, f: (ei, 0, 0)),
                   pl.BlockSpec((1, d, FFN_TF), lambda ei, f: (ei, 0, f)),
                   pl.BlockSpec((1, d, FFN_TF), lambda ei, f: (ei, 0, f)),
                   pl.BlockSpec((1, FFN_TF, d), lambda ei, f: (ei, f, 0))])
    return pl.pallas_call(
        functools.partial(_ffn_kernel, n_experts=e, sources=tuple(zip(offs, counts)), n_rows=r_total),
        grid=(e, D_EXPERT // FFN_TF),
        in_specs=in_specs,
        out_specs=pl.BlockSpec((1, r_total, d), lambda ei, f: (ei, 0, 0)),
        out_shape=jax.ShapeDtypeStruct((e, r_total, d), BF16),
        scratch_shapes=[pltpu.VMEM((r_total, d), F32), pltpu.VMEM((r_total, d), BF16),
                        pltpu.VMEM((r_total, d), F32), pltpu.SemaphoreType.DMA((1,))],
        compiler_params=_cparams(("arbitrary", "arbitrary")),
        name="expert_ffn",
    )(*rows_idx, *rows_idx, *tokens, gate, w_gate, w_up, w_down)


def _prefix_count(m, n):
    lane = lax.broadcasted_iota(jnp.int32, m.shape, 1)
    c = m
    s = 1
    while s < n:
        c = c + jnp.where(lane >= s, pltpu.roll(c, s, 1), 0)
        s *= 2
    return c


COMBINE_TB = 256


def _topk_kernel(aff_ref, idx_ref, gate_ref, off_ref, *, cap):
    a = aff_ref[...]
    rows, n = a.shape

    def search(i, t):
        cand = t | jnp.left_shift(jnp.int32(1), 30 - i)
        cnt = jnp.sum((a >= pltpu.bitcast(cand, F32)).astype(jnp.int32), axis=1, keepdims=True)
        return jnp.where(cnt >= cap, cand, t)

    thr_bits = lax.fori_loop(0, 31, search, jnp.zeros((rows, 1), jnp.int32))
    thr = pltpu.bitcast(thr_bits, F32)
    gt = a >= pltpu.bitcast(thr_bits + 1, F32)
    eq = (a >= thr) & jnp.logical_not(gt)
    need = cap - jnp.sum(gt.astype(jnp.int32), axis=1, keepdims=True)
    eq_i = eq.astype(jnp.int32)
    sel = gt | (eq & (_prefix_count(eq_i, n) - eq_i < need))
    sel_i = sel.astype(jnp.int32)
    lane = lax.broadcasted_iota(jnp.int32, a.shape, 1)
    dist = jnp.where(sel, lane + 1 - _prefix_count(sel_i, n), 0)
    tok = jnp.where(sel, lane, -1)
    val = a
    s = 1
    while s < n:
        tok_in = pltpu.roll(tok, n - s, 1)
        dist_in = pltpu.roll(dist, n - s, 1)
        val_in = pltpu.roll(val, n - s, 1)
        take = (lane < n - s) & (tok_in >= 0) & ((dist_in & s) != 0)
        stay = (tok >= 0) & ((dist & s) == 0)
        tok = jnp.where(take, tok_in, jnp.where(stay, tok, -1))
        dist = jnp.where(take, dist_in, jnp.where(stay, dist, 0))
        val = jnp.where(take, val_in, val)
        s *= 2
    idx_ref[...] = tok[:, :cap]
    gate_ref[...] = val[:, :cap]
    olane = lax.broadcasted_iota(jnp.int32, off_ref.shape, 1)
    off = jnp.zeros(off_ref.shape, jnp.int32)
    for j in range(n // COMBINE_TB + 1):
        below = jnp.sum(jnp.where(lane < j * COMBINE_TB, sel_i, 0), axis=1, keepdims=True)
        off = jnp.where(olane == j, below, off)
    off_ref[...] = off


def _topk_select(aff, cap):
    b, e, n = aff.shape
    assert n & (n - 1) == 0 and n // COMBINE_TB < LANES
    rows = b * e
    full = lambda w: pl.BlockSpec((rows, w), lambda i: (0, 0))
    idx, gate, off = pl.pallas_call(
        functools.partial(_topk_kernel, cap=cap),
        grid=(1,),
        in_specs=[full(n)],
        out_specs=[full(cap), full(cap), full(LANES)],
        out_shape=[jax.ShapeDtypeStruct((rows, cap), jnp.int32), jax.ShapeDtypeStruct((rows, cap), F32),
                   jax.ShapeDtypeStruct((rows, LANES), jnp.int32)],
        compiler_params=_cparams(("arbitrary",)),
        name="expert_choice_topk",
    )(aff.reshape(rows, n))
    return idx.reshape(b, e, cap), gate.reshape(b, e, cap), off


def _combine_kernel(off_ref, y_ref, idx_ref, x_ref, g_ref, o_ref, acc, *, chunk, n_chunks):
    bi = pl.program_id(0)
    j = pl.program_id(1)
    tb = x_ref.shape[0]
    tok = j * tb + lax.broadcasted_iota(jnp.int32, (tb, chunk), 0)
    acc[...] = jnp.zeros_like(acc)
    shift = chunk.bit_length() - 1
    for e in range(N_EXPERTS):
        def add_chunk(c, carry, e=e):
            st = 0 if n_chunks == 1 else pl.multiple_of(c * chunk, chunk)
            onehot = (idx_ref[0, pl.ds(e, 1), pl.ds(st, chunk)] == tok).astype(BF16)
            acc[...] += jnp.dot(onehot, y_ref[e, pl.ds(st, chunk), :], preferred_element_type=F32)
            return carry

        if n_chunks == 1:
            add_chunk(0, 0)
        else:
            lo = off_ref[bi * N_EXPERTS + e, j]
            hi = off_ref[bi * N_EXPERTS + e, j + 1]
            c0 = lax.shift_right_logical(lo, shift)
            c1 = lax.shift_right_logical(hi + (chunk - 1), shift)
            lax.fori_loop(c0, c1, add_chunk, 0)
    o_ref[...] = x_ref[...] + g_ref[0] * acc[...]


def _moe_combine(off, y, idx, x2, g, *, row_block0, cap, seq_len):
    bsz = idx.shape[0]
    d = x2.shape[1]
    tb = min(seq_len, COMBINE_TB)
    nb = seq_len // tb
    chunk = min(cap, LANES)
    n_chunks = cap // chunk
    n_mod = g.shape[0]
    grid_spec = pltpu.PrefetchScalarGridSpec(
        num_scalar_prefetch=1,
        grid=(bsz, nb),
        in_specs=[pl.BlockSpec((N_EXPERTS, cap, d), lambda bi, j, off: (0, row_block0 + bi, 0),
                               pipeline_mode=pl.Buffered(1)),
                  pl.BlockSpec((1, N_EXPERTS, cap), lambda bi, j, off: (bi, 0, 0)),
                  pl.BlockSpec((tb, d), lambda bi, j, off: (bi * nb + j, 0)),
                  pl.BlockSpec((1, 1, d), lambda bi, j, off: (bi if n_mod > 1 else 0, 0, 0))],
        out_specs=pl.BlockSpec((tb, d), lambda bi, j, off: (bi * nb + j, 0)),
        scratch_shapes=[pltpu.VMEM((tb, d), F32)])
    return pl.pallas_call(
        functools.partial(_combine_kernel, chunk=chunk, n_chunks=n_chunks),
        grid_spec=grid_spec,
        out_shape=jax.ShapeDtypeStruct(x2.shape, F32),
        compiler_params=_cparams(("arbitrary", "arbitrary")),
        name="moe_combine",
    )(off, y, idx, x2, g)


def _expert_choice_moe(streams, w_gate, w_up, w_down):
    routed = []
    for h2, aff, x2, g in streams:
        bsz, _, l = aff.shape
        cap = EC_FACTOR * l // N_EXPERTS
        idx, gate, off = _topk_select(aff, cap)
        rows = (idx + (jnp.arange(bsz, dtype=jnp.int32) * l)[:, None, None]).transpose(1, 0, 2)
        routed.append((idx, off, rows.reshape(N_EXPERTS, 1, bsz * cap), gate.transpose(1, 0, 2).reshape(
            N_EXPERTS, bsz * cap, 1), cap))
    y = _expert_ffn([r[2] for r in routed], [s[0] for s in streams],
                    jnp.concatenate([r[3] for r in routed], axis=1), w_gate, w_up, w_down)
    outs = []
    row0 = 0
    for (h2, aff, x2, g), (idx, off, rows, _, cap) in zip(streams, routed):
        assert row0 % cap == 0
        outs.append(_moe_combine(off, y, idx, x2, g, row_block0=row0 // cap, cap=cap, seq_len=aff.shape[2]))
        row0 += rows.shape[2]
    return outs


HY_R = 128
HY_COLS = 4096
HY_KB = 8


def _dft_consts(r):
    k = np.arange(r)
    ang = 2.0 * np.pi * np.outer(k, k) / r
    fre, fim = np.cos(ang), -np.sin(ang)
    k2 = np.arange(r)
    tang = 2.0 * np.pi * np.outer(k2, k2) / (r * r)
    tw = np.stack([np.cos(tang), -np.sin(tang)])
    fwd = np.block([[fre, -fim], [fim, fre]])
    inv = np.block([[fre, fim], [-fim, fre]])
    return fre, fim, tw, fwd, inv


def _hy_prep_kernel(x_ref, prev_ref, next_ref, w_ref, b_ref, x1_ref, x2_ref, v_ref, *, n_tiles):
    t = pl.program_id(1)
    x = x_ref[0]
    tm = x.shape[0]
    row = lax.broadcasted_iota(jnp.int32, x.shape, 0)
    prev = jnp.where(t == 0, 0.0, prev_ref[0, pl.ds(7, 1), :])
    nxt = jnp.where(t == n_tiles - 1, 0.0, next_ref[0, pl.ds(0, 1), :])
    up = jnp.where(row == 0, prev, pltpu.roll(x, 1, 0))
    dn = jnp.where(row == tm - 1, nxt, pltpu.roll(x, tm - 1, 0))
    y = b_ref[...] + up * w_ref[pl.ds(0, 1), :] + x * w_ref[pl.ds(1, 1), :] + dn * w_ref[pl.ds(2, 1), :]
    x1_ref[0] = y[:, :HY_W].astype(BF16)
    x2_ref[0] = y[:, HY_W:2 * HY_W].astype(BF16)
    v_ref[0] = y[:, 2 * HY_W:].astype(BF16)


def _hyena_prep(u, conv_w, conv_b):
    b, l, c = u.shape
    tm = min(l, 1024)
    n_tiles = l // tm
    g = tm // 8
    n_groups = l // 8
    out = jax.ShapeDtypeStruct((b, l, HY_W), BF16)
    ospec = pl.BlockSpec((1, tm, HY_W), lambda bi, t: (bi, t, 0))
    return pl.pallas_call(
        functools.partial(_hy_prep_kernel, n_tiles=n_tiles),
        grid=(b, n_tiles),
        in_specs=[pl.BlockSpec((1, tm, c), lambda bi, t: (bi, t, 0)),
                  pl.BlockSpec((1, 8, c), lambda bi, t: (bi, jnp.maximum(t * g - 1, 0), 0)),
                  pl.BlockSpec((1, 8, c), lambda bi, t: (bi, jnp.minimum((t + 1) * g, n_groups - 1), 0)),
                  pl.BlockSpec((HY_SHORT, c), lambda bi, t: (0, 0)),
                  pl.BlockSpec((1, c), lambda bi, t: (0, 0))],
        out_specs=[ospec, ospec, ospec],
        out_shape=[out, out, out],
        compiler_params=_cparams(("arbitrary", "arbitrary")),
        name="hyena_prep",
    )(u, u, u, conv_w, conv_b.reshape(1, c))


def _hy_filter_kernel(z_ref, w1_ref, b1_ref, w2_ref, b2_ref, w3_ref, fr_ref, dl_ref, k_ref, ss_ref, *, seq_len):
    i = pl.program_id(0)
    hp = lax.Precision.HIGHEST
    z = z_ref[...]
    tm = z.shape[0]
    fr = fr_ref[...]
    h = jnp.sin(fr * (jnp.dot(z, w1_ref[...], preferred_element_type=F32, precision=hp) + b1_ref[...]))
    h = jnp.sin(fr * (jnp.dot(h, w2_ref[...], preferred_element_type=F32, precision=hp) + b2_ref[...]))
    h = jnp.dot(h, w3_ref[...], preferred_element_type=F32, precision=hp)
    dec = jnp.exp(-z[:, 0:1] * dl_ref[...])
    r = i * tm + lax.broadcasted_iota(jnp.int32, (tm, HY_W), 0)
    parts = []
    for o in range(HY_ORDER):
        hf = h[:, (2 * o) * HY_W:(2 * o + 1) * HY_W]
        hb = h[:, (2 * o + 1) * HY_W:(2 * o + 2) * HY_W]
        parts.append(jnp.where(r < seq_len, hf, jnp.where(r > seq_len, hb, 0.0)) * dec)
    k = jnp.concatenate(parts, axis=1)
    k_ref[...] = k
    ss = jnp.sum(k * k, axis=0, keepdims=True)

    @pl.when(i == 0)
    def _():
        ss_ref[...] = ss

    @pl.when(i != 0)
    def _():
        ss_ref[...] += ss


def _hyena_filter_taps(l, w1, b1, w2, b2, w3, freq):
    t = jnp.linspace(0.0, 1.0, l, dtype=F32)[:, None]
    w = (2.0 * math.pi / l) * jnp.arange(l, dtype=F32)[:, None]
    f = jnp.linspace(1e-4, HY_BANDS - 1, HY_BANDS, dtype=F32)[None, :]
    z = jnp.concatenate([t, jnp.cos(f * w), -jnp.sin(f * w)], axis=-1)
    z2 = jnp.concatenate([z, z[:1], z[:0:-1]], axis=0)
    emb = z.shape[1]
    z2 = jnp.pad(z2, ((0, 0), (0, LANES - emb)))
    w1p = jnp.pad(w1, ((0, LANES - emb), (0, 0)))
    deltas = jnp.abs(jnp.linspace(math.log(HY_DECAY_TARGET) / HY_SLOW_DECAY,
                                  math.log(HY_DECAY_TARGET) / HY_FAST_DECAY, HY_W, dtype=F32))
    fo = w2.shape[0]
    nout = w3.shape[1]
    tm = min(2 * l, 1024)
    const = lambda i: (0, 0)
    return pl.pallas_call(
        functools.partial(_hy_filter_kernel, seq_len=l),
        grid=(2 * l // tm,),
        in_specs=[pl.BlockSpec((tm, LANES), lambda i: (i, 0)),
                  pl.BlockSpec((LANES, fo), const), pl.BlockSpec((1, fo), const),
                  pl.BlockSpec((fo, fo), const), pl.BlockSpec((1, fo), const),
                  pl.BlockSpec((fo, nout), const), pl.BlockSpec((1, fo), const),
                  pl.BlockSpec((1, HY_W), const)],
        out_specs=[pl.BlockSpec((tm, HY_ORDER * HY_W), lambda i: (i, 0)),
                   pl.BlockSpec((1, HY_ORDER * HY_W), const)],
        out_shape=[jax.ShapeDtypeStruct((2 * l, HY_ORDER * HY_W), F32),
                   jax.ShapeDtypeStruct((1, HY_ORDER * HY_W), F32)],
        compiler_params=_cparams(("arbitrary",)),
        name="hyena_filter_taps",
    )(z2, w1p, b1.reshape(1, fo), w2, b2.reshape(1, fo), w3, freq.reshape(1, fo), deltas.reshape(1, HY_W))


def _left_matmul_kernel(f_ref, x_ref, o_ref):
    o_ref[0] = jnp.dot(f_ref[...], x_ref[0].astype(BF16), preferred_element_type=F32).astype(o_ref.dtype)


def _left_matmul(f_bf, x3, name):
    b, k, c = x3.shape
    m = f_bf.shape[0]
    return pl.pallas_call(
        _left_matmul_kernel,
        grid=(b, c // HY_COLS),
        in_specs=[pl.BlockSpec((m, k), lambda bi, j: (0, 0)),
                  pl.BlockSpec((1, k, HY_COLS), lambda bi, j: (bi, 0, j))],
        out_specs=pl.BlockSpec((1, m, HY_COLS), lambda bi, j: (bi, 0, j)),
        out_shape=jax.ShapeDtypeStruct((b, m, c), BF16),
        compiler_params=_cparams(("arbitrary", "arbitrary")),
        name=name,
    )(f_bf, x3)


def _twiddle_cols(tw_ref, k1):
    lane = lax.broadcasted_iota(jnp.int32, (HY_R, HY_R), 1)
    pick = lane == k1
    twr = jnp.sum(jnp.where(pick, tw_ref[0], 0.0), axis=1, keepdims=True)
    twi = jnp.sum(jnp.where(pick, tw_ref[1], 0.0), axis=1, keepdims=True)
    return twr, twi


def _hy_spectrum_kernel(a_ref, tw_ref, fwd_ref, ss_ref, kf_ref):
    kb = pl.program_id(0)
    scale = lax.rsqrt(ss_ref[...] + EPS) * (1.0 / (HY_R * HY_R))

    def body(kk, carry):
        twr, twi = _twiddle_cols(tw_ref, kb * HY_KB + kk)
        are = a_ref[0, 0, kk].astype(F32)
        aim = a_ref[0, 1, kk].astype(F32)
        bst = jnp.concatenate([are * twr - aim * twi, are * twi + aim * twr], axis=0).astype(BF16)
        x = jnp.dot(fwd_ref[...], bst, preferred_element_type=F32)
        kf_ref[0, kk] = x[:HY_R] * scale
        kf_ref[1, kk] = x[HY_R:] * scale
        return carry

    lax.fori_loop(0, HY_KB, body, 0)


def _hy_spectrum(a5, tw, fwd_bf, ss):
    nch = a5.shape[-1]
    return pl.pallas_call(
        _hy_spectrum_kernel,
        grid=(HY_R // HY_KB,),
        in_specs=[pl.BlockSpec((1, 2, HY_KB, HY_R, nch), lambda kb: (0, 0, kb, 0, 0)),
                  pl.BlockSpec((2, HY_R, HY_R), lambda kb: (0, 0, 0)),
                  pl.BlockSpec((2 * HY_R, 2 * HY_R), lambda kb: (0, 0)),
                  pl.BlockSpec((1, nch), lambda kb: (0, 0))],
        out_specs=pl.BlockSpec((2, HY_KB, HY_R, nch), lambda kb: (0, kb, 0, 0)),
        out_shape=jax.ShapeDtypeStruct((2, HY_R, HY_R, nch), F32),
        compiler_params=_cparams(("arbitrary",)),
        name="hyena_filter_spectrum",
    )(a5, tw, fwd_bf, ss)


def _hy_mid_kernel(a_ref, kf_ref, tw_ref, fwd_ref, inv_ref, z_ref):
    kb = pl.program_id(0)

    def body(kk, carry):
        twr, twi = _twiddle_cols(tw_ref, kb * HY_KB + kk)
        are = a_ref[0, 0, kk].astype(F32)
        aim = a_ref[0, 1, kk].astype(F32)
        bst = jnp.concatenate([are * twr - aim * twi, are * twi + aim * twr], axis=0).astype(BF16)
        x = jnp.dot(fwd_ref[...], bst, preferred_element_type=F32)
        xre, xim = x[:HY_R], x[HY_R:]
        kre, kim = kf_ref[0, kk], kf_ref[1, kk]
        yst = jnp.concatenate([xre * kre - xim * kim, xre * kim + xim * kre], axis=0).astype(BF16)
        zz = jnp.dot(inv_ref[...], yst, preferred_element_type=F32)
        zre, zim = zz[:HY_R], zz[HY_R:]
        z_ref[0, 0, kk] = (zre * twr + zim * twi).astype(BF16)
        z_ref[0, 1, kk] = (zim * twr - zre * twi).astype(BF16)
        return carry

    lax.fori_loop(0, HY_KB, body, 0)


def _hy_mid(a5, kf, order, tw, fwd_bf, inv_bf):
    b = a5.shape[0]
    blk = (1, 2, HY_KB, HY_R, HY_W)
    return pl.pallas_call(
        _hy_mid_kernel,
        grid=(HY_R // HY_KB, b),
        in_specs=[pl.BlockSpec(blk, lambda kb, bi: (bi, 0, kb, 0, 0)),
                  pl.BlockSpec((2, HY_KB, HY_R, HY_W), lambda kb, bi: (0, kb, 0, order)),
                  pl.BlockSpec((2, HY_R, HY_R), lambda kb, bi: (0, 0, 0)),
                  pl.BlockSpec((2 * HY_R, 2 * HY_R), lambda kb, bi: (0, 0)),
                  pl.BlockSpec((2 * HY_R, 2 * HY_R), lambda kb, bi: (0, 0))],
        out_specs=pl.BlockSpec(blk, lambda kb, bi: (bi, 0, kb, 0, 0)),
        out_shape=jax.ShapeDtypeStruct(a5.shape, BF16),
        compiler_params=_cparams(("arbitrary", "arbitrary")),
        name="hyena_spectral_product",
    )(a5, kf, tw, fwd_bf, inv_bf)


def _hy_out_kernel(f_ref, z_ref, g_ref, w_ref, sk_ref, o_ref):
    y = jnp.dot(f_ref[...], z_ref[0], preferred_element_type=F32)
    w = w_ref[0].astype(F32)
    o_ref[0] = (g_ref[0].astype(F32) * (y + w * sk_ref[...])).astype(BF16)


def _hy_out(f_bf, z3, gate3, w3, skip_row):
    b, m2, c = z3.shape
    m = f_bf.shape[0]
    dspec = pl.BlockSpec((1, m, HY_COLS), lambda bi, j: (bi, 0, j))
    return pl.pallas_call(
        _hy_out_kernel,
        grid=(b, c // HY_COLS),
        in_specs=[pl.BlockSpec((m, m2), lambda bi, j: (0, 0)),
                  pl.BlockSpec((1, m2, HY_COLS), lambda bi, j: (bi, 0, j)),
                  dspec, dspec,
                  pl.BlockSpec((1, HY_COLS), lambda bi, j: (0, 0))],
        out_specs=dspec,
        out_shape=jax.ShapeDtypeStruct((b, m, c), BF16),
        compiler_params=_cparams(("arbitrary", "arbitrary")),
        name="hyena_inverse_gate",
    )(f_bf, z3, gate3, w3, skip_row)


def _hyena_long(u, conv_w, conv_b, skip, filt_args):
    b, l, _ = u.shape
    assert 2 * l == HY_R * HY_R
    fre, fim, tw, fwd, inv = _dft_consts(HY_R)
    half = HY_R // 2
    as_bf = lambda m: jnp.asarray(m, F32).astype(BF16)
    f_a_data = as_bf(np.concatenate([fre[:, :half], fim[:, :half]], axis=0))
    f_a_filt = as_bf(np.concatenate([fre, fim], axis=0))
    f_c = as_bf(np.concatenate([fre[:half], fim[:half]], axis=1))
    tw_j = jnp.asarray(tw, F32)
    fwd_bf = as_bf(fwd)
    inv_bf = as_bf(inv)

    taps, ss = _hyena_filter_taps(l, *filt_args)
    nch = HY_ORDER * HY_W
    ak = _left_matmul(f_a_filt, taps.reshape(1, HY_R, HY_R * nch), "hyena_filter_dft_a")
    kf = _hy_spectrum(ak.reshape(1, 2, HY_R, HY_R, nch), tw_j, fwd_bf, ss)

    x1, x2, v = _hyena_prep(u, conv_w, conv_b)
    z = v
    for o, gate in enumerate((x1, x2)):
        z3 = z.reshape(b, half, HY_R * HY_W)
        a = _left_matmul(f_a_data, z3, "hyena_dft_a")
        zs = _hy_mid(a.reshape(b, 2, HY_R, HY_R, HY_W), kf, o, tw_j, fwd_bf, inv_bf)
        skip_row = jnp.tile(skip[o], HY_COLS // HY_W).reshape(1, HY_COLS)
        z = _hy_out(f_c, zs.reshape(b, 2 * HY_R, HY_R * HY_W), gate.reshape(b, half, HY_R * HY_W), z3,
                    skip_row).reshape(b, l, HY_W)
    return z


def _hy_ctx_kernel(x1_ref, x2_ref, v_ref, taps_ref, ss_ref, fa_ref, fc_ref, sk_ref, o_ref, *, seq_len):
    n2 = 2 * seq_len
    fa = fa_ref[...]
    scale = lax.rsqrt(ss_ref[...] + EPS) * (1.0 / n2)
    kf = jnp.dot(fa, taps_ref[...].astype(BF16), preferred_element_type=F32) * scale
    z = v_ref[0].astype(F32)
    for o, g_ref in enumerate((x1_ref, x2_ref)):
        x = jnp.dot(fa[:, :seq_len], z.astype(BF16), preferred_element_type=F32)
        xre, xim = x[:n2], x[n2:]
        kre = kf[:n2, o * HY_W:(o + 1) * HY_W]
        kim = kf[n2:, o * HY_W:(o + 1) * HY_W]
        yst = jnp.concatenate([xre * kre - xim * kim, xre * kim + xim * kre], axis=0).astype(BF16)
        y = jnp.dot(fc_ref[...], yst, preferred_element_type=F32)
        z = g_ref[0].astype(F32) * (y + z * sk_ref[pl.ds(o, 1), :])
    o_ref[0] = z.astype(BF16)


def _hyena_short(u, conv_w, conv_b, skip, filt_args):
    b, l, _ = u.shape
    n2 = 2 * l
    k = np.arange(n2)
    ang = 2.0 * np.pi * np.outer(k, k) / n2
    fre, fim = np.cos(ang), -np.sin(ang)
    fa = jnp.asarray(np.concatenate([fre, fim], axis=0), F32).astype(BF16)
    fc = jnp.asarray(np.concatenate([fre[:l], fim[:l]], axis=1), F32).astype(BF16)
    taps, ss = _hyena_filter_taps(l, *filt_args)
    x1, x2, v = _hyena_prep(u, conv_w, conv_b)
    nch = HY_ORDER * HY_W
    dspec = pl.BlockSpec((1, l, HY_W), lambda bi: (bi, 0, 0))
    const = lambda bi: (0, 0)
    return pl.pallas_call(
        functools.partial(_hy_ctx_kernel, seq_len=l),
        grid=(b,),
        in_specs=[dspec, dspec, dspec,
                  pl.BlockSpec((n2, nch), const), pl.BlockSpec((1, nch), const),
                  pl.BlockSpec((2 * n2, n2), const), pl.BlockSpec((l, 2 * n2), const),
                  pl.BlockSpec((HY_ORDER, HY_W), const)],
        out_specs=dspec,
        out_shape=jax.ShapeDtypeStruct((b, l, HY_W), BF16),
        compiler_params=_cparams(("arbitrary",)),
        name="hyena_context",
    )(x1, x2, v, taps, ss, fa, fc, skip)


def _rope_tables(n):
    t = jnp.arange(n, dtype=jnp.int32)
    pos = jnp.stack([t // GRID_W, t % GRID_W], axis=-1).astype(F32)
    nf = HEAD_DIM // 4
    inv = ROPE_THETA ** (-jnp.arange(nf, dtype=F32) / nf)
    ang = pos[:, :, None] * inv
    cos, sin = jnp.cos(ang), jnp.sin(ang)
    c64 = jnp.concatenate([cos[:, 0], cos[:, 0], cos[:, 1], cos[:, 1]], axis=-1)
    s64 = jnp.concatenate([-sin[:, 0], sin[:, 0], -sin[:, 1], sin[:, 1]], axis=-1)
    return jnp.tile(c64, (1, 2)), jnp.tile(s64, (1, 2))


def _block_diag(pool_w):
    g, c, _ = pool_w.shape
    out = jnp.zeros((g * c, g * c), pool_w.dtype)
    for i in range(g):
        out = out.at[i * c:(i + 1) * c, i * c:(i + 1) * c].set(pool_w[i])
    return out


def kernel(x, c, ctx, c_ctx, w_mod, b_mod, norm1_w, norm2_w, w_in, w_out, q_norm_w, k_norm_w, na_rpb,
           hy_conv_w, hy_conv_b, hy_w1, hy_b1, hy_w2, hy_b2, hy_w3, hy_freq, hy_skip,
           pool_w, pool_scale, w_router, w_gate, w_up, w_down):
    b, n, d = x.shape
    lc = ctx.shape[1]
    depth = w_mod.shape[0]
    rows = n // GRID_W
    cos128, sin128 = _rope_tables(n)
    seg = jnp.asarray(np.kron(np.eye(2), np.full((HEAD_DIM, HEAD_DIM), 1.0 / HEAD_DIM)), BF16)
    cc = jnp.zeros((8, d), F32).at[:b].set(c).at[b].set(c_ctx)

    xl = x.reshape(b * n, d)
    xc = ctx.reshape(b * lc, d)
    for i in range(depth):
        last = i == depth - 1
        mod = _modulation(cc, w_mod[i], b_mod[i])
        ml = mod[:b].reshape(b, 1, 6, d)
        mc = mod[b].reshape(1, 1, 6, d)
        sh1, sc1, g1, sh2, sc2, g2 = [ml[:, :, j] for j in range(6)]
        csh1, csc1, cg1, csh2, csc2, cg2 = [mc[:, :, j] for j in range(6)]
        w_in_bf = w_in[i].astype(BF16)
        w_out_bf = w_out[i].astype(BF16)
        w_router_t = w_router[i].T
        pool_bd = _block_diag(pool_w[i]).astype(BF16)
        hy_args = (hy_w1[i], hy_b1[i], hy_w2[i], hy_b2[i], hy_w3[i], hy_freq[i])

        qc, kc, vc, hyc, poolc = _in_projection(
            xc, csh1, csc1, norm1_w[i], w_in_bf, q_norm_w[i], k_norm_w[i], cos128, sin128, seg,
            rows_per_mod=b * lc, seq_len=lc, tm=lc, rope=False)
        kc = kc.reshape(b, lc, NA_W)
        vc = vc.reshape(b, lc, NA_W)
        if not last:
            na_c = _dense_attention(qc.reshape(b, lc, NA_W), kc, vc).reshape(b * lc, NA_W)
            hy_c = _hyena_short(hyc.reshape(b, lc, 3 * HY_W), hy_conv_w[i], hy_conv_b[i], hy_skip[i], hy_args)
            pool_c = _pool_mixer(poolc.reshape(b, lc, POOL_W), pool_bd, pool_scale[i])
            xc1, h2c, affc = _out_projection(
                na_c, hy_c.reshape(b * lc, HY_W).astype(BF16), pool_c.reshape(b * lc, POOL_W), w_out_bf,
                xc, cg1, norm2_w[i], csh2, csc2, w_router_t, rows_per_mod=b * lc, seq_len=lc, tm=lc)

        qr, qp, kl, vl, hyl, pooll = _in_projection(
            xl, sh1, sc1, norm1_w[i], w_in_bf, q_norm_w[i], k_norm_w[i], cos128, sin128, seg,
            rows_per_mod=n, seq_len=n, tm=512, rope=True)
        bias = _na_bias_tables(na_rpb[i], rows)
        o_na = _neighborhood_attention(qr.reshape(b, n, NA_W), qp.reshape(b, n, NA_W), kl.reshape(b, n, NA_W),
                                       vl.reshape(b, n, NA_W), kc, vc, bias)
        hy_l = _hyena_long(hyl.reshape(b, n, 3 * HY_W), hy_conv_w[i], hy_conv_b[i], hy_skip[i], hy_args)
        pool_l = _pool_mixer(pooll.reshape(b, n, POOL_W), pool_bd, pool_scale[i])
        xl1, h2l, affl = _out_projection(
            o_na.reshape(b * n, NA_W), hy_l.reshape(b * n, HY_W).astype(BF16), pool_l.reshape(b * n, POOL_W),
            w_out_bf, xl, g1, norm2_w[i], sh2, sc2, w_router_t, rows_per_mod=n, seq_len=n, tm=512)
        streams = [(h2l, affl, xl1, g2)]
        if not last:
            streams.append((h2c, affc, xc1, cg2))
        outs = _expert_choice_moe(streams, w_gate[i], w_up[i], w_down[i])
        xl = outs[0]
        if not last:
            xc = outs[1]
    return xl.reshape(b, n, d)
```

```python
import functools
import math

import jax
import jax.numpy as jnp
import numpy as np
from jax import lax
from jax.experimental import pallas as pl
from jax.experimental.pallas import tpu as pltpu

F32 = jnp.float32
BF16 = jnp.bfloat16

D_MODEL = 1024
GRID_W = 64
NA_HEADS = 8
HEAD_DIM = 64
NA_W = NA_HEADS * HEAD_DIM
NA_WIN_R = 8
NA_WIN_C = 16
ROPE_THETA = 10000.0
HY_W = 256
HY_ORDER = 2
HY_SHORT = 3
HY_BANDS = 16
HY_DECAY_TARGET = 1e-2
HY_FAST_DECAY = 0.3
HY_SLOW_DECAY = 1.5
POOL_W = 256
POOL_SIZES = (2, 4, 8, 16)
POOL_GROUP = POOL_W // len(POOL_SIZES)
MIX_W = NA_W + HY_W + POOL_W
IN_W = 3 * NA_W + (HY_ORDER + 1) * HY_W + POOL_W
N_EXPERTS = 16
EC_FACTOR = 2
D_EXPERT = 2048
EPS = 1e-6

LANES = 128
NEG_BIG = -1e30
VMEM_LIMIT = 56 * 1024 * 1024

NA_Q_ROWS = 4
NA_K_ROWS = NA_Q_ROWS + NA_WIN_R - 1


def _cparams(sem):
    return pltpu.CompilerParams(dimension_semantics=sem, vmem_limit_bytes=VMEM_LIMIT)


def _mod_kernel(c_ref, w_ref, b_ref, o_ref):
    c = c_ref[...]
    s = c * jax.nn.sigmoid(c)
    o_ref[...] = jnp.dot(s, w_ref[0], preferred_element_type=F32,
                         precision=lax.Precision.HIGHEST) + b_ref[...]


def _modulation(cc, w_mod, b_mod, layer):
    rows, d = cc.shape
    n = w_mod.shape[2]
    tn = 1536
    return pl.pallas_call(
        _mod_kernel,
        grid=(n // tn,),
        in_specs=[pl.BlockSpec((rows, d), lambda j: (0, 0)),
                  pl.BlockSpec((1, d, tn), lambda j: (layer, 0, j)),
                  pl.BlockSpec((1, tn), lambda j: (0, j))],
        out_specs=pl.BlockSpec((rows, tn), lambda j: (0, j)),
        out_shape=jax.ShapeDtypeStruct((rows, n), F32),
        compiler_params=_cparams(("arbitrary",)),
        name="modulation",
    )(cc, w_mod, b_mod.reshape(1, n))


def _head_norm(xs, w128, seg):
    sq = xs * xs
    hi = sq.astype(BF16)
    lo = (sq - hi.astype(F32)).astype(BF16)
    m = (jnp.dot(hi, seg, preferred_element_type=F32) + jnp.dot(lo, seg, preferred_element_type=F32))
    return xs * lax.rsqrt(m + EPS) * w128


def _rope(xs, c, s_signed):
    lane = lax.broadcasted_iota(jnp.int32, xs.shape, 1)
    partner = jnp.where((lane & 16) == 0, pltpu.roll(xs, LANES - 16, 1), pltpu.roll(xs, 16, 1))
    return xs * c + partner * s_signed


def _inproj_kernel(x_ref, sh_ref, sc_ref, nw_ref, w_ref, qw_ref, kw_ref, cos_ref, sin_ref, seg_ref,
                   *out_refs, rope):
    if rope:
        qr_ref, qp_ref, k_ref, v_ref, hy_ref, pool_ref = out_refs
    else:
        qp_ref, k_ref, v_ref, hy_ref, pool_ref = out_refs
    x = x_ref[...]
    ms = jnp.mean(x * x, axis=-1, keepdims=True)
    h = x * lax.rsqrt(ms + EPS) * nw_ref[...]
    h = h * (1.0 + sc_ref[0]) + sh_ref[0]
    u = jnp.dot(h.astype(BF16), w_ref[...], preferred_element_type=F32)
    seg = seg_ref[...]
    qk_scale = HEAD_DIM ** -0.5
    for ch in range(NA_W // LANES):
        sl = slice(ch * LANES, (ch + 1) * LANES)
        qn = _head_norm(u[:, sl], qw_ref[...], seg)
        kn = _head_norm(u[:, NA_W + ch * LANES:NA_W + (ch + 1) * LANES], kw_ref[...], seg)
        qp_ref[:, sl] = (qn * qk_scale).astype(BF16)
        if rope:
            c = cos_ref[...]
            s = sin_ref[...]
            qr_ref[:, sl] = (_rope(qn, c, s) * qk_scale).astype(BF16)
            k_ref[:, sl] = _rope(kn, c, s).astype(BF16)
        else:
            k_ref[:, sl] = kn.astype(BF16)
    v_ref[...] = u[:, 2 * NA_W:3 * NA_W].astype(BF16)
    hy_ref[...] = u[:, 3 * NA_W:3 * NA_W + 3 * HY_W]
    pool_ref[...] = u[:, 3 * NA_W + 3 * HY_W:]


def _in_projection(x2, shift, scale, norm_w, w_in_bf, q_norm_w, k_norm_w, cos128, sin128, seg, *,
                   rows_per_mod, seq_len, tm, rope):
    m, d = x2.shape
    tiles_per_mod = rows_per_mod // tm
    tiles_per_seq = seq_len // tm
    row = lambda i: (i, 0)
    const = lambda i: (0, 0)
    modi = lambda i: (i // tiles_per_mod, 0, 0)
    pos = lambda i: (i % tiles_per_seq, 0)
    outs = []
    if rope:
        outs.append((NA_W, BF16))
    outs += [(NA_W, BF16), (NA_W, BF16), (NA_W, BF16), (3 * HY_W, F32), (POOL_W, F32)]
    return pl.pallas_call(
        functools.partial(_inproj_kernel, rope=rope),
        grid=(m // tm,),
        in_specs=[pl.BlockSpec((tm, d), row),
                  pl.BlockSpec((1, 1, d), modi), pl.BlockSpec((1, 1, d), modi),
                  pl.BlockSpec((1, d), const),
                  pl.BlockSpec((d, IN_W), const),
                  pl.BlockSpec((1, LANES), const), pl.BlockSpec((1, LANES), const),
                  pl.BlockSpec((tm, LANES), pos), pl.BlockSpec((tm, LANES), pos),
                  pl.BlockSpec((LANES, LANES), const)],
        out_specs=[pl.BlockSpec((tm, w), row) for w, _ in outs],
        out_shape=[jax.ShapeDtypeStruct((m, w), dt) for w, dt in outs],
        compiler_params=_cparams(("arbitrary",)),
        name="in_projection_rope" if rope else "in_projection_ctx",
    )(x2, shift, scale, norm_w.reshape(1, d), w_in_bf,
      jnp.tile(q_norm_w, 2).reshape(1, LANES), jnp.tile(k_norm_w, 2).reshape(1, LANES),
      cos128, sin128, seg)


def _softmax_pv(s_list, v_list):
    mx = functools.reduce(jnp.maximum, [jnp.max(s, axis=-1, keepdims=True) for s in s_list])
    ps = [jnp.exp(s - mx) for s in s_list]
    l = functools.reduce(jnp.add, [jnp.sum(p, axis=-1, keepdims=True) for p in ps])
    o = functools.reduce(jnp.add, [jnp.dot(p.astype(BF16), v, preferred_element_type=F32)
                                   for p, v in zip(ps, v_list)])
    return o / l


_NT = (((1,), (1,)), ((), ()))


def _na_kernel(qr_ref, qp_ref, k_ref, v_ref, kc_ref, vc_ref, bias_ref, o_ref, *, rows):
    t = pl.program_id(2)
    ks = jnp.clip(t * NA_Q_ROWS - NA_WIN_R // 2, 0, rows - NA_K_ROWS)
    start = pl.multiple_of(ks * GRID_W, GRID_W)
    k = k_ref[0, pl.ds(start, NA_K_ROWS * GRID_W), :]
    v = v_ref[0, pl.ds(start, NA_K_ROWS * GRID_W), :]
    q = qr_ref[0]
    qp = qp_ref[0]
    kc = kc_ref[0]
    vc = vc_ref[0]
    lane = lax.broadcasted_iota(jnp.int32, q.shape, 1)
    outs = []
    for hh in range(2):
        keep = (lane < HEAD_DIM) if hh == 0 else (lane >= HEAD_DIM)
        zero = jnp.zeros_like(q)
        s_lat = lax.dot_general(jnp.where(keep, q, zero), k, _NT, preferred_element_type=F32) + bias_ref[0, hh]
        s_ctx = lax.dot_general(jnp.where(keep, qp, zero), kc, _NT, preferred_element_type=F32)
        outs.append(_softmax_pv([s_lat, s_ctx], [v, vc]))
    o_ref[0] = jnp.where(lane < HEAD_DIM, outs[0], outs[1]).astype(BF16)


def _neighborhood_attention(qr, qp, k, v, kc, vc, bias):
    b, n, _ = qr.shape
    lc = kc.shape[1]
    rows = n // GRID_W
    n_tiles = rows // NA_Q_ROWS
    tq = NA_Q_ROWS * GRID_W
    tk = NA_K_ROWS * GRID_W
    qmap = lambda bi, hp, t: (bi, t, hp)
    smap = lambda bi, hp, t: (bi, 0, hp)
    bmap = lambda bi, hp, t: (jnp.where(t == 0, 0, jnp.where(t == n_tiles - 1, 2, 1)), hp, 0, 0)
    return pl.pallas_call(
        functools.partial(_na_kernel, rows=rows),
        grid=(b, NA_W // LANES, n_tiles),
        in_specs=[pl.BlockSpec((1, tq, LANES), qmap), pl.BlockSpec((1, tq, LANES), qmap),
                  pl.BlockSpec((1, n, LANES), smap), pl.BlockSpec((1, n, LANES), smap),
                  pl.BlockSpec((1, lc, LANES), smap), pl.BlockSpec((1, lc, LANES), smap),
                  pl.BlockSpec((1, 2, tq, tk), bmap)],
        out_specs=pl.BlockSpec((1, tq, LANES), qmap),
        out_shape=jax.ShapeDtypeStruct((b, n, NA_W), BF16),
        compiler_params=_cparams(("arbitrary", "arbitrary", "arbitrary")),
        name="neighborhood_attention",
    )(qr, qp, k, v, kc, vc, bias)


def _na_bias_tables(rpb, rows):
    n_tiles = rows // NA_Q_ROWS
    n_dr, n_dc = 2 * NA_WIN_R - 1, 2 * NA_WIN_C - 1
    qc = np.arange(GRID_W)
    cs = np.clip(qc - NA_WIN_C // 2, 0, GRID_W - NA_WIN_C)
    kcol = np.arange(GRID_W)
    ok_c = (kcol[None, :] >= cs[:, None]) & (kcol[None, :] < cs[:, None] + NA_WIN_C)
    dc = kcol[None, :] - qc[:, None] + NA_WIN_C - 1
    col_sel = (dc[:, :, None] == np.arange(n_dc)) & ok_c[:, :, None]
    row_sel, masks = [], []
    for t in (0, 1, n_tiles - 1):
        r0 = t * NA_Q_ROWS
        ks = int(np.clip(r0 - NA_WIN_R // 2, 0, rows - NA_K_ROWS))
        r = r0 + np.arange(NA_Q_ROWS)
        rs = np.clip(r - NA_WIN_R // 2, 0, rows - NA_WIN_R)
        krow = ks + np.arange(NA_K_ROWS)
        ok_r = (krow[None, :] >= rs[:, None]) & (krow[None, :] < rs[:, None] + NA_WIN_R)
        dr = krow[None, :] - r[:, None] + NA_WIN_R - 1
        row_sel.append((dr[:, :, None] == np.arange(n_dr)) & ok_r[:, :, None])
        ok = ok_r[:, None, :, None] & ok_c[None, :, None, :]
        masks.append(np.where(ok, 0.0, NEG_BIG))
    tab = jnp.einsum("hrd,tijr,qkd->thiqjk", rpb.astype(F32), jnp.asarray(np.stack(row_sel), F32),
                     jnp.asarray(col_sel, F32), precision=lax.Precision.HIGHEST)
    tab = tab + jnp.asarray(np.stack(masks), F32)[:, None]
    return tab.reshape(3, NA_HEADS, NA_Q_ROWS * GRID_W, NA_K_ROWS * GRID_W)


def _dense_attn_kernel(q_ref, k_ref, v_ref, o_ref):
    q = q_ref[0]
    k = k_ref[0]
    v = v_ref[0]
    lane = lax.broadcasted_iota(jnp.int32, q.shape, 1)
    outs = []
    for hh in range(2):
        keep = (lane < HEAD_DIM) if hh == 0 else (lane >= HEAD_DIM)
        s = lax.dot_general(jnp.where(keep, q, jnp.zeros_like(q)), k, _NT, preferred_element_type=F32)
        outs.append(_softmax_pv([s], [v]))
    o_ref[0] = jnp.where(lane < HEAD_DIM, outs[0], outs[1]).astype(BF16)


def _dense_attention(q, k, v):
    b, l, _ = q.shape
    spec = pl.BlockSpec((1, l, LANES), lambda bi, hp: (bi, 0, hp))
    return pl.pallas_call(
        _dense_attn_kernel,
        grid=(b, NA_W // LANES),
        in_specs=[spec, spec, spec],
        out_specs=spec,
        out_shape=jax.ShapeDtypeStruct((b, l, NA_W), BF16),
        compiler_params=_cparams(("arbitrary", "arbitrary")),
        name="context_attention",
    )(q, k, v)


POOL_HALO = max(POOL_SIZES) // 2


def _pool_kernel(x_ref, w_ref, sc_ref, o_ref, xpad, *, seq_len, chunk):
    zeros = jnp.zeros((POOL_HALO, POOL_W), F32)
    xpad[pl.ds(0, POOL_HALO), :] = zeros
    xpad[pl.ds(seq_len + POOL_HALO, POOL_HALO), :] = zeros
    xpad[pl.ds(POOL_HALO, seq_len), :] = x_ref[0]
    span = chunk + 2 * POOL_HALO
    w = w_ref[...]
    sc = sc_ref[...]

    def body(c, carry):
        base = pl.multiple_of(c * chunk, chunk)
        xs = xpad[pl.ds(base, span), :]
        s2 = xs + pltpu.roll(xs, 1, 0)
        s4 = pltpu.roll(s2, 1, 0) + pltpu.roll(s2, span - 1, 0)
        s8 = pltpu.roll(s4, 2, 0) + pltpu.roll(s4, span - 2, 0)
        s16 = pltpu.roll(s8, 4, 0) + pltpu.roll(s8, span - 4, 0)
        mid = slice(POOL_HALO, POOL_HALO + chunk)
        grp = lax.broadcasted_iota(jnp.int32, (chunk, POOL_W), 1) // POOL_GROUP
        tpos = base + lax.broadcasted_iota(jnp.int32, (chunk, POOL_W), 0)
        half = jnp.left_shift(1, grp)
        cnt = (jnp.minimum(tpos + half, seq_len) - jnp.maximum(tpos - half, 0)).astype(F32)
        ssum = jnp.where(grp == 0, s2[mid], jnp.where(grp == 1, s4[mid], jnp.where(grp == 2, s8[mid], s16[mid])))
        diff = ssum / cnt - xs[mid]
        y = jnp.dot(diff.astype(BF16), w, preferred_element_type=F32) * sc
        o_ref[0, pl.ds(base, chunk), :] = y.astype(BF16)
        return carry

    lax.fori_loop(0, seq_len // chunk, body, 0)


def _pool_mixer(u, w_blockdiag_bf, pool_scale):
    b, l, _ = u.shape
    chunk = min(l, 512)
    return pl.pallas_call(
        functools.partial(_pool_kernel, seq_len=l, chunk=chunk),
        grid=(b,),
        in_specs=[pl.BlockSpec((1, l, POOL_W), lambda bi: (bi, 0, 0)),
                  pl.BlockSpec((POOL_W, POOL_W), lambda bi: (0, 0)),
                  pl.BlockSpec((1, POOL_W), lambda bi: (0, 0))],
        out_specs=pl.BlockSpec((1, l, POOL_W), lambda bi: (bi, 0, 0)),
        out_shape=jax.ShapeDtypeStruct((b, l, POOL_W), BF16),
        scratch_shapes=[pltpu.VMEM((l + 2 * POOL_HALO, POOL_W), F32)],
        compiler_params=_cparams(("arbitrary",)),
        name="pool_mixer",
    )(u, w_blockdiag_bf, pool_scale.reshape(1, POOL_W))


def _outproj_kernel(na_ref, hy_ref, pl_ref, w_ref, x_ref, g1_ref, nw_ref, sh_ref, sc_ref, wr_ref,
                    xn_ref, h2_ref, aff_ref):
    w = w_ref
    acc = jnp.dot(na_ref[...], w[pl.ds(0, NA_W), :], preferred_element_type=F32)
    acc += jnp.dot(hy_ref[...], w[pl.ds(NA_W, HY_W), :], preferred_element_type=F32)
    acc += jnp.dot(pl_ref[...], w[pl.ds(NA_W + HY_W, POOL_W), :], preferred_element_type=F32)
    xn = x_ref[...] + g1_ref[0] * acc
    xn_ref[...] = xn
    ms = jnp.mean(xn * xn, axis=-1, keepdims=True)
    h2 = xn * lax.rsqrt(ms + EPS) * nw_ref[...]
    h2 = h2 * (1.0 + sc_ref[0]) + sh_ref[0]
    h2_ref[...] = h2
    logits = lax.dot_general(wr_ref[...], h2, _NT, preferred_element_type=F32,
                             precision=lax.Precision.HIGHEST)
    mx = jnp.max(logits, axis=0, keepdims=True)
    p = jnp.exp(logits - mx)
    aff_ref[0] = p / jnp.sum(p, axis=0, keepdims=True)


def _out_projection(na, hy, pool, w_out_bf, x2, g1, norm_w, shift, scale, w_router_t, *,
                    rows_per_mod, seq_len, tm):
    m, d = x2.shape
    tiles_per_mod = rows_per_mod // tm
    tiles_per_seq = seq_len // tm
    row = lambda i: (i, 0)
    const = lambda i: (0, 0)
    modi = lambda i: (i // tiles_per_mod, 0, 0)
    return pl.pallas_call(
        _outproj_kernel,
        grid=(m // tm,),
        in_specs=[pl.BlockSpec((tm, NA_W), row), pl.BlockSpec((tm, HY_W), row), pl.BlockSpec((tm, POOL_W), row),
                  pl.BlockSpec((MIX_W, d), const),
                  pl.BlockSpec((tm, d), row),
                  pl.BlockSpec((1, 1, d), modi),
                  pl.BlockSpec((1, d), const),
                  pl.BlockSpec((1, 1, d), modi), pl.BlockSpec((1, 1, d), modi),
                  pl.BlockSpec((N_EXPERTS, d), const)],
        out_specs=[pl.BlockSpec((tm, d), row), pl.BlockSpec((tm, d), row),
                   pl.BlockSpec((1, N_EXPERTS, tm), lambda i: (i // tiles_per_seq, 0, i % tiles_per_seq))],
        out_shape=[jax.ShapeDtypeStruct((m, d), F32), jax.ShapeDtypeStruct((m, d), F32),
                   jax.ShapeDtypeStruct((m // seq_len, N_EXPERTS, seq_len), F32)],
        compiler_params=_cparams(("arbitrary",)),
        name="out_projection_router",
    )(na, hy, pool, w_out_bf, x2, g1, norm_w.reshape(1, d), shift, scale, w_router_t)


FFN_TF = 256
FFN_ROW_CHUNKS = 4


def _ffn_kernel(*refs, n_experts, sources, n_rows):
    ns = len(sources)
    idx_cur = refs[0:ns]
    idx_nxt = refs[ns:2 * ns]
    src = refs[2 * ns:3 * ns]
    gate_ref, wg_ref, wu_ref, wd_ref, o_ref, stage, xb, acc, sem = refs[3 * ns:]
    e = pl.program_id(0)
    f = pl.program_id(1)
    n_f = pl.num_programs(1)

    def row_copy(s, row, r):
        return pltpu.make_async_copy(src[s].at[pl.ds(row, 1)], stage.at[pl.ds(sources[s][0] + r, 1)], sem.at[0])

    def gather(idx_refs):
        for s in range(ns):
            def body(r, carry, s=s):
                row_copy(s, idx_refs[s][0, 0, r], r).start()
                return carry
            lax.fori_loop(0, sources[s][1], body, 0, unroll=8)

    def gather_wait():
        for s in range(ns):
            off, cnt = sources[s]
            pltpu.make_async_copy(src[s].at[pl.ds(0, cnt)], stage.at[pl.ds(off, cnt)], sem.at[0]).wait()

    @pl.when(f == 0)
    def _():
        @pl.when(e == 0)
        def _():
            gather(idx_cur)

        gather_wait()
        xb[...] = stage[...].astype(BF16)

        @pl.when(e + 1 < n_experts)
        def _():
            gather(idx_nxt)

    wg = wg_ref[0, 0].astype(BF16)
    wu = wu_ref[0, 0].astype(BF16)
    wd = wd_ref[0, 0].astype(BF16)
    rc = n_rows // FFN_ROW_CHUNKS
    for r in range(FFN_ROW_CHUNKS):
        rs = pl.ds(r * rc, rc)
        x = xb[rs, :]
        a = jnp.dot(x, wg, preferred_element_type=F32)
        b = jnp.dot(x, wu, preferred_element_type=F32)
        h = (a * jax.nn.sigmoid(a) * b).astype(BF16)
        y = jnp.dot(h, wd, preferred_element_type=F32)

        @pl.when(f == 0)
        def _():
            acc[rs, :] = y

        @pl.when(f != 0)
        def _():
            acc[rs, :] += y

    @pl.when(f == n_f - 1)
    def _():
        o_ref[0] = (acc[...] * gate_ref[0]).astype(BF16)


def _expert_ffn(rows_idx, tokens, gate, w_gate, w_up, w_down, layer):
    ns = len(tokens)
    e, d = w_gate.shape[1], w_gate.shape[2]
    counts = [int(ri.shape[2]) for ri in rows_idx]
    offs = [int(sum(counts[:s])) for s in range(ns)]
    r_total = sum(counts)
    assert r_total % (FFN_ROW_CHUNKS * 16) == 0
    smem = lambda cnt, nxt: pl.BlockSpec(
        (1, 1, cnt), (lambda ei, f: (jnp.minimum(ei + 1, e - 1), 0, 0)) if nxt else (lambda ei, f: (ei, 0, 0)),
        memory_space=pltpu.SMEM)
    in_specs = ([smem(cnt, False) for cnt in counts] + [smem(cnt, True) for cnt in counts]
                + [pl.BlockSpec(memory_space=pl.ANY) for _ in range(ns)]
                + [pl.BlockSpec((1, r_total, 1), lambda ei, f: (ei, 0, 0)),
                   pl.BlockSpec((1, 1, d, FFN_TF), lambda ei, f: (layer, ei, 0, f)),
                   pl.BlockSpec((1, 1, d, FFN_TF), lambda ei, f: (layer, ei, 0, f)),
                   pl.BlockSpec((1, 1, FFN_TF, d), lambda ei, f: (layer, ei, f, 0))])
    return pl.pallas_call(
        functools.partial(_ffn_kernel, n_experts=e, sources=tuple(zip(offs, counts)), n_rows=r_total),
        grid=(e, D_EXPERT // FFN_TF),
        in_specs=in_specs,
        out_specs=pl.BlockSpec((1, r_total, d), lambda ei, f: (ei, 0, 0)),
        out_shape=jax.ShapeDtypeStruct((e, r_total, d), BF16),
        scratch_shapes=[pltpu.VMEM((r_total, d), F32), pltpu.VMEM((r_total, d), BF16),
                        pltpu.VMEM((r_total, d), F32), pltpu.SemaphoreType.DMA((1,))],
        compiler_params=_cparams(("arbitrary", "arbitrary")),
        name="expert_ffn",
    )(*rows_idx, *rows_idx, *tokens, gate, w_gate, w_up, w_down)


def _prefix_count(m, n):
    lane = lax.broadcasted_iota(jnp.int32, m.shape, 1)
    c = m
    s = 1
    while s < n:
        c = c + jnp.where(lane >= s, pltpu.roll(c, s, 1), 0)
        s *= 2
    return c


COMBINE_TB = 256


def _topk_kernel(aff_ref, idx_ref, gate_ref, off_ref, *, cap):
    a = aff_ref[...]
    rows, n = a.shape

    def search(i, t):
        cand = t | jnp.left_shift(jnp.int32(1), 30 - i)
        cnt = jnp.sum((a >= pltpu.bitcast(cand, F32)).astype(jnp.int32), axis=1, keepdims=True)
        return jnp.where(cnt >= cap, cand, t)

    thr_bits = lax.fori_loop(0, 31, search, jnp.zeros((rows, 1), jnp.int32))
    thr = pltpu.bitcast(thr_bits, F32)
    gt = a >= pltpu.bitcast(thr_bits + 1, F32)
    eq = (a >= thr) & jnp.logical_not(gt)
    need = cap - jnp.sum(gt.astype(jnp.int32), axis=1, keepdims=True)
    eq_i = eq.astype(jnp.int32)
    sel = gt | (eq & (_prefix_count(eq_i, n) - eq_i < need))
    sel_i = sel.astype(jnp.int32)
    lane = lax.broadcasted_iota(jnp.int32, a.shape, 1)
    dist = jnp.where(sel, lane + 1 - _prefix_count(sel_i, n), 0)
    tok = jnp.where(sel, lane, -1)
    val = a
    s = 1
    while s < n:
        tok_in = pltpu.roll(tok, n - s, 1)
        dist_in = pltpu.roll(dist, n - s, 1)
        val_in = pltpu.roll(val, n - s, 1)
        take = (lane < n - s) & (tok_in >= 0) & ((dist_in & s) != 0)
        stay = (tok >= 0) & ((dist & s) == 0)
        tok = jnp.where(take, tok_in, jnp.where(stay, tok, -1))
        dist = jnp.where(take, dist_in, jnp.where(stay, dist, 0))
        val = jnp.where(take, val_in, val)
        s *= 2
    idx_ref[...] = tok[:, :cap]
    gate_ref[...] = val[:, :cap]
    olane = lax.broadcasted_iota(jnp.int32, off_ref.shape, 1)
    off = jnp.zeros(off_ref.shape, jnp.int32)
    for j in range(n // COMBINE_TB + 1):
        below = jnp.sum(jnp.where(lane < j * COMBINE_TB, sel_i, 0), axis=1, keepdims=True)
        off = jnp.where(olane == j, below, off)
    off_ref[...] = off


def _topk_select(aff, cap):
    b, e, n = aff.shape
    assert n & (n - 1) == 0 and n // COMBINE_TB < LANES
    rows = b * e
    full = lambda w: pl.BlockSpec((rows, w), lambda i: (0, 0))
    idx, gate, off = pl.pallas_call(
        functools.partial(_topk_kernel, cap=cap),
        grid=(1,),
        in_specs=[full(n)],
        out_specs=[full(cap), full(cap), full(LANES)],
        out_shape=[jax.ShapeDtypeStruct((rows, cap), jnp.int32), jax.ShapeDtypeStruct((rows, cap), F32),
                   jax.ShapeDtypeStruct((rows, LANES), jnp.int32)],
        compiler_params=_cparams(("arbitrary",)),
        name="expert_choice_topk",
    )(aff.reshape(rows, n))
    return idx.reshape(b, e, cap), gate.reshape(b, e, cap), off


def _combine_kernel(off_ref, y_ref, idx_ref, x_ref, g_ref, o_ref, acc, ycat, *, chunk, n_chunks):
    bi = pl.program_id(0)
    j = pl.program_id(1)
    tb = x_ref.shape[0]
    tok = j * tb + lax.broadcasted_iota(jnp.int32, (tb, chunk), 0)

    def onehot(e, st):
        return (idx_ref[0, pl.ds(e, 1), pl.ds(st, chunk)] == tok).astype(BF16)

    if n_chunks == 1:
        acc[...] = jnp.zeros_like(acc)
        for e in range(N_EXPERTS):
            acc[...] += jnp.dot(onehot(e, 0), y_ref[e], preferred_element_type=F32)
    else:
        shift = chunk.bit_length() - 1
        spans, hots = [], []
        for e in range(N_EXPERTS):
            lo = off_ref[bi * N_EXPERTS + e, j]
            hi = off_ref[bi * N_EXPERTS + e, j + 1]
            c0 = jnp.minimum(lax.shift_right_logical(lo, shift), n_chunks - 1)
            c1 = lax.shift_right_logical(hi + (chunk - 1), shift)
            spans.append((c0, c1))
            st = pl.multiple_of(c0 * chunk, chunk)
            ycat[pl.ds(e * chunk, chunk), :] = y_ref[e, pl.ds(st, chunk), :]
            hots.append(onehot(e, st))
        acc[...] = jnp.dot(jnp.concatenate(hots, axis=1), ycat[...], preferred_element_type=F32)
        for e in range(N_EXPERTS):
            def add_chunk(c, carry, e=e):
                st = pl.multiple_of(c * chunk, chunk)
                acc[...] += jnp.dot(onehot(e, st), y_ref[e, pl.ds(st, chunk), :], preferred_element_type=F32)
                return carry

            lax.fori_loop(spans[e][0] + 1, spans[e][1], add_chunk, 0)
    o_ref[...] = x_ref[...] + g_ref[0] * acc[...]


def _moe_combine(off, y, idx, x2, g, *, row_block0, cap, seq_len):
    bsz = idx.shape[0]
    d = x2.shape[1]
    tb = min(seq_len, COMBINE_TB)
    nb = seq_len // tb
    chunk = min(cap, LANES)
    n_chunks = cap // chunk
    n_mod = g.shape[0]
    grid_spec = pltpu.PrefetchScalarGridSpec(
        num_scalar_prefetch=1,
        grid=(bsz, nb),
        in_specs=[pl.BlockSpec((N_EXPERTS, cap, d), lambda bi, j, off: (0, row_block0 + bi, 0),
                               pipeline_mode=pl.Buffered(1)),
                  pl.BlockSpec((1, N_EXPERTS, cap), lambda bi, j, off: (bi, 0, 0)),
                  pl.BlockSpec((tb, d), lambda bi, j, off: (bi * nb + j, 0)),
                  pl.BlockSpec((1, 1, d), lambda bi, j, off: (bi if n_mod > 1 else 0, 0, 0))],
        out_specs=pl.BlockSpec((tb, d), lambda bi, j, off: (bi * nb + j, 0)),
        scratch_shapes=[pltpu.VMEM((tb, d), F32), pltpu.VMEM((N_EXPERTS * chunk, d), BF16)])
    return pl.pallas_call(
        functools.partial(_combine_kernel, chunk=chunk, n_chunks=n_chunks),
        grid_spec=grid_spec,
        out_shape=jax.ShapeDtypeStruct(x2.shape, F32),
        compiler_params=_cparams(("arbitrary", "arbitrary")),
        name="moe_combine",
    )(off, y, idx, x2, g)


def _expert_choice_moe(streams, w_gate, w_up, w_down, layer):
    routed = []
    for h2, aff, x2, g in streams:
        bsz, _, l = aff.shape
        cap = EC_FACTOR * l // N_EXPERTS
        idx, gate, off = _topk_select(aff, cap)
        rows = (idx + (jnp.arange(bsz, dtype=jnp.int32) * l)[:, None, None]).transpose(1, 0, 2)
        routed.append((idx, off, rows.reshape(N_EXPERTS, 1, bsz * cap), gate.transpose(1, 0, 2).reshape(
            N_EXPERTS, bsz * cap, 1), cap))
    y = _expert_ffn([r[2] for r in routed], [s[0] for s in streams],
                    jnp.concatenate([r[3] for r in routed], axis=1), w_gate, w_up, w_down, layer)
    outs = []
    row0 = 0
    for (h2, aff, x2, g), (idx, off, rows, _, cap) in zip(streams, routed):
        assert row0 % cap == 0
        outs.append(_moe_combine(off, y, idx, x2, g, row_block0=row0 // cap, cap=cap, seq_len=aff.shape[2]))
        row0 += rows.shape[2]
    return outs


HY_R = 128
HY_COLS = 4096
HY_KB = 8


def _dft_consts(r):
    k = np.arange(r)
    ang = 2.0 * np.pi * np.outer(k, k) / r
    fre, fim = np.cos(ang), -np.sin(ang)
    k2 = np.arange(r)
    tang = 2.0 * np.pi * np.outer(k2, k2) / (r * r)
    tw = np.stack([np.cos(tang), -np.sin(tang)])
    fwd = np.block([[fre, -fim], [fim, fre]])
    inv = np.block([[fre, fim], [-fim, fre]])
    return fre, fim, tw, fwd, inv


def _hy_prep_kernel(x_ref, prev_ref, next_ref, w_ref, b_ref, x1_ref, x2_ref, v_ref, *, n_tiles):
    t = pl.program_id(1)
    x = x_ref[0]
    tm = x.shape[0]
    row = lax.broadcasted_iota(jnp.int32, x.shape, 0)
    prev = jnp.where(t == 0, 0.0, prev_ref[0, pl.ds(7, 1), :])
    nxt = jnp.where(t == n_tiles - 1, 0.0, next_ref[0, pl.ds(0, 1), :])
    up = jnp.where(row == 0, prev, pltpu.roll(x, 1, 0))
    dn = jnp.where(row == tm - 1, nxt, pltpu.roll(x, tm - 1, 0))
    y = b_ref[...] + up * w_ref[pl.ds(0, 1), :] + x * w_ref[pl.ds(1, 1), :] + dn * w_ref[pl.ds(2, 1), :]
    x1_ref[0] = y[:, :HY_W].astype(BF16)
    x2_ref[0] = y[:, HY_W:2 * HY_W].astype(BF16)
    v_ref[0] = y[:, 2 * HY_W:].astype(BF16)


def _hyena_prep(u, conv_w, conv_b):
    b, l, c = u.shape
    tm = min(l, 1024)
    n_tiles = l // tm
    g = tm // 8
    n_groups = l // 8
    out = jax.ShapeDtypeStruct((b, l, HY_W), BF16)
    ospec = pl.BlockSpec((1, tm, HY_W), lambda bi, t: (bi, t, 0))
    return pl.pallas_call(
        functools.partial(_hy_prep_kernel, n_tiles=n_tiles),
        grid=(b, n_tiles),
        in_specs=[pl.BlockSpec((1, tm, c), lambda bi, t: (bi, t, 0)),
                  pl.BlockSpec((1, 8, c), lambda bi, t: (bi, jnp.maximum(t * g - 1, 0), 0)),
                  pl.BlockSpec((1, 8, c), lambda bi, t: (bi, jnp.minimum((t + 1) * g, n_groups - 1), 0)),
                  pl.BlockSpec((HY_SHORT, c), lambda bi, t: (0, 0)),
                  pl.BlockSpec((1, c), lambda bi, t: (0, 0))],
        out_specs=[ospec, ospec, ospec],
        out_shape=[out, out, out],
        compiler_params=_cparams(("arbitrary", "arbitrary")),
        name="hyena_prep",
    )(u, u, u, conv_w, conv_b.reshape(1, c))


def _hy_filter_kernel(z_ref, w1_ref, b1_ref, w2_ref, b2_ref, w3_ref, fr_ref, dl_ref, k_ref, ss_ref, *, seq_len):
    i = pl.program_id(0)
    hp = lax.Precision.HIGHEST
    z = z_ref[...]
    tm = z.shape[0]
    fr = fr_ref[...]
    h = jnp.sin(fr * (jnp.dot(z, w1_ref[...], preferred_element_type=F32, precision=hp) + b1_ref[...]))
    h = jnp.sin(fr * (jnp.dot(h, w2_ref[...], preferred_element_type=F32, precision=hp) + b2_ref[...]))
    h = jnp.dot(h, w3_ref[...], preferred_element_type=F32, precision=hp)
    dec = jnp.exp(-z[:, 0:1] * dl_ref[...])
    r = i * tm + lax.broadcasted_iota(jnp.int32, (tm, HY_W), 0)
    parts = []
    for o in range(HY_ORDER):
        hf = h[:, (2 * o) * HY_W:(2 * o + 1) * HY_W]
        hb = h[:, (2 * o + 1) * HY_W:(2 * o + 2) * HY_W]
        parts.append(jnp.where(r < seq_len, hf, jnp.where(r > seq_len, hb, 0.0)) * dec)
    k = jnp.concatenate(parts, axis=1)
    k_ref[...] = k
    ss = jnp.sum(k * k, axis=0, keepdims=True)

    @pl.when(i == 0)
    def _():
        ss_ref[...] = ss

    @pl.when(i != 0)
    def _():
        ss_ref[...] += ss


def _hyena_filter_taps(l, w1, b1, w2, b2, w3, freq):
    t = jnp.linspace(0.0, 1.0, l, dtype=F32)[:, None]
    w = (2.0 * math.pi / l) * jnp.arange(l, dtype=F32)[:, None]
    f = jnp.linspace(1e-4, HY_BANDS - 1, HY_BANDS, dtype=F32)[None, :]
    z = jnp.concatenate([t, jnp.cos(f * w), -jnp.sin(f * w)], axis=-1)
    z2 = jnp.concatenate([z, z[:1], z[:0:-1]], axis=0)
    emb = z.shape[1]
    z2 = jnp.pad(z2, ((0, 0), (0, LANES - emb)))
    w1p = jnp.pad(w1, ((0, LANES - emb), (0, 0)))
    deltas = jnp.abs(jnp.linspace(math.log(HY_DECAY_TARGET) / HY_SLOW_DECAY,
                                  math.log(HY_DECAY_TARGET) / HY_FAST_DECAY, HY_W, dtype=F32))
    fo = w2.shape[0]
    nout = w3.shape[1]
    tm = min(2 * l, 1024)
    const = lambda i: (0, 0)
    return pl.pallas_call(
        functools.partial(_hy_filter_kernel, seq_len=l),
        grid=(2 * l // tm,),
        in_specs=[pl.BlockSpec((tm, LANES), lambda i: (i, 0)),
                  pl.BlockSpec((LANES, fo), const), pl.BlockSpec((1, fo), const),
                  pl.BlockSpec((fo, fo), const), pl.BlockSpec((1, fo), const),
                  pl.BlockSpec((fo, nout), const), pl.BlockSpec((1, fo), const),
                  pl.BlockSpec((1, HY_W), const)],
        out_specs=[pl.BlockSpec((tm, HY_ORDER * HY_W), lambda i: (i, 0)),
                   pl.BlockSpec((1, HY_ORDER * HY_W), const)],
        out_shape=[jax.ShapeDtypeStruct((2 * l, HY_ORDER * HY_W), F32),
                   jax.ShapeDtypeStruct((1, HY_ORDER * HY_W), F32)],
        compiler_params=_cparams(("arbitrary",)),
        name="hyena_filter_taps",
    )(z2, w1p, b1.reshape(1, fo), w2, b2.reshape(1, fo), w3, freq.reshape(1, fo), deltas.reshape(1, HY_W))


def _left_matmul_kernel(f_ref, x_ref, o_ref):
    o_ref[0] = jnp.dot(f_ref[...], x_ref[0].astype(BF16), preferred_element_type=F32).astype(o_ref.dtype)


def _left_matmul(f_bf, x3, name):
    b, k, c = x3.shape
    m = f_bf.shape[0]
    return pl.pallas_call(
        _left_matmul_kernel,
        grid=(b, c // HY_COLS),
        in_specs=[pl.BlockSpec((m, k), lambda bi, j: (0, 0)),
                  pl.BlockSpec((1, k, HY_COLS), lambda bi, j: (bi, 0, j))],
        out_specs=pl.BlockSpec((1, m, HY_COLS), lambda bi, j: (bi, 0, j)),
        out_shape=jax.ShapeDtypeStruct((b, m, c), BF16),
        compiler_params=_cparams(("arbitrary", "arbitrary")),
        name=name,
    )(f_bf, x3)


def _twiddle_cols(tw_ref, k1):
    lane = lax.broadcasted_iota(jnp.int32, (HY_R, HY_R), 1)
    pick = lane == k1
    twr = jnp.sum(jnp.where(pick, tw_ref[0], 0.0), axis=1, keepdims=True)
    twi = jnp.sum(jnp.where(pick, tw_ref[1], 0.0), axis=1, keepdims=True)
    return twr, twi


def _hy_spectrum_kernel(a_ref, tw_ref, fwd_ref, ss_ref, kf_ref):
    kb = pl.program_id(0)
    scale = lax.rsqrt(ss_ref[...] + EPS) * (1.0 / (HY_R * HY_R))

    def body(kk, carry):
        twr, twi = _twiddle_cols(tw_ref, kb * HY_KB + kk)
        are = a_ref[0, 0, kk].astype(F32)
        aim = a_ref[0, 1, kk].astype(F32)
        bst = jnp.concatenate([are * twr - aim * twi, are * twi + aim * twr], axis=0).astype(BF16)
        x = jnp.dot(fwd_ref[...], bst, preferred_element_type=F32)
        kf_ref[0, kk] = x[:HY_R] * scale
        kf_ref[1, kk] = x[HY_R:] * scale
        return carry

    lax.fori_loop(0, HY_KB, body, 0)


def _hy_spectrum(a5, tw, fwd_bf, ss):
    nch = a5.shape[-1]
    return pl.pallas_call(
        _hy_spectrum_kernel,
        grid=(HY_R // HY_KB,),
        in_specs=[pl.BlockSpec((1, 2, HY_KB, HY_R, nch), lambda kb: (0, 0, kb, 0, 0)),
                  pl.BlockSpec((2, HY_R, HY_R), lambda kb: (0, 0, 0)),
                  pl.BlockSpec((2 * HY_R, 2 * HY_R), lambda kb: (0, 0)),
                  pl.BlockSpec((1, nch), lambda kb: (0, 0))],
        out_specs=pl.BlockSpec((2, HY_KB, HY_R, nch), lambda kb: (0, kb, 0, 0)),
        out_shape=jax.ShapeDtypeStruct((2, HY_R, HY_R, nch), F32),
        compiler_params=_cparams(("arbitrary",)),
        name="hyena_filter_spectrum",
    )(a5, tw, fwd_bf, ss)


def _hy_mid_kernel(a_ref, kf_ref, tw_ref, fwd_ref, inv_ref, z_ref):
    kb = pl.program_id(0)

    def body(kk, carry):
        twr, twi = _twiddle_cols(tw_ref, kb * HY_KB + kk)
        are = a_ref[0, 0, kk].astype(F32)
        aim = a_ref[0, 1, kk].astype(F32)
        bst = jnp.concatenate([are * twr - aim * twi, are * twi + aim * twr], axis=0).astype(BF16)
        x = jnp.dot(fwd_ref[...], bst, preferred_element_type=F32)
        xre, xim = x[:HY_R], x[HY_R:]
        kre, kim = kf_ref[0, kk], kf_ref[1, kk]
        yst = jnp.concatenate([xre * kre - xim * kim, xre * kim + xim * kre], axis=0).astype(BF16)
        zz = jnp.dot(inv_ref[...], yst, preferred_element_type=F32)
        zre, zim = zz[:HY_R], zz[HY_R:]
        z_ref[0, 0, kk] = (zre * twr + zim * twi).astype(BF16)
        z_ref[0, 1, kk] = (zim * twr - zre * twi).astype(BF16)
        return carry

    lax.fori_loop(0, HY_KB, body, 0)


def _hy_mid(a5, kf, order, tw, fwd_bf, inv_bf):
    b = a5.shape[0]
    blk = (1, 2, HY_KB, HY_R, HY_W)
    return pl.pallas_call(
        _hy_mid_kernel,
        grid=(HY_R // HY_KB, b),
        in_specs=[pl.BlockSpec(blk, lambda kb, bi: (bi, 0, kb, 0, 0)),
                  pl.BlockSpec((2, HY_KB, HY_R, HY_W), lambda kb, bi: (0, kb, 0, order)),
                  pl.BlockSpec((2, HY_R, HY_R), lambda kb, bi: (0, 0, 0)),
                  pl.BlockSpec((2 * HY_R, 2 * HY_R), lambda kb, bi: (0, 0)),
                  pl.BlockSpec((2 * HY_R, 2 * HY_R), lambda kb, bi: (0, 0))],
        out_specs=pl.BlockSpec(blk, lambda kb, bi: (bi, 0, kb, 0, 0)),
        out_shape=jax.ShapeDtypeStruct(a5.shape, BF16),
        compiler_params=_cparams(("arbitrary", "arbitrary")),
        name="hyena_spectral_product",
    )(a5, kf, tw, fwd_bf, inv_bf)


def _hy_out_kernel(f_ref, z_ref, g_ref, w_ref, sk_ref, o_ref):
    y = jnp.dot(f_ref[...], z_ref[0], preferred_element_type=F32)
    w = w_ref[0].astype(F32)
    o_ref[0] = (g_ref[0].astype(F32) * (y + w * sk_ref[...])).astype(BF16)


def _hy_out(f_bf, z3, gate3, w3, skip_row):
    b, m2, c = z3.shape
    m = f_bf.shape[0]
    dspec = pl.BlockSpec((1, m, HY_COLS), lambda bi, j: (bi, 0, j))
    return pl.pallas_call(
        _hy_out_kernel,
        grid=(b, c // HY_COLS),
        in_specs=[pl.BlockSpec((m, m2), lambda bi, j: (0, 0)),
                  pl.BlockSpec((1, m2, HY_COLS), lambda bi, j: (bi, 0, j)),
                  dspec, dspec,
                  pl.BlockSpec((1, HY_COLS), lambda bi, j: (0, 0))],
        out_specs=dspec,
        out_shape=jax.ShapeDtypeStruct((b, m, c), BF16),
        compiler_params=_cparams(("arbitrary", "arbitrary")),
        name="hyena_inverse_gate",
    )(f_bf, z3, gate3, w3, skip_row)


def _hyena_long(u, conv_w, conv_b, skip, filt_args):
    b, l, _ = u.shape
    assert 2 * l == HY_R * HY_R
    fre, fim, tw, fwd, inv = _dft_consts(HY_R)
    half = HY_R // 2
    as_bf = lambda m: jnp.asarray(m, F32).astype(BF16)
    f_a_data = as_bf(np.concatenate([fre[:, :half], fim[:, :half]], axis=0))
    f_a_filt = as_bf(np.concatenate([fre, fim], axis=0))
    f_c = as_bf(np.concatenate([fre[:half], fim[:half]], axis=1))
    tw_j = jnp.asarray(tw, F32)
    fwd_bf = as_bf(fwd)
    inv_bf = as_bf(inv)

    taps, ss = _hyena_filter_taps(l, *filt_args)
    nch = HY_ORDER * HY_W
    ak = _left_matmul(f_a_filt, taps.reshape(1, HY_R, HY_R * nch), "hyena_filter_dft_a")
    kf = _hy_spectrum(ak.reshape(1, 2, HY_R, HY_R, nch), tw_j, fwd_bf, ss)

    x1, x2, v = _hyena_prep(u, conv_w, conv_b)
    z = v
    for o, gate in enumerate((x1, x2)):
        z3 = z.reshape(b, half, HY_R * HY_W)
        a = _left_matmul(f_a_data, z3, "hyena_dft_a")
        zs = _hy_mid(a.reshape(b, 2, HY_R, HY_R, HY_W), kf, o, tw_j, fwd_bf, inv_bf)
        skip_row = jnp.tile(skip[o], HY_COLS // HY_W).reshape(1, HY_COLS)
        z = _hy_out(f_c, zs.reshape(b, 2 * HY_R, HY_R * HY_W), gate.reshape(b, half, HY_R * HY_W), z3,
                    skip_row).reshape(b, l, HY_W)
    return z


def _hy_ctx_kernel(x1_ref, x2_ref, v_ref, taps_ref, ss_ref, fa_ref, fc_ref, sk_ref, o_ref, *, seq_len):
    n2 = 2 * seq_len
    fa = fa_ref[...]
    scale = lax.rsqrt(ss_ref[...] + EPS) * (1.0 / n2)
    kf = jnp.dot(fa, taps_ref[...].astype(BF16), preferred_element_type=F32) * scale
    z = v_ref[0].astype(F32)
    for o, g_ref in enumerate((x1_ref, x2_ref)):
        x = jnp.dot(fa[:, :seq_len], z.astype(BF16), preferred_element_type=F32)
        xre, xim = x[:n2], x[n2:]
        kre = kf[:n2, o * HY_W:(o + 1) * HY_W]
        kim = kf[n2:, o * HY_W:(o + 1) * HY_W]
        yst = jnp.concatenate([xre * kre - xim * kim, xre * kim + xim * kre], axis=0).astype(BF16)
        y = jnp.dot(fc_ref[...], yst, preferred_element_type=F32)
        z = g_ref[0].astype(F32) * (y + z * sk_ref[pl.ds(o, 1), :])
    o_ref[0] = z.astype(BF16)


def _hyena_short(u, conv_w, conv_b, skip, filt_args):
    b, l, _ = u.shape
    n2 = 2 * l
    k = np.arange(n2)
    ang = 2.0 * np.pi * np.outer(k, k) / n2
    fre, fim = np.cos(ang), -np.sin(ang)
    fa = jnp.asarray(np.concatenate([fre, fim], axis=0), F32).astype(BF16)
    fc = jnp.asarray(np.concatenate([fre[:l], fim[:l]], axis=1), F32).astype(BF16)
    taps, ss = _hyena_filter_taps(l, *filt_args)
    x1, x2, v = _hyena_prep(u, conv_w, conv_b)
    nch = HY_ORDER * HY_W
    dspec = pl.BlockSpec((1, l, HY_W), lambda bi: (bi, 0, 0))
    const = lambda bi: (0, 0)
    return pl.pallas_call(
        functools.partial(_hy_ctx_kernel, seq_len=l),
        grid=(b,),
        in_specs=[dspec, dspec, dspec,
                  pl.BlockSpec((n2, nch), const), pl.BlockSpec((1, nch), const),
                  pl.BlockSpec((2 * n2, n2), const), pl.BlockSpec((l, 2 * n2), const),
                  pl.BlockSpec((HY_ORDER, HY_W), const)],
        out_specs=dspec,
        out_shape=jax.ShapeDtypeStruct((b, l, HY_W), BF16),
        compiler_params=_cparams(("arbitrary",)),
        name="hyena_context",
    )(x1, x2, v, taps, ss, fa, fc, skip)


def _rope_tables(n):
    t = jnp.arange(n, dtype=jnp.int32)
    pos = jnp.stack([t // GRID_W, t % GRID_W], axis=-1).astype(F32)
    nf = HEAD_DIM // 4
    inv = ROPE_THETA ** (-jnp.arange(nf, dtype=F32) / nf)
    ang = pos[:, :, None] * inv
    cos, sin = jnp.cos(ang), jnp.sin(ang)
    c64 = jnp.concatenate([cos[:, 0], cos[:, 0], cos[:, 1], cos[:, 1]], axis=-1)
    s64 = jnp.concatenate([-sin[:, 0], sin[:, 0], -sin[:, 1], sin[:, 1]], axis=-1)
    return jnp.tile(c64, (1, 2)), jnp.tile(s64, (1, 2))


def _block_diag(pool_w):
    g, c, _ = pool_w.shape
    out = jnp.zeros((g * c, g * c), pool_w.dtype)
    for i in range(g):
        out = out.at[i * c:(i + 1) * c, i * c:(i + 1) * c].set(pool_w[i])
    return out


def kernel(x, c, ctx, c_ctx, w_mod, b_mod, norm1_w, norm2_w, w_in, w_out, q_norm_w, k_norm_w, na_rpb,
           hy_conv_w, hy_conv_b, hy_w1, hy_b1, hy_w2, hy_b2, hy_w3, hy_freq, hy_skip,
           pool_w, pool_scale, w_router, w_gate, w_up, w_down):
    b, n, d = x.shape
    lc = ctx.shape[1]
    depth = w_mod.shape[0]
    rows = n // GRID_W
    cos128, sin128 = _rope_tables(n)
    seg = jnp.asarray(np.kron(np.eye(2), np.full((HEAD_DIM, HEAD_DIM), 1.0 / HEAD_DIM)), BF16)
    cc = jnp.zeros((8, d), F32).at[:b].set(c).at[b].set(c_ctx)

    xl = x.reshape(b * n, d)
    xc = ctx.reshape(b * lc, d)
    for i in range(depth):
        last = i == depth - 1
        mod = _modulation(cc, w_mod, b_mod[i], i)
        ml = mod[:b].reshape(b, 1, 6, d)
        mc = mod[b].reshape(1, 1, 6, d)
        sh1, sc1, g1, sh2, sc2, g2 = [ml[:, :, j] for j in range(6)]
        csh1, csc1, cg1, csh2, csc2, cg2 = [mc[:, :, j] for j in range(6)]
        w_in_bf = w_in[i].astype(BF16)
        w_out_bf = w_out[i].astype(BF16)
        w_router_t = w_router[i].T
        pool_bd = _block_diag(pool_w[i]).astype(BF16)
        hy_args = (hy_w1[i], hy_b1[i], hy_w2[i], hy_b2[i], hy_w3[i], hy_freq[i])

        qc, kc, vc, hyc, poolc = _in_projection(
            xc, csh1, csc1, norm1_w[i], w_in_bf, q_norm_w[i], k_norm_w[i], cos128, sin128, seg,
            rows_per_mod=b * lc, seq_len=lc, tm=lc, rope=False)
        kc = kc.reshape(b, lc, NA_W)
        vc = vc.reshape(b, lc, NA_W)
        if not last:
            na_c = _dense_attention(qc.reshape(b, lc, NA_W), kc, vc).reshape(b * lc, NA_W)
            hy_c = _hyena_short(hyc.reshape(b, lc, 3 * HY_W), hy_conv_w[i], hy_conv_b[i], hy_skip[i], hy_args)
            pool_c = _pool_mixer(poolc.reshape(b, lc, POOL_W), pool_bd, pool_scale[i])
            xc1, h2c, affc = _out_projection(
                na_c, hy_c.reshape(b * lc, HY_W).astype(BF16), pool_c.reshape(b * lc, POOL_W), w_out_bf,
                xc, cg1, norm2_w[i], csh2, csc2, w_router_t, rows_per_mod=b * lc, seq_len=lc, tm=lc)

        qr, qp, kl, vl, hyl, pooll = _in_projection(
            xl, sh1, sc1, norm1_w[i], w_in_bf, q_norm_w[i], k_norm_w[i], cos128, sin128, seg,
            rows_per_mod=n, seq_len=n, tm=512, rope=True)
        bias = _na_bias_tables(na_rpb[i], rows)
        o_na = _neighborhood_attention(qr.reshape(b, n, NA_W), qp.reshape(b, n, NA_W), kl.reshape(b, n, NA_W),
                                       vl.reshape(b, n, NA_W), kc, vc, bias)
        hy_l = _hyena_long(hyl.reshape(b, n, 3 * HY_W), hy_conv_w[i], hy_conv_b[i], hy_skip[i], hy_args)
        pool_l = _pool_mixer(pooll.reshape(b, n, POOL_W), pool_bd, pool_scale[i])
        xl1, h2l, affl = _out_projection(
            o_na.reshape(b * n, NA_W), hy_l.reshape(b * n, HY_W).astype(BF16), pool_l.reshape(b * n, POOL_W),
            w_out_bf, xl, g1, norm2_w[i], sh2, sc2, w_router_t, rows_per_mod=n, seq_len=n, tm=512)
        streams = [(h2l, affl, xl1, g2)]
        if not last:
            streams.append((h2c, affc, xc1, cg2))
        outs = _expert_choice_moe(streams, w_gate, w_up, w_down, i)
        xl = outs[0]
        if not last:
            xc = outs[1]
    return xl.reshape(b, n, d)
```

```python
import functools
import math

import jax
import jax.numpy as jnp
import numpy as np
from jax import lax
from jax.experimental import pallas as pl
from jax.experimental.pallas import tpu as pltpu

F32 = jnp.float32
BF16 = jnp.bfloat16

D_MODEL = 1024
GRID_W = 64
NA_HEADS = 8
HEAD_DIM = 64
NA_W = NA_HEADS * HEAD_DIM
NA_WIN_R = 8
NA_WIN_C = 16
ROPE_THETA = 10000.0
HY_W = 256
HY_ORDER = 2
HY_SHORT = 3
HY_BANDS = 16
HY_DECAY_TARGET = 1e-2
HY_FAST_DECAY = 0.3
HY_SLOW_DECAY = 1.5
POOL_W = 256
POOL_SIZES = (2, 4, 8, 16)
POOL_GROUP = POOL_W // len(POOL_SIZES)
MIX_W = NA_W + HY_W + POOL_W
IN_W = 3 * NA_W + (HY_ORDER + 1) * HY_W + POOL_W
N_EXPERTS = 16
EC_FACTOR = 2
D_EXPERT = 2048
EPS = 1e-6

LANES = 128
NEG_BIG = -1e30
VMEM_LIMIT = 56 * 1024 * 1024

NA_Q_ROWS = 4
NA_K_ROWS = NA_Q_ROWS + NA_WIN_R - 1
NA_TILES_PER_STEP = 2


def _cparams(sem):
    return pltpu.CompilerParams(dimension_semantics=sem, vmem_limit_bytes=VMEM_LIMIT)


def _mod_kernel(c_ref, w_ref, b_ref, o_ref):
    c = c_ref[...]
    s = c * jax.nn.sigmoid(c)
    o_ref[...] = jnp.dot(s, w_ref[0], preferred_element_type=F32,
                         precision=lax.Precision.HIGHEST) + b_ref[...]


def _modulation(cc, w_mod, b_mod, layer):
    rows, d = cc.shape
    n = w_mod.shape[2]
    tn = 1536
    return pl.pallas_call(
        _mod_kernel,
        grid=(n // tn,),
        in_specs=[pl.BlockSpec((rows, d), lambda j: (0, 0)),
                  pl.BlockSpec((1, d, tn), lambda j: (layer, 0, j)),
                  pl.BlockSpec((1, tn), lambda j: (0, j))],
        out_specs=pl.BlockSpec((rows, tn), lambda j: (0, j)),
        out_shape=jax.ShapeDtypeStruct((rows, n), F32),
        compiler_params=_cparams(("arbitrary",)),
        name="modulation",
    )(cc, w_mod, b_mod.reshape(1, n))


def _head_norm(xs, w128, seg):
    sq = xs * xs
    hi = sq.astype(BF16)
    lo = (sq - hi.astype(F32)).astype(BF16)
    m = (jnp.dot(hi, seg, preferred_element_type=F32) + jnp.dot(lo, seg, preferred_element_type=F32))
    return xs * lax.rsqrt(m + EPS) * w128


def _rope(xs, c, s_signed):
    lane = lax.broadcasted_iota(jnp.int32, xs.shape, 1)
    partner = jnp.where((lane & 16) == 0, pltpu.roll(xs, LANES - 16, 1), pltpu.roll(xs, 16, 1))
    return xs * c + partner * s_signed


def _inproj_kernel(x_ref, sh_ref, sc_ref, nw_ref, w_ref, qw_ref, kw_ref, cos_ref, sin_ref, seg_ref,
                   *out_refs, rope):
    if rope:
        qr_ref, qp_ref, k_ref, v_ref, hy_ref, pool_ref = out_refs
    else:
        qp_ref, k_ref, v_ref, hy_ref, pool_ref = out_refs
    x = x_ref[...]
    ms = jnp.mean(x * x, axis=-1, keepdims=True)
    h = x * lax.rsqrt(ms + EPS) * nw_ref[...]
    h = h * (1.0 + sc_ref[0]) + sh_ref[0]
    u = jnp.dot(h.astype(BF16), w_ref[...], preferred_element_type=F32)
    seg = seg_ref[...]
    qk_scale = HEAD_DIM ** -0.5
    for ch in range(NA_W // LANES):
        sl = slice(ch * LANES, (ch + 1) * LANES)
        qn = _head_norm(u[:, sl], qw_ref[...], seg)
        kn = _head_norm(u[:, NA_W + ch * LANES:NA_W + (ch + 1) * LANES], kw_ref[...], seg)
        qp_ref[:, sl] = (qn * qk_scale).astype(BF16)
        if rope:
            c = cos_ref[...]
            s = sin_ref[...]
            qr_ref[:, sl] = (_rope(qn, c, s) * qk_scale).astype(BF16)
            k_ref[:, sl] = _rope(kn, c, s).astype(BF16)
        else:
            k_ref[:, sl] = kn.astype(BF16)
    v_ref[...] = u[:, 2 * NA_W:3 * NA_W].astype(BF16)
    hy_ref[...] = u[:, 3 * NA_W:3 * NA_W + 3 * HY_W]
    pool_ref[...] = u[:, 3 * NA_W + 3 * HY_W:]


def _in_projection(x2, shift, scale, norm_w, w_in_bf, q_norm_w, k_norm_w, cos128, sin128, seg, *,
                   rows_per_mod, seq_len, tm, rope):
    m, d = x2.shape
    tiles_per_mod = rows_per_mod // tm
    tiles_per_seq = seq_len // tm
    row = lambda i: (i, 0)
    const = lambda i: (0, 0)
    modi = lambda i: (i // tiles_per_mod, 0, 0)
    pos = lambda i: (i % tiles_per_seq, 0)
    outs = []
    if rope:
        outs.append((NA_W, BF16))
    outs += [(NA_W, BF16), (NA_W, BF16), (NA_W, BF16), (3 * HY_W, F32), (POOL_W, F32)]
    return pl.pallas_call(
        functools.partial(_inproj_kernel, rope=rope),
        grid=(m // tm,),
        in_specs=[pl.BlockSpec((tm, d), row),
                  pl.BlockSpec((1, 1, d), modi), pl.BlockSpec((1, 1, d), modi),
                  pl.BlockSpec((1, d), const),
                  pl.BlockSpec((d, IN_W), const),
                  pl.BlockSpec((1, LANES), const), pl.BlockSpec((1, LANES), const),
                  pl.BlockSpec((tm, LANES), pos), pl.BlockSpec((tm, LANES), pos),
                  pl.BlockSpec((LANES, LANES), const)],
        out_specs=[pl.BlockSpec((tm, w), row) for w, _ in outs],
        out_shape=[jax.ShapeDtypeStruct((m, w), dt) for w, dt in outs],
        compiler_params=_cparams(("arbitrary",)),
        name="in_projection_rope" if rope else "in_projection_ctx",
    )(x2, shift, scale, norm_w.reshape(1, d), w_in_bf,
      jnp.tile(q_norm_w, 2).reshape(1, LANES), jnp.tile(k_norm_w, 2).reshape(1, LANES),
      cos128, sin128, seg)


def _softmax_pv(s_list, v_list):
    mx = functools.reduce(jnp.maximum, [jnp.max(s, axis=-1, keepdims=True) for s in s_list])
    ps = [jnp.exp(s - mx) for s in s_list]
    l = functools.reduce(jnp.add, [jnp.sum(p, axis=-1, keepdims=True) for p in ps])
    o = functools.reduce(jnp.add, [jnp.dot(p.astype(BF16), v, preferred_element_type=F32)
                                   for p, v in zip(ps, v_list)])
    return o / l


_NT = (((1,), (1,)), ((), ()))


def _na_kernel(qr_ref, qp_ref, k_ref, v_ref, kc_ref, vc_ref, bias_ref, o_ref, *, rows):
    n_tiles = rows // NA_Q_ROWS
    tq = NA_Q_ROWS * GRID_W
    kc = kc_ref[0]
    vc = vc_ref[0]
    lane = lax.broadcasted_iota(jnp.int32, (tq, LANES), 1)
    for sub in range(NA_TILES_PER_STEP):
        t = pl.program_id(2) * NA_TILES_PER_STEP + sub
        ks = jnp.clip(t * NA_Q_ROWS - NA_WIN_R // 2, 0, rows - NA_K_ROWS)
        start = pl.multiple_of(ks * GRID_W, GRID_W)
        k = k_ref[0, pl.ds(start, NA_K_ROWS * GRID_W), :]
        v = v_ref[0, pl.ds(start, NA_K_ROWS * GRID_W), :]
        q = qr_ref[0, pl.ds(sub * tq, tq), :]
        qp = qp_ref[0, pl.ds(sub * tq, tq), :]
        kind = jnp.where(t == 0, 0, jnp.where(t == n_tiles - 1, 2, 1))
        outs = []
        for hh in range(2):
            keep = (lane < HEAD_DIM) if hh == 0 else (lane >= HEAD_DIM)
            zero = jnp.zeros_like(q)
            s_lat = (lax.dot_general(jnp.where(keep, q, zero), k, _NT, preferred_element_type=F32)
                     + bias_ref[kind, hh])
            s_ctx = lax.dot_general(jnp.where(keep, qp, zero), kc, _NT, preferred_element_type=F32)
            outs.append(_softmax_pv([s_lat, s_ctx], [v, vc]))
        o_ref[0, pl.ds(sub * tq, tq), :] = jnp.where(lane < HEAD_DIM, outs[0], outs[1]).astype(BF16)


def _neighborhood_attention(qr, qp, k, v, kc, vc, bias):
    b, n, _ = qr.shape
    lc = kc.shape[1]
    rows = n // GRID_W
    n_steps = rows // (NA_Q_ROWS * NA_TILES_PER_STEP)
    tq = NA_Q_ROWS * GRID_W * NA_TILES_PER_STEP
    tk = NA_K_ROWS * GRID_W
    qmap = lambda bi, hp, t: (bi, t, hp)
    smap = lambda bi, hp, t: (bi, 0, hp)
    bmap = lambda bi, hp, t: (0, hp, 0, 0)
    return pl.pallas_call(
        functools.partial(_na_kernel, rows=rows),
        grid=(b, NA_W // LANES, n_steps),
        in_specs=[pl.BlockSpec((1, tq, LANES), qmap), pl.BlockSpec((1, tq, LANES), qmap),
                  pl.BlockSpec((1, n, LANES), smap), pl.BlockSpec((1, n, LANES), smap),
                  pl.BlockSpec((1, lc, LANES), smap), pl.BlockSpec((1, lc, LANES), smap),
                  pl.BlockSpec((3, 2, NA_Q_ROWS * GRID_W, tk), bmap)],
        out_specs=pl.BlockSpec((1, tq, LANES), qmap),
        out_shape=jax.ShapeDtypeStruct((b, n, NA_W), BF16),
        compiler_params=_cparams(("arbitrary", "arbitrary", "arbitrary")),
        name="neighborhood_attention",
    )(qr, qp, k, v, kc, vc, bias)


def _na_bias_tables(rpb, rows):
    n_tiles = rows // NA_Q_ROWS
    n_dr, n_dc = 2 * NA_WIN_R - 1, 2 * NA_WIN_C - 1
    qc = np.arange(GRID_W)
    cs = np.clip(qc - NA_WIN_C // 2, 0, GRID_W - NA_WIN_C)
    kcol = np.arange(GRID_W)
    ok_c = (kcol[None, :] >= cs[:, None]) & (kcol[None, :] < cs[:, None] + NA_WIN_C)
    dc = kcol[None, :] - qc[:, None] + NA_WIN_C - 1
    col_sel = (dc[:, :, None] == np.arange(n_dc)) & ok_c[:, :, None]
    row_sel, masks = [], []
    for t in (0, 1, n_tiles - 1):
        r0 = t * NA_Q_ROWS
        ks = int(np.clip(r0 - NA_WIN_R // 2, 0, rows - NA_K_ROWS))
        r = r0 + np.arange(NA_Q_ROWS)
        rs = np.clip(r - NA_WIN_R // 2, 0, rows - NA_WIN_R)
        krow = ks + np.arange(NA_K_ROWS)
        ok_r = (krow[None, :] >= rs[:, None]) & (krow[None, :] < rs[:, None] + NA_WIN_R)
        dr = krow[None, :] - r[:, None] + NA_WIN_R - 1
        row_sel.append((dr[:, :, None] == np.arange(n_dr)) & ok_r[:, :, None])
        ok = ok_r[:, None, :, None] & ok_c[None, :, None, :]
        masks.append(np.where(ok, 0.0, NEG_BIG))
    tab = jnp.einsum("hrd,tijr,qkd->thiqjk", rpb.astype(F32), jnp.asarray(np.stack(row_sel), F32),
                     jnp.asarray(col_sel, F32), precision=lax.Precision.HIGHEST)
    tab = tab + jnp.asarray(np.stack(masks), F32)[:, None]
    return tab.reshape(3, NA_HEADS, NA_Q_ROWS * GRID_W, NA_K_ROWS * GRID_W)


def _dense_attn_kernel(q_ref, k_ref, v_ref, o_ref):
    q = q_ref[0]
    k = k_ref[0]
    v = v_ref[0]
    lane = lax.broadcasted_iota(jnp.int32, q.shape, 1)
    outs = []
    for hh in range(2):
        keep = (lane < HEAD_DIM) if hh == 0 else (lane >= HEAD_DIM)
        s = lax.dot_general(jnp.where(keep, q, jnp.zeros_like(q)), k, _NT, preferred_element_type=F32)
        outs.append(_softmax_pv([s], [v]))
    o_ref[0] = jnp.where(lane < HEAD_DIM, outs[0], outs[1]).astype(BF16)


def _dense_attention(q, k, v):
    b, l, _ = q.shape
    spec = pl.BlockSpec((1, l, LANES), lambda bi, hp: (bi, 0, hp))
    return pl.pallas_call(
        _dense_attn_kernel,
        grid=(b, NA_W // LANES),
        in_specs=[spec, spec, spec],
        out_specs=spec,
        out_shape=jax.ShapeDtypeStruct((b, l, NA_W), BF16),
        compiler_params=_cparams(("arbitrary", "arbitrary")),
        name="context_attention",
    )(q, k, v)


POOL_HALO = max(POOL_SIZES) // 2


def _pool_kernel(x_ref, w_ref, sc_ref, o_ref, xpad, *, seq_len, chunk):
    zeros = jnp.zeros((POOL_HALO, POOL_W), F32)
    xpad[pl.ds(0, POOL_HALO), :] = zeros
    xpad[pl.ds(seq_len + POOL_HALO, POOL_HALO), :] = zeros
    xpad[pl.ds(POOL_HALO, seq_len), :] = x_ref[0]
    span = chunk + 2 * POOL_HALO
    w = w_ref[...]
    sc = sc_ref[...]

    def body(c, carry):
        base = pl.multiple_of(c * chunk, chunk)
        xs = xpad[pl.ds(base, span), :]
        s2 = xs + pltpu.roll(xs, 1, 0)
        s4 = pltpu.roll(s2, 1, 0) + pltpu.roll(s2, span - 1, 0)
        s8 = pltpu.roll(s4, 2, 0) + pltpu.roll(s4, span - 2, 0)
        s16 = pltpu.roll(s8, 4, 0) + pltpu.roll(s8, span - 4, 0)
        mid = slice(POOL_HALO, POOL_HALO + chunk)
        grp = lax.broadcasted_iota(jnp.int32, (chunk, POOL_W), 1) // POOL_GROUP
        tpos = base + lax.broadcasted_iota(jnp.int32, (chunk, POOL_W), 0)
        half = jnp.left_shift(1, grp)
        cnt = (jnp.minimum(tpos + half, seq_len) - jnp.maximum(tpos - half, 0)).astype(F32)
        ssum = jnp.where(grp == 0, s2[mid], jnp.where(grp == 1, s4[mid], jnp.where(grp == 2, s8[mid], s16[mid])))
        diff = ssum / cnt - xs[mid]
        y = jnp.dot(diff.astype(BF16), w, preferred_element_type=F32) * sc
        o_ref[0, pl.ds(base, chunk), :] = y.astype(BF16)
        return carry

    lax.fori_loop(0, seq_len // chunk, body, 0)


def _pool_mixer(u, w_blockdiag_bf, pool_scale):
    b, l, _ = u.shape
    chunk = min(l, 512)
    return pl.pallas_call(
        functools.partial(_pool_kernel, seq_len=l, chunk=chunk),
        grid=(b,),
        in_specs=[pl.BlockSpec((1, l, POOL_W), lambda bi: (bi, 0, 0)),
                  pl.BlockSpec((POOL_W, POOL_W), lambda bi: (0, 0)),
                  pl.BlockSpec((1, POOL_W), lambda bi: (0, 0))],
        out_specs=pl.BlockSpec((1, l, POOL_W), lambda bi: (bi, 0, 0)),
        out_shape=jax.ShapeDtypeStruct((b, l, POOL_W), BF16),
        scratch_shapes=[pltpu.VMEM((l + 2 * POOL_HALO, POOL_W), F32)],
        compiler_params=_cparams(("arbitrary",)),
        name="pool_mixer",
    )(u, w_blockdiag_bf, pool_scale.reshape(1, POOL_W))


def _outproj_kernel(na_ref, hy_ref, pl_ref, w_ref, x_ref, g1_ref, nw_ref, sh_ref, sc_ref, wr_ref,
                    xn_ref, h2_ref, aff_ref):
    w = w_ref
    acc = jnp.dot(na_ref[...], w[pl.ds(0, NA_W), :], preferred_element_type=F32)
    acc += jnp.dot(hy_ref[...], w[pl.ds(NA_W, HY_W), :], preferred_element_type=F32)
    acc += jnp.dot(pl_ref[...], w[pl.ds(NA_W + HY_W, POOL_W), :], preferred_element_type=F32)
    xn = x_ref[...] + g1_ref[0] * acc
    xn_ref[...] = xn
    ms = jnp.mean(xn * xn, axis=-1, keepdims=True)
    h2 = xn * lax.rsqrt(ms + EPS) * nw_ref[...]
    h2 = h2 * (1.0 + sc_ref[0]) + sh_ref[0]
    h2_ref[...] = h2
    logits = lax.dot_general(wr_ref[...], h2, _NT, preferred_element_type=F32,
                             precision=lax.Precision.HIGHEST)
    mx = jnp.max(logits, axis=0, keepdims=True)
    p = jnp.exp(logits - mx)
    aff_ref[0] = p / jnp.sum(p, axis=0, keepdims=True)


def _out_projection(na, hy, pool, w_out_bf, x2, g1, norm_w, shift, scale, w_router_t, *,
                    rows_per_mod, seq_len, tm):
    m, d = x2.shape
    tiles_per_mod = rows_per_mod // tm
    tiles_per_seq = seq_len // tm
    row = lambda i: (i, 0)
    const = lambda i: (0, 0)
    modi = lambda i: (i // tiles_per_mod, 0, 0)
    return pl.pallas_call(
        _outproj_kernel,
        grid=(m // tm,),
        in_specs=[pl.BlockSpec((tm, NA_W), row), pl.BlockSpec((tm, HY_W), row), pl.BlockSpec((tm, POOL_W), row),
                  pl.BlockSpec((MIX_W, d), const),
                  pl.BlockSpec((tm, d), row),
                  pl.BlockSpec((1, 1, d), modi),
                  pl.BlockSpec((1, d), const),
                  pl.BlockSpec((1, 1, d), modi), pl.BlockSpec((1, 1, d), modi),
                  pl.BlockSpec((N_EXPERTS, d), const)],
        out_specs=[pl.BlockSpec((tm, d), row), pl.BlockSpec((tm, d), row),
                   pl.BlockSpec((1, N_EXPERTS, tm), lambda i: (i // tiles_per_seq, 0, i % tiles_per_seq))],
        out_shape=[jax.ShapeDtypeStruct((m, d), F32), jax.ShapeDtypeStruct((m, d), F32),
                   jax.ShapeDtypeStruct((m // seq_len, N_EXPERTS, seq_len), F32)],
        compiler_params=_cparams(("arbitrary",)),
        name="out_projection_router",
    )(na, hy, pool, w_out_bf, x2, g1, norm_w.reshape(1, d), shift, scale, w_router_t)


FFN_TF = 256
FFN_ROW_CHUNKS = 4


def _ffn_kernel(*refs, n_experts, sources, n_rows):
    ns = len(sources)
    idx_cur = refs[0:ns]
    idx_nxt = refs[ns:2 * ns]
    src = refs[2 * ns:3 * ns]
    gate_ref, wg_ref, wu_ref, wd_ref, o_ref, stage, xb, acc, sem = refs[3 * ns:]
    e = pl.program_id(0)
    f = pl.program_id(1)
    n_f = pl.num_programs(1)

    def row_copy(s, row, r):
        return pltpu.make_async_copy(src[s].at[pl.ds(row, 1)], stage.at[pl.ds(sources[s][0] + r, 1)], sem.at[0])

    def gather(idx_refs):
        for s in range(ns):
            def body(r, carry, s=s):
                row_copy(s, idx_refs[s][0, 0, r], r).start()
                return carry
            lax.fori_loop(0, sources[s][1], body, 0, unroll=8)

    def gather_wait():
        for s in range(ns):
            off, cnt = sources[s]
            pltpu.make_async_copy(src[s].at[pl.ds(0, cnt)], stage.at[pl.ds(off, cnt)], sem.at[0]).wait()

    @pl.when(f == 0)
    def _():
        @pl.when(e == 0)
        def _():
            gather(idx_cur)

        gather_wait()
        xb[...] = stage[...].astype(BF16)

        @pl.when(e + 1 < n_experts)
        def _():
            gather(idx_nxt)

    wg = wg_ref[0, 0].astype(BF16)
    wu = wu_ref[0, 0].astype(BF16)
    wd = wd_ref[0, 0].astype(BF16)
    rc = n_rows // FFN_ROW_CHUNKS
    for r in range(FFN_ROW_CHUNKS):
        rs = pl.ds(r * rc, rc)
        x = xb[rs, :]
        a = jnp.dot(x, wg, preferred_element_type=F32)
        b = jnp.dot(x, wu, preferred_element_type=F32)
        h = (a * jax.nn.sigmoid(a) * b).astype(BF16)
        y = jnp.dot(h, wd, preferred_element_type=F32)

        @pl.when(f == 0)
        def _():
            acc[rs, :] = y

        @pl.when(f != 0)
        def _():
            acc[rs, :] += y

    @pl.when(f == n_f - 1)
    def _():
        o_ref[0] = (acc[...] * gate_ref[0]).astype(BF16)


def _expert_ffn(rows_idx, tokens, gate, w_gate, w_up, w_down, layer):
    ns = len(tokens)
    e, d = w_gate.shape[1], w_gate.shape[2]
    counts = [int(ri.shape[2]) for ri in rows_idx]
    offs = [int(sum(counts[:s])) for s in range(ns)]
    r_total = sum(counts)
    assert r_total % (FFN_ROW_CHUNKS * 16) == 0
    smem = lambda cnt, nxt: pl.BlockSpec(
        (1, 1, cnt), (lambda ei, f: (jnp.minimum(ei + 1, e - 1), 0, 0)) if nxt else (lambda ei, f: (ei, 0, 0)),
        memory_space=pltpu.SMEM)
    in_specs = ([smem(cnt, False) for cnt in counts] + [smem(cnt, True) for cnt in counts]
                + [pl.BlockSpec(memory_space=pl.ANY) for _ in range(ns)]
                + [pl.BlockSpec((1, r_total, 1), lambda ei, f: (ei, 0, 0)),
                   pl.BlockSpec((1, 1, d, FFN_TF), lambda ei, f: (layer, ei, 0, f)),
                   pl.BlockSpec((1, 1, d, FFN_TF), lambda ei, f: (layer, ei, 0, f)),
                   pl.BlockSpec((1, 1, FFN_TF, d), lambda ei, f: (layer, ei, f, 0))])
    return pl.pallas_call(
        functools.partial(_ffn_kernel, n_experts=e, sources=tuple(zip(offs, counts)), n_rows=r_total),
        grid=(e, D_EXPERT // FFN_TF),
        in_specs=in_specs,
        out_specs=pl.BlockSpec((1, r_total, d), lambda ei, f: (ei, 0, 0)),
        out_shape=jax.ShapeDtypeStruct((e, r_total, d), BF16),
        scratch_shapes=[pltpu.VMEM((r_total, d), F32), pltpu.VMEM((r_total, d), BF16),
                        pltpu.VMEM((r_total, d), F32), pltpu.SemaphoreType.DMA((1,))],
        compiler_params=_cparams(("arbitrary", "arbitrary")),
        name="expert_ffn",
    )(*rows_idx, *rows_idx, *tokens, gate, w_gate, w_up, w_down)


def _prefix_count(m, n):
    lane = lax.broadcasted_iota(jnp.int32, m.shape, 1)
    c = m
    s = 1
    while s < n:
        c = c + jnp.where(lane >= s, pltpu.roll(c, s, 1), 0)
        s *= 2
    return c


COMBINE_TB = 256


def _topk_kernel(aff_ref, idx_ref, gate_ref, off_ref, *, cap):
    a = aff_ref[...]
    rows, n = a.shape

    def search(i, t):
        cand = t | jnp.left_shift(jnp.int32(1), 30 - i)
        cnt = jnp.sum((a >= pltpu.bitcast(cand, F32)).astype(jnp.int32), axis=1, keepdims=True)
        return jnp.where(cnt >= cap, cand, t)

    thr_bits = lax.fori_loop(0, 31, search, jnp.zeros((rows, 1), jnp.int32))
    thr = pltpu.bitcast(thr_bits, F32)
    gt = a >= pltpu.bitcast(thr_bits + 1, F32)
    eq = (a >= thr) & jnp.logical_not(gt)
    need = cap - jnp.sum(gt.astype(jnp.int32), axis=1, keepdims=True)
    eq_i = eq.astype(jnp.int32)
    sel = gt | (eq & (_prefix_count(eq_i, n) - eq_i < need))
    sel_i = sel.astype(jnp.int32)
    lane = lax.broadcasted_iota(jnp.int32, a.shape, 1)
    dist = jnp.where(sel, lane + 1 - _prefix_count(sel_i, n), 0)
    tok = jnp.where(sel, lane, -1)
    val = a
    s = 1
    while s < n:
        tok_in = pltpu.roll(tok, n - s, 1)
        dist_in = pltpu.roll(dist, n - s, 1)
        val_in = pltpu.roll(val, n - s, 1)
        take = (lane < n - s) & (tok_in >= 0) & ((dist_in & s) != 0)
        stay = (tok >= 0) & ((dist & s) == 0)
        tok = jnp.where(take, tok_in, jnp.where(stay, tok, -1))
        dist = jnp.where(take, dist_in, jnp.where(stay, dist, 0))
        val = jnp.where(take, val_in, val)
        s *= 2
    idx_ref[...] = tok[:, :cap]
    gate_ref[...] = val[:, :cap]
    olane = lax.broadcasted_iota(jnp.int32, off_ref.shape, 1)
    off = jnp.zeros(off_ref.shape, jnp.int32)
    for j in range(n // COMBINE_TB + 1):
        below = jnp.sum(jnp.where(lane < j * COMBINE_TB, sel_i, 0), axis=1, keepdims=True)
        off = jnp.where(olane == j, below, off)
    off_ref[...] = off


def _topk_select(aff, cap):
    b, e, n = aff.shape
    assert n & (n - 1) == 0 and n // COMBINE_TB < LANES
    rows = b * e
    full = lambda w: pl.BlockSpec((rows, w), lambda i: (0, 0))
    idx, gate, off = pl.pallas_call(
        functools.partial(_topk_kernel, cap=cap),
        grid=(1,),
        in_specs=[full(n)],
        out_specs=[full(cap), full(cap), full(LANES)],
        out_shape=[jax.ShapeDtypeStruct((rows, cap), jnp.int32), jax.ShapeDtypeStruct((rows, cap), F32),
                   jax.ShapeDtypeStruct((rows, LANES), jnp.int32)],
        compiler_params=_cparams(("arbitrary",)),
        name="expert_choice_topk",
    )(aff.reshape(rows, n))
    return idx.reshape(b, e, cap), gate.reshape(b, e, cap), off


def _combine_kernel(off_ref, y_ref, idx_ref, x_ref, g_ref, o_ref, acc, ycat, *, chunk, n_chunks):
    bi = pl.program_id(0)
    j = pl.program_id(1)
    tb = x_ref.shape[0]
    tok = j * tb + lax.broadcasted_iota(jnp.int32, (tb, chunk), 0)

    def onehot(e, st):
        return (idx_ref[0, pl.ds(e, 1), pl.ds(st, chunk)] == tok).astype(BF16)

    if n_chunks == 1:
        acc[...] = jnp.zeros_like(acc)
        for e in range(N_EXPERTS):
            acc[...] += jnp.dot(onehot(e, 0), y_ref[e], preferred_element_type=F32)
    else:
        shift = chunk.bit_length() - 1
        spans, hots = [], []
        for e in range(N_EXPERTS):
            lo = off_ref[bi * N_EXPERTS + e, j]
            hi = off_ref[bi * N_EXPERTS + e, j + 1]
            c0 = jnp.minimum(lax.shift_right_logical(lo, shift), n_chunks - 1)
            c1 = lax.shift_right_logical(hi + (chunk - 1), shift)
            spans.append((c0, c1))
            st = pl.multiple_of(c0 * chunk, chunk)
            ycat[pl.ds(e * chunk, chunk), :] = y_ref[e, pl.ds(st, chunk), :]
            hots.append(onehot(e, st))
        acc[...] = jnp.dot(jnp.concatenate(hots, axis=1), ycat[...], preferred_element_type=F32)
        for e in range(N_EXPERTS):
            def add_chunk(c, carry, e=e):
                st = pl.multiple_of(c * chunk, chunk)
                acc[...] += jnp.dot(onehot(e, st), y_ref[e, pl.ds(st, chunk), :], preferred_element_type=F32)
                return carry

            lax.fori_loop(spans[e][0] + 1, spans[e][1], add_chunk, 0)
    o_ref[...] = x_ref[...] + g_ref[0] * acc[...]


def _moe_combine(off, y, idx, x2, g, *, row_block0, cap, seq_len):
    bsz = idx.shape[0]
    d = x2.shape[1]
    tb = min(seq_len, COMBINE_TB)
    nb = seq_len // tb
    chunk = min(cap, LANES)
    n_chunks = cap // chunk
    n_mod = g.shape[0]
    grid_spec = pltpu.PrefetchScalarGridSpec(
        num_scalar_prefetch=1,
        grid=(bsz, nb),
        in_specs=[pl.BlockSpec((N_EXPERTS, cap, d), lambda bi, j, off: (0, row_block0 + bi, 0),
                               pipeline_mode=pl.Buffered(1)),
                  pl.BlockSpec((1, N_EXPERTS, cap), lambda bi, j, off: (bi, 0, 0)),
                  pl.BlockSpec((tb, d), lambda bi, j, off: (bi * nb + j, 0)),
                  pl.BlockSpec((1, 1, d), lambda bi, j, off: (bi if n_mod > 1 else 0, 0, 0))],
        out_specs=pl.BlockSpec((tb, d), lambda bi, j, off: (bi * nb + j, 0)),
        scratch_shapes=[pltpu.VMEM((tb, d), F32), pltpu.VMEM((N_EXPERTS * chunk, d), BF16)])
    return pl.pallas_call(
        functools.partial(_combine_kernel, chunk=chunk, n_chunks=n_chunks),
        grid_spec=grid_spec,
        out_shape=jax.ShapeDtypeStruct(x2.shape, F32),
        compiler_params=_cparams(("arbitrary", "arbitrary")),
        name="moe_combine",
    )(off, y, idx, x2, g)


def _expert_choice_moe(streams, w_gate, w_up, w_down, layer):
    routed = []
    for h2, aff, x2, g in streams:
        bsz, _, l = aff.shape
        cap = EC_FACTOR * l // N_EXPERTS
        idx, gate, off = _topk_select(aff, cap)
        rows = (idx + (jnp.arange(bsz, dtype=jnp.int32) * l)[:, None, None]).transpose(1, 0, 2)
        routed.append((idx, off, rows.reshape(N_EXPERTS, 1, bsz * cap), gate.transpose(1, 0, 2).reshape(
            N_EXPERTS, bsz * cap, 1), cap))
    y = _expert_ffn([r[2] for r in routed], [s[0] for s in streams],
                    jnp.concatenate([r[3] for r in routed], axis=1), w_gate, w_up, w_down, layer)
    outs = []
    row0 = 0
    for (h2, aff, x2, g), (idx, off, rows, _, cap) in zip(streams, routed):
        assert row0 % cap == 0
        outs.append(_moe_combine(off, y, idx, x2, g, row_block0=row0 // cap, cap=cap, seq_len=aff.shape[2]))
        row0 += rows.shape[2]
    return outs


HY_R = 128
HY_COLS = 4096
HY_KB = 8
HY_UNROLL = 4


def _dft_consts(r):
    k = np.arange(r)
    ang = 2.0 * np.pi * np.outer(k, k) / r
    fre, fim = np.cos(ang), -np.sin(ang)
    k2 = np.arange(r)
    tang = 2.0 * np.pi * np.outer(k2, k2) / (r * r)
    tw = np.stack([np.cos(tang), -np.sin(tang)])
    fwd = np.block([[fre, -fim], [fim, fre]])
    inv = np.block([[fre, fim], [-fim, fre]])
    return fre, fim, tw, fwd, inv


def _hy_prep_kernel(x_ref, prev_ref, next_ref, w_ref, b_ref, x1_ref, x2_ref, v_ref, *, n_tiles):
    t = pl.program_id(1)
    x = x_ref[0]
    tm = x.shape[0]
    row = lax.broadcasted_iota(jnp.int32, x.shape, 0)
    prev = jnp.where(t == 0, 0.0, prev_ref[0, pl.ds(7, 1), :])
    nxt = jnp.where(t == n_tiles - 1, 0.0, next_ref[0, pl.ds(0, 1), :])
    up = jnp.where(row == 0, prev, pltpu.roll(x, 1, 0))
    dn = jnp.where(row == tm - 1, nxt, pltpu.roll(x, tm - 1, 0))
    y = b_ref[...] + up * w_ref[pl.ds(0, 1), :] + x * w_ref[pl.ds(1, 1), :] + dn * w_ref[pl.ds(2, 1), :]
    x1_ref[0] = y[:, :HY_W].astype(BF16)
    x2_ref[0] = y[:, HY_W:2 * HY_W].astype(BF16)
    v_ref[0] = y[:, 2 * HY_W:].astype(BF16)


def _hyena_prep(u, conv_w, conv_b):
    b, l, c = u.shape
    tm = min(l, 1024)
    n_tiles = l // tm
    g = tm // 8
    n_groups = l // 8
    out = jax.ShapeDtypeStruct((b, l, HY_W), BF16)
    ospec = pl.BlockSpec((1, tm, HY_W), lambda bi, t: (bi, t, 0))
    return pl.pallas_call(
        functools.partial(_hy_prep_kernel, n_tiles=n_tiles),
        grid=(b, n_tiles),
        in_specs=[pl.BlockSpec((1, tm, c), lambda bi, t: (bi, t, 0)),
                  pl.BlockSpec((1, 8, c), lambda bi, t: (bi, jnp.maximum(t * g - 1, 0), 0)),
                  pl.BlockSpec((1, 8, c), lambda bi, t: (bi, jnp.minimum((t + 1) * g, n_groups - 1), 0)),
                  pl.BlockSpec((HY_SHORT, c), lambda bi, t: (0, 0)),
                  pl.BlockSpec((1, c), lambda bi, t: (0, 0))],
        out_specs=[ospec, ospec, ospec],
        out_shape=[out, out, out],
        compiler_params=_cparams(("arbitrary", "arbitrary")),
        name="hyena_prep",
    )(u, u, u, conv_w, conv_b.reshape(1, c))


def _hy_filter_kernel(z_ref, w1_ref, b1_ref, w2_ref, b2_ref, w3_ref, fr_ref, dl_ref, k_ref, ss_ref, *, seq_len):
    i = pl.program_id(0)
    hp = lax.Precision.HIGHEST
    z = z_ref[...]
    tm = z.shape[0]
    fr = fr_ref[...]
    h = jnp.sin(fr * (jnp.dot(z, w1_ref[...], preferred_element_type=F32, precision=hp) + b1_ref[...]))
    h = jnp.sin(fr * (jnp.dot(h, w2_ref[...], preferred_element_type=F32, precision=hp) + b2_ref[...]))
    h = jnp.dot(h.astype(BF16), w3_ref[...].astype(BF16), preferred_element_type=F32)
    dec = jnp.exp(-z[:, 0:1] * dl_ref[...])
    r = i * tm + lax.broadcasted_iota(jnp.int32, (tm, HY_W), 0)
    parts = []
    for o in range(HY_ORDER):
        hf = h[:, (2 * o) * HY_W:(2 * o + 1) * HY_W]
        hb = h[:, (2 * o + 1) * HY_W:(2 * o + 2) * HY_W]
        parts.append(jnp.where(r < seq_len, hf, jnp.where(r > seq_len, hb, 0.0)) * dec)
    k = jnp.concatenate(parts, axis=1)
    k_ref[...] = k
    ss = jnp.sum(k * k, axis=0, keepdims=True)

    @pl.when(i == 0)
    def _():
        ss_ref[...] = ss

    @pl.when(i != 0)
    def _():
        ss_ref[...] += ss


def _hyena_filter_taps(l, w1, b1, w2, b2, w3, freq):
    t = jnp.linspace(0.0, 1.0, l, dtype=F32)[:, None]
    w = (2.0 * math.pi / l) * jnp.arange(l, dtype=F32)[:, None]
    f = jnp.linspace(1e-4, HY_BANDS - 1, HY_BANDS, dtype=F32)[None, :]
    z = jnp.concatenate([t, jnp.cos(f * w), -jnp.sin(f * w)], axis=-1)
    z2 = jnp.concatenate([z, z[:1], z[:0:-1]], axis=0)
    emb = z.shape[1]
    z2 = jnp.pad(z2, ((0, 0), (0, LANES - emb)))
    w1p = jnp.pad(w1, ((0, LANES - emb), (0, 0)))
    deltas = jnp.abs(jnp.linspace(math.log(HY_DECAY_TARGET) / HY_SLOW_DECAY,
                                  math.log(HY_DECAY_TARGET) / HY_FAST_DECAY, HY_W, dtype=F32))
    fo = w2.shape[0]
    nout = w3.shape[1]
    tm = min(2 * l, 1024)
    const = lambda i: (0, 0)
    return pl.pallas_call(
        functools.partial(_hy_filter_kernel, seq_len=l),
        grid=(2 * l // tm,),
        in_specs=[pl.BlockSpec((tm, LANES), lambda i: (i, 0)),
                  pl.BlockSpec((LANES, fo), const), pl.BlockSpec((1, fo), const),
                  pl.BlockSpec((fo, fo), const), pl.BlockSpec((1, fo), const),
                  pl.BlockSpec((fo, nout), const), pl.BlockSpec((1, fo), const),
                  pl.BlockSpec((1, HY_W), const)],
        out_specs=[pl.BlockSpec((tm, HY_ORDER * HY_W), lambda i: (i, 0)),
                   pl.BlockSpec((1, HY_ORDER * HY_W), const)],
        out_shape=[jax.ShapeDtypeStruct((2 * l, HY_ORDER * HY_W), F32),
                   jax.ShapeDtypeStruct((1, HY_ORDER * HY_W), F32)],
        compiler_params=_cparams(("arbitrary",)),
        name="hyena_filter_taps",
    )(z2, w1p, b1.reshape(1, fo), w2, b2.reshape(1, fo), w3, freq.reshape(1, fo), deltas.reshape(1, HY_W))


def _left_matmul_kernel(f_ref, x_ref, o_ref):
    o_ref[0] = jnp.dot(f_ref[...], x_ref[0].astype(BF16), preferred_element_type=F32).astype(o_ref.dtype)


def _left_matmul(f_bf, x3, name):
    b, k, c = x3.shape
    m = f_bf.shape[0]
    return pl.pallas_call(
        _left_matmul_kernel,
        grid=(b, c // HY_COLS),
        in_specs=[pl.BlockSpec((m, k), lambda bi, j: (0, 0)),
                  pl.BlockSpec((1, k, HY_COLS), lambda bi, j: (bi, 0, j))],
        out_specs=pl.BlockSpec((1, m, HY_COLS), lambda bi, j: (bi, 0, j)),
        out_shape=jax.ShapeDtypeStruct((b, m, c), BF16),
        compiler_params=_cparams(("arbitrary", "arbitrary")),
        name=name,
    )(f_bf, x3)


def _twiddle_cols(tw_ref, k1):
    lane = lax.broadcasted_iota(jnp.int32, (HY_R, HY_R), 1)
    pick = lane == k1
    twr = jnp.sum(jnp.where(pick, tw_ref[0], 0.0), axis=1, keepdims=True)
    twi = jnp.sum(jnp.where(pick, tw_ref[1], 0.0), axis=1, keepdims=True)
    return twr, twi


def _hy_spectrum_kernel(a_ref, tw_ref, fwd_ref, ss_ref, kf_ref):
    kb = pl.program_id(0)
    scale = lax.rsqrt(ss_ref[...] + EPS) * (1.0 / (HY_R * HY_R))

    def body(kk, carry):
        twr, twi = _twiddle_cols(tw_ref, kb * HY_KB + kk)
        are = a_ref[0, 0, kk].astype(F32)
        aim = a_ref[0, 1, kk].astype(F32)
        bst = jnp.concatenate([are * twr - aim * twi, are * twi + aim * twr], axis=0).astype(BF16)
        x = jnp.dot(fwd_ref[...], bst, preferred_element_type=F32)
        kf_ref[0, kk] = x[:HY_R] * scale
        kf_ref[1, kk] = x[HY_R:] * scale
        return carry

    lax.fori_loop(0, HY_KB, body, 0, unroll=HY_UNROLL)


def _hy_spectrum(a5, tw, fwd_bf, ss):
    nch = a5.shape[-1]
    return pl.pallas_call(
        _hy_spectrum_kernel,
        grid=(HY_R // HY_KB,),
        in_specs=[pl.BlockSpec((1, 2, HY_KB, HY_R, nch), lambda kb: (0, 0, kb, 0, 0)),
                  pl.BlockSpec((2, HY_R, HY_R), lambda kb: (0, 0, 0)),
                  pl.BlockSpec((2 * HY_R, 2 * HY_R), lambda kb: (0, 0)),
                  pl.BlockSpec((1, nch), lambda kb: (0, 0))],
        out_specs=pl.BlockSpec((2, HY_KB, HY_R, nch), lambda kb: (0, kb, 0, 0)),
        out_shape=jax.ShapeDtypeStruct((2, HY_R, HY_R, nch), F32),
        compiler_params=_cparams(("arbitrary",)),
        name="hyena_filter_spectrum",
    )(a5, tw, fwd_bf, ss)


def _hy_mid_kernel(a_ref, kf_ref, tw_ref, fwd_ref, inv_ref, z_ref):
    kb = pl.program_id(0)

    def body(kk, carry):
        twr, twi = _twiddle_cols(tw_ref, kb * HY_KB + kk)
        are = a_ref[0, 0, kk].astype(F32)
        aim = a_ref[0, 1, kk].astype(F32)
        bst = jnp.concatenate([are * twr - aim * twi, are * twi + aim * twr], axis=0).astype(BF16)
        x = jnp.dot(fwd_ref[...], bst, preferred_element_type=F32)
        xre, xim = x[:HY_R], x[HY_R:]
        kre, kim = kf_ref[0, kk], kf_ref[1, kk]
        yst = jnp.concatenate([xre * kre - xim * kim, xre * kim + xim * kre], axis=0).astype(BF16)
        zz = jnp.dot(inv_ref[...], yst, preferred_element_type=F32)
        zre, zim = zz[:HY_R], zz[HY_R:]
        z_ref[0, 0, kk] = (zre * twr + zim * twi).astype(BF16)
        z_ref[0, 1, kk] = (zim * twr - zre * twi).astype(BF16)
        return carry

    lax.fori_loop(0, HY_KB, body, 0, unroll=HY_UNROLL)


def _hy_mid(a5, kf, order, tw, fwd_bf, inv_bf):
    b = a5.shape[0]
    blk = (1, 2, HY_KB, HY_R, HY_W)
    return pl.pallas_call(
        _hy_mid_kernel,
        grid=(HY_R // HY_KB, b),
        in_specs=[pl.BlockSpec(blk, lambda kb, bi: (bi, 0, kb, 0, 0)),
                  pl.BlockSpec((2, HY_KB, HY_R, HY_W), lambda kb, bi: (0, kb, 0, order)),
                  pl.BlockSpec((2, HY_R, HY_R), lambda kb, bi: (0, 0, 0)),
                  pl.BlockSpec((2 * HY_R, 2 * HY_R), lambda kb, bi: (0, 0)),
                  pl.BlockSpec((2 * HY_R, 2 * HY_R), lambda kb, bi: (0, 0))],
        out_specs=pl.BlockSpec(blk, lambda kb, bi: (bi, 0, kb, 0, 0)),
        out_shape=jax.ShapeDtypeStruct(a5.shape, BF16),
        compiler_params=_cparams(("arbitrary", "arbitrary")),
        name="hyena_spectral_product",
    )(a5, kf, tw, fwd_bf, inv_bf)


def _hy_out_kernel(f_ref, z_ref, g_ref, w_ref, sk_ref, o_ref):
    y = jnp.dot(f_ref[...], z_ref[0], preferred_element_type=F32)
    w = w_ref[0].astype(F32)
    o_ref[0] = (g_ref[0].astype(F32) * (y + w * sk_ref[...])).astype(BF16)


def _hy_out(f_bf, z3, gate3, w3, skip_row):
    b, m2, c = z3.shape
    m = f_bf.shape[0]
    dspec = pl.BlockSpec((1, m, HY_COLS), lambda bi, j: (bi, 0, j))
    return pl.pallas_call(
        _hy_out_kernel,
        grid=(b, c // HY_COLS),
        in_specs=[pl.BlockSpec((m, m2), lambda bi, j: (0, 0)),
                  pl.BlockSpec((1, m2, HY_COLS), lambda bi, j: (bi, 0, j)),
                  dspec, dspec,
                  pl.BlockSpec((1, HY_COLS), lambda bi, j: (0, 0))],
        out_specs=dspec,
        out_shape=jax.ShapeDtypeStruct((b, m, c), BF16),
        compiler_params=_cparams(("arbitrary", "arbitrary")),
        name="hyena_inverse_gate",
    )(f_bf, z3, gate3, w3, skip_row)


def _hyena_long(u, conv_w, conv_b, skip, filt_args):
    b, l, _ = u.shape
    assert 2 * l == HY_R * HY_R
    fre, fim, tw, fwd, inv = _dft_consts(HY_R)
    half = HY_R // 2
    as_bf = lambda m: jnp.asarray(m, F32).astype(BF16)
    f_a_data = as_bf(np.concatenate([fre[:, :half], fim[:, :half]], axis=0))
    f_a_filt = as_bf(np.concatenate([fre, fim], axis=0))
    f_c = as_bf(np.concatenate([fre[:half], fim[:half]], axis=1))
    tw_j = jnp.asarray(tw, F32)
    fwd_bf = as_bf(fwd)
    inv_bf = as_bf(inv)

    taps, ss = _hyena_filter_taps(l, *filt_args)
    nch = HY_ORDER * HY_W
    ak = _left_matmul(f_a_filt, taps.reshape(1, HY_R, HY_R * nch), "hyena_filter_dft_a")
    kf = _hy_spectrum(ak.reshape(1, 2, HY_R, HY_R, nch), tw_j, fwd_bf, ss)

    x1, x2, v = _hyena_prep(u, conv_w, conv_b)
    z = v
    for o, gate in enumerate((x1, x2)):
        z3 = z.reshape(b, half, HY_R * HY_W)
        a = _left_matmul(f_a_data, z3, "hyena_dft_a")
        zs = _hy_mid(a.reshape(b, 2, HY_R, HY_R, HY_W), kf, o, tw_j, fwd_bf, inv_bf)
        skip_row = jnp.tile(skip[o], HY_COLS // HY_W).reshape(1, HY_COLS)
        z = _hy_out(f_c, zs.reshape(b, 2 * HY_R, HY_R * HY_W), gate.reshape(b, half, HY_R * HY_W), z3,
                    skip_row).reshape(b, l, HY_W)
    return z


def _hy_ctx_kernel(x1_ref, x2_ref, v_ref, taps_ref, ss_ref, fa_ref, fc_ref, sk_ref, o_ref, *, seq_len):
    n2 = 2 * seq_len
    fa = fa_ref[...]
    scale = lax.rsqrt(ss_ref[...] + EPS) * (1.0 / n2)
    kf = jnp.dot(fa, taps_ref[...].astype(BF16), preferred_element_type=F32) * scale
    z = v_ref[0].astype(F32)
    for o, g_ref in enumerate((x1_ref, x2_ref)):
        x = jnp.dot(fa[:, :seq_len], z.astype(BF16), preferred_element_type=F32)
        xre, xim = x[:n2], x[n2:]
        kre = kf[:n2, o * HY_W:(o + 1) * HY_W]
        kim = kf[n2:, o * HY_W:(o + 1) * HY_W]
        yst = jnp.concatenate([xre * kre - xim * kim, xre * kim + xim * kre], axis=0).astype(BF16)
        y = jnp.dot(fc_ref[...], yst, preferred_element_type=F32)
        z = g_ref[0].astype(F32) * (y + z * sk_ref[pl.ds(o, 1), :])
    o_ref[0] = z.astype(BF16)


def _hyena_short(u, conv_w, conv_b, skip, filt_args):
    b, l, _ = u.shape
    n2 = 2 * l
    k = np.arange(n2)
    ang = 2.0 * np.pi * np.outer(k, k) / n2
    fre, fim = np.cos(ang), -np.sin(ang)
    fa = jnp.asarray(np.concatenate([fre, fim], axis=0), F32).astype(BF16)
    fc = jnp.asarray(np.concatenate([fre[:l], fim[:l]], axis=1), F32).astype(BF16)
    taps, ss = _hyena_filter_taps(l, *filt_args)
    x1, x2, v = _hyena_prep(u, conv_w, conv_b)
    nch = HY_ORDER * HY_W
    dspec = pl.BlockSpec((1, l, HY_W), lambda bi: (bi, 0, 0))
    const = lambda bi: (0, 0)
    return pl.pallas_call(
        functools.partial(_hy_ctx_kernel, seq_len=l),
        grid=(b,),
        in_specs=[dspec, dspec, dspec,
                  pl.BlockSpec((n2, nch), const), pl.BlockSpec((1, nch), const),
                  pl.BlockSpec((2 * n2, n2), const), pl.BlockSpec((l, 2 * n2), const),
                  pl.BlockSpec((HY_ORDER, HY_W), const)],
        out_specs=dspec,
        out_shape=jax.ShapeDtypeStruct((b, l, HY_W), BF16),
        compiler_params=_cparams(("arbitrary",)),
        name="hyena_context",
    )(x1, x2, v, taps, ss, fa, fc, skip)


def _rope_tables(n):
    t = jnp.arange(n, dtype=jnp.int32)
    pos = jnp.stack([t // GRID_W, t % GRID_W], axis=-1).astype(F32)
    nf = HEAD_DIM // 4
    inv = ROPE_THETA ** (-jnp.arange(nf, dtype=F32) / nf)
    ang = pos[:, :, None] * inv
    cos, sin = jnp.cos(ang), jnp.sin(ang)
    c64 = jnp.concatenate([cos[:, 0], cos[:, 0], cos[:, 1], cos[:, 1]], axis=-1)
    s64 = jnp.concatenate([-sin[:, 0], sin[:, 0], -sin[:, 1], sin[:, 1]], axis=-1)
    return jnp.tile(c64, (1, 2)), jnp.tile(s64, (1, 2))


def _block_diag(pool_w):
    g, c, _ = pool_w.shape
    out = jnp.zeros((g * c, g * c), pool_w.dtype)
    for i in range(g):
        out = out.at[i * c:(i + 1) * c, i * c:(i + 1) * c].set(pool_w[i])
    return out


def kernel(x, c, ctx, c_ctx, w_mod, b_mod, norm1_w, norm2_w, w_in, w_out, q_norm_w, k_norm_w, na_rpb,
           hy_conv_w, hy_conv_b, hy_w1, hy_b1, hy_w2, hy_b2, hy_w3, hy_freq, hy_skip,
           pool_w, pool_scale, w_router, w_gate, w_up, w_down):
    b, n, d = x.shape
    lc = ctx.shape[1]
    depth = w_mod.shape[0]
    rows = n // GRID_W
    cos128, sin128 = _rope_tables(n)
    seg = jnp.asarray(np.kron(np.eye(2), np.full((HEAD_DIM, HEAD_DIM), 1.0 / HEAD_DIM)), BF16)
    cc = jnp.zeros((8, d), F32).at[:b].set(c).at[b].set(c_ctx)

    xl = x.reshape(b * n, d)
    xc = ctx.reshape(b * lc, d)
    for i in range(depth):
        last = i == depth - 1
        mod = _modulation(cc, w_mod, b_mod[i], i)
        ml = mod[:b].reshape(b, 1, 6, d)
        mc = mod[b].reshape(1, 1, 6, d)
        sh1, sc1, g1, sh2, sc2, g2 = [ml[:, :, j] for j in range(6)]
        csh1, csc1, cg1, csh2, csc2, cg2 = [mc[:, :, j] for j in range(6)]
        w_in_bf = w_in[i].astype(BF16)
        w_out_bf = w_out[i].astype(BF16)
        w_router_t = w_router[i].T
        pool_bd = _block_diag(pool_w[i]).astype(BF16)
        hy_args = (hy_w1[i], hy_b1[i], hy_w2[i], hy_b2[i], hy_w3[i], hy_freq[i])

        qc, kc, vc, hyc, poolc = _in_projection(
            xc, csh1, csc1, norm1_w[i], w_in_bf, q_norm_w[i], k_norm_w[i], cos128, sin128, seg,
            rows_per_mod=b * lc, seq_len=lc, tm=lc, rope=False)
        kc = kc.reshape(b, lc, NA_W)
        vc = vc.reshape(b, lc, NA_W)
        if not last:
            na_c = _dense_attention(qc.reshape(b, lc, NA_W), kc, vc).reshape(b * lc, NA_W)
            hy_c = _hyena_short(hyc.reshape(b, lc, 3 * HY_W), hy_conv_w[i], hy_conv_b[i], hy_skip[i], hy_args)
            pool_c = _pool_mixer(poolc.reshape(b, lc, POOL_W), pool_bd, pool_scale[i])
            xc1, h2c, affc = _out_projection(
                na_c, hy_c.reshape(b * lc, HY_W).astype(BF16), pool_c.reshape(b * lc, POOL_W), w_out_bf,
                xc, cg1, norm2_w[i], csh2, csc2, w_router_t, rows_per_mod=b * lc, seq_len=lc, tm=lc)

        qr, qp, kl, vl, hyl, pooll = _in_projection(
            xl, sh1, sc1, norm1_w[i], w_in_bf, q_norm_w[i], k_norm_w[i], cos128, sin128, seg,
            rows_per_mod=n, seq_len=n, tm=512, rope=True)
        bias = _na_bias_tables(na_rpb[i], rows)
        o_na = _neighborhood_attention(qr.reshape(b, n, NA_W), qp.reshape(b, n, NA_W), kl.reshape(b, n, NA_W),
                                       vl.reshape(b, n, NA_W), kc, vc, bias)
        hy_l = _hyena_long(hyl.reshape(b, n, 3 * HY_W), hy_conv_w[i], hy_conv_b[i], hy_skip[i], hy_args)
        pool_l = _pool_mixer(pooll.reshape(b, n, POOL_W), pool_bd, pool_scale[i])
        xl1, h2l, affl = _out_projection(
            o_na.reshape(b * n, NA_W), hy_l.reshape(b * n, HY_W).astype(BF16), pool_l.reshape(b * n, POOL_W),
            w_out_bf, xl, g1, norm2_w[i], sh2, sc2, w_router_t, rows_per_mod=n, seq_len=n, tm=512)
        streams = [(h2l, affl, xl1, g2)]
        if not last:
            streams.append((h2c, affc, xc1, cg2))
        outs = _expert_choice_moe(streams, w_gate, w_up, w_down, i)
        xl = outs[0]
        if not last:
            xc = outs[1]
    return xl.reshape(b, n, d)
```

```python
import functools
import math

import jax
import jax.numpy as jnp
import numpy as np
from jax import lax
from jax.experimental import pallas as pl
from jax.experimental.pallas import tpu as pltpu

F32 = jnp.float32
BF16 = jnp.bfloat16

D_MODEL = 1024
GRID_W = 64
NA_HEADS = 8
HEAD_DIM = 64
NA_W = NA_HEADS * HEAD_DIM
NA_WIN_R = 8
NA_WIN_C = 16
ROPE_THETA = 10000.0
HY_W = 256
HY_ORDER = 2
HY_SHORT = 3
HY_BANDS = 16
HY_DECAY_TARGET = 1e-2
HY_FAST_DECAY = 0.3
HY_SLOW_DECAY = 1.5
POOL_W = 256
POOL_SIZES = (2, 4, 8, 16)
POOL_GROUP = POOL_W // len(POOL_SIZES)
MIX_W = NA_W + HY_W + POOL_W
IN_W = 3 * NA_W + (HY_ORDER + 1) * HY_W + POOL_W
N_EXPERTS = 16
EC_FACTOR = 2
D_EXPERT = 2048
EPS = 1e-6

LANES = 128
TOKEN_SUBLANES = D_MODEL // 128
NEG_BIG = -1e30
VMEM_LIMIT = 56 * 1024 * 1024

NA_Q_ROWS = 4
NA_K_ROWS = NA_Q_ROWS + NA_WIN_R - 1
NA_TILES_PER_STEP = 2


def _cparams(sem):
    return pltpu.CompilerParams(dimension_semantics=sem, vmem_limit_bytes=VMEM_LIMIT)


def _mod_kernel(c_ref, w_ref, b_ref, o_ref):
    c = c_ref[...]
    s = c * jax.nn.sigmoid(c)
    o_ref[...] = jnp.dot(s, w_ref[0], preferred_element_type=F32,
                         precision=lax.Precision.HIGHEST) + b_ref[...]


def _modulation(cc, w_mod, b_mod, layer):
    rows, d = cc.shape
    n = w_mod.shape[2]
    tn = 1536
    return pl.pallas_call(
        _mod_kernel,
        grid=(n // tn,),
        in_specs=[pl.BlockSpec((rows, d), lambda j: (0, 0)),
                  pl.BlockSpec((1, d, tn), lambda j: (layer, 0, j)),
                  pl.BlockSpec((1, tn), lambda j: (0, j))],
        out_specs=pl.BlockSpec((rows, tn), lambda j: (0, j)),
        out_shape=jax.ShapeDtypeStruct((rows, n), F32),
        compiler_params=_cparams(("arbitrary",)),
        name="modulation",
    )(cc, w_mod, b_mod.reshape(1, n))


def _head_norm(xs, w128, seg):
    sq = xs * xs
    hi = sq.astype(BF16)
    lo = (sq - hi.astype(F32)).astype(BF16)
    m = (jnp.dot(hi, seg, preferred_element_type=F32) + jnp.dot(lo, seg, preferred_element_type=F32))
    return xs * lax.rsqrt(m + EPS) * w128


def _rope(xs, c, s_signed):
    lane = lax.broadcasted_iota(jnp.int32, xs.shape, 1)
    partner = jnp.where((lane & 16) == 0, pltpu.roll(xs, LANES - 16, 1), pltpu.roll(xs, 16, 1))
    return xs * c + partner * s_signed


def _inproj_kernel(x_ref, sh_ref, sc_ref, nw_ref, w_ref, qw_ref, kw_ref, cos_ref, sin_ref, seg_ref,
                   *out_refs, rope):
    if rope:
        qr_ref, qp_ref, k_ref, v_ref, hy_ref, pool_ref = out_refs
    else:
        qp_ref, k_ref, v_ref, hy_ref, pool_ref = out_refs
    x = x_ref[...]
    ms = jnp.mean(x * x, axis=-1, keepdims=True)
    h = x * lax.rsqrt(ms + EPS) * nw_ref[...]
    h = h * (1.0 + sc_ref[0]) + sh_ref[0]
    u = jnp.dot(h.astype(BF16), w_ref[...], preferred_element_type=F32)
    seg = seg_ref[...]
    qk_scale = HEAD_DIM ** -0.5
    for ch in range(NA_W // LANES):
        sl = slice(ch * LANES, (ch + 1) * LANES)
        qn = _head_norm(u[:, sl], qw_ref[...], seg)
        kn = _head_norm(u[:, NA_W + ch * LANES:NA_W + (ch + 1) * LANES], kw_ref[...], seg)
        qp_ref[:, sl] = (qn * qk_scale).astype(BF16)
        if rope:
            c = cos_ref[...]
            s = sin_ref[...]
            qr_ref[:, sl] = (_rope(qn, c, s) * qk_scale).astype(BF16)
            k_ref[:, sl] = _rope(kn, c, s).astype(BF16)
        else:
            k_ref[:, sl] = kn.astype(BF16)
    v_ref[...] = u[:, 2 * NA_W:3 * NA_W].astype(BF16)
    hy_ref[...] = u[:, 3 * NA_W:3 * NA_W + 3 * HY_W]
    pool_ref[...] = u[:, 3 * NA_W + 3 * HY_W:]


def _in_projection(x2, shift, scale, norm_w, w_in_bf, q_norm_w, k_norm_w, cos128, sin128, seg, *,
                   rows_per_mod, seq_len, tm, rope):
    m, d = x2.shape
    tiles_per_mod = rows_per_mod // tm
    tiles_per_seq = seq_len // tm
    row = lambda i: (i, 0)
    const = lambda i: (0, 0)
    modi = lambda i: (i // tiles_per_mod, 0, 0)
    pos = lambda i: (i % tiles_per_seq, 0)
    outs = []
    if rope:
        outs.append((NA_W, BF16))
    outs += [(NA_W, BF16), (NA_W, BF16), (NA_W, BF16), (3 * HY_W, F32), (POOL_W, F32)]
    return pl.pallas_call(
        functools.partial(_inproj_kernel, rope=rope),
        grid=(m // tm,),
        in_specs=[pl.BlockSpec((tm, d), row),
                  pl.BlockSpec((1, 1, d), modi), pl.BlockSpec((1, 1, d), modi),
                  pl.BlockSpec((1, d), const),
                  pl.BlockSpec((d, IN_W), const),
                  pl.BlockSpec((1, LANES), const), pl.BlockSpec((1, LANES), const),
                  pl.BlockSpec((tm, LANES), pos), pl.BlockSpec((tm, LANES), pos),
                  pl.BlockSpec((LANES, LANES), const)],
        out_specs=[pl.BlockSpec((tm, w), row) for w, _ in outs],
        out_shape=[jax.ShapeDtypeStruct((m, w), dt) for w, dt in outs],
        compiler_params=_cparams(("arbitrary",)),
        name="in_projection_rope" if rope else "in_projection_ctx",
    )(x2, shift, scale, norm_w.reshape(1, d), w_in_bf,
      jnp.tile(q_norm_w, 2).reshape(1, LANES), jnp.tile(k_norm_w, 2).reshape(1, LANES),
      cos128, sin128, seg)


def _softmax_pv(s_list, v_list):
    mx = functools.reduce(jnp.maximum, [jnp.max(s, axis=-1, keepdims=True) for s in s_list])
    ps = [jnp.exp(s - mx) for s in s_list]
    l = functools.reduce(jnp.add, [jnp.sum(p, axis=-1, keepdims=True) for p in ps])
    o = functools.reduce(jnp.add, [jnp.dot(p.astype(BF16), v, preferred_element_type=F32)
                                   for p, v in zip(ps, v_list)])
    return o / l


_NT = (((1,), (1,)), ((), ()))


def _na_kernel(qr_ref, qp_ref, k_ref, v_ref, kc_ref, vc_ref, bias_ref, o_ref, *, rows):
    n_tiles = rows // NA_Q_ROWS
    tq = NA_Q_ROWS * GRID_W
    kc = kc_ref[0]
    vc = vc_ref[0]
    lane = lax.broadcasted_iota(jnp.int32, (tq, LANES), 1)
    for sub in range(NA_TILES_PER_STEP):
        t = pl.program_id(2) * NA_TILES_PER_STEP + sub
        ks = jnp.clip(t * NA_Q_ROWS - NA_WIN_R // 2, 0, rows - NA_K_ROWS)
        start = pl.multiple_of(ks * GRID_W, GRID_W)
        k = k_ref[0, pl.ds(start, NA_K_ROWS * GRID_W), :]
        v = v_ref[0, pl.ds(start, NA_K_ROWS * GRID_W), :]
        q = qr_ref[0, pl.ds(sub * tq, tq), :]
        qp = qp_ref[0, pl.ds(sub * tq, tq), :]
        kind = jnp.where(t == 0, 0, jnp.where(t == n_tiles - 1, 2, 1))
        outs = []
        for hh in range(2):
            keep = (lane < HEAD_DIM) if hh == 0 else (lane >= HEAD_DIM)
            zero = jnp.zeros_like(q)
            s_lat = (lax.dot_general(jnp.where(keep, q, zero), k, _NT, preferred_element_type=F32)
                     + bias_ref[kind, hh])
            s_ctx = lax.dot_general(jnp.where(keep, qp, zero), kc, _NT, preferred_element_type=F32)
            outs.append(_softmax_pv([s_lat, s_ctx], [v, vc]))
        o_ref[0, pl.ds(sub * tq, tq), :] = jnp.where(lane < HEAD_DIM, outs[0], outs[1]).astype(BF16)


def _neighborhood_attention(qr, qp, k, v, kc, vc, bias):
    b, n, _ = qr.shape
    lc = kc.shape[1]
    rows = n // GRID_W
    n_steps = rows // (NA_Q_ROWS * NA_TILES_PER_STEP)
    tq = NA_Q_ROWS * GRID_W * NA_TILES_PER_STEP
    tk = NA_K_ROWS * GRID_W
    qmap = lambda bi, hp, t: (bi, t, hp)
    smap = lambda bi, hp, t: (bi, 0, hp)
    bmap = lambda bi, hp, t: (0, hp, 0, 0)
    return pl.pallas_call(
        functools.partial(_na_kernel, rows=rows),
        grid=(b, NA_W // LANES, n_steps),
        in_specs=[pl.BlockSpec((1, tq, LANES), qmap), pl.BlockSpec((1, tq, LANES), qmap),
                  pl.BlockSpec((1, n, LANES), smap), pl.BlockSpec((1, n, LANES), smap),
                  pl.BlockSpec((1, lc, LANES), smap), pl.BlockSpec((1, lc, LANES), smap),
                  pl.BlockSpec((3, 2, NA_Q_ROWS * GRID_W, tk), bmap)],
        out_specs=pl.BlockSpec((1, tq, LANES), qmap),
        out_shape=jax.ShapeDtypeStruct((b, n, NA_W), BF16),
        compiler_params=_cparams(("arbitrary", "arbitrary", "arbitrary")),
        name="neighborhood_attention",
    )(qr, qp, k, v, kc, vc, bias)


def _na_bias_tables(rpb, rows):
    n_tiles = rows // NA_Q_ROWS
    n_dr, n_dc = 2 * NA_WIN_R - 1, 2 * NA_WIN_C - 1
    qc = np.arange(GRID_W)
    cs = np.clip(qc - NA_WIN_C // 2, 0, GRID_W - NA_WIN_C)
    kcol = np.arange(GRID_W)
    ok_c = (kcol[None, :] >= cs[:, None]) & (kcol[None, :] < cs[:, None] + NA_WIN_C)
    dc = kcol[None, :] - qc[:, None] + NA_WIN_C - 1
    col_sel = np.concatenate([(dc[:, :, None] == np.arange(n_dc)) & ok_c[:, :, None],
                              ~ok_c[:, :, None]], axis=2)
    row_sel = []
    for t in (0, 1, n_tiles - 1):
        r0 = t * NA_Q_ROWS
        ks = int(np.clip(r0 - NA_WIN_R // 2, 0, rows - NA_K_ROWS))
        r = r0 + np.arange(NA_Q_ROWS)
        rs = np.clip(r - NA_WIN_R // 2, 0, rows - NA_WIN_R)
        krow = ks + np.arange(NA_K_ROWS)
        ok_r = (krow[None, :] >= rs[:, None]) & (krow[None, :] < rs[:, None] + NA_WIN_R)
        dr = krow[None, :] - r[:, None] + NA_WIN_R - 1
        row_sel.append(np.concatenate([(dr[:, :, None] == np.arange(n_dr)) & ok_r[:, :, None],
                                       ~ok_r[:, :, None]], axis=2))
    ext = jnp.pad(rpb.astype(F32), ((0, 0), (0, 1), (0, 1)), constant_values=NEG_BIG)
    tab = jnp.einsum("hrd,tijr,qkd->thiqjk", ext, jnp.asarray(np.stack(row_sel), F32),
                     jnp.asarray(col_sel, F32), precision=lax.Precision.HIGHEST)
    return tab.reshape(3, NA_HEADS, NA_Q_ROWS * GRID_W, NA_K_ROWS * GRID_W)


def _dense_attn_kernel(q_ref, k_ref, v_ref, o_ref):
    q = q_ref[0]
    k = k_ref[0]
    v = v_ref[0]
    lane = lax.broadcasted_iota(jnp.int32, q.shape, 1)
    outs = []
    for hh in range(2):
        keep = (lane < HEAD_DIM) if hh == 0 else (lane >= HEAD_DIM)
        s = lax.dot_general(jnp.where(keep, q, jnp.zeros_like(q)), k, _NT, preferred_element_type=F32)
        outs.append(_softmax_pv([s], [v]))
    o_ref[0] = jnp.where(lane < HEAD_DIM, outs[0], outs[1]).astype(BF16)


def _dense_attention(q, k, v):
    b, l, _ = q.shape
    spec = pl.BlockSpec((1, l, LANES), lambda bi, hp: (bi, 0, hp))
    return pl.pallas_call(
        _dense_attn_kernel,
        grid=(b, NA_W // LANES),
        in_specs=[spec, spec, spec],
        out_specs=spec,
        out_shape=jax.ShapeDtypeStruct((b, l, NA_W), BF16),
        compiler_params=_cparams(("arbitrary", "arbitrary")),
        name="context_attention",
    )(q, k, v)


POOL_HALO = max(POOL_SIZES) // 2


def _pool_kernel(x_ref, w_ref, sc_ref, o_ref, xpad, *, seq_len, chunk):
    zeros = jnp.zeros((POOL_HALO, POOL_W), F32)
    xpad[pl.ds(0, POOL_HALO), :] = zeros
    xpad[pl.ds(seq_len + POOL_HALO, POOL_HALO), :] = zeros
    xpad[pl.ds(POOL_HALO, seq_len), :] = x_ref[0]
    span = chunk + 2 * POOL_HALO
    w = w_ref[...]
    sc = sc_ref[...]

    def body(c, carry):
        base = pl.multiple_of(c * chunk, chunk)
        xs = xpad[pl.ds(base, span), :]
        s2 = xs + pltpu.roll(xs, 1, 0)
        s4 = pltpu.roll(s2, 1, 0) + pltpu.roll(s2, span - 1, 0)
        s8 = pltpu.roll(s4, 2, 0) + pltpu.roll(s4, span - 2, 0)
        s16 = pltpu.roll(s8, 4, 0) + pltpu.roll(s8, span - 4, 0)
        mid = slice(POOL_HALO, POOL_HALO + chunk)
        grp = lax.broadcasted_iota(jnp.int32, (chunk, POOL_W), 1) // POOL_GROUP
        tpos = base + lax.broadcasted_iota(jnp.int32, (chunk, POOL_W), 0)
        half = jnp.left_shift(1, grp)
        cnt = (jnp.minimum(tpos + half, seq_len) - jnp.maximum(tpos - half, 0)).astype(F32)
        ssum = jnp.where(grp == 0, s2[mid], jnp.where(grp == 1, s4[mid], jnp.where(grp == 2, s8[mid], s16[mid])))
        diff = ssum / cnt - xs[mid]
        y = jnp.dot(diff.astype(BF16), w, preferred_element_type=F32) * sc
        o_ref[0, pl.ds(base, chunk), :] = y.astype(BF16)
        return carry

    lax.fori_loop(0, seq_len // chunk, body, 0)


def _pool_mixer(u, w_blockdiag_bf, pool_scale):
    b, l, _ = u.shape
    chunk = min(l, 512)
    return pl.pallas_call(
        functools.partial(_pool_kernel, seq_len=l, chunk=chunk),
        grid=(b,),
        in_specs=[pl.BlockSpec((1, l, POOL_W), lambda bi: (bi, 0, 0)),
                  pl.BlockSpec((POOL_W, POOL_W), lambda bi: (0, 0)),
                  pl.BlockSpec((1, POOL_W), lambda bi: (0, 0))],
        out_specs=pl.BlockSpec((1, l, POOL_W), lambda bi: (bi, 0, 0)),
        out_shape=jax.ShapeDtypeStruct((b, l, POOL_W), BF16),
        scratch_shapes=[pltpu.VMEM((l + 2 * POOL_HALO, POOL_W), F32)],
        compiler_params=_cparams(("arbitrary",)),
        name="pool_mixer",
    )(u, w_blockdiag_bf, pool_scale.reshape(1, POOL_W))


def _outproj_kernel(na_ref, hy_ref, pl_ref, w_ref, x_ref, g1_ref, nw_ref, sh_ref, sc_ref, wr_ref,
                    xn_ref, h2_ref, aff_ref):
    w = w_ref
    acc = jnp.dot(na_ref[...], w[pl.ds(0, NA_W), :], preferred_element_type=F32)
    acc += jnp.dot(hy_ref[...], w[pl.ds(NA_W, HY_W), :], preferred_element_type=F32)
    acc += jnp.dot(pl_ref[...], w[pl.ds(NA_W + HY_W, POOL_W), :], preferred_element_type=F32)
    xn = x_ref[...] + g1_ref[0] * acc
    xn_ref[...] = xn
    ms = jnp.mean(xn * xn, axis=-1, keepdims=True)
    h2 = xn * lax.rsqrt(ms + EPS) * nw_ref[...]
    h2 = h2 * (1.0 + sc_ref[0]) + sh_ref[0]
    tm = h2.shape[0]
    for s in range(TOKEN_SUBLANES):
        h2_ref[pl.ds(s, tm, stride=TOKEN_SUBLANES), :] = h2[:, s * LANES:(s + 1) * LANES]
    logits = lax.dot_general(wr_ref[...], h2, _NT, preferred_element_type=F32,
                             precision=lax.Precision.HIGHEST)
    mx = jnp.max(logits, axis=0, keepdims=True)
    p = jnp.exp(logits - mx)
    aff_ref[0] = p / jnp.sum(p, axis=0, keepdims=True)


def _out_projection(na, hy, pool, w_out_bf, x2, g1, norm_w, shift, scale, w_router_t, *,
                    rows_per_mod, seq_len, tm):
    m, d = x2.shape
    tiles_per_mod = rows_per_mod // tm
    tiles_per_seq = seq_len // tm
    row = lambda i: (i, 0)
    const = lambda i: (0, 0)
    modi = lambda i: (i // tiles_per_mod, 0, 0)
    return pl.pallas_call(
        _outproj_kernel,
        grid=(m // tm,),
        in_specs=[pl.BlockSpec((tm, NA_W), row), pl.BlockSpec((tm, HY_W), row), pl.BlockSpec((tm, POOL_W), row),
                  pl.BlockSpec((MIX_W, d), const),
                  pl.BlockSpec((tm, d), row),
                  pl.BlockSpec((1, 1, d), modi),
                  pl.BlockSpec((1, d), const),
                  pl.BlockSpec((1, 1, d), modi), pl.BlockSpec((1, 1, d), modi),
                  pl.BlockSpec((N_EXPERTS, d), const)],
        out_specs=[pl.BlockSpec((tm, d), row), pl.BlockSpec((tm * TOKEN_SUBLANES, LANES), row),
                   pl.BlockSpec((1, N_EXPERTS, tm), lambda i: (i // tiles_per_seq, 0, i % tiles_per_seq))],
        out_shape=[jax.ShapeDtypeStruct((m, d), F32), jax.ShapeDtypeStruct((m * TOKEN_SUBLANES, LANES), F32),
                   jax.ShapeDtypeStruct((m // seq_len, N_EXPERTS, seq_len), F32)],
        compiler_params=_cparams(("arbitrary",)),
        name="out_projection_router",
    )(na, hy, pool, w_out_bf, x2, g1, norm_w.reshape(1, d), shift, scale, w_router_t)


FFN_TF = 256
FFN_ROW_CHUNKS = 2


def _ffn_kernel(*refs, n_experts, sources, n_rows):
    ns = len(sources)
    idx_cur = refs[0:ns]
    idx_nxt = refs[ns:2 * ns]
    src = refs[2 * ns:3 * ns]
    gate_ref, wg_ref, wu_ref, wd_ref, o_ref, stage, xb, acc, sem = refs[3 * ns:]
    e = pl.program_id(0)
    f = pl.program_id(1)
    n_f = pl.num_programs(1)

    ts = TOKEN_SUBLANES

    def row_copy(s, row, r):
        return pltpu.make_async_copy(src[s].at[pl.ds(pl.multiple_of(row * ts, ts), ts)],
                                     stage.at[pl.ds(pl.multiple_of((sources[s][0] + r) * ts, ts), ts)], sem.at[0])

    def gather(idx_refs):
        for s in range(ns):
            def body(r, carry, s=s):
                row_copy(s, idx_refs[s][0, 0, r], r).start()
                return carry
            lax.fori_loop(0, sources[s][1], body, 0, unroll=8)

    def gather_wait():
        for s in range(ns):
            off, cnt = sources[s]
            pltpu.make_async_copy(src[s].at[pl.ds(0, cnt * ts)], stage.at[pl.ds(off * ts, cnt * ts)],
                                  sem.at[0]).wait()

    @pl.when(f == 0)
    def _():
        @pl.when(e == 0)
        def _():
            gather(idx_cur)

        gather_wait()
        for s in range(ts):
            xb[:, s * LANES:(s + 1) * LANES] = stage[pl.ds(s, n_rows, stride=ts), :].astype(BF16)

        @pl.when(e + 1 < n_experts)
        def _():
            gather(idx_nxt)

    wg = wg_ref[0, 0].astype(BF16)
    wu = wu_ref[0, 0].astype(BF16)
    wd = wd_ref[0, 0].astype(BF16)
    rc = n_rows // FFN_ROW_CHUNKS
    for r in range(FFN_ROW_CHUNKS):
        rs = pl.ds(r * rc, rc)
        x = xb[rs, :]
        a = jnp.dot(x, wg, preferred_element_type=F32)
        b = jnp.dot(x, wu, preferred_element_type=F32)
        h = (a * jax.nn.sigmoid(a) * b).astype(BF16)
        y = jnp.dot(h, wd, preferred_element_type=F32)

        @pl.when(f == 0)
        def _():
            acc[rs, :] = y

        @pl.when(f != 0)
        def _():
            acc[rs, :] += y

    @pl.when(f == n_f - 1)
    def _():
        o_ref[0] = (acc[...] * gate_ref[0]).astype(BF16)


def _expert_ffn(rows_idx, tokens, gate, w_gate, w_up, w_down, layer):
    ns = len(tokens)
    e, d = w_gate.shape[1], w_gate.shape[2]
    counts = [int(ri.shape[2]) for ri in rows_idx]
    offs = [int(sum(counts[:s])) for s in range(ns)]
    r_total = sum(counts)
    assert r_total % (FFN_ROW_CHUNKS * 16) == 0
    smem = lambda cnt, nxt: pl.BlockSpec(
        (1, 1, cnt), (lambda ei, f: (jnp.minimum(ei + 1, e - 1), 0, 0)) if nxt else (lambda ei, f: (ei, 0, 0)),
        memory_space=pltpu.SMEM)
    in_specs = ([smem(cnt, False) for cnt in counts] + [smem(cnt, True) for cnt in counts]
                + [pl.BlockSpec(memory_space=pl.ANY) for _ in range(ns)]
                + [pl.BlockSpec((1, r_total, 1), lambda ei, f: (ei, 0, 0)),
                   pl.BlockSpec((1, 1, d, FFN_TF), lambda ei, f: (layer, ei, 0, f)),
                   pl.BlockSpec((1, 1, d, FFN_TF), lambda ei, f: (layer, ei, 0, f)),
                   pl.BlockSpec((1, 1, FFN_TF, d), lambda ei, f: (layer, ei, f, 0))])
    return pl.pallas_call(
        functools.partial(_ffn_kernel, n_experts=e, sources=tuple(zip(offs, counts)), n_rows=r_total),
        grid=(e, D_EXPERT // FFN_TF),
        in_specs=in_specs,
        out_specs=pl.BlockSpec((1, r_total, d), lambda ei, f: (ei, 0, 0)),
        out_shape=jax.ShapeDtypeStruct((e, r_total, d), BF16),
        scratch_shapes=[pltpu.VMEM((r_total * TOKEN_SUBLANES, LANES), F32), pltpu.VMEM((r_total, d), BF16),
                        pltpu.VMEM((r_total, d), F32), pltpu.SemaphoreType.DMA((1,))],
        compiler_params=_cparams(("arbitrary", "arbitrary")),
        name="expert_ffn",
    )(*rows_idx, *rows_idx, *tokens, gate, w_gate, w_up, w_down)


def _prefix_count(m, n):
    lane = lax.broadcasted_iota(jnp.int32, m.shape, 1)
    c = m
    s = 1
    while s < n:
        c = c + jnp.where(lane >= s, pltpu.roll(c, s, 1), 0)
        s *= 2
    return c


COMBINE_TB = 512
COMBINE_CHUNK = 256


def _topk_kernel(aff_ref, idx_ref, gate_ref, off_ref, *, cap):
    a = aff_ref[...]
    rows, n = a.shape

    def search(i, t):
        cand = t | jnp.left_shift(jnp.int32(1), 30 - i)
        cnt = jnp.sum((a >= pltpu.bitcast(cand, F32)).astype(jnp.int32), axis=1, keepdims=True)
        return jnp.where(cnt >= cap, cand, t)

    thr_bits = lax.fori_loop(0, 31, search, jnp.zeros((rows, 1), jnp.int32))
    thr = pltpu.bitcast(thr_bits, F32)
    gt = a >= pltpu.bitcast(thr_bits + 1, F32)
    eq = (a >= thr) & jnp.logical_not(gt)
    need = cap - jnp.sum(gt.astype(jnp.int32), axis=1, keepdims=True)
    eq_i = eq.astype(jnp.int32)
    sel = gt | (eq & (_prefix_count(eq_i, n) - eq_i < need))
    sel_i = sel.astype(jnp.int32)
    lane = lax.broadcasted_iota(jnp.int32, a.shape, 1)
    dist = jnp.where(sel, lane + 1 - _prefix_count(sel_i, n), 0)
    tok = jnp.where(sel, lane, -1)
    val = a
    s = 1
    while s < n:
        tok_in = pltpu.roll(tok, n - s, 1)
        dist_in = pltpu.roll(dist, n - s, 1)
        val_in = pltpu.roll(val, n - s, 1)
        take = (lane < n - s) & (tok_in >= 0) & ((dist_in & s) != 0)
        stay = (tok >= 0) & ((dist & s) == 0)
        tok = jnp.where(take, tok_in, jnp.where(stay, tok, -1))
        dist = jnp.where(take, dist_in, jnp.where(stay, dist, 0))
        val = jnp.where(take, val_in, val)
        s *= 2
    idx_ref[...] = tok[:, :cap]
    gate_ref[...] = val[:, :cap]
    olane = lax.broadcasted_iota(jnp.int32, off_ref.shape, 1)
    off = jnp.zeros(off_ref.shape, jnp.int32)
    for j in range(n // COMBINE_TB + 1):
        below = jnp.sum(jnp.where(lane < j * COMBINE_TB, sel_i, 0), axis=1, keepdims=True)
        off = jnp.where(olane == j, below, off)
    off_ref[...] = off


def _topk_select(aff, cap):
    b, e, n = aff.shape
    assert n & (n - 1) == 0 and n // COMBINE_TB < LANES
    rows = b * e
    full = lambda w: pl.BlockSpec((rows, w), lambda i: (0, 0))
    idx, gate, off = pl.pallas_call(
        functools.partial(_topk_kernel, cap=cap),
        grid=(1,),
        in_specs=[full(n)],
        out_specs=[full(cap), full(cap), full(LANES)],
        out_shape=[jax.ShapeDtypeStruct((rows, cap), jnp.int32), jax.ShapeDtypeStruct((rows, cap), F32),
                   jax.ShapeDtypeStruct((rows, LANES), jnp.int32)],
        compiler_params=_cparams(("arbitrary",)),
        name="expert_choice_topk",
    )(aff.reshape(rows, n))
    return idx.reshape(b, e, cap), gate.reshape(b, e, cap), off


def _combine_kernel(off_ref, y_ref, idx_ref, x_ref, g_ref, o_ref, acc, ycat, *, chunk, n_chunks):
    bi = pl.program_id(0)
    j = pl.program_id(1)
    tb = x_ref.shape[0]
    tok = j * tb + lax.broadcasted_iota(jnp.int32, (tb, chunk), 0)

    def onehot(e, st):
        return (idx_ref[0, pl.ds(e, 1), pl.ds(st, chunk)] == tok).astype(BF16)

    if n_chunks == 1:
        acc[...] = jnp.zeros_like(acc)
        for e in range(N_EXPERTS):
            acc[...] += jnp.dot(onehot(e, 0), y_ref[e], preferred_element_type=F32)
    else:
        shift = chunk.bit_length() - 1
        spans, hots = [], []
        for e in range(N_EXPERTS):
            lo = off_ref[bi * N_EXPERTS + e, j]
            hi = off_ref[bi * N_EXPERTS + e, j + 1]
            c0 = jnp.minimum(lax.shift_right_logical(lo, shift), n_chunks - 1)
            c1 = lax.shift_right_logical(hi + (chunk - 1), shift)
            spans.append((c0, c1))
            st = pl.multiple_of(c0 * chunk, chunk)
            ycat[pl.ds(e * chunk, chunk), :] = y_ref[e, pl.ds(st, chunk), :]
            hots.append(onehot(e, st))
        acc[...] = jnp.dot(jnp.concatenate(hots, axis=1), ycat[...], preferred_element_type=F32)
        for e in range(N_EXPERTS):
            def add_chunk(c, carry, e=e):
                st = pl.multiple_of(c * chunk, chunk)
                acc[...] += jnp.dot(onehot(e, st), y_ref[e, pl.ds(st, chunk), :], preferred_element_type=F32)
                return carry

            lax.fori_loop(spans[e][0] + 1, spans[e][1], add_chunk, 0)
    o_ref[...] = x_ref[...] + g_ref[0] * acc[...]


def _moe_combine(off, y, idx, x2, g, *, row_block0, cap, seq_len):
    bsz = idx.shape[0]
    d = x2.shape[1]
    tb = min(seq_len, COMBINE_TB)
    nb = seq_len // tb
    chunk = min(cap, COMBINE_CHUNK)
    n_chunks = cap // chunk
    n_mod = g.shape[0]
    grid_spec = pltpu.PrefetchScalarGridSpec(
        num_scalar_prefetch=1,
        grid=(bsz, nb),
        in_specs=[pl.BlockSpec((N_EXPERTS, cap, d), lambda bi, j, off: (0, row_block0 + bi, 0),
                               pipeline_mode=pl.Buffered(1)),
                  pl.BlockSpec((1, N_EXPERTS, cap), lambda bi, j, off: (bi, 0, 0)),
                  pl.BlockSpec((tb, d), lambda bi, j, off: (bi * nb + j, 0)),
                  pl.BlockSpec((1, 1, d), lambda bi, j, off: (bi if n_mod > 1 else 0, 0, 0))],
        out_specs=pl.BlockSpec((tb, d), lambda bi, j, off: (bi * nb + j, 0)),
        scratch_shapes=[pltpu.VMEM((tb, d), F32), pltpu.VMEM((N_EXPERTS * chunk, d), BF16)])
    return pl.pallas_call(
        functools.partial(_combine_kernel, chunk=chunk, n_chunks=n_chunks),
        grid_spec=grid_spec,
        out_shape=jax.ShapeDtypeStruct(x2.shape, F32),
        compiler_params=_cparams(("arbitrary", "arbitrary")),
        name="moe_combine",
    )(off, y, idx, x2, g)


def _expert_choice_moe(streams, w_gate, w_up, w_down, layer):
    routed = []
    for h2, aff, x2, g in streams:
        bsz, _, l = aff.shape
        cap = EC_FACTOR * l // N_EXPERTS
        idx, gate, off = _topk_select(aff, cap)
        rows = (idx + (jnp.arange(bsz, dtype=jnp.int32) * l)[:, None, None]).transpose(1, 0, 2)
        routed.append((idx, off, rows.reshape(N_EXPERTS, 1, bsz * cap), gate.transpose(1, 0, 2).reshape(
            N_EXPERTS, bsz * cap, 1), cap))
    y = _expert_ffn([r[2] for r in routed], [s[0] for s in streams],
                    jnp.concatenate([r[3] for r in routed], axis=1), w_gate, w_up, w_down, layer)
    outs = []
    row0 = 0
    for (h2, aff, x2, g), (idx, off, rows, _, cap) in zip(streams, routed):
        assert row0 % cap == 0
        outs.append(_moe_combine(off, y, idx, x2, g, row_block0=row0 // cap, cap=cap, seq_len=aff.shape[2]))
        row0 += rows.shape[2]
    return outs


HY_R = 128
HY_COLS = 4096
HY_KB = 8
HY_UNROLL = 4


def _dft_consts(r):
    k = np.arange(r)
    ang = 2.0 * np.pi * np.outer(k, k) / r
    fre, fim = np.cos(ang), -np.sin(ang)
    k2 = np.arange(r)
    tang = 2.0 * np.pi * np.outer(k2, k2) / (r * r)
    tw = np.stack([np.cos(tang), -np.sin(tang)])
    fwd = np.block([[fre, -fim], [fim, fre]])
    inv = np.block([[fre, fim], [-fim, fre]])
    return fre, fim, tw, fwd, inv


def _hy_prep_kernel(x_ref, prev_ref, next_ref, w_ref, b_ref, x1_ref, x2_ref, v_ref, *, n_tiles):
    t = pl.program_id(1)
    x = x_ref[0]
    tm = x.shape[0]
    row = lax.broadcasted_iota(jnp.int32, x.shape, 0)
    prev = jnp.where(t == 0, 0.0, prev_ref[0, pl.ds(7, 1), :])
    nxt = jnp.where(t == n_tiles - 1, 0.0, next_ref[0, pl.ds(0, 1), :])
    up = jnp.where(row == 0, prev, pltpu.roll(x, 1, 0))
    dn = jnp.where(row == tm - 1, nxt, pltpu.roll(x, tm - 1, 0))
    y = b_ref[...] + up * w_ref[pl.ds(0, 1), :] + x * w_ref[pl.ds(1, 1), :] + dn * w_ref[pl.ds(2, 1), :]
    x1_ref[0] = y[:, :HY_W].astype(BF16)
    x2_ref[0] = y[:, HY_W:2 * HY_W].astype(BF16)
    v_ref[0] = y[:, 2 * HY_W:].astype(BF16)


def _hyena_prep(u, conv_w, conv_b):
    b, l, c = u.shape
    tm = min(l, 1024)
    n_tiles = l // tm
    g = tm // 8
    n_groups = l // 8
    out = jax.ShapeDtypeStruct((b, l, HY_W), BF16)
    ospec = pl.BlockSpec((1, tm, HY_W), lambda bi, t: (bi, t, 0))
    return pl.pallas_call(
        functools.partial(_hy_prep_kernel, n_tiles=n_tiles),
        grid=(b, n_tiles),
        in_specs=[pl.BlockSpec((1, tm, c), lambda bi, t: (bi, t, 0)),
                  pl.BlockSpec((1, 8, c), lambda bi, t: (bi, jnp.maximum(t * g - 1, 0), 0)),
                  pl.BlockSpec((1, 8, c), lambda bi, t: (bi, jnp.minimum((t + 1) * g, n_groups - 1), 0)),
                  pl.BlockSpec((HY_SHORT, c), lambda bi, t: (0, 0)),
                  pl.BlockSpec((1, c), lambda bi, t: (0, 0))],
        out_specs=[ospec, ospec, ospec],
        out_shape=[out, out, out],
        compiler_params=_cparams(("arbitrary", "arbitrary")),
        name="hyena_prep",
    )(u, u, u, conv_w, conv_b.reshape(1, c))


def _hy_filter_kernel(z_ref, w1_ref, b1_ref, w2_ref, b2_ref, w3_ref, fr_ref, dl_ref, k_ref, ss_ref, *, seq_len):
    i = pl.program_id(0)
    hp = lax.Precision.HIGHEST
    z = z_ref[...]
    tm = z.shape[0]
    fr = fr_ref[...]
    h = jnp.sin(fr * (jnp.dot(z, w1_ref[...], preferred_element_type=F32, precision=hp) + b1_ref[...]))
    h = jnp.sin(fr * (jnp.dot(h, w2_ref[...], preferred_element_type=F32, precision=hp) + b2_ref[...]))
    h = jnp.dot(h.astype(BF16), w3_ref[...].astype(BF16), preferred_element_type=F32)
    dec = jnp.exp(-z[:, 0:1] * dl_ref[...])
    r = i * tm + lax.broadcasted_iota(jnp.int32, (tm, HY_W), 0)
    parts = []
    for o in range(HY_ORDER):
        hf = h[:, (2 * o) * HY_W:(2 * o + 1) * HY_W]
        hb = h[:, (2 * o + 1) * HY_W:(2 * o + 2) * HY_W]
        parts.append(jnp.where(r < seq_len, hf, jnp.where(r > seq_len, hb, 0.0)) * dec)
    k = jnp.concatenate(parts, axis=1)
    k_ref[...] = k
    ss = jnp.sum(k * k, axis=0, keepdims=True)

    @pl.when(i == 0)
    def _():
        ss_ref[...] = ss

    @pl.when(i != 0)
    def _():
        ss_ref[...] += ss


def _hyena_filter_taps(l, w1, b1, w2, b2, w3, freq):
    t = jnp.linspace(0.0, 1.0, l, dtype=F32)[:, None]
    w = (2.0 * math.pi / l) * jnp.arange(l, dtype=F32)[:, None]
    f = jnp.linspace(1e-4, HY_BANDS - 1, HY_BANDS, dtype=F32)[None, :]
    z = jnp.concatenate([t, jnp.cos(f * w), -jnp.sin(f * w)], axis=-1)
    z2 = jnp.concatenate([z, z[:1], z[:0:-1]], axis=0)
    emb = z.shape[1]
    z2 = jnp.pad(z2, ((0, 0), (0, LANES - emb)))
    w1p = jnp.pad(w1, ((0, LANES - emb), (0, 0)))
    deltas = jnp.abs(jnp.linspace(math.log(HY_DECAY_TARGET) / HY_SLOW_DECAY,
                                  math.log(HY_DECAY_TARGET) / HY_FAST_DECAY, HY_W, dtype=F32))
    fo = w2.shape[0]
    nout = w3.shape[1]
    tm = min(2 * l, 1024)
    const = lambda i: (0, 0)
    return pl.pallas_call(
        functools.partial(_hy_filter_kernel, seq_len=l),
        grid=(2 * l // tm,),
        in_specs=[pl.BlockSpec((tm, LANES), lambda i: (i, 0)),
                  pl.BlockSpec((LANES, fo), const), pl.BlockSpec((1, fo), const),
                  pl.BlockSpec((fo, fo), const), pl.BlockSpec((1, fo), const),
                  pl.BlockSpec((fo, nout), const), pl.BlockSpec((1, fo), const),
                  pl.BlockSpec((1, HY_W), const)],
        out_specs=[pl.BlockSpec((tm, HY_ORDER * HY_W), lambda i: (i, 0)),
                   pl.BlockSpec((1, HY_ORDER * HY_W), const)],
        out_shape=[jax.ShapeDtypeStruct((2 * l, HY_ORDER * HY_W), F32),
                   jax.ShapeDtypeStruct((1, HY_ORDER * HY_W), F32)],
        compiler_params=_cparams(("arbitrary",)),
        name="hyena_filter_taps",
    )(z2, w1p, b1.reshape(1, fo), w2, b2.reshape(1, fo), w3, freq.reshape(1, fo), deltas.reshape(1, HY_W))


def _left_matmul_kernel(f_ref, x_ref, o_ref):
    o_ref[0] = jnp.dot(f_ref[...], x_ref[0].astype(BF16), preferred_element_type=F32).astype(o_ref.dtype)


def _left_matmul(f_bf, x3, name):
    b, k, c = x3.shape
    m = f_bf.shape[0]
    return pl.pallas_call(
        _left_matmul_kernel,
        grid=(b, c // HY_COLS),
        in_specs=[pl.BlockSpec((m, k), lambda bi, j: (0, 0)),
                  pl.BlockSpec((1, k, HY_COLS), lambda bi, j: (bi, 0, j))],
        out_specs=pl.BlockSpec((1, m, HY_COLS), lambda bi, j: (bi, 0, j)),
        out_shape=jax.ShapeDtypeStruct((b, m, c), BF16),
        compiler_params=_cparams(("arbitrary", "arbitrary")),
        name=name,
    )(f_bf, x3)


def _twiddle_cols(tw_ref, k1):
    lane = lax.broadcasted_iota(jnp.int32, (HY_R, HY_R), 1)
    pick = lane == k1
    twr = jnp.sum(jnp.where(pick, tw_ref[0], 0.0), axis=1, keepdims=True)
    twi = jnp.sum(jnp.where(pick, tw_ref[1], 0.0), axis=1, keepdims=True)
    return twr, twi


def _hy_spectrum_kernel(a_ref, tw_ref, fwd_ref, ss_ref, kf_ref):
    kb = pl.program_id(0)
    scale = lax.rsqrt(ss_ref[...] + EPS) * (1.0 / (HY_R * HY_R))

    def body(kk, carry):
        twr, twi = _twiddle_cols(tw_ref, kb * HY_KB + kk)
        are = a_ref[0, 0, kk].astype(F32)
        aim = a_ref[0, 1, kk].astype(F32)
        bst = jnp.concatenate([are * twr - aim * twi, are * twi + aim * twr], axis=0).astype(BF16)
        x = jnp.dot(fwd_ref[...], bst, preferred_element_type=F32)
        kf_ref[0, kk] = x[:HY_R] * scale
        kf_ref[1, kk] = x[HY_R:] * scale
        return carry

    lax.fori_loop(0, HY_KB, body, 0, unroll=HY_UNROLL)


def _hy_spectrum(a5, tw, fwd_bf, ss):
    nch = a5.shape[-1]
    return pl.pallas_call(
        _hy_spectrum_kernel,
        grid=(HY_R // HY_KB,),
        in_specs=[pl.BlockSpec((1, 2, HY_KB, HY_R, nch), lambda kb: (0, 0, kb, 0, 0)),
                  pl.BlockSpec((2, HY_R, HY_R), lambda kb: (0, 0, 0)),
                  pl.BlockSpec((2 * HY_R, 2 * HY_R), lambda kb: (0, 0)),
                  pl.BlockSpec((1, nch), lambda kb: (0, 0))],
        out_specs=pl.BlockSpec((2, HY_KB, HY_R, nch), lambda kb: (0, kb, 0, 0)),
        out_shape=jax.ShapeDtypeStruct((2, HY_R, HY_R, nch), F32),
        compiler_params=_cparams(("arbitrary",)),
        name="hyena_filter_spectrum",
    )(a5, tw, fwd_bf, ss)


def _hy_mid_kernel(a_ref, kf_ref, tw_ref, fwd_ref, inv_ref, z_ref):
    kb = pl.program_id(0)

    def body(kk, carry):
        twr, twi = _twiddle_cols(tw_ref, kb * HY_KB + kk)
        are = a_ref[0, 0, kk].astype(F32)
        aim = a_ref[0, 1, kk].astype(F32)
        bst = jnp.concatenate([are * twr - aim * twi, are * twi + aim * twr], axis=0).astype(BF16)
        x = jnp.dot(fwd_ref[...], bst, preferred_element_type=F32)
        xre, xim = x[:HY_R], x[HY_R:]
        kre, kim = kf_ref[0, kk], kf_ref[1, kk]
        yst = jnp.concatenate([xre * kre - xim * kim, xre * kim + xim * kre], axis=0).astype(BF16)
        zz = jnp.dot(inv_ref[...], yst, preferred_element_type=F32)
        zre, zim = zz[:HY_R], zz[HY_R:]
        z_ref[0, 0, kk] = (zre * twr + zim * twi).astype(BF16)
        z_ref[0, 1, kk] = (zim * twr - zre * twi).astype(BF16)
        return carry

    lax.fori_loop(0, HY_KB, body, 0, unroll=HY_UNROLL)


def _hy_mid(a5, kf, order, tw, fwd_bf, inv_bf):
    b = a5.shape[0]
    blk = (1, 2, HY_KB, HY_R, HY_W)
    return pl.pallas_call(
        _hy_mid_kernel,
        grid=(HY_R // HY_KB, b),
        in_specs=[pl.BlockSpec(blk, lambda kb, bi: (bi, 0, kb, 0, 0)),
                  pl.BlockSpec((2, HY_KB, HY_R, HY_W), lambda kb, bi: (0, kb, 0, order)),
                  pl.BlockSpec((2, HY_R, HY_R), lambda kb, bi: (0, 0, 0)),
                  pl.BlockSpec((2 * HY_R, 2 * HY_R), lambda kb, bi: (0, 0)),
                  pl.BlockSpec((2 * HY_R, 2 * HY_R), lambda kb, bi: (0, 0))],
        out_specs=pl.BlockSpec(blk, lambda kb, bi: (bi, 0, kb, 0, 0)),
        out_shape=jax.ShapeDtypeStruct(a5.shape, BF16),
        compiler_params=_cparams(("arbitrary", "arbitrary")),
        name="hyena_spectral_product",
    )(a5, kf, tw, fwd_bf, inv_bf)


def _hy_out_kernel(f_ref, z_ref, g_ref, w_ref, sk_ref, o_ref):
    y = jnp.dot(f_ref[...], z_ref[0], preferred_element_type=F32)
    w = w_ref[0].astype(F32)
    o_ref[0] = (g_ref[0].astype(F32) * (y + w * sk_ref[...])).astype(BF16)


def _hy_out(f_bf, z3, gate3, w3, skip_row):
    b, m2, c = z3.shape
    m = f_bf.shape[0]
    dspec = pl.BlockSpec((1, m, HY_COLS), lambda bi, j: (bi, 0, j))
    return pl.pallas_call(
        _hy_out_kernel,
        grid=(b, c // HY_COLS),
        in_specs=[pl.BlockSpec((m, m2), lambda bi, j: (0, 0)),
                  pl.BlockSpec((1, m2, HY_COLS), lambda bi, j: (bi, 0, j)),
                  dspec, dspec,
                  pl.BlockSpec((1, HY_COLS), lambda bi, j: (0, 0))],
        out_specs=dspec,
        out_shape=jax.ShapeDtypeStruct((b, m, c), BF16),
        compiler_params=_cparams(("arbitrary", "arbitrary")),
        name="hyena_inverse_gate",
    )(f_bf, z3, gate3, w3, skip_row)


def _hyena_long(u, conv_w, conv_b, skip, filt_args):
    b, l, _ = u.shape
    assert 2 * l == HY_R * HY_R
    fre, fim, tw, fwd, inv = _dft_consts(HY_R)
    half = HY_R // 2
    as_bf = lambda m: jnp.asarray(m, F32).astype(BF16)
    f_a_data = as_bf(np.concatenate([fre[:, :half], fim[:, :half]], axis=0))
    f_a_filt = as_bf(np.concatenate([fre, fim], axis=0))
    f_c = as_bf(np.concatenate([fre[:half], fim[:half]], axis=1))
    tw_j = jnp.asarray(tw, F32)
    fwd_bf = as_bf(fwd)
    inv_bf = as_bf(inv)

    taps, ss = _hyena_filter_taps(l, *filt_args)
    nch = HY_ORDER * HY_W
    ak = _left_matmul(f_a_filt, taps.reshape(1, HY_R, HY_R * nch), "hyena_filter_dft_a")
    kf = _hy_spectrum(ak.reshape(1, 2, HY_R, HY_R, nch), tw_j, fwd_bf, ss)

    x1, x2, v = _hyena_prep(u, conv_w, conv_b)
    z = v
    for o, gate in enumerate((x1, x2)):
        z3 = z.reshape(b, half, HY_R * HY_W)
        a = _left_matmul(f_a_data, z3, "hyena_dft_a")
        zs = _hy_mid(a.reshape(b, 2, HY_R, HY_R, HY_W), kf, o, tw_j, fwd_bf, inv_bf)
        skip_row = jnp.tile(skip[o], HY_COLS // HY_W).reshape(1, HY_COLS)
        z = _hy_out(f_c, zs.reshape(b, 2 * HY_R, HY_R * HY_W), gate.reshape(b, half, HY_R * HY_W), z3,
                    skip_row).reshape(b, l, HY_W)
    return z


def _hy_ctx_kernel(x1_ref, x2_ref, v_ref, taps_ref, ss_ref, fa_ref, fc_ref, sk_ref, o_ref, *, seq_len):
    n2 = 2 * seq_len
    fa = fa_ref[...]
    scale = lax.rsqrt(ss_ref[...] + EPS) * (1.0 / n2)
    kf = jnp.dot(fa, taps_ref[...].astype(BF16), preferred_element_type=F32) * scale
    z = v_ref[0].astype(F32)
    for o, g_ref in enumerate((x1_ref, x2_ref)):
        x = jnp.dot(fa[:, :seq_len], z.astype(BF16), preferred_element_type=F32)
        xre, xim = x[:n2], x[n2:]
        kre = kf[:n2, o * HY_W:(o + 1) * HY_W]
        kim = kf[n2:, o * HY_W:(o + 1) * HY_W]
        yst = jnp.concatenate([xre * kre - xim * kim, xre * kim + xim * kre], axis=0).astype(BF16)
        y = jnp.dot(fc_ref[...], yst, preferred_element_type=F32)
        z = g_ref[0].astype(F32) * (y + z * sk_ref[pl.ds(o, 1), :])
    o_ref[0] = z.astype(BF16)


def _hyena_short(u, conv_w, conv_b, skip, filt_args):
    b, l, _ = u.shape
    n2 = 2 * l
    k = np.arange(n2)
    ang = 2.0 * np.pi * np.outer(k, k) / n2
    fre, fim = np.cos(ang), -np.sin(ang)
    fa = jnp.asarray(np.concatenate([fre, fim], axis=0), F32).astype(BF16)
    fc = jnp.asarray(np.concatenate([fre[:l], fim[:l]], axis=1), F32).astype(BF16)
    taps, ss = _hyena_filter_taps(l, *filt_args)
    x1, x2, v = _hyena_prep(u, conv_w, conv_b)
    nch = HY_ORDER * HY_W
    dspec = pl.BlockSpec((1, l, HY_W), lambda bi: (bi, 0, 0))
    const = lambda bi: (0, 0)
    return pl.pallas_call(
        functools.partial(_hy_ctx_kernel, seq_len=l),
        grid=(b,),
        in_specs=[dspec, dspec, dspec,
                  pl.BlockSpec((n2, nch), const), pl.BlockSpec((1, nch), const),
                  pl.BlockSpec((2 * n2, n2), const), pl.BlockSpec((l, 2 * n2), const),
                  pl.BlockSpec((HY_ORDER, HY_W), const)],
        out_specs=dspec,
        out_shape=jax.ShapeDtypeStruct((b, l, HY_W), BF16),
        compiler_params=_cparams(("arbitrary",)),
        name="hyena_context",
    )(x1, x2, v, taps, ss, fa, fc, skip)


def _rope_tables(n):
    t = jnp.arange(n, dtype=jnp.int32)
    pos = jnp.stack([t // GRID_W, t % GRID_W], axis=-1).astype(F32)
    nf = HEAD_DIM // 4
    inv = ROPE_THETA ** (-jnp.arange(nf, dtype=F32) / nf)
    ang = pos[:, :, None] * inv
    cos, sin = jnp.cos(ang), jnp.sin(ang)
    c64 = jnp.concatenate([cos[:, 0], cos[:, 0], cos[:, 1], cos[:, 1]], axis=-1)
    s64 = jnp.concatenate([-sin[:, 0], sin[:, 0], -sin[:, 1], sin[:, 1]], axis=-1)
    return jnp.tile(c64, (1, 2)), jnp.tile(s64, (1, 2))


def _block_diag(pool_w):
    g, c, _ = pool_w.shape
    out = jnp.zeros((g * c, g * c), pool_w.dtype)
    for i in range(g):
        out = out.at[i * c:(i + 1) * c, i * c:(i + 1) * c].set(pool_w[i])
    return out


def kernel(x, c, ctx, c_ctx, w_mod, b_mod, norm1_w, norm2_w, w_in, w_out, q_norm_w, k_norm_w, na_rpb,
           hy_conv_w, hy_conv_b, hy_w1, hy_b1, hy_w2, hy_b2, hy_w3, hy_freq, hy_skip,
           pool_w, pool_scale, w_router, w_gate, w_up, w_down):
    b, n, d = x.shape
    lc = ctx.shape[1]
    depth = w_mod.shape[0]
    rows = n // GRID_W
    cos128, sin128 = _rope_tables(n)
    seg = jnp.asarray(np.kron(np.eye(2), np.full((HEAD_DIM, HEAD_DIM), 1.0 / HEAD_DIM)), BF16)
    cc = jnp.zeros((8, d), F32).at[:b].set(c).at[b].set(c_ctx)

    xl = x.reshape(b * n, d)
    xc = ctx.reshape(b * lc, d)
    for i in range(depth):
        last = i == depth - 1
        mod = _modulation(cc, w_mod, b_mod[i], i)
        ml = mod[:b].reshape(b, 1, 6, d)
        mc = mod[b].reshape(1, 1, 6, d)
        sh1, sc1, g1, sh2, sc2, g2 = [ml[:, :, j] for j in range(6)]
        csh1, csc1, cg1, csh2, csc2, cg2 = [mc[:, :, j] for j in range(6)]
        w_in_bf = w_in[i].astype(BF16)
        w_out_bf = w_out[i].astype(BF16)
        w_router_t = w_router[i].T
        pool_bd = _block_diag(pool_w[i]).astype(BF16)
        hy_args = (hy_w1[i], hy_b1[i], hy_w2[i], hy_b2[i], hy_w3[i], hy_freq[i])

        qc, kc, vc, hyc, poolc = _in_projection(
            xc, csh1, csc1, norm1_w[i], w_in_bf, q_norm_w[i], k_norm_w[i], cos128, sin128, seg,
            rows_per_mod=b * lc, seq_len=lc, tm=lc, rope=False)
        kc = kc.reshape(b, lc, NA_W)
        vc = vc.reshape(b, lc, NA_W)
        if not last:
            na_c = _dense_attention(qc.reshape(b, lc, NA_W), kc, vc).reshape(b * lc, NA_W)
            hy_c = _hyena_short(hyc.reshape(b, lc, 3 * HY_W), hy_conv_w[i], hy_conv_b[i], hy_skip[i], hy_args)
            pool_c = _pool_mixer(poolc.reshape(b, lc, POOL_W), pool_bd, pool_scale[i])
            xc1, h2c, affc = _out_projection(
                na_c, hy_c.reshape(b * lc, HY_W).astype(BF16), pool_c.reshape(b * lc, POOL_W), w_out_bf,
                xc, cg1, norm2_w[i], csh2, csc2, w_router_t, rows_per_mod=b * lc, seq_len=lc, tm=lc)

        qr, qp, kl, vl, hyl, pooll = _in_projection(
            xl, sh1, sc1, norm1_w[i], w_in_bf, q_norm_w[i], k_norm_w[i], cos128, sin128, seg,
            rows_per_mod=n, seq_len=n, tm=512, rope=True)
        bias = _na_bias_tables(na_rpb[i], rows)
        o_na = _neighborhood_attention(qr.reshape(b, n, NA_W), qp.reshape(b, n, NA_W), kl.reshape(b, n, NA_W),
                                       vl.reshape(b, n, NA_W), kc, vc, bias)
        hy_l = _hyena_long(hyl.reshape(b, n, 3 * HY_W), hy_conv_w[i], hy_conv_b[i], hy_skip[i], hy_args)
        pool_l = _pool_mixer(pooll.reshape(b, n, POOL_W), pool_bd, pool_scale[i])
        xl1, h2l, affl = _out_projection(
            o_na.reshape(b * n, NA_W), hy_l.reshape(b * n, HY_W).astype(BF16), pool_l.reshape(b * n, POOL_W),
            w_out_bf, xl, g1, norm2_w[i], sh2, sc2, w_router_t, rows_per_mod=n, seq_len=n, tm=512)
        streams = [(h2l, affl, xl1, g2)]
        if not last:
            streams.append((h2c, affc, xc1, cg2))
        outs = _expert_choice_moe(streams, w_gate, w_up, w_down, i)
        xl = outs[0]
        if not last:
            xc = outs[1]
    return xl.reshape(b, n, d)
```

```python
import functools
import math

import jax
import jax.numpy as jnp
import numpy as np
from jax import lax
from jax.experimental import pallas as pl
from jax.experimental.pallas import tpu as pltpu

F32 = jnp.float32
BF16 = jnp.bfloat16

D_MODEL = 1024
GRID_W = 64
NA_HEADS = 8
HEAD_DIM = 64
NA_W = NA_HEADS * HEAD_DIM
NA_WIN_R = 8
NA_WIN_C = 16
ROPE_THETA = 10000.0
HY_W = 256
HY_ORDER = 2
HY_SHORT = 3
HY_BANDS = 16
HY_DECAY_TARGET = 1e-2
HY_FAST_DECAY = 0.3
HY_SLOW_DECAY = 1.5
POOL_W = 256
POOL_SIZES = (2, 4, 8, 16)
POOL_GROUP = POOL_W // len(POOL_SIZES)
MIX_W = NA_W + HY_W + POOL_W
IN_W = 3 * NA_W + (HY_ORDER + 1) * HY_W + POOL_W
N_EXPERTS = 16
EC_FACTOR = 2
D_EXPERT = 2048
EPS = 1e-6

LANES = 128
TOKEN_SUBLANES = D_MODEL // 128
NEG_BIG = -1e30
VMEM_LIMIT = 56 * 1024 * 1024

NA_Q_ROWS = 4
NA_K_ROWS = NA_Q_ROWS + NA_WIN_R - 1
NA_TILES_PER_STEP = 2


def _cparams(sem):
    return pltpu.CompilerParams(dimension_semantics=sem, vmem_limit_bytes=VMEM_LIMIT)


def _mod_kernel(c_ref, w_ref, b_ref, o_ref):
    c = c_ref[...]
    s = c * jax.nn.sigmoid(c)
    o_ref[...] = jnp.dot(s, w_ref[0], preferred_element_type=F32,
                         precision=lax.Precision.HIGHEST) + b_ref[...]


def _modulation(cc, w_mod, b_mod, layer):
    rows, d = cc.shape
    n = w_mod.shape[2]
    tn = 1536
    return pl.pallas_call(
        _mod_kernel,
        grid=(n // tn,),
        in_specs=[pl.BlockSpec((rows, d), lambda j: (0, 0)),
                  pl.BlockSpec((1, d, tn), lambda j: (layer, 0, j)),
                  pl.BlockSpec((1, tn), lambda j: (0, j))],
        out_specs=pl.BlockSpec((rows, tn), lambda j: (0, j)),
        out_shape=jax.ShapeDtypeStruct((rows, n), F32),
        compiler_params=_cparams(("arbitrary",)),
        name="modulation",
    )(cc, w_mod, b_mod.reshape(1, n))


def _head_norm(xs, w128, seg):
    m = jnp.dot((xs * xs).astype(BF16), seg, preferred_element_type=F32)
    return xs * lax.rsqrt(m + EPS) * w128


def _rope(xs, c, s_signed):
    lane = lax.broadcasted_iota(jnp.int32, xs.shape, 1)
    partner = jnp.where((lane & 16) == 0, pltpu.roll(xs, LANES - 16, 1), pltpu.roll(xs, 16, 1))
    return xs * c + partner * s_signed


def _inproj_kernel(x_ref, sh_ref, sc_ref, nw_ref, w_ref, qw_ref, kw_ref, cos_ref, sin_ref, seg_ref,
                   *out_refs, rope):
    if rope:
        qr_ref, qp_ref, k_ref, v_ref, hy_ref, pool_ref = out_refs
    else:
        qp_ref, k_ref, v_ref, hy_ref, pool_ref = out_refs
    x = x_ref[...]
    ms = jnp.mean(x * x, axis=-1, keepdims=True)
    h = x * lax.rsqrt(ms + EPS) * nw_ref[...]
    h = h * (1.0 + sc_ref[0]) + sh_ref[0]
    u = jnp.dot(h.astype(BF16), w_ref[...], preferred_element_type=F32)
    seg = seg_ref[...]
    qk_scale = HEAD_DIM ** -0.5
    for ch in range(NA_W // LANES):
        sl = slice(ch * LANES, (ch + 1) * LANES)
        qn = _head_norm(u[:, sl], qw_ref[...], seg)
        kn = _head_norm(u[:, NA_W + ch * LANES:NA_W + (ch + 1) * LANES], kw_ref[...], seg)
        qp_ref[:, sl] = (qn * qk_scale).astype(BF16)
        if rope:
            c = cos_ref[...]
            s = sin_ref[...]
            qr_ref[:, sl] = (_rope(qn, c, s) * qk_scale).astype(BF16)
            k_ref[:, sl] = _rope(kn, c, s).astype(BF16)
        else:
            k_ref[:, sl] = kn.astype(BF16)
    v_ref[...] = u[:, 2 * NA_W:3 * NA_W].astype(BF16)
    hy_ref[...] = u[:, 3 * NA_W:3 * NA_W + 3 * HY_W]
    pool_ref[...] = u[:, 3 * NA_W + 3 * HY_W:]


def _in_projection(x2, shift, scale, norm_w, w_in_bf, q_norm_w, k_norm_w, cos128, sin128, seg, *,
                   rows_per_mod, seq_len, tm, rope):
    m, d = x2.shape
    tiles_per_mod = rows_per_mod // tm
    tiles_per_seq = seq_len // tm
    row = lambda i: (i, 0)
    const = lambda i: (0, 0)
    modi = lambda i: (i // tiles_per_mod, 0, 0)
    pos = lambda i: (i % tiles_per_seq, 0)
    outs = []
    if rope:
        outs.append((NA_W, BF16))
    outs += [(NA_W, BF16), (NA_W, BF16), (NA_W, BF16), (3 * HY_W, F32), (POOL_W, F32)]
    return pl.pallas_call(
        functools.partial(_inproj_kernel, rope=rope),
        grid=(m // tm,),
        in_specs=[pl.BlockSpec((tm, d), row),
                  pl.BlockSpec((1, 1, d), modi), pl.BlockSpec((1, 1, d), modi),
                  pl.BlockSpec((1, d), const),
                  pl.BlockSpec((d, IN_W), const),
                  pl.BlockSpec((1, LANES), const), pl.BlockSpec((1, LANES), const),
                  pl.BlockSpec((tm, LANES), pos), pl.BlockSpec((tm, LANES), pos),
                  pl.BlockSpec((LANES, LANES), const)],
        out_specs=[pl.BlockSpec((tm, w), row) for w, _ in outs],
        out_shape=[jax.ShapeDtypeStruct((m, w), dt) for w, dt in outs],
        compiler_params=_cparams(("arbitrary",)),
        name="in_projection_rope" if rope else "in_projection_ctx",
    )(x2, shift, scale, norm_w.reshape(1, d), w_in_bf,
      jnp.tile(q_norm_w, 2).reshape(1, LANES), jnp.tile(k_norm_w, 2).reshape(1, LANES),
      cos128, sin128, seg)


def _softmax_pv(s_list, v_list):
    mx = functools.reduce(jnp.maximum, [jnp.max(s, axis=-1, keepdims=True) for s in s_list])
    ps = [jnp.exp(s - mx) for s in s_list]
    l = functools.reduce(jnp.add, [jnp.sum(p, axis=-1, keepdims=True) for p in ps])
    o = functools.reduce(jnp.add, [jnp.dot(p.astype(BF16), v, preferred_element_type=F32)
                                   for p, v in zip(ps, v_list)])
    return o / l


_NT = (((1,), (1,)), ((), ()))


def _na_kernel(qr_ref, qp_ref, k_ref, v_ref, kc_ref, vc_ref, bias_ref, o_ref, *, rows):
    n_tiles = rows // NA_Q_ROWS
    tq = NA_Q_ROWS * GRID_W
    kc = kc_ref[0]
    vc = vc_ref[0]
    lane = lax.broadcasted_iota(jnp.int32, (tq, LANES), 1)
    for sub in range(NA_TILES_PER_STEP):
        t = pl.program_id(2) * NA_TILES_PER_STEP + sub
        ks = jnp.clip(t * NA_Q_ROWS - NA_WIN_R // 2, 0, rows - NA_K_ROWS)
        start = pl.multiple_of(ks * GRID_W, GRID_W)
        k = k_ref[0, pl.ds(start, NA_K_ROWS * GRID_W), :]
        v = v_ref[0, pl.ds(start, NA_K_ROWS * GRID_W), :]
        q = qr_ref[0, pl.ds(sub * tq, tq), :]
        qp = qp_ref[0, pl.ds(sub * tq, tq), :]
        kind = jnp.where(t == 0, 0, jnp.where(t == n_tiles - 1, 2, 1))
        outs = []
        for hh in range(2):
            keep = (lane < HEAD_DIM) if hh == 0 else (lane >= HEAD_DIM)
            zero = jnp.zeros_like(q)
            s_lat = (lax.dot_general(jnp.where(keep, q, zero), k, _NT, preferred_element_type=F32)
                     + bias_ref[kind, hh])
            s_ctx = lax.dot_general(jnp.where(keep, qp, zero), kc, _NT, preferred_element_type=F32)
            outs.append(_softmax_pv([s_lat, s_ctx], [v, vc]))
        o_ref[0, pl.ds(sub * tq, tq), :] = jnp.where(lane < HEAD_DIM, outs[0], outs[1]).astype(BF16)


def _neighborhood_attention(qr, qp, k, v, kc, vc, bias):
    b, n, _ = qr.shape
    lc = kc.shape[1]
    rows = n // GRID_W
    n_steps = rows // (NA_Q_ROWS * NA_TILES_PER_STEP)
    tq = NA_Q_ROWS * GRID_W * NA_TILES_PER_STEP
    tk = NA_K_ROWS * GRID_W
    qmap = lambda bi, hp, t: (bi, t, hp)
    smap = lambda bi, hp, t: (bi, 0, hp)
    bmap = lambda bi, hp, t: (0, hp, 0, 0)
    return pl.pallas_call(
        functools.partial(_na_kernel, rows=rows),
        grid=(b, NA_W // LANES, n_steps),
        in_specs=[pl.BlockSpec((1, tq, LANES), qmap), pl.BlockSpec((1, tq, LANES), qmap),
                  pl.BlockSpec((1, n, LANES), smap), pl.BlockSpec((1, n, LANES), smap),
                  pl.BlockSpec((1, lc, LANES), smap), pl.BlockSpec((1, lc, LANES), smap),
                  pl.BlockSpec((3, 2, NA_Q_ROWS * GRID_W, tk), bmap)],
        out_specs=pl.BlockSpec((1, tq, LANES), qmap),
        out_shape=jax.ShapeDtypeStruct((b, n, NA_W), BF16),
        compiler_params=_cparams(("arbitrary", "arbitrary", "arbitrary")),
        name="neighborhood_attention",
    )(qr, qp, k, v, kc, vc, bias)


def _na_bias_tables(rpb, rows):
    n_tiles = rows // NA_Q_ROWS
    n_dr, n_dc = 2 * NA_WIN_R - 1, 2 * NA_WIN_C - 1
    qc = np.arange(GRID_W)
    cs = np.clip(qc - NA_WIN_C // 2, 0, GRID_W - NA_WIN_C)
    kcol = np.arange(GRID_W)
    ok_c = (kcol[None, :] >= cs[:, None]) & (kcol[None, :] < cs[:, None] + NA_WIN_C)
    dc = kcol[None, :] - qc[:, None] + NA_WIN_C - 1
    col_sel = np.concatenate([(dc[:, :, None] == np.arange(n_dc)) & ok_c[:, :, None],
                              ~ok_c[:, :, None]], axis=2)
    ext = jnp.pad(rpb.astype(F32), ((0, 0), (0, 1), (0, 1)), constant_values=NEG_BIG)
    blocks = jnp.einsum("hrd,qkd->hrqk", ext, jnp.asarray(col_sel, F32), precision=lax.Precision.HIGHEST)
    tabs = []
    for t in (0, 1, n_tiles - 1):
        r0 = t * NA_Q_ROWS
        ks = int(np.clip(r0 - NA_WIN_R // 2, 0, rows - NA_K_ROWS))
        r = r0 + np.arange(NA_Q_ROWS)
        rs = np.clip(r - NA_WIN_R // 2, 0, rows - NA_WIN_R)
        krow = ks + np.arange(NA_K_ROWS)
        ok_r = (krow[None, :] >= rs[:, None]) & (krow[None, :] < rs[:, None] + NA_WIN_R)
        dr = np.where(ok_r, krow[None, :] - r[:, None] + NA_WIN_R - 1, n_dr)
        tabs.append(jnp.concatenate(
            [jnp.concatenate([blocks[:, int(dr[i, j])] for j in range(NA_K_ROWS)], axis=-1)
             for i in range(NA_Q_ROWS)], axis=-2))
    return jnp.stack(tabs)


def _dense_attn_kernel(q_ref, k_ref, v_ref, o_ref):
    q = q_ref[0]
    k = k_ref[0]
    v = v_ref[0]
    lane = lax.broadcasted_iota(jnp.int32, q.shape, 1)
    outs = []
    for hh in range(2):
        keep = (lane < HEAD_DIM) if hh == 0 else (lane >= HEAD_DIM)
        s = lax.dot_general(jnp.where(keep, q, jnp.zeros_like(q)), k, _NT, preferred_element_type=F32)
        outs.append(_softmax_pv([s], [v]))
    o_ref[0] = jnp.where(lane < HEAD_DIM, outs[0], outs[1]).astype(BF16)


def _dense_attention(q, k, v):
    b, l, _ = q.shape
    spec = pl.BlockSpec((1, l, LANES), lambda bi, hp: (bi, 0, hp))
    return pl.pallas_call(
        _dense_attn_kernel,
        grid=(b, NA_W // LANES),
        in_specs=[spec, spec, spec],
        out_specs=spec,
        out_shape=jax.ShapeDtypeStruct((b, l, NA_W), BF16),
        compiler_params=_cparams(("arbitrary", "arbitrary")),
        name="context_attention",
    )(q, k, v)


POOL_HALO = max(POOL_SIZES) // 2


def _pool_kernel(x_ref, w_ref, sc_ref, o_ref, xpad, *, seq_len, chunk):
    zeros = jnp.zeros((POOL_HALO, POOL_W), F32)
    xpad[pl.ds(0, POOL_HALO), :] = zeros
    xpad[pl.ds(seq_len + POOL_HALO, POOL_HALO), :] = zeros
    xpad[pl.ds(POOL_HALO, seq_len), :] = x_ref[0]
    span = chunk + 2 * POOL_HALO
    w = w_ref[...]
    sc = sc_ref[...]

    def body(c, carry):
        base = pl.multiple_of(c * chunk, chunk)
        xs = xpad[pl.ds(base, span), :]
        s2 = xs + pltpu.roll(xs, 1, 0)
        s4 = pltpu.roll(s2, 1, 0) + pltpu.roll(s2, span - 1, 0)
        s8 = pltpu.roll(s4, 2, 0) + pltpu.roll(s4, span - 2, 0)
        s16 = pltpu.roll(s8, 4, 0) + pltpu.roll(s8, span - 4, 0)
        mid = slice(POOL_HALO, POOL_HALO + chunk)
        grp = lax.broadcasted_iota(jnp.int32, (chunk, POOL_W), 1) // POOL_GROUP
        tpos = base + lax.broadcasted_iota(jnp.int32, (chunk, POOL_W), 0)
        half = jnp.left_shift(1, grp)
        cnt = (jnp.minimum(tpos + half, seq_len) - jnp.maximum(tpos - half, 0)).astype(F32)
        ssum = jnp.where(grp == 0, s2[mid], jnp.where(grp == 1, s4[mid], jnp.where(grp == 2, s8[mid], s16[mid])))
        diff = ssum / cnt - xs[mid]
        y = jnp.dot(diff.astype(BF16), w, preferred_element_type=F32) * sc
        o_ref[0, pl.ds(base, chunk), :] = y.astype(BF16)
        return carry

    lax.fori_loop(0, seq_len // chunk, body, 0)


def _pool_mixer(u, w_blockdiag_bf, pool_scale):
    b, l, _ = u.shape
    chunk = min(l, 512)
    return pl.pallas_call(
        functools.partial(_pool_kernel, seq_len=l, chunk=chunk),
        grid=(b,),
        in_specs=[pl.BlockSpec((1, l, POOL_W), lambda bi: (bi, 0, 0)),
                  pl.BlockSpec((POOL_W, POOL_W), lambda bi: (0, 0)),
                  pl.BlockSpec((1, POOL_W), lambda bi: (0, 0))],
        out_specs=pl.BlockSpec((1, l, POOL_W), lambda bi: (bi, 0, 0)),
        out_shape=jax.ShapeDtypeStruct((b, l, POOL_W), BF16),
        scratch_shapes=[pltpu.VMEM((l + 2 * POOL_HALO, POOL_W), F32)],
        compiler_params=_cparams(("arbitrary",)),
        name="pool_mixer",
    )(u, w_blockdiag_bf, pool_scale.reshape(1, POOL_W))


def _outproj_kernel(na_ref, hy_ref, pl_ref, w_ref, x_ref, g1_ref, nw_ref, sh_ref, sc_ref, wr_ref,
                    xn_ref, h2_ref, aff_ref):
    w = w_ref
    acc = jnp.dot(na_ref[...], w[pl.ds(0, NA_W), :], preferred_element_type=F32)
    acc += jnp.dot(hy_ref[...], w[pl.ds(NA_W, HY_W), :], preferred_element_type=F32)
    acc += jnp.dot(pl_ref[...], w[pl.ds(NA_W + HY_W, POOL_W), :], preferred_element_type=F32)
    xn = x_ref[...] + g1_ref[0] * acc
    xn_ref[...] = xn
    ms = jnp.mean(xn * xn, axis=-1, keepdims=True)
    h2 = xn * lax.rsqrt(ms + EPS) * nw_ref[...]
    h2 = h2 * (1.0 + sc_ref[0]) + sh_ref[0]
    tm = h2.shape[0]
    for s in range(TOKEN_SUBLANES):
        h2_ref[pl.ds(s, tm, stride=TOKEN_SUBLANES), :] = h2[:, s * LANES:(s + 1) * LANES]
    logits = lax.dot_general(wr_ref[...], h2, _NT, preferred_element_type=F32,
                             precision=lax.Precision.HIGHEST)
    mx = jnp.max(logits, axis=0, keepdims=True)
    p = jnp.exp(logits - mx)
    aff_ref[0] = p / jnp.sum(p, axis=0, keepdims=True)


def _out_projection(na, hy, pool, w_out_bf, x2, g1, norm_w, shift, scale, w_router_t, *,
                    rows_per_mod, seq_len, tm):
    m, d = x2.shape
    tiles_per_mod = rows_per_mod // tm
    tiles_per_seq = seq_len // tm
    row = lambda i: (i, 0)
    const = lambda i: (0, 0)
    modi = lambda i: (i // tiles_per_mod, 0, 0)
    return pl.pallas_call(
        _outproj_kernel,
        grid=(m // tm,),
        in_specs=[pl.BlockSpec((tm, NA_W), row), pl.BlockSpec((tm, HY_W), row), pl.BlockSpec((tm, POOL_W), row),
                  pl.BlockSpec((MIX_W, d), const),
                  pl.BlockSpec((tm, d), row),
                  pl.BlockSpec((1, 1, d), modi),
                  pl.BlockSpec((1, d), const),
                  pl.BlockSpec((1, 1, d), modi), pl.BlockSpec((1, 1, d), modi),
                  pl.BlockSpec((N_EXPERTS, d), const)],
        out_specs=[pl.BlockSpec((tm, d), row), pl.BlockSpec((tm * TOKEN_SUBLANES, LANES), row),
                   pl.BlockSpec((1, N_EXPERTS, tm), lambda i: (i // tiles_per_seq, 0, i % tiles_per_seq))],
        out_shape=[jax.ShapeDtypeStruct((m, d), F32), jax.ShapeDtypeStruct((m * TOKEN_SUBLANES, LANES), F32),
                   jax.ShapeDtypeStruct((m // seq_len, N_EXPERTS, seq_len), F32)],
        compiler_params=_cparams(("arbitrary",)),
        name="out_projection_router",
    )(na, hy, pool, w_out_bf, x2, g1, norm_w.reshape(1, d), shift, scale, w_router_t)


FFN_TF = 256
FFN_ROW_CHUNKS = 2


def _ffn_kernel(*refs, n_experts, sources, n_rows):
    ns = len(sources)
    idx_cur = refs[0:ns]
    idx_nxt = refs[ns:2 * ns]
    src = refs[2 * ns:3 * ns]
    gate_ref, wg_ref, wu_ref, wd_ref, o_ref, stage, xb, acc, sem = refs[3 * ns:]
    e = pl.program_id(0)
    f = pl.program_id(1)
    n_f = pl.num_programs(1)

    ts = TOKEN_SUBLANES

    def row_copy(s, row, r):
        return pltpu.make_async_copy(src[s].at[pl.ds(pl.multiple_of(row * ts, ts), ts)],
                                     stage.at[pl.ds(pl.multiple_of((sources[s][0] + r) * ts, ts), ts)], sem.at[0])

    def gather(idx_refs):
        for s in range(ns):
            def body(r, carry, s=s):
                row_copy(s, idx_refs[s][0, 0, r], r).start()
                return carry
            lax.fori_loop(0, sources[s][1], body, 0, unroll=8)

    def gather_wait():
        for s in range(ns):
            off, cnt = sources[s]
            pltpu.make_async_copy(src[s].at[pl.ds(0, cnt * ts)], stage.at[pl.ds(off * ts, cnt * ts)],
                                  sem.at[0]).wait()

    @pl.when(f == 0)
    def _():
        @pl.when(e == 0)
        def _():
            gather(idx_cur)

        gather_wait()
        for s in range(ts):
            xb[:, s * LANES:(s + 1) * LANES] = stage[pl.ds(s, n_rows, stride=ts), :].astype(BF16)

        @pl.when(e + 1 < n_experts)
        def _():
            gather(idx_nxt)

    wg = wg_ref[0, 0].astype(BF16)
    wu = wu_ref[0, 0].astype(BF16)
    wd = wd_ref[0, 0].astype(BF16)
    rc = n_rows // FFN_ROW_CHUNKS
    for r in range(FFN_ROW_CHUNKS):
        rs = pl.ds(r * rc, rc)
        x = xb[rs, :]
        a = jnp.dot(x, wg, preferred_element_type=F32)
        b = jnp.dot(x, wu, preferred_element_type=F32)
        h = (a * jax.nn.sigmoid(a) * b).astype(BF16)
        y = jnp.dot(h, wd, preferred_element_type=F32)

        @pl.when(f == 0)
        def _():
            acc[rs, :] = y

        @pl.when(f != 0)
        def _():
            acc[rs, :] += y

    @pl.when(f == n_f - 1)
    def _():
        o_ref[0] = (acc[...] * gate_ref[0]).astype(BF16)


def _expert_ffn(rows_idx, tokens, gate, w_gate, w_up, w_down, layer):
    ns = len(tokens)
    e, d = w_gate.shape[1], w_gate.shape[2]
    counts = [int(ri.shape[2]) for ri in rows_idx]
    offs = [int(sum(counts[:s])) for s in range(ns)]
    r_total = sum(counts)
    assert r_total % (FFN_ROW_CHUNKS * 16) == 0
    smem = lambda cnt, nxt: pl.BlockSpec(
        (1, 1, cnt), (lambda ei, f: (jnp.minimum(ei + 1, e - 1), 0, 0)) if nxt else (lambda ei, f: (ei, 0, 0)),
        memory_space=pltpu.SMEM)
    in_specs = ([smem(cnt, False) for cnt in counts] + [smem(cnt, True) for cnt in counts]
                + [pl.BlockSpec(memory_space=pl.ANY) for _ in range(ns)]
                + [pl.BlockSpec((1, r_total, 1), lambda ei, f: (ei, 0, 0)),
                   pl.BlockSpec((1, 1, d, FFN_TF), lambda ei, f: (layer, ei, 0, f)),
                   pl.BlockSpec((1, 1, d, FFN_TF), lambda ei, f: (layer, ei, 0, f)),
                   pl.BlockSpec((1, 1, FFN_TF, d), lambda ei, f: (layer, ei, f, 0))])
    return pl.pallas_call(
        functools.partial(_ffn_kernel, n_experts=e, sources=tuple(zip(offs, counts)), n_rows=r_total),
        grid=(e, D_EXPERT // FFN_TF),
        in_specs=in_specs,
        out_specs=pl.BlockSpec((1, r_total, d), lambda ei, f: (ei, 0, 0)),
        out_shape=jax.ShapeDtypeStruct((e, r_total, d), BF16),
        scratch_shapes=[pltpu.VMEM((r_total * TOKEN_SUBLANES, LANES), F32), pltpu.VMEM((r_total, d), BF16),
                        pltpu.VMEM((r_total, d), F32), pltpu.SemaphoreType.DMA((1,))],
        compiler_params=_cparams(("arbitrary", "arbitrary")),
        name="expert_ffn",
    )(*rows_idx, *rows_idx, *tokens, gate, w_gate, w_up, w_down)


def _prefix_count(m, n):
    lane = lax.broadcasted_iota(jnp.int32, m.shape, 1)
    c = m
    s = 1
    while s < n:
        c = c + jnp.where(lane >= s, pltpu.roll(c, s, 1), 0)
        s *= 2
    return c


COMBINE_TB = 256
COMBINE_CHUNK = 128


def _topk_kernel(aff_ref, idx_ref, gate_ref, off_ref, *, cap):
    a = aff_ref[...]
    rows, n = a.shape

    def search(i, t):
        cand = t | jnp.left_shift(jnp.int32(1), 30 - i)
        cnt = jnp.sum((a >= pltpu.bitcast(cand, F32)).astype(jnp.int32), axis=1, keepdims=True)
        return jnp.where(cnt >= cap, cand, t)

    thr_bits = lax.fori_loop(0, 31, search, jnp.zeros((rows, 1), jnp.int32))
    thr = pltpu.bitcast(thr_bits, F32)
    gt = a >= pltpu.bitcast(thr_bits + 1, F32)
    eq = (a >= thr) & jnp.logical_not(gt)
    need = cap - jnp.sum(gt.astype(jnp.int32), axis=1, keepdims=True)
    eq_i = eq.astype(jnp.int32)
    sel = gt | (eq & (_prefix_count(eq_i, n) - eq_i < need))
    sel_i = sel.astype(jnp.int32)
    lane = lax.broadcasted_iota(jnp.int32, a.shape, 1)
    dist = jnp.where(sel, lane + 1 - _prefix_count(sel_i, n), 0)
    tok = jnp.where(sel, lane, -1)
    val = a
    s = 1
    while s < n:
        tok_in = pltpu.roll(tok, n - s, 1)
        dist_in = pltpu.roll(dist, n - s, 1)
        val_in = pltpu.roll(val, n - s, 1)
        take = (lane < n - s) & (tok_in >= 0) & ((dist_in & s) != 0)
        stay = (tok >= 0) & ((dist & s) == 0)
        tok = jnp.where(take, tok_in, jnp.where(stay, tok, -1))
        dist = jnp.where(take, dist_in, jnp.where(stay, dist, 0))
        val = jnp.where(take, val_in, val)
        s *= 2
    idx_ref[...] = tok[:, :cap]
    gate_ref[...] = val[:, :cap]
    olane = lax.broadcasted_iota(jnp.int32, off_ref.shape, 1)
    off = jnp.zeros(off_ref.shape, jnp.int32)
    for j in range(n // COMBINE_TB + 1):
        below = jnp.sum(jnp.where(lane < j * COMBINE_TB, sel_i, 0), axis=1, keepdims=True)
        off = jnp.where(olane == j, below, off)
    off_ref[...] = off


def _topk_select(aff, cap):
    b, e, n = aff.shape
    assert n & (n - 1) == 0 and n // COMBINE_TB < LANES
    rows = b * e
    full = lambda w: pl.BlockSpec((rows, w), lambda i: (0, 0))
    idx, gate, off = pl.pallas_call(
        functools.partial(_topk_kernel, cap=cap),
        grid=(1,),
        in_specs=[full(n)],
        out_specs=[full(cap), full(cap), full(LANES)],
        out_shape=[jax.ShapeDtypeStruct((rows, cap), jnp.int32), jax.ShapeDtypeStruct((rows, cap), F32),
                   jax.ShapeDtypeStruct((rows, LANES), jnp.int32)],
        compiler_params=_cparams(("arbitrary",)),
        name="expert_choice_topk",
    )(aff.reshape(rows, n))
    return idx.reshape(b, e, cap), gate.reshape(b, e, cap), off


def _combine_kernel(off_ref, y_ref, idx_ref, x_ref, g_ref, o_ref, acc, ycat, *, chunk, n_chunks):
    bi = pl.program_id(0)
    j = pl.program_id(1)
    tb = x_ref.shape[0]
    tok = j * tb + lax.broadcasted_iota(jnp.int32, (tb, chunk), 0)

    def onehot(e, st):
        return (idx_ref[0, pl.ds(e, 1), pl.ds(st, chunk)] == tok).astype(BF16)

    if n_chunks == 1:
        acc[...] = jnp.zeros_like(acc)
        for e in range(N_EXPERTS):
            acc[...] += jnp.dot(onehot(e, 0), y_ref[e], preferred_element_type=F32)
    else:
        shift = chunk.bit_length() - 1
        spans, hots = [], []
        for e in range(N_EXPERTS):
            lo = off_ref[bi * N_EXPERTS + e, j]
            hi = off_ref[bi * N_EXPERTS + e, j + 1]
            c0 = jnp.minimum(lax.shift_right_logical(lo, shift), n_chunks - 1)
            c1 = lax.shift_right_logical(hi + (chunk - 1), shift)
            spans.append((c0, c1))
            st = pl.multiple_of(c0 * chunk, chunk)
            ycat[pl.ds(e * chunk, chunk), :] = y_ref[e, pl.ds(st, chunk), :]
            hots.append(onehot(e, st))
        acc[...] = jnp.dot(jnp.concatenate(hots, axis=1), ycat[...], preferred_element_type=F32)
        for e in range(N_EXPERTS):
            def add_chunk(c, carry, e=e):
                st = pl.multiple_of(c * chunk, chunk)
                acc[...] += jnp.dot(onehot(e, st), y_ref[e, pl.ds(st, chunk), :], preferred_element_type=F32)
                return carry

            lax.fori_loop(spans[e][0] + 1, spans[e][1], add_chunk, 0)
    o_ref[...] = x_ref[...] + g_ref[0] * acc[...]


def _moe_combine(off, y, idx, x2, g, *, row_block0, cap, seq_len):
    bsz = idx.shape[0]
    d = x2.shape[1]
    tb = min(seq_len, COMBINE_TB)
    nb = seq_len // tb
    chunk = min(cap, COMBINE_CHUNK)
    n_chunks = cap // chunk
    n_mod = g.shape[0]
    grid_spec = pltpu.PrefetchScalarGridSpec(
        num_scalar_prefetch=1,
        grid=(bsz, nb),
        in_specs=[pl.BlockSpec((N_EXPERTS, cap, d), lambda bi, j, off: (0, row_block0 + bi, 0),
                               pipeline_mode=pl.Buffered(1)),
                  pl.BlockSpec((1, N_EXPERTS, cap), lambda bi, j, off: (bi, 0, 0)),
                  pl.BlockSpec((tb, d), lambda bi, j, off: (bi * nb + j, 0)),
                  pl.BlockSpec((1, 1, d), lambda bi, j, off: (bi if n_mod > 1 else 0, 0, 0))],
        out_specs=pl.BlockSpec((tb, d), lambda bi, j, off: (bi * nb + j, 0)),
        scratch_shapes=[pltpu.VMEM((tb, d), F32), pltpu.VMEM((N_EXPERTS * chunk, d), BF16)])
    return pl.pallas_call(
        functools.partial(_combine_kernel, chunk=chunk, n_chunks=n_chunks),
        grid_spec=grid_spec,
        out_shape=jax.ShapeDtypeStruct(x2.shape, F32),
        compiler_params=_cparams(("arbitrary", "arbitrary")),
        name="moe_combine",
    )(off, y, idx, x2, g)


def _expert_choice_moe(streams, w_gate, w_up, w_down, layer):
    routed = []
    for h2, aff, x2, g in streams:
        bsz, _, l = aff.shape
        cap = EC_FACTOR * l // N_EXPERTS
        idx, gate, off = _topk_select(aff, cap)
        rows = (idx + (jnp.arange(bsz, dtype=jnp.int32) * l)[:, None, None]).transpose(1, 0, 2)
        routed.append((idx, off, rows.reshape(N_EXPERTS, 1, bsz * cap), gate.transpose(1, 0, 2).reshape(
            N_EXPERTS, bsz * cap, 1), cap))
    y = _expert_ffn([r[2] for r in routed], [s[0] for s in streams],
                    jnp.concatenate([r[3] for r in routed], axis=1), w_gate, w_up, w_down, layer)
    outs = []
    row0 = 0
    for (h2, aff, x2, g), (idx, off, rows, _, cap) in zip(streams, routed):
        assert row0 % cap == 0
        outs.append(_moe_combine(off, y, idx, x2, g, row_block0=row0 // cap, cap=cap, seq_len=aff.shape[2]))
        row0 += rows.shape[2]
    return outs


HY_R = 128
HY_COLS = 4096
HY_KB = 8
HY_UNROLL = 4


def _dft_consts(r):
    k = np.arange(r)
    ang = 2.0 * np.pi * np.outer(k, k) / r
    fre, fim = np.cos(ang), -np.sin(ang)
    k2 = np.arange(r)
    tang = 2.0 * np.pi * np.outer(k2, k2) / (r * r)
    tw = np.stack([np.cos(tang), -np.sin(tang)])
    fwd = np.block([[fre, -fim], [fim, fre]])
    inv = np.block([[fre, fim], [-fim, fre]])
    return fre, fim, tw, fwd, inv


def _hy_prep_kernel(x_ref, prev_ref, next_ref, w_ref, b_ref, x1_ref, x2_ref, v_ref, *, n_tiles):
    t = pl.program_id(1)
    x = x_ref[0]
    tm = x.shape[0]
    row = lax.broadcasted_iota(jnp.int32, x.shape, 0)
    prev = jnp.where(t == 0, 0.0, prev_ref[0, pl.ds(7, 1), :])
    nxt = jnp.where(t == n_tiles - 1, 0.0, next_ref[0, pl.ds(0, 1), :])
    up = jnp.where(row == 0, prev, pltpu.roll(x, 1, 0))
    dn = jnp.where(row == tm - 1, nxt, pltpu.roll(x, tm - 1, 0))
    y = b_ref[...] + up * w_ref[pl.ds(0, 1), :] + x * w_ref[pl.ds(1, 1), :] + dn * w_ref[pl.ds(2, 1), :]
    x1_ref[0] = y[:, :HY_W].astype(BF16)
    x2_ref[0] = y[:, HY_W:2 * HY_W].astype(BF16)
    v_ref[0] = y[:, 2 * HY_W:].astype(BF16)


def _hyena_prep(u, conv_w, conv_b):
    b, l, c = u.shape
    tm = min(l, 1024)
    n_tiles = l // tm
    g = tm // 8
    n_groups = l // 8
    out = jax.ShapeDtypeStruct((b, l, HY_W), BF16)
    ospec = pl.BlockSpec((1, tm, HY_W), lambda bi, t: (bi, t, 0))
    return pl.pallas_call(
        functools.partial(_hy_prep_kernel, n_tiles=n_tiles),
        grid=(b, n_tiles),
        in_specs=[pl.BlockSpec((1, tm, c), lambda bi, t: (bi, t, 0)),
                  pl.BlockSpec((1, 8, c), lambda bi, t: (bi, jnp.maximum(t * g - 1, 0), 0)),
                  pl.BlockSpec((1, 8, c), lambda bi, t: (bi, jnp.minimum((t + 1) * g, n_groups - 1), 0)),
                  pl.BlockSpec((HY_SHORT, c), lambda bi, t: (0, 0)),
                  pl.BlockSpec((1, c), lambda bi, t: (0, 0))],
        out_specs=[ospec, ospec, ospec],
        out_shape=[out, out, out],
        compiler_params=_cparams(("arbitrary", "arbitrary")),
        name="hyena_prep",
    )(u, u, u, conv_w, conv_b.reshape(1, c))


def _hy_filter_kernel(z_ref, w1_ref, b1_ref, w2_ref, b2_ref, w3_ref, fr_ref, dl_ref, k_ref, ss_ref, *, seq_len):
    i = pl.program_id(0)
    hp = lax.Precision.HIGHEST
    z = z_ref[...]
    tm = z.shape[0]
    half = tm // 2
    fr = fr_ref[...]
    zz = jnp.concatenate([z[:half], z[half:]], axis=1)
    h = jnp.sin(fr * (jnp.dot(zz, w1_ref[...], preferred_element_type=F32, precision=hp) + b1_ref[...]))
    h = jnp.sin(fr * (jnp.dot(h, w2_ref[...], preferred_element_type=F32, precision=hp) + b2_ref[...]))
    hb = h.astype(BF16)
    w3 = w3_ref[...].astype(BF16)
    fo = w3.shape[0]
    h = jnp.concatenate([jnp.dot(hb[:, :fo], w3, preferred_element_type=F32),
                         jnp.dot(hb[:, fo:], w3, preferred_element_type=F32)], axis=0)
    dec = jnp.exp(-z[:, 0:1] * dl_ref[...])
    r = i * tm + lax.broadcasted_iota(jnp.int32, (tm, HY_W), 0)
    parts = []
    for o in range(HY_ORDER):
        hf = h[:, (2 * o) * HY_W:(2 * o + 1) * HY_W]
        hb = h[:, (2 * o + 1) * HY_W:(2 * o + 2) * HY_W]
        parts.append(jnp.where(r < seq_len, hf, jnp.where(r > seq_len, hb, 0.0)) * dec)
    k = jnp.concatenate(parts, axis=1)
    k_ref[...] = k
    ss = jnp.sum(k * k, axis=0, keepdims=True)

    @pl.when(i == 0)
    def _():
        ss_ref[...] = ss

    @pl.when(i != 0)
    def _():
        ss_ref[...] += ss


def _hyena_filter_taps(l, w1, b1, w2, b2, w3, freq):
    t = jnp.linspace(0.0, 1.0, l, dtype=F32)[:, None]
    w = (2.0 * math.pi / l) * jnp.arange(l, dtype=F32)[:, None]
    f = jnp.linspace(1e-4, HY_BANDS - 1, HY_BANDS, dtype=F32)[None, :]
    z = jnp.concatenate([t, jnp.cos(f * w), -jnp.sin(f * w)], axis=-1)
    z2 = jnp.concatenate([z, z[:1], z[:0:-1]], axis=0)
    emb = z.shape[1]
    z2 = jnp.pad(z2, ((0, 0), (0, LANES - emb)))
    w1p = jnp.pad(w1, ((0, LANES - emb), (0, 0)))
    two = lambda m: jnp.kron(jnp.eye(2, dtype=m.dtype), m)
    pair = lambda v: jnp.tile(v, 2).reshape(1, 2 * v.shape[0])
    deltas = jnp.abs(jnp.linspace(math.log(HY_DECAY_TARGET) / HY_SLOW_DECAY,
                                  math.log(HY_DECAY_TARGET) / HY_FAST_DECAY, HY_W, dtype=F32))
    fo = w2.shape[0]
    nout = w3.shape[1]
    tm = min(2 * l, 1024)
    const = lambda i: (0, 0)
    return pl.pallas_call(
        functools.partial(_hy_filter_kernel, seq_len=l),
        grid=(2 * l // tm,),
        in_specs=[pl.BlockSpec((tm, LANES), lambda i: (i, 0)),
                  pl.BlockSpec((2 * LANES, 2 * fo), const), pl.BlockSpec((1, 2 * fo), const),
                  pl.BlockSpec((2 * fo, 2 * fo), const), pl.BlockSpec((1, 2 * fo), const),
                  pl.BlockSpec((fo, nout), const), pl.BlockSpec((1, 2 * fo), const),
                  pl.BlockSpec((1, HY_W), const)],
        out_specs=[pl.BlockSpec((tm, HY_ORDER * HY_W), lambda i: (i, 0)),
                   pl.BlockSpec((1, HY_ORDER * HY_W), const)],
        out_shape=[jax.ShapeDtypeStruct((2 * l, HY_ORDER * HY_W), F32),
                   jax.ShapeDtypeStruct((1, HY_ORDER * HY_W), F32)],
        compiler_params=_cparams(("arbitrary",)),
        name="hyena_filter_taps",
    )(z2, two(w1p), pair(b1), two(w2), pair(b2), w3, pair(freq), deltas.reshape(1, HY_W))


def _left_matmul_kernel(f_ref, x_ref, o_ref):
    o_ref[0] = jnp.dot(f_ref[...], x_ref[0].astype(BF16), preferred_element_type=F32).astype(o_ref.dtype)


def _left_matmul(f_bf, x3, name):
    b, k, c = x3.shape
    m = f_bf.shape[0]
    return pl.pallas_call(
        _left_matmul_kernel,
        grid=(b, c // HY_COLS),
        in_specs=[pl.BlockSpec((m, k), lambda bi, j: (0, 0)),
                  pl.BlockSpec((1, k, HY_COLS), lambda bi, j: (bi, 0, j))],
        out_specs=pl.BlockSpec((1, m, HY_COLS), lambda bi, j: (bi, 0, j)),
        out_shape=jax.ShapeDtypeStruct((b, m, c), BF16),
        compiler_params=_cparams(("arbitrary", "arbitrary")),
        name=name,
    )(f_bf, x3)


def _twiddle_cols(tw_ref, k1):
    lane = lax.broadcasted_iota(jnp.int32, (HY_R, HY_R), 1)
    pick = lane == k1
    twr = jnp.sum(jnp.where(pick, tw_ref[0], 0.0), axis=1, keepdims=True)
    twi = jnp.sum(jnp.where(pick, tw_ref[1], 0.0), axis=1, keepdims=True)
    return twr, twi


def _hy_spectrum_kernel(a_ref, tw_ref, fwd_ref, ss_ref, kf_ref):
    kb = pl.program_id(0)
    scale = lax.rsqrt(ss_ref[...] + EPS) * (1.0 / (HY_R * HY_R))

    def body(kk, carry):
        twr, twi = _twiddle_cols(tw_ref, kb * HY_KB + kk)
        are = a_ref[0, 0, kk].astype(F32)
        aim = a_ref[0, 1, kk].astype(F32)
        bst = jnp.concatenate([are * twr - aim * twi, are * twi + aim * twr], axis=0).astype(BF16)
        x = jnp.dot(fwd_ref[...], bst, preferred_element_type=F32)
        kf_ref[0, kk] = x[:HY_R] * scale
        kf_ref[1, kk] = x[HY_R:] * scale
        return carry

    lax.fori_loop(0, HY_KB, body, 0, unroll=HY_UNROLL)


def _hy_spectrum(a5, tw, fwd_bf, ss):
    nch = a5.shape[-1]
    return pl.pallas_call(
        _hy_spectrum_kernel,
        grid=(HY_R // HY_KB,),
        in_specs=[pl.BlockSpec((1, 2, HY_KB, HY_R, nch), lambda kb: (0, 0, kb, 0, 0)),
                  pl.BlockSpec((2, HY_R, HY_R), lambda kb: (0, 0, 0)),
                  pl.BlockSpec((2 * HY_R, 2 * HY_R), lambda kb: (0, 0)),
                  pl.BlockSpec((1, nch), lambda kb: (0, 0))],
        out_specs=pl.BlockSpec((2, HY_KB, HY_R, nch), lambda kb: (0, kb, 0, 0)),
        out_shape=jax.ShapeDtypeStruct((2, HY_R, HY_R, nch), F32),
        compiler_params=_cparams(("arbitrary",)),
        name="hyena_filter_spectrum",
    )(a5, tw, fwd_bf, ss)


def _hy_mid_kernel(a_ref, kf_ref, tw_ref, fwd_ref, inv_ref, z_ref):
    kb = pl.program_id(0)

    def body(kk, carry):
        twr, twi = _twiddle_cols(tw_ref, kb * HY_KB + kk)
        are = a_ref[0, 0, kk].astype(F32)
        aim = a_ref[0, 1, kk].astype(F32)
        bst = jnp.concatenate([are * twr - aim * twi, are * twi + aim * twr], axis=0).astype(BF16)
        x = jnp.dot(fwd_ref[...], bst, preferred_element_type=F32)
        xre, xim = x[:HY_R], x[HY_R:]
        kre, kim = kf_ref[0, kk], kf_ref[1, kk]
        yst = jnp.concatenate([xre * kre - xim * kim, xre * kim + xim * kre], axis=0).astype(BF16)
        zz = jnp.dot(inv_ref[...], yst, preferred_element_type=F32)
        zre, zim = zz[:HY_R], zz[HY_R:]
        z_ref[0, 0, kk] = (zre * twr + zim * twi).astype(BF16)
        z_ref[0, 1, kk] = (zim * twr - zre * twi).astype(BF16)
        return carry

    lax.fori_loop(0, HY_KB, body, 0, unroll=HY_UNROLL)


def _hy_mid(a5, kf, order, tw, fwd_bf, inv_bf):
    b = a5.shape[0]
    blk = (1, 2, HY_KB, HY_R, HY_W)
    return pl.pallas_call(
        _hy_mid_kernel,
        grid=(HY_R // HY_KB, b),
        in_specs=[pl.BlockSpec(blk, lambda kb, bi: (bi, 0, kb, 0, 0)),
                  pl.BlockSpec((2, HY_KB, HY_R, HY_W), lambda kb, bi: (0, kb, 0, order)),
                  pl.BlockSpec((2, HY_R, HY_R), lambda kb, bi: (0, 0, 0)),
                  pl.BlockSpec((2 * HY_R, 2 * HY_R), lambda kb, bi: (0, 0)),
                  pl.BlockSpec((2 * HY_R, 2 * HY_R), lambda kb, bi: (0, 0))],
        out_specs=pl.BlockSpec(blk, lambda kb, bi: (bi, 0, kb, 0, 0)),
        out_shape=jax.ShapeDtypeStruct(a5.shape, BF16),
        compiler_params=_cparams(("arbitrary", "arbitrary")),
        name="hyena_spectral_product",
    )(a5, kf, tw, fwd_bf, inv_bf)


def _hy_out_kernel(f_ref, z_ref, g_ref, w_ref, sk_ref, o_ref):
    y = jnp.dot(f_ref[...], z_ref[0], preferred_element_type=F32)
    w = w_ref[0].astype(F32)
    o_ref[0] = (g_ref[0].astype(F32) * (y + w * sk_ref[...])).astype(BF16)


def _hy_out(f_bf, z3, gate3, w3, skip_row):
    b, m2, c = z3.shape
    m = f_bf.shape[0]
    dspec = pl.BlockSpec((1, m, HY_COLS), lambda bi, j: (bi, 0, j))
    return pl.pallas_call(
        _hy_out_kernel,
        grid=(b, c // HY_COLS),
        in_specs=[pl.BlockSpec((m, m2), lambda bi, j: (0, 0)),
                  pl.BlockSpec((1, m2, HY_COLS), lambda bi, j: (bi, 0, j)),
                  dspec, dspec,
                  pl.BlockSpec((1, HY_COLS), lambda bi, j: (0, 0))],
        out_specs=dspec,
        out_shape=jax.ShapeDtypeStruct((b, m, c), BF16),
        compiler_params=_cparams(("arbitrary", "arbitrary")),
        name="hyena_inverse_gate",
    )(f_bf, z3, gate3, w3, skip_row)


def _hyena_long(u, conv_w, conv_b, skip, filt_args):
    b, l, _ = u.shape
    assert 2 * l == HY_R * HY_R
    fre, fim, tw, fwd, inv = _dft_consts(HY_R)
    half = HY_R // 2
    as_bf = lambda m: jnp.asarray(m, F32).astype(BF16)
    f_a_data = as_bf(np.concatenate([fre[:, :half], fim[:, :half]], axis=0))
    f_a_filt = as_bf(np.concatenate([fre, fim], axis=0))
    f_c = as_bf(np.concatenate([fre[:half], fim[:half]], axis=1))
    tw_j = jnp.asarray(tw, F32)
    fwd_bf = as_bf(fwd)
    inv_bf = as_bf(inv)

    taps, ss = _hyena_filter_taps(l, *filt_args)
    nch = HY_ORDER * HY_W
    ak = _left_matmul(f_a_filt, taps.reshape(1, HY_R, HY_R * nch), "hyena_filter_dft_a")
    kf = _hy_spectrum(ak.reshape(1, 2, HY_R, HY_R, nch), tw_j, fwd_bf, ss)

    x1, x2, v = _hyena_prep(u, conv_w, conv_b)
    z = v
    for o, gate in enumerate((x1, x2)):
        z3 = z.reshape(b, half, HY_R * HY_W)
        a = _left_matmul(f_a_data, z3, "hyena_dft_a")
        zs = _hy_mid(a.reshape(b, 2, HY_R, HY_R, HY_W), kf, o, tw_j, fwd_bf, inv_bf)
        skip_row = jnp.tile(skip[o], HY_COLS // HY_W).reshape(1, HY_COLS)
        z = _hy_out(f_c, zs.reshape(b, 2 * HY_R, HY_R * HY_W), gate.reshape(b, half, HY_R * HY_W), z3,
                    skip_row).reshape(b, l, HY_W)
    return z


def _hy_ctx_kernel(x1_ref, x2_ref, v_ref, taps_ref, ss_ref, fa_ref, fc_ref, sk_ref, o_ref, *, seq_len):
    n2 = 2 * seq_len
    fa = fa_ref[...]
    scale = lax.rsqrt(ss_ref[...] + EPS) * (1.0 / n2)
    kf = jnp.dot(fa, taps_ref[...].astype(BF16), preferred_element_type=F32) * scale
    z = v_ref[0].astype(F32)
    for o, g_ref in enumerate((x1_ref, x2_ref)):
        x = jnp.dot(fa[:, :seq_len], z.astype(BF16), preferred_element_type=F32)
        xre, xim = x[:n2], x[n2:]
        kre = kf[:n2, o * HY_W:(o + 1) * HY_W]
        kim = kf[n2:, o * HY_W:(o + 1) * HY_W]
        yst = jnp.concatenate([xre * kre - xim * kim, xre * kim + xim * kre], axis=0).astype(BF16)
        y = jnp.dot(fc_ref[...], yst, preferred_element_type=F32)
        z = g_ref[0].astype(F32) * (y + z * sk_ref[pl.ds(o, 1), :])
    o_ref[0] = z.astype(BF16)


def _hyena_short(u, conv_w, conv_b, skip, filt_args):
    b, l, _ = u.shape
    n2 = 2 * l
    k = np.arange(n2)
    ang = 2.0 * np.pi * np.outer(k, k) / n2
    fre, fim = np.cos(ang), -np.sin(ang)
    fa = jnp.asarray(np.concatenate([fre, fim], axis=0), F32).astype(BF16)
    fc = jnp.asarray(np.concatenate([fre[:l], fim[:l]], axis=1), F32).astype(BF16)
    taps, ss = _hyena_filter_taps(l, *filt_args)
    x1, x2, v = _hyena_prep(u, conv_w, conv_b)
    nch = HY_ORDER * HY_W
    dspec = pl.BlockSpec((1, l, HY_W), lambda bi: (bi, 0, 0))
    const = lambda bi: (0, 0)
    return pl.pallas_call(
        functools.partial(_hy_ctx_kernel, seq_len=l),
        grid=(b,),
        in_specs=[dspec, dspec, dspec,
                  pl.BlockSpec((n2, nch), const), pl.BlockSpec((1, nch), const),
                  pl.BlockSpec((2 * n2, n2), const), pl.BlockSpec((l, 2 * n2), const),
                  pl.BlockSpec((HY_ORDER, HY_W), const)],
        out_specs=dspec,
        out_shape=jax.ShapeDtypeStruct((b, l, HY_W), BF16),
        compiler_params=_cparams(("arbitrary",)),
        name="hyena_context",
    )(x1, x2, v, taps, ss, fa, fc, skip)


def _rope_tables(n):
    t = jnp.arange(n, dtype=jnp.int32)
    pos = jnp.stack([t // GRID_W, t % GRID_W], axis=-1).astype(F32)
    nf = HEAD_DIM // 4
    inv = ROPE_THETA ** (-jnp.arange(nf, dtype=F32) / nf)
    ang = pos[:, :, None] * inv
    cos, sin = jnp.cos(ang), jnp.sin(ang)
    c64 = jnp.concatenate([cos[:, 0], cos[:, 0], cos[:, 1], cos[:, 1]], axis=-1)
    s64 = jnp.concatenate([-sin[:, 0], sin[:, 0], -sin[:, 1], sin[:, 1]], axis=-1)
    return jnp.tile(c64, (1, 2)), jnp.tile(s64, (1, 2))


def _block_diag(pool_w):
    g, c, _ = pool_w.shape
    out = jnp.zeros((g * c, g * c), pool_w.dtype)
    for i in range(g):
        out = out.at[i * c:(i + 1) * c, i * c:(i + 1) * c].set(pool_w[i])
    return out


def kernel(x, c, ctx, c_ctx, w_mod, b_mod, norm1_w, norm2_w, w_in, w_out, q_norm_w, k_norm_w, na_rpb,
           hy_conv_w, hy_conv_b, hy_w1, hy_b1, hy_w2, hy_b2, hy_w3, hy_freq, hy_skip,
           pool_w, pool_scale, w_router, w_gate, w_up, w_down):
    b, n, d = x.shape
    lc = ctx.shape[1]
    depth = w_mod.shape[0]
    rows = n // GRID_W
    cos128, sin128 = _rope_tables(n)
    seg = jnp.asarray(np.kron(np.eye(2), np.full((HEAD_DIM, HEAD_DIM), 1.0 / HEAD_DIM)), BF16)
    cc = jnp.zeros((8, d), F32).at[:b].set(c).at[b].set(c_ctx)

    xl = x.reshape(b * n, d)
    xc = ctx.reshape(b * lc, d)
    for i in range(depth):
        last = i == depth - 1
        mod = _modulation(cc, w_mod, b_mod[i], i)
        ml = mod[:b].reshape(b, 1, 6, d)
        mc = mod[b].reshape(1, 1, 6, d)
        sh1, sc1, g1, sh2, sc2, g2 = [ml[:, :, j] for j in range(6)]
        csh1, csc1, cg1, csh2, csc2, cg2 = [mc[:, :, j] for j in range(6)]
        w_in_bf = w_in[i].astype(BF16)
        w_out_bf = w_out[i].astype(BF16)
        w_router_t = w_router[i].T
        pool_bd = _block_diag(pool_w[i]).astype(BF16)
        hy_args = (hy_w1[i], hy_b1[i], hy_w2[i], hy_b2[i], hy_w3[i], hy_freq[i])

        qc, kc, vc, hyc, poolc = _in_projection(
            xc, csh1, csc1, norm1_w[i], w_in_bf, q_norm_w[i], k_norm_w[i], cos128, sin128, seg,
            rows_per_mod=b * lc, seq_len=lc, tm=lc, rope=False)
        kc = kc.reshape(b, lc, NA_W)
        vc = vc.reshape(b, lc, NA_W)
        if not last:
            na_c = _dense_attention(qc.reshape(b, lc, NA_W), kc, vc).reshape(b * lc, NA_W)
            hy_c = _hyena_short(hyc.reshape(b, lc, 3 * HY_W), hy_conv_w[i], hy_conv_b[i], hy_skip[i], hy_args)
            pool_c = _pool_mixer(poolc.reshape(b, lc, POOL_W), pool_bd, pool_scale[i])
            xc1, h2c, affc = _out_projection(
                na_c, hy_c.reshape(b * lc, HY_W).astype(BF16), pool_c.reshape(b * lc, POOL_W), w_out_bf,
                xc, cg1, norm2_w[i], csh2, csc2, w_router_t, rows_per_mod=b * lc, seq_len=lc, tm=lc)

        qr, qp, kl, vl, hyl, pooll = _in_projection(
            xl, sh1, sc1, norm1_w[i], w_in_bf, q_norm_w[i], k_norm_w[i], cos128, sin128, seg,
            rows_per_mod=n, seq_len=n, tm=512, rope=True)
        bias = _na_bias_tables(na_rpb[i], rows)
        o_na = _neighborhood_attention(qr.reshape(b, n, NA_W), qp.reshape(b, n, NA_W), kl.reshape(b, n, NA_W),
                                       vl.reshape(b, n, NA_W), kc, vc, bias)
        hy_l = _hyena_long(hyl.reshape(b, n, 3 * HY_W), hy_conv_w[i], hy_conv_b[i], hy_skip[i], hy_args)
        pool_l = _pool_mixer(pooll.reshape(b, n, POOL_W), pool_bd, pool_scale[i])
        xl1, h2l, affl = _out_projection(
            o_na.reshape(b * n, NA_W), hy_l.reshape(b * n, HY_W).astype(BF16), pool_l.reshape(b * n, POOL_W),
            w_out_bf, xl, g1, norm2_w[i], sh2, sc2, w_router_t, rows_per_mod=n, seq_len=n, tm=512)
        streams = [(h2l, affl, xl1, g2)]
        if not last:
            streams.append((h2c, affc, xc1, cg2))
        outs = _expert_choice_moe(streams, w_gate, w_up, w_down, i)
        xl = outs[0]
        if not last:
            xc = outs[1]
    return xl.reshape(b, n, d)
```

```python
import functools
import math

import jax
import jax.numpy as jnp
import numpy as np
from jax import lax
from jax.experimental import pallas as pl
from jax.experimental.pallas import tpu as pltpu

F32 = jnp.float32
BF16 = jnp.bfloat16

D_MODEL = 1024
GRID_W = 64
NA_HEADS = 8
HEAD_DIM = 64
NA_W = NA_HEADS * HEAD_DIM
NA_WIN_R = 8
NA_WIN_C = 16
ROPE_THETA = 10000.0
HY_W = 256
HY_ORDER = 2
HY_SHORT = 3
HY_BANDS = 16
HY_DECAY_TARGET = 1e-2
HY_FAST_DECAY = 0.3
HY_SLOW_DECAY = 1.5
POOL_W = 256
POOL_SIZES = (2, 4, 8, 16)
POOL_GROUP = POOL_W // len(POOL_SIZES)
MIX_W = NA_W + HY_W + POOL_W
IN_W = 3 * NA_W + (HY_ORDER + 1) * HY_W + POOL_W
N_EXPERTS = 16
EC_FACTOR = 2
D_EXPERT = 2048
EPS = 1e-6

LANES = 128
TOKEN_SUBLANES = D_MODEL // 128
NEG_BIG = -1e30
VMEM_LIMIT = 56 * 1024 * 1024

NA_Q_ROWS = 4
NA_K_ROWS = NA_Q_ROWS + NA_WIN_R - 1
NA_TILES_PER_STEP = 2


def _cparams(sem):
    return pltpu.CompilerParams(dimension_semantics=sem, vmem_limit_bytes=VMEM_LIMIT)


def _mod_kernel(c_ref, w_ref, b_ref, o_ref):
    c = c_ref[...]
    s = c * jax.nn.sigmoid(c)
    o_ref[...] = jnp.dot(s, w_ref[0], preferred_element_type=F32,
                         precision=lax.Precision.HIGHEST) + b_ref[...]


def _modulation(cc, w_mod, b_mod, layer):
    rows, d = cc.shape
    n = w_mod.shape[2]
    tn = 1536
    return pl.pallas_call(
        _mod_kernel,
        grid=(n // tn,),
        in_specs=[pl.BlockSpec((rows, d), lambda j: (0, 0)),
                  pl.BlockSpec((1, d, tn), lambda j: (layer, 0, j)),
                  pl.BlockSpec((1, tn), lambda j: (0, j))],
        out_specs=pl.BlockSpec((rows, tn), lambda j: (0, j)),
        out_shape=jax.ShapeDtypeStruct((rows, n), F32),
        compiler_params=_cparams(("arbitrary",)),
        name="modulation",
    )(cc, w_mod, b_mod.reshape(1, n))


def _head_norm(xs, w128, seg):
    m = jnp.dot((xs * xs).astype(BF16), seg, preferred_element_type=F32)
    return xs * lax.rsqrt(m + EPS) * w128


def _rope(xs, c, s_signed):
    lane = lax.broadcasted_iota(jnp.int32, xs.shape, 1)
    partner = jnp.where((lane & 16) == 0, pltpu.roll(xs, LANES - 16, 1), pltpu.roll(xs, 16, 1))
    return xs * c + partner * s_signed


def _inproj_kernel(x_ref, sh_ref, sc_ref, nw_ref, w_ref, qw_ref, kw_ref, cos_ref, sin_ref, seg_ref,
                   *out_refs, rope):
    if rope:
        qr_ref, qp_ref, k_ref, v_ref, hy_ref, pool_ref = out_refs
    else:
        qp_ref, k_ref, v_ref, hy_ref, pool_ref = out_refs
    x = x_ref[...]
    ms = jnp.mean(x * x, axis=-1, keepdims=True)
    h = x * lax.rsqrt(ms + EPS) * nw_ref[...]
    h = h * (1.0 + sc_ref[0]) + sh_ref[0]
    u = jnp.dot(h.astype(BF16), w_ref[...], preferred_element_type=F32)
    seg = seg_ref[...]
    qk_scale = HEAD_DIM ** -0.5
    for ch in range(NA_W // LANES):
        sl = slice(ch * LANES, (ch + 1) * LANES)
        qn = _head_norm(u[:, sl], qw_ref[...], seg)
        kn = _head_norm(u[:, NA_W + ch * LANES:NA_W + (ch + 1) * LANES], kw_ref[...], seg)
        qp_ref[:, sl] = (qn * qk_scale).astype(BF16)
        if rope:
            c = cos_ref[...]
            s = sin_ref[...]
            qr_ref[:, sl] = (_rope(qn, c, s) * qk_scale).astype(BF16)
            k_ref[:, sl] = _rope(kn, c, s).astype(BF16)
        else:
            k_ref[:, sl] = kn.astype(BF16)
    v_ref[...] = u[:, 2 * NA_W:3 * NA_W].astype(BF16)
    hy_ref[...] = u[:, 3 * NA_W:3 * NA_W + 3 * HY_W]
    pool_ref[...] = u[:, 3 * NA_W + 3 * HY_W:]


def _in_projection(x2, shift, scale, norm_w, w_in_bf, q_norm_w, k_norm_w, cos128, sin128, seg, *,
                   rows_per_mod, seq_len, tm, rope):
    m, d = x2.shape
    tiles_per_mod = rows_per_mod // tm
    tiles_per_seq = seq_len // tm
    row = lambda i: (i, 0)
    const = lambda i: (0, 0)
    modi = lambda i: (i // tiles_per_mod, 0, 0)
    pos = lambda i: (i % tiles_per_seq, 0)
    outs = []
    if rope:
        outs.append((NA_W, BF16))
    outs += [(NA_W, BF16), (NA_W, BF16), (NA_W, BF16), (3 * HY_W, F32), (POOL_W, F32)]
    return pl.pallas_call(
        functools.partial(_inproj_kernel, rope=rope),
        grid=(m // tm,),
        in_specs=[pl.BlockSpec((tm, d), row),
                  pl.BlockSpec((1, 1, d), modi), pl.BlockSpec((1, 1, d), modi),
                  pl.BlockSpec((1, d), const),
                  pl.BlockSpec((d, IN_W), const),
                  pl.BlockSpec((1, LANES), const), pl.BlockSpec((1, LANES), const),
                  pl.BlockSpec((tm, LANES), pos), pl.BlockSpec((tm, LANES), pos),
                  pl.BlockSpec((LANES, LANES), const)],
        out_specs=[pl.BlockSpec((tm, w), row) for w, _ in outs],
        out_shape=[jax.ShapeDtypeStruct((m, w), dt) for w, dt in outs],
        compiler_params=_cparams(("arbitrary",)),
        name="in_projection_rope" if rope else "in_projection_ctx",
    )(x2, shift, scale, norm_w.reshape(1, d), w_in_bf,
      jnp.tile(q_norm_w, 2).reshape(1, LANES), jnp.tile(k_norm_w, 2).reshape(1, LANES),
      cos128, sin128, seg)


def _softmax_pv(s_list, v_list):
    mx = functools.reduce(jnp.maximum, [jnp.max(s, axis=-1, keepdims=True) for s in s_list])
    ps = [jnp.exp(s - mx) for s in s_list]
    l = functools.reduce(jnp.add, [jnp.sum(p, axis=-1, keepdims=True) for p in ps])
    o = functools.reduce(jnp.add, [jnp.dot(p.astype(BF16), v, preferred_element_type=F32)
                                   for p, v in zip(ps, v_list)])
    return o / l


_NT = (((1,), (1,)), ((), ()))


def _na_kernel(qr_ref, qp_ref, k_ref, v_ref, kc_ref, vc_ref, bias_ref, o_ref, *, rows):
    n_tiles = rows // NA_Q_ROWS
    tq = NA_Q_ROWS * GRID_W
    kc = kc_ref[0]
    vc = vc_ref[0]
    lane = lax.broadcasted_iota(jnp.int32, (tq, LANES), 1)
    for sub in range(NA_TILES_PER_STEP):
        t = pl.program_id(2) * NA_TILES_PER_STEP + sub
        ks = jnp.clip(t * NA_Q_ROWS - NA_WIN_R // 2, 0, rows - NA_K_ROWS)
        start = pl.multiple_of(ks * GRID_W, GRID_W)
        k = k_ref[0, pl.ds(start, NA_K_ROWS * GRID_W), :]
        v = v_ref[0, pl.ds(start, NA_K_ROWS * GRID_W), :]
        q = qr_ref[0, pl.ds(sub * tq, tq), :]
        qp = qp_ref[0, pl.ds(sub * tq, tq), :]
        kind = jnp.where(t == 0, 0, jnp.where(t == n_tiles - 1, 2, 1))
        outs = []
        for hh in range(2):
            keep = (lane < HEAD_DIM) if hh == 0 else (lane >= HEAD_DIM)
            zero = jnp.zeros_like(q)
            s_lat = (lax.dot_general(jnp.where(keep, q, zero), k, _NT, preferred_element_type=F32)
                     + bias_ref[kind, hh])
            s_ctx = lax.dot_general(jnp.where(keep, qp, zero), kc, _NT, preferred_element_type=F32)
            outs.append(_softmax_pv([s_lat, s_ctx], [v, vc]))
        o_ref[0, pl.ds(sub * tq, tq), :] = jnp.where(lane < HEAD_DIM, outs[0], outs[1]).astype(BF16)


def _neighborhood_attention(qr, qp, k, v, kc, vc, bias):
    b, n, _ = qr.shape
    lc = kc.shape[1]
    rows = n // GRID_W
    n_steps = rows // (NA_Q_ROWS * NA_TILES_PER_STEP)
    tq = NA_Q_ROWS * GRID_W * NA_TILES_PER_STEP
    tk = NA_K_ROWS * GRID_W
    qmap = lambda bi, hp, t: (bi, t, hp)
    smap = lambda bi, hp, t: (bi, 0, hp)
    bmap = lambda bi, hp, t: (0, hp, 0, 0)
    return pl.pallas_call(
        functools.partial(_na_kernel, rows=rows),
        grid=(b, NA_W // LANES, n_steps),
        in_specs=[pl.BlockSpec((1, tq, LANES), qmap), pl.BlockSpec((1, tq, LANES), qmap),
                  pl.BlockSpec((1, n, LANES), smap), pl.BlockSpec((1, n, LANES), smap),
                  pl.BlockSpec((1, lc, LANES), smap), pl.BlockSpec((1, lc, LANES), smap),
                  pl.BlockSpec((3, 2, NA_Q_ROWS * GRID_W, tk), bmap)],
        out_specs=pl.BlockSpec((1, tq, LANES), qmap),
        out_shape=jax.ShapeDtypeStruct((b, n, NA_W), BF16),
        compiler_params=_cparams(("arbitrary", "arbitrary", "arbitrary")),
        name="neighborhood_attention",
    )(qr, qp, k, v, kc, vc, bias)


def _na_bias_tables(rpb, rows):
    n_tiles = rows // NA_Q_ROWS
    n_dr, n_dc = 2 * NA_WIN_R - 1, 2 * NA_WIN_C - 1
    qc = np.arange(GRID_W)
    cs = np.clip(qc - NA_WIN_C // 2, 0, GRID_W - NA_WIN_C)
    kcol = np.arange(GRID_W)
    ok_c = (kcol[None, :] >= cs[:, None]) & (kcol[None, :] < cs[:, None] + NA_WIN_C)
    dc = kcol[None, :] - qc[:, None] + NA_WIN_C - 1
    col_sel = np.concatenate([(dc[:, :, None] == np.arange(n_dc)) & ok_c[:, :, None],
                              ~ok_c[:, :, None]], axis=2)
    ext = jnp.pad(rpb.astype(F32), ((0, 0), (0, 1), (0, 1)), constant_values=NEG_BIG)
    blocks = jnp.einsum("hrd,qkd->hrqk", ext, jnp.asarray(col_sel, F32), precision=lax.Precision.HIGHEST)
    tabs = []
    for t in (0, 1, n_tiles - 1):
        r0 = t * NA_Q_ROWS
        ks = int(np.clip(r0 - NA_WIN_R // 2, 0, rows - NA_K_ROWS))
        r = r0 + np.arange(NA_Q_ROWS)
        rs = np.clip(r - NA_WIN_R // 2, 0, rows - NA_WIN_R)
        krow = ks + np.arange(NA_K_ROWS)
        ok_r = (krow[None, :] >= rs[:, None]) & (krow[None, :] < rs[:, None] + NA_WIN_R)
        dr = np.where(ok_r, krow[None, :] - r[:, None] + NA_WIN_R - 1, n_dr)
        tabs.append(jnp.concatenate(
            [jnp.concatenate([blocks[:, int(dr[i, j])] for j in range(NA_K_ROWS)], axis=-1)
             for i in range(NA_Q_ROWS)], axis=-2))
    return jnp.stack(tabs)


def _dense_attn_kernel(q_ref, k_ref, v_ref, o_ref):
    q = q_ref[0]
    k = k_ref[0]
    v = v_ref[0]
    lane = lax.broadcasted_iota(jnp.int32, q.shape, 1)
    outs = []
    for hh in range(2):
        keep = (lane < HEAD_DIM) if hh == 0 else (lane >= HEAD_DIM)
        s = lax.dot_general(jnp.where(keep, q, jnp.zeros_like(q)), k, _NT, preferred_element_type=F32)
        outs.append(_softmax_pv([s], [v]))
    o_ref[0] = jnp.where(lane < HEAD_DIM, outs[0], outs[1]).astype(BF16)


def _dense_attention(q, k, v):
    b, l, _ = q.shape
    spec = pl.BlockSpec((1, l, LANES), lambda bi, hp: (bi, 0, hp))
    return pl.pallas_call(
        _dense_attn_kernel,
        grid=(b, NA_W // LANES),
        in_specs=[spec, spec, spec],
        out_specs=spec,
        out_shape=jax.ShapeDtypeStruct((b, l, NA_W), BF16),
        compiler_params=_cparams(("arbitrary", "arbitrary")),
        name="context_attention",
    )(q, k, v)


POOL_HALO = max(POOL_SIZES) // 2


def _pool_kernel(x_ref, w_ref, sc_ref, o_ref, xpad, *, seq_len, chunk):
    zeros = jnp.zeros((POOL_HALO, POOL_W), F32)
    xpad[pl.ds(0, POOL_HALO), :] = zeros
    xpad[pl.ds(seq_len + POOL_HALO, POOL_HALO), :] = zeros
    xpad[pl.ds(POOL_HALO, seq_len), :] = x_ref[0]
    span = chunk + 2 * POOL_HALO
    w = w_ref[...]
    sc = sc_ref[...]

    def body(c, carry):
        base = pl.multiple_of(c * chunk, chunk)
        xs = xpad[pl.ds(base, span), :]
        s2 = xs + pltpu.roll(xs, 1, 0)
        s4 = pltpu.roll(s2, 1, 0) + pltpu.roll(s2, span - 1, 0)
        s8 = pltpu.roll(s4, 2, 0) + pltpu.roll(s4, span - 2, 0)
        s16 = pltpu.roll(s8, 4, 0) + pltpu.roll(s8, span - 4, 0)
        mid = slice(POOL_HALO, POOL_HALO + chunk)
        grp = lax.broadcasted_iota(jnp.int32, (chunk, POOL_W), 1) // POOL_GROUP
        tpos = base + lax.broadcasted_iota(jnp.int32, (chunk, POOL_W), 0)
        half = jnp.left_shift(1, grp)
        cnt = (jnp.minimum(tpos + half, seq_len) - jnp.maximum(tpos - half, 0)).astype(F32)
        ssum = jnp.where(grp == 0, s2[mid], jnp.where(grp == 1, s4[mid], jnp.where(grp == 2, s8[mid], s16[mid])))
        diff = ssum / cnt - xs[mid]
        y = jnp.dot(diff.astype(BF16), w, preferred_element_type=F32) * sc
        o_ref[0, pl.ds(base, chunk), :] = y.astype(BF16)
        return carry

    lax.fori_loop(0, seq_len // chunk, body, 0)


def _pool_mixer(u, w_blockdiag_bf, pool_scale):
    b, l, _ = u.shape
    chunk = min(l, 512)
    return pl.pallas_call(
        functools.partial(_pool_kernel, seq_len=l, chunk=chunk),
        grid=(b,),
        in_specs=[pl.BlockSpec((1, l, POOL_W), lambda bi: (bi, 0, 0)),
                  pl.BlockSpec((POOL_W, POOL_W), lambda bi: (0, 0)),
                  pl.BlockSpec((1, POOL_W), lambda bi: (0, 0))],
        out_specs=pl.BlockSpec((1, l, POOL_W), lambda bi: (bi, 0, 0)),
        out_shape=jax.ShapeDtypeStruct((b, l, POOL_W), BF16),
        scratch_shapes=[pltpu.VMEM((l + 2 * POOL_HALO, POOL_W), F32)],
        compiler_params=_cparams(("arbitrary",)),
        name="pool_mixer",
    )(u, w_blockdiag_bf, pool_scale.reshape(1, POOL_W))


def _outproj_kernel(na_ref, hy_ref, pl_ref, w_ref, x_ref, g1_ref, nw_ref, sh_ref, sc_ref, wr_ref,
                    xn_ref, h2_ref, aff_ref):
    w = w_ref
    acc = jnp.dot(na_ref[...], w[pl.ds(0, NA_W), :], preferred_element_type=F32)
    acc += jnp.dot(hy_ref[...], w[pl.ds(NA_W, HY_W), :], preferred_element_type=F32)
    acc += jnp.dot(pl_ref[...], w[pl.ds(NA_W + HY_W, POOL_W), :], preferred_element_type=F32)
    xn = x_ref[...] + g1_ref[0] * acc
    xn_ref[...] = xn
    ms = jnp.mean(xn * xn, axis=-1, keepdims=True)
    h2 = xn * lax.rsqrt(ms + EPS) * nw_ref[...]
    h2 = h2 * (1.0 + sc_ref[0]) + sh_ref[0]
    tm = h2.shape[0]
    for s in range(TOKEN_SUBLANES):
        h2_ref[pl.ds(s, tm, stride=TOKEN_SUBLANES), :] = h2[:, s * LANES:(s + 1) * LANES]
    logits = lax.dot_general(wr_ref[...], h2, _NT, preferred_element_type=F32,
                             precision=lax.Precision.HIGHEST)
    mx = jnp.max(logits, axis=0, keepdims=True)
    p = jnp.exp(logits - mx)
    aff_ref[0] = p / jnp.sum(p, axis=0, keepdims=True)


def _out_projection(na, hy, pool, w_out_bf, x2, g1, norm_w, shift, scale, w_router_t, *,
                    rows_per_mod, seq_len, tm):
    m, d = x2.shape
    tiles_per_mod = rows_per_mod // tm
    tiles_per_seq = seq_len // tm
    row = lambda i: (i, 0)
    const = lambda i: (0, 0)
    modi = lambda i: (i // tiles_per_mod, 0, 0)
    return pl.pallas_call(
        _outproj_kernel,
        grid=(m // tm,),
        in_specs=[pl.BlockSpec((tm, NA_W), row), pl.BlockSpec((tm, HY_W), row), pl.BlockSpec((tm, POOL_W), row),
                  pl.BlockSpec((MIX_W, d), const),
                  pl.BlockSpec((tm, d), row),
                  pl.BlockSpec((1, 1, d), modi),
                  pl.BlockSpec((1, d), const),
                  pl.BlockSpec((1, 1, d), modi), pl.BlockSpec((1, 1, d), modi),
                  pl.BlockSpec((N_EXPERTS, d), const)],
        out_specs=[pl.BlockSpec((tm, d), row), pl.BlockSpec((tm * TOKEN_SUBLANES, LANES), row),
                   pl.BlockSpec((1, N_EXPERTS, tm), lambda i: (i // tiles_per_seq, 0, i % tiles_per_seq))],
        out_shape=[jax.ShapeDtypeStruct((m, d), F32), jax.ShapeDtypeStruct((m * TOKEN_SUBLANES, LANES), F32),
                   jax.ShapeDtypeStruct((m // seq_len, N_EXPERTS, seq_len), F32)],
        compiler_params=_cparams(("arbitrary",)),
        name="out_projection_router",
    )(na, hy, pool, w_out_bf, x2, g1, norm_w.reshape(1, d), shift, scale, w_router_t)


FFN_TF = 256
FFN_ROW_CHUNKS = 2


def _ffn_kernel(*refs, n_experts, sources, n_rows):
    ns = len(sources)
    idx_cur = refs[0:ns]
    idx_nxt = refs[ns:2 * ns]
    src = refs[2 * ns:3 * ns]
    gate_ref, wg_ref, wu_ref, wd_ref, o_ref, stage, xb, acc, sem = refs[3 * ns:]
    e = pl.program_id(0)
    f = pl.program_id(1)
    n_f = pl.num_programs(1)

    ts = TOKEN_SUBLANES

    def row_copy(s, row, r):
        return pltpu.make_async_copy(src[s].at[pl.ds(pl.multiple_of(row * ts, ts), ts)],
                                     stage.at[pl.ds(pl.multiple_of((sources[s][0] + r) * ts, ts), ts)], sem.at[0])

    def gather(idx_refs):
        for s in range(ns):
            def body(r, carry, s=s):
                row_copy(s, idx_refs[s][0, 0, r], r).start()
                return carry
            lax.fori_loop(0, sources[s][1], body, 0, unroll=8)

    def gather_wait():
        for s in range(ns):
            off, cnt = sources[s]
            pltpu.make_async_copy(src[s].at[pl.ds(0, cnt * ts)], stage.at[pl.ds(off * ts, cnt * ts)],
                                  sem.at[0]).wait()

    @pl.when(f == 0)
    def _():
        @pl.when(e == 0)
        def _():
            gather(idx_cur)

        acc[...] = jnp.zeros_like(acc)
        gather_wait()
        for s in range(ts):
            xb[:, s * LANES:(s + 1) * LANES] = stage[pl.ds(s, n_rows, stride=ts), :].astype(BF16)

        @pl.when(e + 1 < n_experts)
        def _():
            gather(idx_nxt)

    wg = wg_ref[0, 0].astype(BF16)
    wu = wu_ref[0, 0].astype(BF16)
    wd = wd_ref[0, 0].astype(BF16)
    rc = n_rows // FFN_ROW_CHUNKS
    for r in range(FFN_ROW_CHUNKS):
        rs = pl.ds(r * rc, rc)
        x = xb[rs, :]
        a = jnp.dot(x, wg, preferred_element_type=F32)
        b = jnp.dot(x, wu, preferred_element_type=F32)
        h = (a * jax.nn.sigmoid(a) * b).astype(BF16)
        y = jnp.dot(h, wd, preferred_element_type=F32)
        acc[rs, :] += y

    @pl.when(f == n_f - 1)
    def _():
        o_ref[0] = (acc[...] * gate_ref[0]).astype(BF16)


def _expert_ffn(rows_idx, tokens, gate, w_gate, w_up, w_down, layer):
    ns = len(tokens)
    e, d = w_gate.shape[1], w_gate.shape[2]
    counts = [int(ri.shape[2]) for ri in rows_idx]
    offs = [int(sum(counts[:s])) for s in range(ns)]
    r_total = sum(counts)
    assert r_total % (FFN_ROW_CHUNKS * 16) == 0
    smem = lambda cnt, nxt: pl.BlockSpec(
        (1, 1, cnt), (lambda ei, f: (jnp.minimum(ei + 1, e - 1), 0, 0)) if nxt else (lambda ei, f: (ei, 0, 0)),
        memory_space=pltpu.SMEM)
    in_specs = ([smem(cnt, False) for cnt in counts] + [smem(cnt, True) for cnt in counts]
                + [pl.BlockSpec(memory_space=pl.ANY) for _ in range(ns)]
                + [pl.BlockSpec((1, r_total, 1), lambda ei, f: (ei, 0, 0)),
                   pl.BlockSpec((1, 1, d, FFN_TF), lambda ei, f: (layer, ei, 0, f)),
                   pl.BlockSpec((1, 1, d, FFN_TF), lambda ei, f: (layer, ei, 0, f)),
                   pl.BlockSpec((1, 1, FFN_TF, d), lambda ei, f: (layer, ei, f, 0))])
    return pl.pallas_call(
        functools.partial(_ffn_kernel, n_experts=e, sources=tuple(zip(offs, counts)), n_rows=r_total),
        grid=(e, D_EXPERT // FFN_TF),
        in_specs=in_specs,
        out_specs=pl.BlockSpec((1, r_total, d), lambda ei, f: (ei, 0, 0)),
        out_shape=jax.ShapeDtypeStruct((e, r_total, d), BF16),
        scratch_shapes=[pltpu.VMEM((r_total * TOKEN_SUBLANES, LANES), F32), pltpu.VMEM((r_total, d), BF16),
                        pltpu.VMEM((r_total, d), F32), pltpu.SemaphoreType.DMA((1,))],
        compiler_params=_cparams(("arbitrary", "arbitrary")),
        name="expert_ffn",
    )(*rows_idx, *rows_idx, *tokens, gate, w_gate, w_up, w_down)


def _prefix_count(m, n):
    lane = lax.broadcasted_iota(jnp.int32, m.shape, 1)
    c = m
    s = 1
    while s < n:
        c = c + jnp.where(lane >= s, pltpu.roll(c, s, 1), 0)
        s *= 2
    return c


COMBINE_TB = 256
COMBINE_CHUNK = 128


def _topk_kernel(aff_ref, idx_ref, gate_ref, off_ref, *, cap):
    a = aff_ref[...]
    rows, n = a.shape

    def search(i, t):
        cand = t | jnp.left_shift(jnp.int32(1), 30 - i)
        cnt = jnp.sum((a >= pltpu.bitcast(cand, F32)).astype(jnp.int32), axis=1, keepdims=True)
        return jnp.where(cnt >= cap, cand, t)

    thr_bits = lax.fori_loop(0, 31, search, jnp.zeros((rows, 1), jnp.int32))
    thr = pltpu.bitcast(thr_bits, F32)
    gt = a >= pltpu.bitcast(thr_bits + 1, F32)
    eq = (a >= thr) & jnp.logical_not(gt)
    need = cap - jnp.sum(gt.astype(jnp.int32), axis=1, keepdims=True)
    eq_i = eq.astype(jnp.int32)
    sel = gt | (eq & (_prefix_count(eq_i, n) - eq_i < need))
    sel_i = sel.astype(jnp.int32)
    lane = lax.broadcasted_iota(jnp.int32, a.shape, 1)
    dist = jnp.where(sel, lane + 1 - _prefix_count(sel_i, n), 0)
    tok = jnp.where(sel, lane, -1)
    val = a
    s = 1
    while s < n:
        tok_in = pltpu.roll(tok, n - s, 1)
        dist_in = pltpu.roll(dist, n - s, 1)
        val_in = pltpu.roll(val, n - s, 1)
        take = (lane < n - s) & (tok_in >= 0) & ((dist_in & s) != 0)
        stay = (tok >= 0) & ((dist & s) == 0)
        tok = jnp.where(take, tok_in, jnp.where(stay, tok, -1))
        dist = jnp.where(take, dist_in, jnp.where(stay, dist, 0))
        val = jnp.where(take, val_in, val)
        s *= 2
    idx_ref[...] = tok[:, :cap]
    gate_ref[...] = val[:, :cap]
    olane = lax.broadcasted_iota(jnp.int32, off_ref.shape, 1)
    off = jnp.zeros(off_ref.shape, jnp.int32)
    for j in range(n // COMBINE_TB + 1):
        below = jnp.sum(jnp.where(lane < j * COMBINE_TB, sel_i, 0), axis=1, keepdims=True)
        off = jnp.where(olane == j, below, off)
    off_ref[...] = off


def _topk_select(aff, cap):
    b, e, n = aff.shape
    assert n & (n - 1) == 0 and n // COMBINE_TB < LANES
    rows = b * e
    full = lambda w: pl.BlockSpec((rows, w), lambda i: (0, 0))
    idx, gate, off = pl.pallas_call(
        functools.partial(_topk_kernel, cap=cap),
        grid=(1,),
        in_specs=[full(n)],
        out_specs=[full(cap), full(cap), full(LANES)],
        out_shape=[jax.ShapeDtypeStruct((rows, cap), jnp.int32), jax.ShapeDtypeStruct((rows, cap), F32),
                   jax.ShapeDtypeStruct((rows, LANES), jnp.int32)],
        compiler_params=_cparams(("arbitrary",)),
        name="expert_choice_topk",
    )(aff.reshape(rows, n))
    return idx.reshape(b, e, cap), gate.reshape(b, e, cap), off


def _combine_kernel(off_ref, y_ref, idx_ref, x_ref, g_ref, o_ref, acc, ycat, *, chunk, n_chunks):
    bi = pl.program_id(0)
    j = pl.program_id(1)
    tb = x_ref.shape[0]
    tok = j * tb + lax.broadcasted_iota(jnp.int32, (tb, chunk), 0)

    def onehot(e, st):
        return (idx_ref[0, pl.ds(e, 1), pl.ds(st, chunk)] == tok).astype(BF16)

    if n_chunks == 1:
        acc[...] = jnp.zeros_like(acc)
        for e in range(N_EXPERTS):
            acc[...] += jnp.dot(onehot(e, 0), y_ref[e], preferred_element_type=F32)
    else:
        shift = chunk.bit_length() - 1
        spans, hots = [], []
        for e in range(N_EXPERTS):
            lo = off_ref[bi * N_EXPERTS + e, j]
            hi = off_ref[bi * N_EXPERTS + e, j + 1]
            c0 = jnp.minimum(lax.shift_right_logical(lo, shift), n_chunks - 1)
            c1 = lax.shift_right_logical(hi + (chunk - 1), shift)
            spans.append((c0, c1))
            st = pl.multiple_of(c0 * chunk, chunk)
            ycat[pl.ds(e * chunk, chunk), :] = y_ref[e, pl.ds(st, chunk), :]
            hots.append(onehot(e, st))
        acc[...] = jnp.dot(jnp.concatenate(hots, axis=1), ycat[...], preferred_element_type=F32)
        for e in range(N_EXPERTS):
            def add_chunk(c, carry, e=e):
                st = pl.multiple_of(c * chunk, chunk)
                acc[...] += jnp.dot(onehot(e, st), y_ref[e, pl.ds(st, chunk), :], preferred_element_type=F32)
                return carry

            lax.fori_loop(spans[e][0] + 1, spans[e][1], add_chunk, 0)
    o_ref[...] = x_ref[...] + g_ref[0] * acc[...]


def _moe_combine(off, y, idx, x2, g, *, row_block0, cap, seq_len):
    bsz = idx.shape[0]
    d = x2.shape[1]
    tb = min(seq_len, COMBINE_TB)
    nb = seq_len // tb
    chunk = min(cap, COMBINE_CHUNK)
    n_chunks = cap // chunk
    n_mod = g.shape[0]
    grid_spec = pltpu.PrefetchScalarGridSpec(
        num_scalar_prefetch=1,
        grid=(bsz, nb),
        in_specs=[pl.BlockSpec((N_EXPERTS, cap, d), lambda bi, j, off: (0, row_block0 + bi, 0),
                               pipeline_mode=pl.Buffered(1)),
                  pl.BlockSpec((1, N_EXPERTS, cap), lambda bi, j, off: (bi, 0, 0)),
                  pl.BlockSpec((tb, d), lambda bi, j, off: (bi * nb + j, 0)),
                  pl.BlockSpec((1, 1, d), lambda bi, j, off: (bi if n_mod > 1 else 0, 0, 0))],
        out_specs=pl.BlockSpec((tb, d), lambda bi, j, off: (bi * nb + j, 0)),
        scratch_shapes=[pltpu.VMEM((tb, d), F32), pltpu.VMEM((N_EXPERTS * chunk, d), BF16)])
    return pl.pallas_call(
        functools.partial(_combine_kernel, chunk=chunk, n_chunks=n_chunks),
        grid_spec=grid_spec,
        out_shape=jax.ShapeDtypeStruct(x2.shape, F32),
        compiler_params=_cparams(("arbitrary", "arbitrary")),
        name="moe_combine",
    )(off, y, idx, x2, g)


def _expert_choice_moe(streams, w_gate, w_up, w_down, layer):
    routed = []
    for h2, aff, x2, g in streams:
        bsz, _, l = aff.shape
        cap = EC_FACTOR * l // N_EXPERTS
        idx, gate, off = _topk_select(aff, cap)
        rows = (idx + (jnp.arange(bsz, dtype=jnp.int32) * l)[:, None, None]).transpose(1, 0, 2)
        routed.append((idx, off, rows.reshape(N_EXPERTS, 1, bsz * cap), gate.transpose(1, 0, 2).reshape(
            N_EXPERTS, bsz * cap, 1), cap))
    y = _expert_ffn([r[2] for r in routed], [s[0] for s in streams],
                    jnp.concatenate([r[3] for r in routed], axis=1), w_gate, w_up, w_down, layer)
    outs = []
    row0 = 0
    for (h2, aff, x2, g), (idx, off, rows, _, cap) in zip(streams, routed):
        assert row0 % cap == 0
        outs.append(_moe_combine(off, y, idx, x2, g, row_block0=row0 // cap, cap=cap, seq_len=aff.shape[2]))
        row0 += rows.shape[2]
    return outs


HY_R = 128
HY_COLS = 4096
HY_KB = 8
HY_UNROLL = 4


def _dft_consts(r):
    k = np.arange(r)
    ang = 2.0 * np.pi * np.outer(k, k) / r
    fre, fim = np.cos(ang), -np.sin(ang)
    k2 = np.arange(r)
    tang = 2.0 * np.pi * np.outer(k2, k2) / (r * r)
    tw = np.stack([np.cos(tang), -np.sin(tang)])
    fwd = np.block([[fre, -fim], [fim, fre]])
    inv = np.block([[fre, fim], [-fim, fre]])
    return fre, fim, tw, fwd, inv


def _hy_prep_kernel(x_ref, prev_ref, next_ref, w_ref, b_ref, x1_ref, x2_ref, v_ref, *, n_tiles):
    t = pl.program_id(1)
    x = x_ref[0]
    tm = x.shape[0]
    row = lax.broadcasted_iota(jnp.int32, x.shape, 0)
    prev = jnp.where(t == 0, 0.0, prev_ref[0, pl.ds(7, 1), :])
    nxt = jnp.where(t == n_tiles - 1, 0.0, next_ref[0, pl.ds(0, 1), :])
    up = jnp.where(row == 0, prev, pltpu.roll(x, 1, 0))
    dn = jnp.where(row == tm - 1, nxt, pltpu.roll(x, tm - 1, 0))
    y = b_ref[...] + up * w_ref[pl.ds(0, 1), :] + x * w_ref[pl.ds(1, 1), :] + dn * w_ref[pl.ds(2, 1), :]
    x1_ref[0] = y[:, :HY_W].astype(BF16)
    x2_ref[0] = y[:, HY_W:2 * HY_W].astype(BF16)
    v_ref[0] = y[:, 2 * HY_W:].astype(BF16)


def _hyena_prep(u, conv_w, conv_b):
    b, l, c = u.shape
    tm = min(l, 1024)
    n_tiles = l // tm
    g = tm // 8
    n_groups = l // 8
    out = jax.ShapeDtypeStruct((b, l, HY_W), BF16)
    ospec = pl.BlockSpec((1, tm, HY_W), lambda bi, t: (bi, t, 0))
    return pl.pallas_call(
        functools.partial(_hy_prep_kernel, n_tiles=n_tiles),
        grid=(b, n_tiles),
        in_specs=[pl.BlockSpec((1, tm, c), lambda bi, t: (bi, t, 0)),
                  pl.BlockSpec((1, 8, c), lambda bi, t: (bi, jnp.maximum(t * g - 1, 0), 0)),
                  pl.BlockSpec((1, 8, c), lambda bi, t: (bi, jnp.minimum((t + 1) * g, n_groups - 1), 0)),
                  pl.BlockSpec((HY_SHORT, c), lambda bi, t: (0, 0)),
                  pl.BlockSpec((1, c), lambda bi, t: (0, 0))],
        out_specs=[ospec, ospec, ospec],
        out_shape=[out, out, out],
        compiler_params=_cparams(("arbitrary", "arbitrary")),
        name="hyena_prep",
    )(u, u, u, conv_w, conv_b.reshape(1, c))


def _hy_filter_kernel(z_ref, w1_ref, b1_ref, w2_ref, b2_ref, w3_ref, fr_ref, dl_ref, k_ref, ss_ref, *, seq_len):
    i = pl.program_id(0)
    hp = lax.Precision.HIGHEST
    z = z_ref[...]
    tm = z.shape[0]
    half = tm // 2
    fr = fr_ref[...]
    zz = jnp.concatenate([z[:half], z[half:]], axis=1)
    h = jnp.sin(fr * (jnp.dot(zz, w1_ref[...], preferred_element_type=F32, precision=hp) + b1_ref[...]))
    h = jnp.sin(fr * (jnp.dot(h, w2_ref[...], preferred_element_type=F32, precision=hp) + b2_ref[...]))
    hb = h.astype(BF16)
    w3 = w3_ref[...].astype(BF16)
    fo = w3.shape[0]
    h = jnp.concatenate([jnp.dot(hb[:, :fo], w3, preferred_element_type=F32),
                         jnp.dot(hb[:, fo:], w3, preferred_element_type=F32)], axis=0)
    dec = jnp.exp(-z[:, 0:1] * dl_ref[...])
    r = i * tm + lax.broadcasted_iota(jnp.int32, (tm, HY_W), 0)
    parts = []
    for o in range(HY_ORDER):
        hf = h[:, (2 * o) * HY_W:(2 * o + 1) * HY_W]
        hb = h[:, (2 * o + 1) * HY_W:(2 * o + 2) * HY_W]
        parts.append(jnp.where(r < seq_len, hf, jnp.where(r > seq_len, hb, 0.0)) * dec)
    k = jnp.concatenate(parts, axis=1)
    k_ref[...] = k
    ss = jnp.sum(k * k, axis=0, keepdims=True)

    @pl.when(i == 0)
    def _():
        ss_ref[...] = ss

    @pl.when(i != 0)
    def _():
        ss_ref[...] += ss


def _hyena_filter_taps(l, w1, b1, w2, b2, w3, freq):
    t = jnp.linspace(0.0, 1.0, l, dtype=F32)[:, None]
    w = (2.0 * math.pi / l) * jnp.arange(l, dtype=F32)[:, None]
    f = jnp.linspace(1e-4, HY_BANDS - 1, HY_BANDS, dtype=F32)[None, :]
    z = jnp.concatenate([t, jnp.cos(f * w), -jnp.sin(f * w)], axis=-1)
    z2 = jnp.concatenate([z, z[:1], z[:0:-1]], axis=0)
    emb = z.shape[1]
    z2 = jnp.pad(z2, ((0, 0), (0, LANES - emb)))
    w1p = jnp.pad(w1, ((0, LANES - emb), (0, 0)))
    two = lambda m: jnp.kron(jnp.eye(2, dtype=m.dtype), m)
    pair = lambda v: jnp.tile(v, 2).reshape(1, 2 * v.shape[0])
    deltas = jnp.abs(jnp.linspace(math.log(HY_DECAY_TARGET) / HY_SLOW_DECAY,
                                  math.log(HY_DECAY_TARGET) / HY_FAST_DECAY, HY_W, dtype=F32))
    fo = w2.shape[0]
    nout = w3.shape[1]
    tm = min(2 * l, 1024)
    const = lambda i: (0, 0)
    return pl.pallas_call(
        functools.partial(_hy_filter_kernel, seq_len=l),
        grid=(2 * l // tm,),
        in_specs=[pl.BlockSpec((tm, LANES), lambda i: (i, 0)),
                  pl.BlockSpec((2 * LANES, 2 * fo), const), pl.BlockSpec((1, 2 * fo), const),
                  pl.BlockSpec((2 * fo, 2 * fo), const), pl.BlockSpec((1, 2 * fo), const),
                  pl.BlockSpec((fo, nout), const), pl.BlockSpec((1, 2 * fo), const),
                  pl.BlockSpec((1, HY_W), const)],
        out_specs=[pl.BlockSpec((tm, HY_ORDER * HY_W), lambda i: (i, 0)),
                   pl.BlockSpec((1, HY_ORDER * HY_W), const)],
        out_shape=[jax.ShapeDtypeStruct((2 * l, HY_ORDER * HY_W), F32),
                   jax.ShapeDtypeStruct((1, HY_ORDER * HY_W), F32)],
        compiler_params=_cparams(("arbitrary",)),
        name="hyena_filter_taps",
    )(z2, two(w1p), pair(b1), two(w2), pair(b2), w3, pair(freq), deltas.reshape(1, HY_W))


def _left_matmul_kernel(f_ref, x_ref, o_ref):
    o_ref[0] = jnp.dot(f_ref[...], x_ref[0].astype(BF16), preferred_element_type=F32).astype(o_ref.dtype)


def _left_matmul(f_bf, x3, name):
    b, k, c = x3.shape
    m = f_bf.shape[0]
    return pl.pallas_call(
        _left_matmul_kernel,
        grid=(b, c // HY_COLS),
        in_specs=[pl.BlockSpec((m, k), lambda bi, j: (0, 0)),
                  pl.BlockSpec((1, k, HY_COLS), lambda bi, j: (bi, 0, j))],
        out_specs=pl.BlockSpec((1, m, HY_COLS), lambda bi, j: (bi, 0, j)),
        out_shape=jax.ShapeDtypeStruct((b, m, c), BF16),
        compiler_params=_cparams(("arbitrary", "arbitrary")),
        name=name,
    )(f_bf, x3)


def _twiddle_cols(tw_ref, k1):
    lane = lax.broadcasted_iota(jnp.int32, (HY_R, HY_R), 1)
    pick = lane == k1
    twr = jnp.sum(jnp.where(pick, tw_ref[0], 0.0), axis=1, keepdims=True)
    twi = jnp.sum(jnp.where(pick, tw_ref[1], 0.0), axis=1, keepdims=True)
    return twr, twi


def _hy_spectrum_kernel(a_ref, tw_ref, fwd_ref, ss_ref, kf_ref):
    kb = pl.program_id(0)
    scale = lax.rsqrt(ss_ref[...] + EPS) * (1.0 / (HY_R * HY_R))

    def body(kk, carry):
        twr, twi = _twiddle_cols(tw_ref, kb * HY_KB + kk)
        are = a_ref[0, 0, kk].astype(F32)
        aim = a_ref[0, 1, kk].astype(F32)
        bst = jnp.concatenate([are * twr - aim * twi, are * twi + aim * twr], axis=0).astype(BF16)
        x = jnp.dot(fwd_ref[...], bst, preferred_element_type=F32)
        kf_ref[0, kk] = x[:HY_R] * scale
        kf_ref[1, kk] = x[HY_R:] * scale
        return carry

    lax.fori_loop(0, HY_KB, body, 0, unroll=HY_UNROLL)


def _hy_spectrum(a5, tw, fwd_bf, ss):
    nch = a5.shape[-1]
    return pl.pallas_call(
        _hy_spectrum_kernel,
        grid=(HY_R // HY_KB,),
        in_specs=[pl.BlockSpec((1, 2, HY_KB, HY_R, nch), lambda kb: (0, 0, kb, 0, 0)),
                  pl.BlockSpec((2, HY_R, HY_R), lambda kb: (0, 0, 0)),
                  pl.BlockSpec((2 * HY_R, 2 * HY_R), lambda kb: (0, 0)),
                  pl.BlockSpec((1, nch), lambda kb: (0, 0))],
        out_specs=pl.BlockSpec((2, HY_KB, HY_R, nch), lambda kb: (0, kb, 0, 0)),
        out_shape=jax.ShapeDtypeStruct((2, HY_R, HY_R, nch), F32),
        compiler_params=_cparams(("arbitrary",)),
        name="hyena_filter_spectrum",
    )(a5, tw, fwd_bf, ss)


def _hy_mid_kernel(a_ref, kf_ref, tw_ref, fwd_ref, inv_ref, z_ref):
    kb = pl.program_id(0)

    def body(kk, carry):
        twr, twi = _twiddle_cols(tw_ref, kb * HY_KB + kk)
        are = a_ref[0, 0, kk].astype(F32)
        aim = a_ref[0, 1, kk].astype(F32)
        bst = jnp.concatenate([are * twr - aim * twi, are * twi + aim * twr], axis=0).astype(BF16)
        x = jnp.dot(fwd_ref[...], bst, preferred_element_type=F32)
        xre, xim = x[:HY_R], x[HY_R:]
        kre, kim = kf_ref[0, kk], kf_ref[1, kk]
        yst = jnp.concatenate([xre * kre - xim * kim, xre * kim + xim * kre], axis=0).astype(BF16)
        zz = jnp.dot(inv_ref[...], yst, preferred_element_type=F32)
        zre, zim = zz[:HY_R], zz[HY_R:]
        z_ref[0, 0, kk] = (zre * twr + zim * twi).astype(BF16)
        z_ref[0, 1, kk] = (zim * twr - zre * twi).astype(BF16)
        return carry

    lax.fori_loop(0, HY_KB, body, 0, unroll=HY_UNROLL)


def _hy_mid(a5, kf, order, tw, fwd_bf, inv_bf):
    b = a5.shape[0]
    blk = (1, 2, HY_KB, HY_R, HY_W)
    return pl.pallas_call(
        _hy_mid_kernel,
        grid=(HY_R // HY_KB, b),
        in_specs=[pl.BlockSpec(blk, lambda kb, bi: (bi, 0, kb, 0, 0)),
                  pl.BlockSpec((2, HY_KB, HY_R, HY_W), lambda kb, bi: (0, kb, 0, order)),
                  pl.BlockSpec((2, HY_R, HY_R), lambda kb, bi: (0, 0, 0)),
                  pl.BlockSpec((2 * HY_R, 2 * HY_R), lambda kb, bi: (0, 0)),
                  pl.BlockSpec((2 * HY_R, 2 * HY_R), lambda kb, bi: (0, 0))],
        out_specs=pl.BlockSpec(blk, lambda kb, bi: (bi, 0, kb, 0, 0)),
        out_shape=jax.ShapeDtypeStruct(a5.shape, BF16),
        compiler_params=_cparams(("arbitrary", "arbitrary")),
        name="hyena_spectral_product",
    )(a5, kf, tw, fwd_bf, inv_bf)


def _hy_out_kernel(f_ref, z_ref, g_ref, w_ref, sk_ref, o_ref):
    y = jnp.dot(f_ref[...], z_ref[0], preferred_element_type=F32)
    w = w_ref[0].astype(F32)
    o_ref[0] = (g_ref[0].astype(F32) * (y + w * sk_ref[...])).astype(BF16)


def _hy_out(f_bf, z3, gate3, w3, skip_row):
    b, m2, c = z3.shape
    m = f_bf.shape[0]
    dspec = pl.BlockSpec((1, m, HY_COLS), lambda bi, j: (bi, 0, j))
    return pl.pallas_call(
        _hy_out_kernel,
        grid=(b, c // HY_COLS),
        in_specs=[pl.BlockSpec((m, m2), lambda bi, j: (0, 0)),
                  pl.BlockSpec((1, m2, HY_COLS), lambda bi, j: (bi, 0, j)),
                  dspec, dspec,
                  pl.BlockSpec((1, HY_COLS), lambda bi, j: (0, 0))],
        out_specs=dspec,
        out_shape=jax.ShapeDtypeStruct((b, m, c), BF16),
        compiler_params=_cparams(("arbitrary", "arbitrary")),
        name="hyena_inverse_gate",
    )(f_bf, z3, gate3, w3, skip_row)


def _hyena_long(u, conv_w, conv_b, skip, filt_args):
    b, l, _ = u.shape
    assert 2 * l == HY_R * HY_R
    fre, fim, tw, fwd, inv = _dft_consts(HY_R)
    half = HY_R // 2
    as_bf = lambda m: jnp.asarray(m, F32).astype(BF16)
    f_a_data = as_bf(np.concatenate([fre[:, :half], fim[:, :half]], axis=0))
    f_a_filt = as_bf(np.concatenate([fre, fim], axis=0))
    f_c = as_bf(np.concatenate([fre[:half], fim[:half]], axis=1))
    tw_j = jnp.asarray(tw, F32)
    fwd_bf = as_bf(fwd)
    inv_bf = as_bf(inv)

    taps, ss = _hyena_filter_taps(l, *filt_args)
    nch = HY_ORDER * HY_W
    ak = _left_matmul(f_a_filt, taps.reshape(1, HY_R, HY_R * nch), "hyena_filter_dft_a")
    kf = _hy_spectrum(ak.reshape(1, 2, HY_R, HY_R, nch), tw_j, fwd_bf, ss)

    x1, x2, v = _hyena_prep(u, conv_w, conv_b)
    z = v
    for o, gate in enumerate((x1, x2)):
        z3 = z.reshape(b, half, HY_R * HY_W)
        a = _left_matmul(f_a_data, z3, "hyena_dft_a")
        zs = _hy_mid(a.reshape(b, 2, HY_R, HY_R, HY_W), kf, o, tw_j, fwd_bf, inv_bf)
        skip_row = jnp.tile(skip[o], HY_COLS // HY_W).reshape(1, HY_COLS)
        z = _hy_out(f_c, zs.reshape(b, 2 * HY_R, HY_R * HY_W), gate.reshape(b, half, HY_R * HY_W), z3,
                    skip_row).reshape(b, l, HY_W)
    return z


def _hy_ctx_kernel(x1_ref, x2_ref, v_ref, taps_ref, ss_ref, fa_ref, fc_ref, sk_ref, o_ref, *, seq_len):
    n2 = 2 * seq_len
    fa = fa_ref[...]
    scale = lax.rsqrt(ss_ref[...] + EPS) * (1.0 / n2)
    kf = jnp.dot(fa, taps_ref[...].astype(BF16), preferred_element_type=F32) * scale
    z = v_ref[0].astype(F32)
    for o, g_ref in enumerate((x1_ref, x2_ref)):
        x = jnp.dot(fa[:, :seq_len], z.astype(BF16), preferred_element_type=F32)
        xre, xim = x[:n2], x[n2:]
        kre = kf[:n2, o * HY_W:(o + 1) * HY_W]
        kim = kf[n2:, o * HY_W:(o + 1) * HY_W]
        yst = jnp.concatenate([xre * kre - xim * kim, xre * kim + xim * kre], axis=0).astype(BF16)
        y = jnp.dot(fc_ref[...], yst, preferred_element_type=F32)
        z = g_ref[0].astype(F32) * (y + z * sk_ref[pl.ds(o, 1), :])
    o_ref[0] = z.astype(BF16)


def _hyena_short(u, conv_w, conv_b, skip, filt_args):
    b, l, _ = u.shape
    n2 = 2 * l
    k = np.arange(n2)
    ang = 2.0 * np.pi * np.outer(k, k) / n2
    fre, fim = np.cos(ang), -np.sin(ang)
    fa = jnp.asarray(np.concatenate([fre, fim], axis=0), F32).astype(BF16)
    fc = jnp.asarray(np.concatenate([fre[:l], fim[:l]], axis=1), F32).astype(BF16)
    taps, ss = _hyena_filter_taps(l, *filt_args)
    x1, x2, v = _hyena_prep(u, conv_w, conv_b)
    nch = HY_ORDER * HY_W
    dspec = pl.BlockSpec((1, l, HY_W), lambda bi: (bi, 0, 0))
    const = lambda bi: (0, 0)
    return pl.pallas_call(
        functools.partial(_hy_ctx_kernel, seq_len=l),
        grid=(b,),
        in_specs=[dspec, dspec, dspec,
                  pl.BlockSpec((n2, nch), const), pl.BlockSpec((1, nch), const),
                  pl.BlockSpec((2 * n2, n2), const), pl.BlockSpec((l, 2 * n2), const),
                  pl.BlockSpec((HY_ORDER, HY_W), const)],
        out_specs=dspec,
        out_shape=jax.ShapeDtypeStruct((b, l, HY_W), BF16),
        compiler_params=_cparams(("arbitrary",)),
        name="hyena_context",
    )(x1, x2, v, taps, ss, fa, fc, skip)


def _rope_tables(n):
    t = jnp.arange(n, dtype=jnp.int32)
    pos = jnp.stack([t // GRID_W, t % GRID_W], axis=-1).astype(F32)
    nf = HEAD_DIM // 4
    inv = ROPE_THETA ** (-jnp.arange(nf, dtype=F32) / nf)
    ang = pos[:, :, None] * inv
    cos, sin = jnp.cos(ang), jnp.sin(ang)
    c64 = jnp.concatenate([cos[:, 0], cos[:, 0], cos[:, 1], cos[:, 1]], axis=-1)
    s64 = jnp.concatenate([-sin[:, 0], sin[:, 0], -sin[:, 1], sin[:, 1]], axis=-1)
    return jnp.tile(c64, (1, 2)), jnp.tile(s64, (1, 2))


def _block_diag(pool_w):
    g, c, _ = pool_w.shape
    out = jnp.zeros((g * c, g * c), pool_w.dtype)
    for i in range(g):
        out = out.at[i * c:(i + 1) * c, i * c:(i + 1) * c].set(pool_w[i])
    return out


def kernel(x, c, ctx, c_ctx, w_mod, b_mod, norm1_w, norm2_w, w_in, w_out, q_norm_w, k_norm_w, na_rpb,
           hy_conv_w, hy_conv_b, hy_w1, hy_b1, hy_w2, hy_b2, hy_w3, hy_freq, hy_skip,
           pool_w, pool_scale, w_router, w_gate, w_up, w_down):
    b, n, d = x.shape
    lc = ctx.shape[1]
    depth = w_mod.shape[0]
    rows = n // GRID_W
    cos128, sin128 = _rope_tables(n)
    seg = jnp.asarray(np.kron(np.eye(2), np.full((HEAD_DIM, HEAD_DIM), 1.0 / HEAD_DIM)), BF16)
    cc = jnp.zeros((8, d), F32).at[:b].set(c).at[b].set(c_ctx)

    xl = x.reshape(b * n, d)
    xc = ctx.reshape(b * lc, d)
    for i in range(depth):
        last = i == depth - 1
        mod = _modulation(cc, w_mod, b_mod[i], i)
        ml = mod[:b].reshape(b, 1, 6, d)
        mc = mod[b].reshape(1, 1, 6, d)
        sh1, sc1, g1, sh2, sc2, g2 = [ml[:, :, j] for j in range(6)]
        csh1, csc1, cg1, csh2, csc2, cg2 = [mc[:, :, j] for j in range(6)]
        w_in_bf = w_in[i].astype(BF16)
        w_out_bf = w_out[i].astype(BF16)
        w_router_t = w_router[i].T
        pool_bd = _block_diag(pool_w[i]).astype(BF16)
        hy_args = (hy_w1[i], hy_b1[i], hy_w2[i], hy_b2[i], hy_w3[i], hy_freq[i])

        qc, kc, vc, hyc, poolc = _in_projection(
            xc, csh1, csc1, norm1_w[i], w_in_bf, q_norm_w[i], k_norm_w[i], cos128, sin128, seg,
            rows_per_mod=b * lc, seq_len=lc, tm=lc, rope=False)
        kc = kc.reshape(b, lc, NA_W)
        vc = vc.reshape(b, lc, NA_W)
        if not last:
            na_c = _dense_attention(qc.reshape(b, lc, NA_W), kc, vc).reshape(b * lc, NA_W)
            hy_c = _hyena_short(hyc.reshape(b, lc, 3 * HY_W), hy_conv_w[i], hy_conv_b[i], hy_skip[i], hy_args)
            pool_c = _pool_mixer(poolc.reshape(b, lc, POOL_W), pool_bd, pool_scale[i])
            xc1, h2c, affc = _out_projection(
                na_c, hy_c.reshape(b * lc, HY_W).astype(BF16), pool_c.reshape(b * lc, POOL_W), w_out_bf,
                xc, cg1, norm2_w[i], csh2, csc2, w_router_t, rows_per_mod=b * lc, seq_len=lc, tm=lc)

        qr, qp, kl, vl, hyl, pooll = _in_projection(
            xl, sh1, sc1, norm1_w[i], w_in_bf, q_norm_w[i], k_norm_w[i], cos128, sin128, seg,
            rows_per_mod=n, seq_len=n, tm=512, rope=True)
        bias = _na_bias_tables(na_rpb[i], rows)
        o_na = _neighborhood_attention(qr.reshape(b, n, NA_W), qp.reshape(b, n, NA_W), kl.reshape(b, n, NA_W),
                                       vl.reshape(b, n, NA_W), kc, vc, bias)
        hy_l = _hyena_long(hyl.reshape(b, n, 3 * HY_W), hy_conv_w[i], hy_conv_b[i], hy_skip[i], hy_args)
        pool_l = _pool_mixer(pooll.reshape(b, n, POOL_W), pool_bd, pool_scale[i])
        xl1, h2l, affl = _out_projection(
            o_na.reshape(b * n, NA_W), hy_l.reshape(b * n, HY_W).astype(BF16), pool_l.reshape(b * n, POOL_W),
            w_out_bf, xl, g1, norm2_w[i], sh2, sc2, w_router_t, rows_per_mod=n, seq_len=n, tm=512)
        streams = [(h2l, affl, xl1, g2)]
        if not last:
            streams.append((h2c, affc, xc1, cg2))
        outs = _expert_choice_moe(streams, w_gate, w_up, w_down, i)
        xl = outs[0]
        if not last:
            xc = outs[1]
    return xl.reshape(b, n, d)
```

```python
import functools
import math

import jax
import jax.numpy as jnp
import numpy as np
from jax import lax
from jax.experimental import pallas as pl
from jax.experimental.pallas import tpu as pltpu

F32 = jnp.float32
BF16 = jnp.bfloat16

D_MODEL = 1024
GRID_W = 64
NA_HEADS = 8
HEAD_DIM = 64
NA_W = NA_HEADS * HEAD_DIM
NA_WIN_R = 8
NA_WIN_C = 16
ROPE_THETA = 10000.0
HY_W = 256
HY_ORDER = 2
HY_SHORT = 3
HY_BANDS = 16
HY_DECAY_TARGET = 1e-2
HY_FAST_DECAY = 0.3
HY_SLOW_DECAY = 1.5
POOL_W = 256
POOL_SIZES = (2, 4, 8, 16)
POOL_GROUP = POOL_W // len(POOL_SIZES)
MIX_W = NA_W + HY_W + POOL_W
IN_W = 3 * NA_W + (HY_ORDER + 1) * HY_W + POOL_W
N_EXPERTS = 16
EC_FACTOR = 2
D_EXPERT = 2048
EPS = 1e-6

LANES = 128
TOKEN_SUBLANES = D_MODEL // 128
NEG_BIG = -1e30
VMEM_LIMIT = 56 * 1024 * 1024

NA_Q_ROWS = 4
NA_K_ROWS = NA_Q_ROWS + NA_WIN_R - 1
NA_TILES_PER_STEP = 2


def _cparams(sem):
    return pltpu.CompilerParams(dimension_semantics=sem, vmem_limit_bytes=VMEM_LIMIT)


def _mod_kernel(c_ref, w_ref, b_ref, o_ref):
    c = c_ref[...]
    s = c * jax.nn.sigmoid(c)
    o_ref[...] = jnp.dot(s, w_ref[0], preferred_element_type=F32,
                         precision=lax.Precision.HIGHEST) + b_ref[...]


def _modulation(cc, w_mod, b_mod, layer):
    rows, d = cc.shape
    n = w_mod.shape[2]
    tn = 1536
    return pl.pallas_call(
        _mod_kernel,
        grid=(n // tn,),
        in_specs=[pl.BlockSpec((rows, d), lambda j: (0, 0)),
                  pl.BlockSpec((1, d, tn), lambda j: (layer, 0, j)),
                  pl.BlockSpec((1, tn), lambda j: (0, j))],
        out_specs=pl.BlockSpec((rows, tn), lambda j: (0, j)),
        out_shape=jax.ShapeDtypeStruct((rows, n), F32),
        compiler_params=_cparams(("arbitrary",)),
        name="modulation",
    )(cc, w_mod, b_mod.reshape(1, n))


def _head_norm(xs, w128, seg):
    m = jnp.dot((xs * xs).astype(BF16), seg, preferred_element_type=F32)
    return xs * lax.rsqrt(m + EPS) * w128


def _rope(xs, c, s_signed):
    lane = lax.broadcasted_iota(jnp.int32, xs.shape, 1)
    partner = jnp.where((lane & 16) == 0, pltpu.roll(xs, LANES - 16, 1), pltpu.roll(xs, 16, 1))
    return xs * c + partner * s_signed


def _inproj_kernel(x_ref, sh_ref, sc_ref, nw_ref, w_ref, qw_ref, kw_ref, cos_ref, sin_ref, seg_ref,
                   *out_refs, rope):
    if rope:
        qr_ref, qp_ref, k_ref, v_ref, hy_ref, pool_ref = out_refs
    else:
        qp_ref, k_ref, v_ref, hy_ref, pool_ref = out_refs
    x = x_ref[...]
    ms = jnp.mean(x * x, axis=-1, keepdims=True)
    h = x * lax.rsqrt(ms + EPS) * nw_ref[...]
    h = h * (1.0 + sc_ref[0]) + sh_ref[0]
    u = jnp.dot(h.astype(BF16), w_ref[...], preferred_element_type=F32)
    seg = seg_ref[...]
    qk_scale = HEAD_DIM ** -0.5
    for ch in range(NA_W // LANES):
        sl = slice(ch * LANES, (ch + 1) * LANES)
        qn = _head_norm(u[:, sl], qw_ref[...], seg)
        kn = _head_norm(u[:, NA_W + ch * LANES:NA_W + (ch + 1) * LANES], kw_ref[...], seg)
        qp_ref[:, sl] = (qn * qk_scale).astype(BF16)
        if rope:
            c = cos_ref[...]
            s = sin_ref[...]
            qr_ref[:, sl] = (_rope(qn, c, s) * qk_scale).astype(BF16)
            k_ref[:, sl] = _rope(kn, c, s).astype(BF16)
        else:
            k_ref[:, sl] = kn.astype(BF16)
    v_ref[...] = u[:, 2 * NA_W:3 * NA_W].astype(BF16)
    hy_ref[...] = u[:, 3 * NA_W:3 * NA_W + 3 * HY_W]
    pool_ref[...] = u[:, 3 * NA_W + 3 * HY_W:]


def _in_projection(x2, shift, scale, norm_w, w_in_bf, q_norm_w, k_norm_w, cos128, sin128, seg, *,
                   rows_per_mod, seq_len, tm, rope):
    m, d = x2.shape
    tiles_per_mod = rows_per_mod // tm
    tiles_per_seq = seq_len // tm
    row = lambda i: (i, 0)
    const = lambda i: (0, 0)
    modi = lambda i: (i // tiles_per_mod, 0, 0)
    pos = lambda i: (i % tiles_per_seq, 0)
    outs = []
    if rope:
        outs.append((NA_W, BF16))
    outs += [(NA_W, BF16), (NA_W, BF16), (NA_W, BF16), (3 * HY_W, F32), (POOL_W, F32)]
    return pl.pallas_call(
        functools.partial(_inproj_kernel, rope=rope),
        grid=(m // tm,),
        in_specs=[pl.BlockSpec((tm, d), row),
                  pl.BlockSpec((1, 1, d), modi), pl.BlockSpec((1, 1, d), modi),
                  pl.BlockSpec((1, d), const),
                  pl.BlockSpec((d, IN_W), const),
                  pl.BlockSpec((1, LANES), const), pl.BlockSpec((1, LANES), const),
                  pl.BlockSpec((tm, LANES), pos), pl.BlockSpec((tm, LANES), pos),
                  pl.BlockSpec((LANES, LANES), const)],
        out_specs=[pl.BlockSpec((tm, w), row) for w, _ in outs],
        out_shape=[jax.ShapeDtypeStruct((m, w), dt) for w, dt in outs],
        compiler_params=_cparams(("arbitrary",)),
        name="in_projection_rope" if rope else "in_projection_ctx",
    )(x2, shift, scale, norm_w.reshape(1, d), w_in_bf,
      jnp.tile(q_norm_w, 2).reshape(1, LANES), jnp.tile(k_norm_w, 2).reshape(1, LANES),
      cos128, sin128, seg)


def _softmax_pv(s_list, v_list):
    mx = functools.reduce(jnp.maximum, [jnp.max(s, axis=-1, keepdims=True) for s in s_list])
    ps = [jnp.exp(s - mx) for s in s_list]
    l = functools.reduce(jnp.add, [jnp.sum(p, axis=-1, keepdims=True) for p in ps])
    o = functools.reduce(jnp.add, [jnp.dot(p.astype(BF16), v, preferred_element_type=F32)
                                   for p, v in zip(ps, v_list)])
    return o / l


_NT = (((1,), (1,)), ((), ()))


def _na_kernel(qr_ref, qp_ref, k_ref, v_ref, kc_ref, vc_ref, bias_ref, o_ref, *, rows):
    n_tiles = rows // NA_Q_ROWS
    tq = NA_Q_ROWS * GRID_W
    kc = kc_ref[0]
    vc = vc_ref[0]
    lane = lax.broadcasted_iota(jnp.int32, (tq, LANES), 1)
    for sub in range(NA_TILES_PER_STEP):
        t = pl.program_id(2) * NA_TILES_PER_STEP + sub
        ks = jnp.clip(t * NA_Q_ROWS - NA_WIN_R // 2, 0, rows - NA_K_ROWS)
        start = pl.multiple_of(ks * GRID_W, GRID_W)
        k = k_ref[0, pl.ds(start, NA_K_ROWS * GRID_W), :]
        v = v_ref[0, pl.ds(start, NA_K_ROWS * GRID_W), :]
        q = qr_ref[0, pl.ds(sub * tq, tq), :]
        qp = qp_ref[0, pl.ds(sub * tq, tq), :]
        kind = jnp.where(t == 0, 0, jnp.where(t == n_tiles - 1, 2, 1))
        outs = []
        for hh in range(2):
            keep = (lane < HEAD_DIM) if hh == 0 else (lane >= HEAD_DIM)
            zero = jnp.zeros_like(q)
            s_lat = (lax.dot_general(jnp.where(keep, q, zero), k, _NT, preferred_element_type=F32)
                     + bias_ref[kind, hh])
            s_ctx = lax.dot_general(jnp.where(keep, qp, zero), kc, _NT, preferred_element_type=F32)
            outs.append(_softmax_pv([s_lat, s_ctx], [v, vc]))
        o_ref[0, pl.ds(sub * tq, tq), :] = jnp.where(lane < HEAD_DIM, outs[0], outs[1]).astype(BF16)


def _neighborhood_attention(qr, qp, k, v, kc, vc, bias):
    b, n, _ = qr.shape
    lc = kc.shape[1]
    rows = n // GRID_W
    n_steps = rows // (NA_Q_ROWS * NA_TILES_PER_STEP)
    tq = NA_Q_ROWS * GRID_W * NA_TILES_PER_STEP
    tk = NA_K_ROWS * GRID_W
    qmap = lambda bi, hp, t: (bi, t, hp)
    smap = lambda bi, hp, t: (bi, 0, hp)
    bmap = lambda bi, hp, t: (0, hp, 0, 0)
    return pl.pallas_call(
        functools.partial(_na_kernel, rows=rows),
        grid=(b, NA_W // LANES, n_steps),
        in_specs=[pl.BlockSpec((1, tq, LANES), qmap), pl.BlockSpec((1, tq, LANES), qmap),
                  pl.BlockSpec((1, n, LANES), smap), pl.BlockSpec((1, n, LANES), smap),
                  pl.BlockSpec((1, lc, LANES), smap), pl.BlockSpec((1, lc, LANES), smap),
                  pl.BlockSpec((3, 2, NA_Q_ROWS * GRID_W, tk), bmap)],
        out_specs=pl.BlockSpec((1, tq, LANES), qmap),
        out_shape=jax.ShapeDtypeStruct((b, n, NA_W), BF16),
        compiler_params=_cparams(("arbitrary", "arbitrary", "arbitrary")),
        name="neighborhood_attention",
    )(qr, qp, k, v, kc, vc, bias)


def _na_bias_tables(rpb, rows):
    n_tiles = rows // NA_Q_ROWS
    n_dr, n_dc = 2 * NA_WIN_R - 1, 2 * NA_WIN_C - 1
    qc = np.arange(GRID_W)
    cs = np.clip(qc - NA_WIN_C // 2, 0, GRID_W - NA_WIN_C)
    kcol = np.arange(GRID_W)
    ok_c = (kcol[None, :] >= cs[:, None]) & (kcol[None, :] < cs[:, None] + NA_WIN_C)
    dc = kcol[None, :] - qc[:, None] + NA_WIN_C - 1
    col_sel = np.concatenate([(dc[:, :, None] == np.arange(n_dc)) & ok_c[:, :, None],
                              ~ok_c[:, :, None]], axis=2)
    ext = jnp.pad(rpb.astype(F32), ((0, 0), (0, 1), (0, 1)), constant_values=NEG_BIG)
    blocks = jnp.einsum("hrd,qkd->hrqk", ext, jnp.asarray(col_sel, F32), precision=lax.Precision.HIGHEST)
    tabs = []
    for t in (0, 1, n_tiles - 1):
        r0 = t * NA_Q_ROWS
        ks = int(np.clip(r0 - NA_WIN_R // 2, 0, rows - NA_K_ROWS))
        r = r0 + np.arange(NA_Q_ROWS)
        rs = np.clip(r - NA_WIN_R // 2, 0, rows - NA_WIN_R)
        krow = ks + np.arange(NA_K_ROWS)
        ok_r = (krow[None, :] >= rs[:, None]) & (krow[None, :] < rs[:, None] + NA_WIN_R)
        dr = np.where(ok_r, krow[None, :] - r[:, None] + NA_WIN_R - 1, n_dr)
        tabs.append(jnp.concatenate(
            [jnp.concatenate([blocks[:, int(dr[i, j])] for j in range(NA_K_ROWS)], axis=-1)
             for i in range(NA_Q_ROWS)], axis=-2))
    return jnp.stack(tabs)


def _dense_attn_kernel(q_ref, k_ref, v_ref, o_ref):
    q = q_ref[0]
    k = k_ref[0]
    v = v_ref[0]
    lane = lax.broadcasted_iota(jnp.int32, q.shape, 1)
    outs = []
    for hh in range(2):
        keep = (lane < HEAD_DIM) if hh == 0 else (lane >= HEAD_DIM)
        s = lax.dot_general(jnp.where(keep, q, jnp.zeros_like(q)), k, _NT, preferred_element_type=F32)
        outs.append(_softmax_pv([s], [v]))
    o_ref[0] = jnp.where(lane < HEAD_DIM, outs[0], outs[1]).astype(BF16)


def _dense_attention(q, k, v):
    b, l, _ = q.shape
    spec = pl.BlockSpec((1, l, LANES), lambda bi, hp: (bi, 0, hp))
    return pl.pallas_call(
        _dense_attn_kernel,
        grid=(b, NA_W // LANES),
        in_specs=[spec, spec, spec],
        out_specs=spec,
        out_shape=jax.ShapeDtypeStruct((b, l, NA_W), BF16),
        compiler_params=_cparams(("arbitrary", "arbitrary")),
        name="context_attention",
    )(q, k, v)


POOL_HALO = max(POOL_SIZES) // 2


def _pool_kernel(x_ref, w_ref, sc_ref, o_ref, xpad, *, seq_len, chunk):
    zeros = jnp.zeros((POOL_HALO, POOL_W), F32)
    xpad[pl.ds(0, POOL_HALO), :] = zeros
    xpad[pl.ds(seq_len + POOL_HALO, POOL_HALO), :] = zeros
    xpad[pl.ds(POOL_HALO, seq_len), :] = x_ref[0]
    span = chunk + 2 * POOL_HALO
    w = w_ref[...]
    sc = sc_ref[...]

    def body(c, carry):
        base = pl.multiple_of(c * chunk, chunk)
        xs = xpad[pl.ds(base, span), :]
        s2 = xs + pltpu.roll(xs, 1, 0)
        s4 = pltpu.roll(s2, 1, 0) + pltpu.roll(s2, span - 1, 0)
        s8 = pltpu.roll(s4, 2, 0) + pltpu.roll(s4, span - 2, 0)
        s16 = pltpu.roll(s8, 4, 0) + pltpu.roll(s8, span - 4, 0)
        mid = slice(POOL_HALO, POOL_HALO + chunk)
        grp = lax.broadcasted_iota(jnp.int32, (chunk, POOL_W), 1) // POOL_GROUP
        tpos = base + lax.broadcasted_iota(jnp.int32, (chunk, POOL_W), 0)
        half = jnp.left_shift(1, grp)
        cnt = (jnp.minimum(tpos + half, seq_len) - jnp.maximum(tpos - half, 0)).astype(F32)
        ssum = jnp.where(grp == 0, s2[mid], jnp.where(grp == 1, s4[mid], jnp.where(grp == 2, s8[mid], s16[mid])))
        diff = ssum / cnt - xs[mid]
        y = jnp.dot(diff.astype(BF16), w, preferred_element_type=F32) * sc
        o_ref[0, pl.ds(base, chunk), :] = y.astype(BF16)
        return carry

    lax.fori_loop(0, seq_len // chunk, body, 0)


def _pool_mixer(u, w_blockdiag_bf, pool_scale):
    b, l, _ = u.shape
    chunk = min(l, 512)
    return pl.pallas_call(
        functools.partial(_pool_kernel, seq_len=l, chunk=chunk),
        grid=(b,),
        in_specs=[pl.BlockSpec((1, l, POOL_W), lambda bi: (bi, 0, 0)),
                  pl.BlockSpec((POOL_W, POOL_W), lambda bi: (0, 0)),
                  pl.BlockSpec((1, POOL_W), lambda bi: (0, 0))],
        out_specs=pl.BlockSpec((1, l, POOL_W), lambda bi: (bi, 0, 0)),
        out_shape=jax.ShapeDtypeStruct((b, l, POOL_W), BF16),
        scratch_shapes=[pltpu.VMEM((l + 2 * POOL_HALO, POOL_W), F32)],
        compiler_params=_cparams(("arbitrary",)),
        name="pool_mixer",
    )(u, w_blockdiag_bf, pool_scale.reshape(1, POOL_W))


def _outproj_kernel(na_ref, hy_ref, pl_ref, w_ref, x_ref, g1_ref, nw_ref, sh_ref, sc_ref, wr_ref,
                    xn_ref, h2_ref, aff_ref):
    w = w_ref
    acc = jnp.dot(na_ref[...], w[pl.ds(0, NA_W), :], preferred_element_type=F32)
    acc += jnp.dot(hy_ref[...], w[pl.ds(NA_W, HY_W), :], preferred_element_type=F32)
    acc += jnp.dot(pl_ref[...], w[pl.ds(NA_W + HY_W, POOL_W), :], preferred_element_type=F32)
    xn = x_ref[...] + g1_ref[0] * acc
    xn_ref[...] = xn
    ms = jnp.mean(xn * xn, axis=-1, keepdims=True)
    h2 = xn * lax.rsqrt(ms + EPS) * nw_ref[...]
    h2 = h2 * (1.0 + sc_ref[0]) + sh_ref[0]
    tm = h2.shape[0]
    for s in range(TOKEN_SUBLANES):
        h2_ref[pl.ds(s, tm, stride=TOKEN_SUBLANES), :] = h2[:, s * LANES:(s + 1) * LANES]
    logits = lax.dot_general(wr_ref[...], h2, _NT, preferred_element_type=F32,
                             precision=lax.Precision.HIGHEST)
    mx = jnp.max(logits, axis=0, keepdims=True)
    p = jnp.exp(logits - mx)
    aff_ref[0] = p / jnp.sum(p, axis=0, keepdims=True)


def _out_projection(na, hy, pool, w_out_bf, x2, g1, norm_w, shift, scale, w_router_t, *,
                    rows_per_mod, seq_len, tm):
    m, d = x2.shape
    tiles_per_mod = rows_per_mod // tm
    tiles_per_seq = seq_len // tm
    row = lambda i: (i, 0)
    const = lambda i: (0, 0)
    modi = lambda i: (i // tiles_per_mod, 0, 0)
    return pl.pallas_call(
        _outproj_kernel,
        grid=(m // tm,),
        in_specs=[pl.BlockSpec((tm, NA_W), row), pl.BlockSpec((tm, HY_W), row), pl.BlockSpec((tm, POOL_W), row),
                  pl.BlockSpec((MIX_W, d), const),
                  pl.BlockSpec((tm, d), row),
                  pl.BlockSpec((1, 1, d), modi),
                  pl.BlockSpec((1, d), const),
                  pl.BlockSpec((1, 1, d), modi), pl.BlockSpec((1, 1, d), modi),
                  pl.BlockSpec((N_EXPERTS, d), const)],
        out_specs=[pl.BlockSpec((tm, d), row), pl.BlockSpec((tm * TOKEN_SUBLANES, LANES), row),
                   pl.BlockSpec((1, N_EXPERTS, tm), lambda i: (i // tiles_per_seq, 0, i % tiles_per_seq))],
        out_shape=[jax.ShapeDtypeStruct((m, d), F32), jax.ShapeDtypeStruct((m * TOKEN_SUBLANES, LANES), F32),
                   jax.ShapeDtypeStruct((m // seq_len, N_EXPERTS, seq_len), F32)],
        compiler_params=_cparams(("arbitrary",)),
        name="out_projection_router",
    )(na, hy, pool, w_out_bf, x2, g1, norm_w.reshape(1, d), shift, scale, w_router_t)


FFN_TF = 256
FFN_ROW_CHUNKS = 2


def _ffn_kernel(*refs, n_experts, sources, n_rows):
    ns = len(sources)
    idx_cur = refs[0:ns]
    idx_nxt = refs[ns:2 * ns]
    src = refs[2 * ns:3 * ns]
    gate_ref, wg_ref, wu_ref, wd_ref, o_ref, stage, xb, acc, sem = refs[3 * ns:]
    e = pl.program_id(0)
    f = pl.program_id(1)
    n_f = pl.num_programs(1)

    ts = TOKEN_SUBLANES

    def row_copy(s, row, r):
        return pltpu.make_async_copy(src[s].at[pl.ds(pl.multiple_of(row * ts, ts), ts)],
                                     stage.at[pl.ds(pl.multiple_of((sources[s][0] + r) * ts, ts), ts)], sem.at[0])

    def gather(idx_refs):
        for s in range(ns):
            def body(r, carry, s=s):
                row_copy(s, idx_refs[s][0, 0, r], r).start()
                return carry
            lax.fori_loop(0, sources[s][1], body, 0, unroll=8)

    def gather_wait():
        for s in range(ns):
            off, cnt = sources[s]
            pltpu.make_async_copy(src[s].at[pl.ds(0, cnt * ts)], stage.at[pl.ds(off * ts, cnt * ts)],
                                  sem.at[0]).wait()

    @pl.when(f == 0)
    def _():
        @pl.when(e == 0)
        def _():
            gather(idx_cur)

        acc[...] = jnp.zeros_like(acc)
        gather_wait()
        for s in range(ts):
            xb[:, s * LANES:(s + 1) * LANES] = stage[pl.ds(s, n_rows, stride=ts), :].astype(BF16)

        @pl.when(e + 1 < n_experts)
        def _():
            gather(idx_nxt)

    wg = wg_ref[0, 0].astype(BF16)
    wu = wu_ref[0, 0].astype(BF16)
    wd = wd_ref[0, 0].astype(BF16)
    rc = n_rows // FFN_ROW_CHUNKS
    for r in range(FFN_ROW_CHUNKS):
        rs = pl.ds(r * rc, rc)
        x = xb[rs, :]
        a = jnp.dot(x, wg, preferred_element_type=F32)
        b = jnp.dot(x, wu, preferred_element_type=F32)
        h = (a * jax.nn.sigmoid(a) * b).astype(BF16)
        y = jnp.dot(h, wd, preferred_element_type=F32)
        acc[rs, :] += y

    @pl.when(f == n_f - 1)
    def _():
        o_ref[0] = (acc[...] * gate_ref[0]).astype(BF16)


def _expert_ffn(rows_idx, tokens, gate, w_gate, w_up, w_down, layer):
    ns = len(tokens)
    e, d = w_gate.shape[1], w_gate.shape[2]
    counts = [int(ri.shape[2]) for ri in rows_idx]
    offs = [int(sum(counts[:s])) for s in range(ns)]
    r_total = sum(counts)
    assert r_total % (FFN_ROW_CHUNKS * 16) == 0
    smem = lambda cnt, nxt: pl.BlockSpec(
        (1, 1, cnt), (lambda ei, f: (jnp.minimum(ei + 1, e - 1), 0, 0)) if nxt else (lambda ei, f: (ei, 0, 0)),
        memory_space=pltpu.SMEM)
    in_specs = ([smem(cnt, False) for cnt in counts] + [smem(cnt, True) for cnt in counts]
                + [pl.BlockSpec(memory_space=pl.ANY) for _ in range(ns)]
                + [pl.BlockSpec((1, r_total, 1), lambda ei, f: (ei, 0, 0)),
                   pl.BlockSpec((1, 1, d, FFN_TF), lambda ei, f: (layer, ei, 0, f)),
                   pl.BlockSpec((1, 1, d, FFN_TF), lambda ei, f: (layer, ei, 0, f)),
                   pl.BlockSpec((1, 1, FFN_TF, d), lambda ei, f: (layer, ei, f, 0))])
    return pl.pallas_call(
        functools.partial(_ffn_kernel, n_experts=e, sources=tuple(zip(offs, counts)), n_rows=r_total),
        grid=(e, D_EXPERT // FFN_TF),
        in_specs=in_specs,
        out_specs=pl.BlockSpec((1, r_total, d), lambda ei, f: (ei, 0, 0)),
        out_shape=jax.ShapeDtypeStruct((e, r_total, d), BF16),
        scratch_shapes=[pltpu.VMEM((r_total * TOKEN_SUBLANES, LANES), F32), pltpu.VMEM((r_total, d), BF16),
                        pltpu.VMEM((r_total, d), F32), pltpu.SemaphoreType.DMA((1,))],
        compiler_params=_cparams(("arbitrary", "arbitrary")),
        name="expert_ffn",
    )(*rows_idx, *rows_idx, *tokens, gate, w_gate, w_up, w_down)


def _prefix_count(m, n):
    lane = lax.broadcasted_iota(jnp.int32, m.shape, 1)
    c = m
    s = 1
    while s < n:
        c = c + jnp.where(lane >= s, pltpu.roll(c, s, 1), 0)
        s *= 2
    return c


COMBINE_TB = 256
COMBINE_CHUNK = 128


def _topk_kernel(aff_ref, idx_ref, gate_ref, off_ref, *, cap):
    a = aff_ref[...]
    rows, n = a.shape

    def search(i, t):
        cand = t | jnp.left_shift(jnp.int32(1), 30 - i)
        cnt = jnp.sum((a >= pltpu.bitcast(cand, F32)).astype(jnp.int32), axis=1, keepdims=True)
        return jnp.where(cnt >= cap, cand, t)

    thr_bits = lax.fori_loop(0, 31, search, jnp.zeros((rows, 1), jnp.int32))
    thr = pltpu.bitcast(thr_bits, F32)
    gt = a >= pltpu.bitcast(thr_bits + 1, F32)
    eq = (a >= thr) & jnp.logical_not(gt)
    need = cap - jnp.sum(gt.astype(jnp.int32), axis=1, keepdims=True)
    eq_i = eq.astype(jnp.int32)
    sel = gt | (eq & (_prefix_count(eq_i, n) - eq_i < need))
    sel_i = sel.astype(jnp.int32)
    lane = lax.broadcasted_iota(jnp.int32, a.shape, 1)
    dist = jnp.where(sel, lane + 1 - _prefix_count(sel_i, n), 0)
    tok = jnp.where(sel, lane, -1)
    val = a
    s = 1
    while s < n:
        tok_in = pltpu.roll(tok, n - s, 1)
        dist_in = pltpu.roll(dist, n - s, 1)
        val_in = pltpu.roll(val, n - s, 1)
        take = (lane < n - s) & (tok_in >= 0) & ((dist_in & s) != 0)
        stay = (tok >= 0) & ((dist & s) == 0)
        tok = jnp.where(take, tok_in, jnp.where(stay, tok, -1))
        dist = jnp.where(take, dist_in, jnp.where(stay, dist, 0))
        val = jnp.where(take, val_in, val)
        s *= 2
    idx_ref[...] = tok[:, :cap]
    gate_ref[...] = val[:, :cap]
    olane = lax.broadcasted_iota(jnp.int32, off_ref.shape, 1)
    off = jnp.zeros(off_ref.shape, jnp.int32)
    for j in range(n // COMBINE_TB + 1):
        below = jnp.sum(jnp.where(lane < j * COMBINE_TB, sel_i, 0), axis=1, keepdims=True)
        off = jnp.where(olane == j, below, off)
    off_ref[...] = off


def _topk_select(aff, cap):
    b, e, n = aff.shape
    assert n & (n - 1) == 0 and n // COMBINE_TB < LANES
    rows = b * e
    full = lambda w: pl.BlockSpec((rows, w), lambda i: (0, 0))
    idx, gate, off = pl.pallas_call(
        functools.partial(_topk_kernel, cap=cap),
        grid=(1,),
        in_specs=[full(n)],
        out_specs=[full(cap), full(cap), full(LANES)],
        out_shape=[jax.ShapeDtypeStruct((rows, cap), jnp.int32), jax.ShapeDtypeStruct((rows, cap), F32),
                   jax.ShapeDtypeStruct((rows, LANES), jnp.int32)],
        compiler_params=_cparams(("arbitrary",)),
        name="expert_choice_topk",
    )(aff.reshape(rows, n))
    return idx.reshape(b, e, cap), gate.reshape(b, e, cap), off


def _combine_kernel(off_ref, y_ref, idx_ref, x_ref, g_ref, o_ref, acc, ycat, *, chunk, n_chunks):
    bi = pl.program_id(0)
    j = pl.program_id(1)
    tb = x_ref.shape[0]
    tok = j * tb + lax.broadcasted_iota(jnp.int32, (tb, chunk), 0)

    def onehot(e, st):
        return (idx_ref[0, pl.ds(e, 1), pl.ds(st, chunk)] == tok).astype(BF16)

    if n_chunks == 1:
        acc[...] = jnp.zeros_like(acc)
        for e in range(N_EXPERTS):
            acc[...] += jnp.dot(onehot(e, 0), y_ref[e], preferred_element_type=F32)
    else:
        cap = n_chunks * chunk
        shift = chunk.bit_length() - 1
        col = lax.broadcasted_iota(jnp.int32, (1, chunk), 1)

        def slot_ids(e, st):
            base = jnp.minimum(lax.shift_left(lax.shift_right_logical(st, shift), shift), cap - 2 * chunk)
            wide = idx_ref[0, pl.ds(e, 1), pl.ds(pl.multiple_of(base, chunk), 2 * chunk)]
            wide = jnp.broadcast_to(wide, (8, 2 * chunk))
            return pltpu.roll(wide, (2 * chunk - (st - base)) & (2 * chunk - 1), 1)[:1, :chunk]

        starts, ends, hots = [], [], []
        for e in range(N_EXPERTS):
            lo = off_ref[bi * N_EXPERTS + e, j]
            hi = off_ref[bi * N_EXPERTS + e, j + 1]
            st = jnp.minimum(lax.shift_left(lax.shift_right_logical(lo, 4), 4), cap - chunk)
            starts.append(st)
            ends.append(hi)
            ycat[pl.ds(e * chunk, chunk), :] = y_ref[e, pl.ds(pl.multiple_of(st, 16), chunk), :]
            hots.append((slot_ids(e, st) == tok).astype(BF16))
        acc[...] = jnp.dot(jnp.concatenate(hots, axis=1), ycat[...], preferred_element_type=F32)
        for e in range(N_EXPERTS):
            def add_window(c, carry, e=e):
                first = starts[e] + c * chunk
                st = jnp.minimum(first, cap - chunk)
                hot = ((slot_ids(e, st) == tok) & (st + col >= first)).astype(BF16)
                acc[...] += jnp.dot(hot, y_ref[e, pl.ds(pl.multiple_of(st, 16), chunk), :],
                                    preferred_element_type=F32)
                return carry

            n_more = lax.shift_right_logical(jnp.maximum(ends[e] - starts[e] - 1, 0), shift)
            lax.fori_loop(1, n_more + 1, add_window, 0)
    o_ref[...] = x_ref[...] + g_ref[0] * acc[...]


def _moe_combine(off, y, idx, x2, g, *, row_block0, cap, seq_len):
    bsz = idx.shape[0]
    d = x2.shape[1]
    tb = min(seq_len, COMBINE_TB)
    nb = seq_len // tb
    chunk = min(cap, COMBINE_CHUNK)
    n_chunks = cap // chunk
    n_mod = g.shape[0]
    grid_spec = pltpu.PrefetchScalarGridSpec(
        num_scalar_prefetch=1,
        grid=(bsz, nb),
        in_specs=[pl.BlockSpec((N_EXPERTS, cap, d), lambda bi, j, off: (0, row_block0 + bi, 0),
                               pipeline_mode=pl.Buffered(1)),
                  pl.BlockSpec((1, N_EXPERTS, cap), lambda bi, j, off: (bi, 0, 0)),
                  pl.BlockSpec((tb, d), lambda bi, j, off: (bi * nb + j, 0)),
                  pl.BlockSpec((1, 1, d), lambda bi, j, off: (bi if n_mod > 1 else 0, 0, 0))],
        out_specs=pl.BlockSpec((tb, d), lambda bi, j, off: (bi * nb + j, 0)),
        scratch_shapes=[pltpu.VMEM((tb, d), F32), pltpu.VMEM((N_EXPERTS * chunk, d), BF16)])
    return pl.pallas_call(
        functools.partial(_combine_kernel, chunk=chunk, n_chunks=n_chunks),
        grid_spec=grid_spec,
        out_shape=jax.ShapeDtypeStruct(x2.shape, F32),
        compiler_params=_cparams(("arbitrary", "arbitrary")),
        name="moe_combine",
    )(off, y, idx, x2, g)


def _expert_choice_moe(streams, w_gate, w_up, w_down, layer):
    routed = []
    for h2, aff, x2, g in streams:
        bsz, _, l = aff.shape
        cap = EC_FACTOR * l // N_EXPERTS
        idx, gate, off = _topk_select(aff, cap)
        rows = (idx + (jnp.arange(bsz, dtype=jnp.int32) * l)[:, None, None]).transpose(1, 0, 2)
        routed.append((idx, off, rows.reshape(N_EXPERTS, 1, bsz * cap), gate.transpose(1, 0, 2).reshape(
            N_EXPERTS, bsz * cap, 1), cap))
    y = _expert_ffn([r[2] for r in routed], [s[0] for s in streams],
                    jnp.concatenate([r[3] for r in routed], axis=1), w_gate, w_up, w_down, layer)
    outs = []
    row0 = 0
    for (h2, aff, x2, g), (idx, off, rows, _, cap) in zip(streams, routed):
        assert row0 % cap == 0
        outs.append(_moe_combine(off, y, idx, x2, g, row_block0=row0 // cap, cap=cap, seq_len=aff.shape[2]))
        row0 += rows.shape[2]
    return outs


HY_R = 128
HY_COLS = 4096
HY_KB = 8
HY_UNROLL = 4


def _dft_consts(r):
    k = np.arange(r)
    ang = 2.0 * np.pi * np.outer(k, k) / r
    fre, fim = np.cos(ang), -np.sin(ang)
    k2 = np.arange(r)
    tang = 2.0 * np.pi * np.outer(k2, k2) / (r * r)
    tw = np.stack([np.cos(tang), -np.sin(tang)])
    fwd = np.block([[fre, -fim], [fim, fre]])
    inv = np.block([[fre, fim], [-fim, fre]])
    return fre, fim, tw, fwd, inv


def _hy_prep_kernel(x_ref, prev_ref, next_ref, w_ref, b_ref, x1_ref, x2_ref, v_ref, *, n_tiles):
    t = pl.program_id(1)
    x = x_ref[0]
    tm = x.shape[0]
    row = lax.broadcasted_iota(jnp.int32, x.shape, 0)
    prev = jnp.where(t == 0, 0.0, prev_ref[0, pl.ds(7, 1), :])
    nxt = jnp.where(t == n_tiles - 1, 0.0, next_ref[0, pl.ds(0, 1), :])
    up = jnp.where(row == 0, prev, pltpu.roll(x, 1, 0))
    dn = jnp.where(row == tm - 1, nxt, pltpu.roll(x, tm - 1, 0))
    y = b_ref[...] + up * w_ref[pl.ds(0, 1), :] + x * w_ref[pl.ds(1, 1), :] + dn * w_ref[pl.ds(2, 1), :]
    x1_ref[0] = y[:, :HY_W].astype(BF16)
    x2_ref[0] = y[:, HY_W:2 * HY_W].astype(BF16)
    v_ref[0] = y[:, 2 * HY_W:].astype(BF16)


def _hyena_prep(u, conv_w, conv_b):
    b, l, c = u.shape
    tm = min(l, 1024)
    n_tiles = l // tm
    g = tm // 8
    n_groups = l // 8
    out = jax.ShapeDtypeStruct((b, l, HY_W), BF16)
    ospec = pl.BlockSpec((1, tm, HY_W), lambda bi, t: (bi, t, 0))
    return pl.pallas_call(
        functools.partial(_hy_prep_kernel, n_tiles=n_tiles),
        grid=(b, n_tiles),
        in_specs=[pl.BlockSpec((1, tm, c), lambda bi, t: (bi, t, 0)),
                  pl.BlockSpec((1, 8, c), lambda bi, t: (bi, jnp.maximum(t * g - 1, 0), 0)),
                  pl.BlockSpec((1, 8, c), lambda bi, t: (bi, jnp.minimum((t + 1) * g, n_groups - 1), 0)),
                  pl.BlockSpec((HY_SHORT, c), lambda bi, t: (0, 0)),
                  pl.BlockSpec((1, c), lambda bi, t: (0, 0))],
        out_specs=[ospec, ospec, ospec],
        out_shape=[out, out, out],
        compiler_params=_cparams(("arbitrary", "arbitrary")),
        name="hyena_prep",
    )(u, u, u, conv_w, conv_b.reshape(1, c))


def _hy_filter_kernel(z_ref, w1_ref, b1_ref, w2_ref, b2_ref, w3_ref, fr_ref, dl_ref, k_ref, ss_ref, *, seq_len):
    i = pl.program_id(0)
    hp = lax.Precision.HIGHEST
    z = z_ref[...]
    tm = z.shape[0]
    half = tm // 2
    fr = fr_ref[...]
    zz = jnp.concatenate([z[:half], z[half:]], axis=1)
    h = jnp.sin(fr * (jnp.dot(zz, w1_ref[...], preferred_element_type=F32, precision=hp) + b1_ref[...]))
    h = jnp.sin(fr * (jnp.dot(h, w2_ref[...], preferred_element_type=F32, precision=hp) + b2_ref[...]))
    hb = h.astype(BF16)
    w3 = w3_ref[...].astype(BF16)
    fo = w3.shape[0]
    h = jnp.concatenate([jnp.dot(hb[:, :fo], w3, preferred_element_type=F32),
                         jnp.dot(hb[:, fo:], w3, preferred_element_type=F32)], axis=0)
    dec = jnp.exp(-z[:, 0:1] * dl_ref[...])
    r = i * tm + lax.broadcasted_iota(jnp.int32, (tm, HY_W), 0)
    parts = []
    for o in range(HY_ORDER):
        hf = h[:, (2 * o) * HY_W:(2 * o + 1) * HY_W]
        hb = h[:, (2 * o + 1) * HY_W:(2 * o + 2) * HY_W]
        parts.append(jnp.where(r < seq_len, hf, jnp.where(r > seq_len, hb, 0.0)) * dec)
    k = jnp.concatenate(parts, axis=1)
    k_ref[...] = k
    ss = jnp.sum(k * k, axis=0, keepdims=True)

    @pl.when(i == 0)
    def _():
        ss_ref[...] = ss

    @pl.when(i != 0)
    def _():
        ss_ref[...] += ss


def _hyena_filter_taps(l, w1, b1, w2, b2, w3, freq):
    t = jnp.linspace(0.0, 1.0, l, dtype=F32)[:, None]
    w = (2.0 * math.pi / l) * jnp.arange(l, dtype=F32)[:, None]
    f = jnp.linspace(1e-4, HY_BANDS - 1, HY_BANDS, dtype=F32)[None, :]
    z = jnp.concatenate([t, jnp.cos(f * w), -jnp.sin(f * w)], axis=-1)
    z2 = jnp.concatenate([z, z[:1], z[:0:-1]], axis=0)
    emb = z.shape[1]
    z2 = jnp.pad(z2, ((0, 0), (0, LANES - emb)))
    w1p = jnp.pad(w1, ((0, LANES - emb), (0, 0)))
    two = lambda m: jnp.kron(jnp.eye(2, dtype=m.dtype), m)
    pair = lambda v: jnp.tile(v, 2).reshape(1, 2 * v.shape[0])
    deltas = jnp.abs(jnp.linspace(math.log(HY_DECAY_TARGET) / HY_SLOW_DECAY,
                                  math.log(HY_DECAY_TARGET) / HY_FAST_DECAY, HY_W, dtype=F32))
    fo = w2.shape[0]
    nout = w3.shape[1]
    tm = min(2 * l, 1024)
    const = lambda i: (0, 0)
    return pl.pallas_call(
        functools.partial(_hy_filter_kernel, seq_len=l),
        grid=(2 * l // tm,),
        in_specs=[pl.BlockSpec((tm, LANES), lambda i: (i, 0)),
                  pl.BlockSpec((2 * LANES, 2 * fo), const), pl.BlockSpec((1, 2 * fo), const),
                  pl.BlockSpec((2 * fo, 2 * fo), const), pl.BlockSpec((1, 2 * fo), const),
                  pl.BlockSpec((fo, nout), const), pl.BlockSpec((1, 2 * fo), const),
                  pl.BlockSpec((1, HY_W), const)],
        out_specs=[pl.BlockSpec((tm, HY_ORDER * HY_W), lambda i: (i, 0)),
                   pl.BlockSpec((1, HY_ORDER * HY_W), const)],
        out_shape=[jax.ShapeDtypeStruct((2 * l, HY_ORDER * HY_W), F32),
                   jax.ShapeDtypeStruct((1, HY_ORDER * HY_W), F32)],
        compiler_params=_cparams(("arbitrary",)),
        name="hyena_filter_taps",
    )(z2, two(w1p), pair(b1), two(w2), pair(b2), w3, pair(freq), deltas.reshape(1, HY_W))


def _left_matmul_kernel(f_ref, x_ref, o_ref):
    o_ref[0] = jnp.dot(f_ref[...], x_ref[0].astype(BF16), preferred_element_type=F32).astype(o_ref.dtype)


def _left_matmul(f_bf, x3, name):
    b, k, c = x3.shape
    m = f_bf.shape[0]
    return pl.pallas_call(
        _left_matmul_kernel,
        grid=(b, c // HY_COLS),
        in_specs=[pl.BlockSpec((m, k), lambda bi, j: (0, 0)),
                  pl.BlockSpec((1, k, HY_COLS), lambda bi, j: (bi, 0, j))],
        out_specs=pl.BlockSpec((1, m, HY_COLS), lambda bi, j: (bi, 0, j)),
        out_shape=jax.ShapeDtypeStruct((b, m, c), BF16),
        compiler_params=_cparams(("arbitrary", "arbitrary")),
        name=name,
    )(f_bf, x3)


def _twiddle_cols(tw_ref, k1):
    lane = lax.broadcasted_iota(jnp.int32, (HY_R, HY_R), 1)
    pick = lane == k1
    twr = jnp.sum(jnp.where(pick, tw_ref[0], 0.0), axis=1, keepdims=True)
    twi = jnp.sum(jnp.where(pick, tw_ref[1], 0.0), axis=1, keepdims=True)
    return twr, twi


def _hy_spectrum_kernel(a_ref, tw_ref, fwd_ref, ss_ref, kf_ref):
    kb = pl.program_id(0)
    scale = lax.rsqrt(ss_ref[...] + EPS) * (1.0 / (HY_R * HY_R))

    def body(kk, carry):
        twr, twi = _twiddle_cols(tw_ref, kb * HY_KB + kk)
        are = a_ref[0, 0, kk].astype(F32)
        aim = a_ref[0, 1, kk].astype(F32)
        bst = jnp.concatenate([are * twr - aim * twi, are * twi + aim * twr], axis=0).astype(BF16)
        x = jnp.dot(fwd_ref[...], bst, preferred_element_type=F32)
        kf_ref[0, kk] = x[:HY_R] * scale
        kf_ref[1, kk] = x[HY_R:] * scale
        return carry

    lax.fori_loop(0, HY_KB, body, 0, unroll=HY_UNROLL)


def _hy_spectrum(a5, tw, fwd_bf, ss):
    nch = a5.shape[-1]
    return pl.pallas_call(
        _hy_spectrum_kernel,
        grid=(HY_R // HY_KB,),
        in_specs=[pl.BlockSpec((1, 2, HY_KB, HY_R, nch), lambda kb: (0, 0, kb, 0, 0)),
                  pl.BlockSpec((2, HY_R, HY_R), lambda kb: (0, 0, 0)),
                  pl.BlockSpec((2 * HY_R, 2 * HY_R), lambda kb: (0, 0)),
                  pl.BlockSpec((1, nch), lambda kb: (0, 0))],
        out_specs=pl.BlockSpec((2, HY_KB, HY_R, nch), lambda kb: (0, kb, 0, 0)),
        out_shape=jax.ShapeDtypeStruct((2, HY_R, HY_R, nch), F32),
        compiler_params=_cparams(("arbitrary",)),
        name="hyena_filter_spectrum",
    )(a5, tw, fwd_bf, ss)


def _hy_mid_kernel(a_ref, kf_ref, tw_ref, fwd_ref, inv_ref, z_ref):
    kb = pl.program_id(0)

    def body(kk, carry):
        twr, twi = _twiddle_cols(tw_ref, kb * HY_KB + kk)
        are = a_ref[0, 0, kk].astype(F32)
        aim = a_ref[0, 1, kk].astype(F32)
        bst = jnp.concatenate([are * twr - aim * twi, are * twi + aim * twr], axis=0).astype(BF16)
        x = jnp.dot(fwd_ref[...], bst, preferred_element_type=F32)
        xre, xim = x[:HY_R], x[HY_R:]
        kre, kim = kf_ref[0, kk], kf_ref[1, kk]
        yst = jnp.concatenate([xre * kre - xim * kim, xre * kim + xim * kre], axis=0).astype(BF16)
        zz = jnp.dot(inv_ref[...], yst, preferred_element_type=F32)
        zre, zim = zz[:HY_R], zz[HY_R:]
        z_ref[0, 0, kk] = (zre * twr + zim * twi).astype(BF16)
        z_ref[0, 1, kk] = (zim * twr - zre * twi).astype(BF16)
        return carry

    lax.fori_loop(0, HY_KB, body, 0, unroll=HY_UNROLL)


def _hy_mid(a5, kf, order, tw, fwd_bf, inv_bf):
    b = a5.shape[0]
    blk = (1, 2, HY_KB, HY_R, HY_W)
    return pl.pallas_call(
        _hy_mid_kernel,
        grid=(HY_R // HY_KB, b),
        in_specs=[pl.BlockSpec(blk, lambda kb, bi: (bi, 0, kb, 0, 0)),
                  pl.BlockSpec((2, HY_KB, HY_R, HY_W), lambda kb, bi: (0, kb, 0, order)),
                  pl.BlockSpec((2, HY_R, HY_R), lambda kb, bi: (0, 0, 0)),
                  pl.BlockSpec((2 * HY_R, 2 * HY_R), lambda kb, bi: (0, 0)),
                  pl.BlockSpec((2 * HY_R, 2 * HY_R), lambda kb, bi: (0, 0))],
        out_specs=pl.BlockSpec(blk, lambda kb, bi: (bi, 0, kb, 0, 0)),
        out_shape=jax.ShapeDtypeStruct(a5.shape, BF16),
        compiler_params=_cparams(("arbitrary", "arbitrary")),
        name="hyena_spectral_product",
    )(a5, kf, tw, fwd_bf, inv_bf)


def _hy_out_kernel(f_ref, z_ref, g_ref, w_ref, sk_ref, o_ref):
    y = jnp.dot(f_ref[...], z_ref[0], preferred_element_type=F32)
    w = w_ref[0].astype(F32)
    o_ref[0] = (g_ref[0].astype(F32) * (y + w * sk_ref[...])).astype(BF16)


def _hy_out(f_bf, z3, gate3, w3, skip_row):
    b, m2, c = z3.shape
    m = f_bf.shape[0]
    dspec = pl.BlockSpec((1, m, HY_COLS), lambda bi, j: (bi, 0, j))
    return pl.pallas_call(
        _hy_out_kernel,
        grid=(b, c // HY_COLS),
        in_specs=[pl.BlockSpec((m, m2), lambda bi, j: (0, 0)),
                  pl.BlockSpec((1, m2, HY_COLS), lambda bi, j: (bi, 0, j)),
                  dspec, dspec,
                  pl.BlockSpec((1, HY_COLS), lambda bi, j: (0, 0))],
        out_specs=dspec,
        out_shape=jax.ShapeDtypeStruct((b, m, c), BF16),
        compiler_params=_cparams(("arbitrary", "arbitrary")),
        name="hyena_inverse_gate",
    )(f_bf, z3, gate3, w3, skip_row)


def _hyena_long(u, conv_w, conv_b, skip, filt_args):
    b, l, _ = u.shape
    assert 2 * l == HY_R * HY_R
    fre, fim, tw, fwd, inv = _dft_consts(HY_R)
    half = HY_R // 2
    as_bf = lambda m: jnp.asarray(m, F32).astype(BF16)
    f_a_data = as_bf(np.concatenate([fre[:, :half], fim[:, :half]], axis=0))
    f_a_filt = as_bf(np.concatenate([fre, fim], axis=0))
    f_c = as_bf(np.concatenate([fre[:half], fim[:half]], axis=1))
    tw_j = jnp.asarray(tw, F32)
    fwd_bf = as_bf(fwd)
    inv_bf = as_bf(inv)

    taps, ss = _hyena_filter_taps(l, *filt_args)
    nch = HY_ORDER * HY_W
    ak = _left_matmul(f_a_filt, taps.reshape(1, HY_R, HY_R * nch), "hyena_filter_dft_a")
    kf = _hy_spectrum(ak.reshape(1, 2, HY_R, HY_R, nch), tw_j, fwd_bf, ss)

    x1, x2, v = _hyena_prep(u, conv_w, conv_b)
    z = v
    for o, gate in enumerate((x1, x2)):
        z3 = z.reshape(b, half, HY_R * HY_W)
        a = _left_matmul(f_a_data, z3, "hyena_dft_a")
        zs = _hy_mid(a.reshape(b, 2, HY_R, HY_R, HY_W), kf, o, tw_j, fwd_bf, inv_bf)
        skip_row = jnp.tile(skip[o], HY_COLS // HY_W).reshape(1, HY_COLS)
        z = _hy_out(f_c, zs.reshape(b, 2 * HY_R, HY_R * HY_W), gate.reshape(b, half, HY_R * HY_W), z3,
                    skip_row).reshape(b, l, HY_W)
    return z


def _hy_ctx_kernel(x1_ref, x2_ref, v_ref, taps_ref, ss_ref, fa_ref, fc_ref, sk_ref, o_ref, *, seq_len):
    n2 = 2 * seq_len
    fa = fa_ref[...]
    scale = lax.rsqrt(ss_ref[...] + EPS) * (1.0 / n2)
    kf = jnp.dot(fa, taps_ref[...].astype(BF16), preferred_element_type=F32) * scale
    z = v_ref[0].astype(F32)
    for o, g_ref in enumerate((x1_ref, x2_ref)):
        x = jnp.dot(fa[:, :seq_len], z.astype(BF16), preferred_element_type=F32)
        xre, xim = x[:n2], x[n2:]
        kre = kf[:n2, o * HY_W:(o + 1) * HY_W]
        kim = kf[n2:, o * HY_W:(o + 1) * HY_W]
        yst = jnp.concatenate([xre * kre - xim * kim, xre * kim + xim * kre], axis=0).astype(BF16)
        y = jnp.dot(fc_ref[...], yst, preferred_element_type=F32)
        z = g_ref[0].astype(F32) * (y + z * sk_ref[pl.ds(o, 1), :])
    o_ref[0] = z.astype(BF16)


def _hyena_short(u, conv_w, conv_b, skip, filt_args):
    b, l, _ = u.shape
    n2 = 2 * l
    k = np.arange(n2)
    ang = 2.0 * np.pi * np.outer(k, k) / n2
    fre, fim = np.cos(ang), -np.sin(ang)
    fa = jnp.asarray(np.concatenate([fre, fim], axis=0), F32).astype(BF16)
    fc = jnp.asarray(np.concatenate([fre[:l], fim[:l]], axis=1), F32).astype(BF16)
    taps, ss = _hyena_filter_taps(l, *filt_args)
    x1, x2, v = _hyena_prep(u, conv_w, conv_b)
    nch = HY_ORDER * HY_W
    dspec = pl.BlockSpec((1, l, HY_W), lambda bi: (bi, 0, 0))
    const = lambda bi: (0, 0)
    return pl.pallas_call(
        functools.partial(_hy_ctx_kernel, seq_len=l),
        grid=(b,),
        in_specs=[dspec, dspec, dspec,
                  pl.BlockSpec((n2, nch), const), pl.BlockSpec((1, nch), const),
                  pl.BlockSpec((2 * n2, n2), const), pl.BlockSpec((l, 2 * n2), const),
                  pl.BlockSpec((HY_ORDER, HY_W), const)],
        out_specs=dspec,
        out_shape=jax.ShapeDtypeStruct((b, l, HY_W), BF16),
        compiler_params=_cparams(("arbitrary",)),
        name="hyena_context",
    )(x1, x2, v, taps, ss, fa, fc, skip)


def _rope_tables(n):
    t = jnp.arange(n, dtype=jnp.int32)
    pos = jnp.stack([t // GRID_W, t % GRID_W], axis=-1).astype(F32)
    nf = HEAD_DIM // 4
    inv = ROPE_THETA ** (-jnp.arange(nf, dtype=F32) / nf)
    ang = pos[:, :, None] * inv
    cos, sin = jnp.cos(ang), jnp.sin(ang)
    c64 = jnp.concatenate([cos[:, 0], cos[:, 0], cos[:, 1], cos[:, 1]], axis=-1)
    s64 = jnp.concatenate([-sin[:, 0], sin[:, 0], -sin[:, 1], sin[:, 1]], axis=-1)
    return jnp.tile(c64, (1, 2)), jnp.tile(s64, (1, 2))


def _block_diag(pool_w):
    g, c, _ = pool_w.shape
    out = jnp.zeros((g * c, g * c), pool_w.dtype)
    for i in range(g):
        out = out.at[i * c:(i + 1) * c, i * c:(i + 1) * c].set(pool_w[i])
    return out


def kernel(x, c, ctx, c_ctx, w_mod, b_mod, norm1_w, norm2_w, w_in, w_out, q_norm_w, k_norm_w, na_rpb,
           hy_conv_w, hy_conv_b, hy_w1, hy_b1, hy_w2, hy_b2, hy_w3, hy_freq, hy_skip,
           pool_w, pool_scale, w_router, w_gate, w_up, w_down):
    b, n, d = x.shape
    lc = ctx.shape[1]
    depth = w_mod.shape[0]
    rows = n // GRID_W
    cos128, sin128 = _rope_tables(n)
    seg = jnp.asarray(np.kron(np.eye(2), np.full((HEAD_DIM, HEAD_DIM), 1.0 / HEAD_DIM)), BF16)
    cc = jnp.zeros((8, d), F32).at[:b].set(c).at[b].set(c_ctx)

    xl = x.reshape(b * n, d)
    xc = ctx.reshape(b * lc, d)
    for i in range(depth):
        last = i == depth - 1
        mod = _modulation(cc, w_mod, b_mod[i], i)
        ml = mod[:b].reshape(b, 1, 6, d)
        mc = mod[b].reshape(1, 1, 6, d)
        sh1, sc1, g1, sh2, sc2, g2 = [ml[:, :, j] for j in range(6)]
        csh1, csc1, cg1, csh2, csc2, cg2 = [mc[:, :, j] for j in range(6)]
        w_in_bf = w_in[i].astype(BF16)
        w_out_bf = w_out[i].astype(BF16)
        w_router_t = w_router[i].T
        pool_bd = _block_diag(pool_w[i]).astype(BF16)
        hy_args = (hy_w1[i], hy_b1[i], hy_w2[i], hy_b2[i], hy_w3[i], hy_freq[i])

        qc, kc, vc, hyc, poolc = _in_projection(
            xc, csh1, csc1, norm1_w[i], w_in_bf, q_norm_w[i], k_norm_w[i], cos128, sin128, seg,
            rows_per_mod=b * lc, seq_len=lc, tm=lc, rope=False)
        kc = kc.reshape(b, lc, NA_W)
        vc = vc.reshape(b, lc, NA_W)
        if not last:
            na_c = _dense_attention(qc.reshape(b, lc, NA_W), kc, vc).reshape(b * lc, NA_W)
            hy_c = _hyena_short(hyc.reshape(b, lc, 3 * HY_W), hy_conv_w[i], hy_conv_b[i], hy_skip[i], hy_args)
            pool_c = _pool_mixer(poolc.reshape(b, lc, POOL_W), pool_bd, pool_scale[i])
            xc1, h2c, affc = _out_projection(
                na_c, hy_c.reshape(b * lc, HY_W).astype(BF16), pool_c.reshape(b * lc, POOL_W), w_out_bf,
                xc, cg1, norm2_w[i], csh2, csc2, w_router_t, rows_per_mod=b * lc, seq_len=lc, tm=lc)

        qr, qp, kl, vl, hyl, pooll = _in_projection(
            xl, sh1, sc1, norm1_w[i], w_in_bf, q_norm_w[i], k_norm_w[i], cos128, sin128, seg,
            rows_per_mod=n, seq_len=n, tm=512, rope=True)
        bias = _na_bias_tables(na_rpb[i], rows)
        o_na = _neighborhood_attention(qr.reshape(b, n, NA_W), qp.reshape(b, n, NA_W), kl.reshape(b, n, NA_W),
                                       vl.reshape(b, n, NA_W), kc, vc, bias)
        hy_l = _hyena_long(hyl.reshape(b, n, 3 * HY_W), hy_conv_w[i], hy_conv_b[i], hy_skip[i], hy_args)
        pool_l = _pool_mixer(pooll.reshape(b, n, POOL_W), pool_bd, pool_scale[i])
        xl1, h2l, affl = _out_projection(
            o_na.reshape(b * n, NA_W), hy_l.reshape(b * n, HY_W).astype(BF16), pool_l.reshape(b * n, POOL_W),
            w_out_bf, xl, g1, norm2_w[i], sh2, sc2, w_router_t, rows_per_mod=n, seq_len=n, tm=512)
        streams = [(h2l, affl, xl1, g2)]
        if not last:
            streams.append((h2c, affc, xc1, cg2))
        outs = _expert_choice_moe(streams, w_gate, w_up, w_down, i)
        xl = outs[0]
        if not last:
            xc = outs[1]
    return xl.reshape(b, n, d)
```

```python
import functools
import math

import jax
import jax.numpy as jnp
import numpy as np
from jax import lax
from jax.experimental import pallas as pl
from jax.experimental.pallas import tpu as pltpu

F32 = jnp.float32
BF16 = jnp.bfloat16

D_MODEL = 1024
GRID_W = 64
NA_HEADS = 8
HEAD_DIM = 64
NA_W = NA_HEADS * HEAD_DIM
NA_WIN_R = 8
NA_WIN_C = 16
ROPE_THETA = 10000.0
HY_W = 256
HY_ORDER = 2
HY_SHORT = 3
HY_BANDS = 16
HY_DECAY_TARGET = 1e-2
HY_FAST_DECAY = 0.3
HY_SLOW_DECAY = 1.5
POOL_W = 256
POOL_SIZES = (2, 4, 8, 16)
POOL_GROUP = POOL_W // len(POOL_SIZES)
MIX_W = NA_W + HY_W + POOL_W
IN_W = 3 * NA_W + (HY_ORDER + 1) * HY_W + POOL_W
N_EXPERTS = 16
EC_FACTOR = 2
D_EXPERT = 2048
EPS = 1e-6

LANES = 128
TOKEN_SUBLANES = D_MODEL // 128
ROW_SUBTILES = 2
NEG_BIG = -1e30
VMEM_LIMIT = 56 * 1024 * 1024

NA_Q_ROWS = 4
NA_K_ROWS = NA_Q_ROWS + NA_WIN_R - 1
NA_TILES_PER_STEP = 4


def _cparams(sem):
    return pltpu.CompilerParams(dimension_semantics=sem, vmem_limit_bytes=VMEM_LIMIT)


def _mod_kernel(c_ref, w_ref, b_ref, o_ref):
    c = c_ref[...]
    s = c * jax.nn.sigmoid(c)
    o_ref[...] = jnp.dot(s, w_ref[0], preferred_element_type=F32,
                         precision=lax.Precision.HIGHEST) + b_ref[...]


def _modulation(cc, w_mod, b_mod, layer):
    rows, d = cc.shape
    n = w_mod.shape[2]
    tn = 1536
    return pl.pallas_call(
        _mod_kernel,
        grid=(n // tn,),
        in_specs=[pl.BlockSpec((rows, d), lambda j: (0, 0)),
                  pl.BlockSpec((1, d, tn), lambda j: (layer, 0, j)),
                  pl.BlockSpec((1, tn), lambda j: (0, j))],
        out_specs=pl.BlockSpec((rows, tn), lambda j: (0, j)),
        out_shape=jax.ShapeDtypeStruct((rows, n), F32),
        compiler_params=_cparams(("arbitrary",)),
        name="modulation",
    )(cc, w_mod, b_mod.reshape(1, n))


def _head_norm(xs, w128, seg):
    m = jnp.dot((xs * xs).astype(BF16), seg, preferred_element_type=F32)
    return xs * lax.rsqrt(m + EPS) * w128


def _rope(xs, c, s_signed):
    lane = lax.broadcasted_iota(jnp.int32, xs.shape, 1)
    partner = jnp.where((lane & 16) == 0, pltpu.roll(xs, LANES - 16, 1), pltpu.roll(xs, 16, 1))
    return xs * c + partner * s_signed


def _inproj_kernel(x_ref, sh_ref, sc_ref, nw_ref, w_ref, qw_ref, kw_ref, cos_ref, sin_ref, seg_ref,
                   *out_refs, rope):
    if rope:
        qr_ref, qp_ref, k_ref, v_ref, hy_ref, pool_ref = out_refs
    else:
        qp_ref, k_ref, v_ref, hy_ref, pool_ref = out_refs
    seg = seg_ref[...]
    qk_scale = HEAD_DIM ** -0.5
    sub = x_ref.shape[0] // ROW_SUBTILES
    for t in range(ROW_SUBTILES):
        rs = pl.ds(t * sub, sub)
        x = x_ref[rs, :]
        ms = jnp.mean(x * x, axis=-1, keepdims=True)
        h = x * lax.rsqrt(ms + EPS) * nw_ref[...]
        h = h * (1.0 + sc_ref[0]) + sh_ref[0]
        u = jnp.dot(h.astype(BF16), w_ref[...], preferred_element_type=F32)
        for ch in range(NA_W // LANES):
            sl = slice(ch * LANES, (ch + 1) * LANES)
            qn = _head_norm(u[:, sl], qw_ref[...], seg)
            kn = _head_norm(u[:, NA_W + ch * LANES:NA_W + (ch + 1) * LANES], kw_ref[...], seg)
            qp_ref[rs, sl] = (qn * qk_scale).astype(BF16)
            if rope:
                c = cos_ref[rs, :]
                s = sin_ref[rs, :]
                qr_ref[rs, sl] = (_rope(qn, c, s) * qk_scale).astype(BF16)
                k_ref[rs, sl] = _rope(kn, c, s).astype(BF16)
            else:
                k_ref[rs, sl] = kn.astype(BF16)
        v_ref[rs, :] = u[:, 2 * NA_W:3 * NA_W].astype(BF16)
        hy_ref[rs, :] = u[:, 3 * NA_W:3 * NA_W + 3 * HY_W]
        pool_ref[rs, :] = u[:, 3 * NA_W + 3 * HY_W:]


def _in_projection(x2, shift, scale, norm_w, w_in_bf, q_norm_w, k_norm_w, cos128, sin128, seg, *,
                   rows_per_mod, seq_len, tm, rope):
    m, d = x2.shape
    tiles_per_mod = rows_per_mod // tm
    tiles_per_seq = seq_len // tm
    row = lambda i: (i, 0)
    const = lambda i: (0, 0)
    modi = lambda i: (i // tiles_per_mod, 0, 0)
    pos = lambda i: (i % tiles_per_seq, 0)
    outs = []
    if rope:
        outs.append((NA_W, BF16))
    outs += [(NA_W, BF16), (NA_W, BF16), (NA_W, BF16), (3 * HY_W, F32), (POOL_W, F32)]
    return pl.pallas_call(
        functools.partial(_inproj_kernel, rope=rope),
        grid=(m // tm,),
        in_specs=[pl.BlockSpec((tm, d), row),
                  pl.BlockSpec((1, 1, d), modi), pl.BlockSpec((1, 1, d), modi),
                  pl.BlockSpec((1, d), const),
                  pl.BlockSpec((d, IN_W), const),
                  pl.BlockSpec((1, LANES), const), pl.BlockSpec((1, LANES), const),
                  pl.BlockSpec((tm, LANES), pos), pl.BlockSpec((tm, LANES), pos),
                  pl.BlockSpec((LANES, LANES), const)],
        out_specs=[pl.BlockSpec((tm, w), row) for w, _ in outs],
        out_shape=[jax.ShapeDtypeStruct((m, w), dt) for w, dt in outs],
        compiler_params=_cparams(("arbitrary",)),
        name="in_projection_rope" if rope else "in_projection_ctx",
    )(x2, shift, scale, norm_w.reshape(1, d), w_in_bf,
      jnp.tile(q_norm_w, 2).reshape(1, LANES), jnp.tile(k_norm_w, 2).reshape(1, LANES),
      cos128, sin128, seg)


def _softmax_pv(s_list, v_list):
    mx = functools.reduce(jnp.maximum, [jnp.max(s, axis=-1, keepdims=True) for s in s_list])
    ps = [jnp.exp(s - mx) for s in s_list]
    l = functools.reduce(jnp.add, [jnp.sum(p, axis=-1, keepdims=True) for p in ps])
    o = functools.reduce(jnp.add, [jnp.dot(p.astype(BF16), v, preferred_element_type=F32)
                                   for p, v in zip(ps, v_list)])
    return o / l


_NT = (((1,), (1,)), ((), ()))


def _na_kernel(qr_ref, qp_ref, k_ref, v_ref, kc_ref, vc_ref, bias_ref, o_ref, *, rows):
    n_tiles = rows // NA_Q_ROWS
    tq = NA_Q_ROWS * GRID_W
    kc = kc_ref[0]
    vc = vc_ref[0]
    lane = lax.broadcasted_iota(jnp.int32, (tq, LANES), 1)
    for sub in range(NA_TILES_PER_STEP):
        t = pl.program_id(2) * NA_TILES_PER_STEP + sub
        ks = jnp.clip(t * NA_Q_ROWS - NA_WIN_R // 2, 0, rows - NA_K_ROWS)
        start = pl.multiple_of(ks * GRID_W, GRID_W)
        k = k_ref[0, pl.ds(start, NA_K_ROWS * GRID_W), :]
        v = v_ref[0, pl.ds(start, NA_K_ROWS * GRID_W), :]
        q = qr_ref[0, pl.ds(sub * tq, tq), :]
        qp = qp_ref[0, pl.ds(sub * tq, tq), :]
        kind = jnp.where(t == 0, 0, jnp.where(t == n_tiles - 1, 2, 1))
        outs = []
        for hh in range(2):
            keep = (lane < HEAD_DIM) if hh == 0 else (lane >= HEAD_DIM)
            zero = jnp.zeros_like(q)
            s_lat = (lax.dot_general(jnp.where(keep, q, zero), k, _NT, preferred_element_type=F32)
                     + bias_ref[kind, hh])
            s_ctx = lax.dot_general(jnp.where(keep, qp, zero), kc, _NT, preferred_element_type=F32)
            outs.append(_softmax_pv([s_lat, s_ctx], [v, vc]))
        o_ref[0, pl.ds(sub * tq, tq), :] = jnp.where(lane < HEAD_DIM, outs[0], outs[1]).astype(BF16)


def _neighborhood_attention(qr, qp, k, v, kc, vc, bias):
    b, n, _ = qr.shape
    lc = kc.shape[1]
    rows = n // GRID_W
    n_steps = rows // (NA_Q_ROWS * NA_TILES_PER_STEP)
    tq = NA_Q_ROWS * GRID_W * NA_TILES_PER_STEP
    tk = NA_K_ROWS * GRID_W
    qmap = lambda bi, hp, t: (bi, t, hp)
    smap = lambda bi, hp, t: (bi, 0, hp)
    bmap = lambda bi, hp, t: (0, hp, 0, 0)
    return pl.pallas_call(
        functools.partial(_na_kernel, rows=rows),
        grid=(b, NA_W // LANES, n_steps),
        in_specs=[pl.BlockSpec((1, tq, LANES), qmap), pl.BlockSpec((1, tq, LANES), qmap),
                  pl.BlockSpec((1, n, LANES), smap), pl.BlockSpec((1, n, LANES), smap),
                  pl.BlockSpec((1, lc, LANES), smap), pl.BlockSpec((1, lc, LANES), smap),
                  pl.BlockSpec((3, 2, NA_Q_ROWS * GRID_W, tk), bmap)],
        out_specs=pl.BlockSpec((1, tq, LANES), qmap),
        out_shape=jax.ShapeDtypeStruct((b, n, NA_W), BF16),
        compiler_params=_cparams(("arbitrary", "arbitrary", "arbitrary")),
        name="neighborhood_attention",
    )(qr, qp, k, v, kc, vc, bias)


def _na_bias_tables(rpb, rows):
    n_tiles = rows // NA_Q_ROWS
    n_dr, n_dc = 2 * NA_WIN_R - 1, 2 * NA_WIN_C - 1
    qc = np.arange(GRID_W)
    cs = np.clip(qc - NA_WIN_C // 2, 0, GRID_W - NA_WIN_C)
    kcol = np.arange(GRID_W)
    ok_c = (kcol[None, :] >= cs[:, None]) & (kcol[None, :] < cs[:, None] + NA_WIN_C)
    dc = kcol[None, :] - qc[:, None] + NA_WIN_C - 1
    col_sel = np.concatenate([(dc[:, :, None] == np.arange(n_dc)) & ok_c[:, :, None],
                              ~ok_c[:, :, None]], axis=2)
    ext = jnp.pad(rpb.astype(F32), ((0, 0), (0, 1), (0, 1)), constant_values=NEG_BIG)
    blocks = jnp.einsum("hrd,qkd->hrqk", ext, jnp.asarray(col_sel, F32), precision=lax.Precision.HIGHEST)
    tabs = []
    for t in (0, 1, n_tiles - 1):
        r0 = t * NA_Q_ROWS
        ks = int(np.clip(r0 - NA_WIN_R // 2, 0, rows - NA_K_ROWS))
        r = r0 + np.arange(NA_Q_ROWS)
        rs = np.clip(r - NA_WIN_R // 2, 0, rows - NA_WIN_R)
        krow = ks + np.arange(NA_K_ROWS)
        ok_r = (krow[None, :] >= rs[:, None]) & (krow[None, :] < rs[:, None] + NA_WIN_R)
        dr = np.where(ok_r, krow[None, :] - r[:, None] + NA_WIN_R - 1, n_dr)
        tabs.append(jnp.concatenate(
            [jnp.concatenate([blocks[:, int(dr[i, j])] for j in range(NA_K_ROWS)], axis=-1)
             for i in range(NA_Q_ROWS)], axis=-2))
    return jnp.stack(tabs)


def _dense_attn_kernel(q_ref, k_ref, v_ref, o_ref):
    q = q_ref[0]
    k = k_ref[0]
    v = v_ref[0]
    lane = lax.broadcasted_iota(jnp.int32, q.shape, 1)
    outs = []
    for hh in range(2):
        keep = (lane < HEAD_DIM) if hh == 0 else (lane >= HEAD_DIM)
        s = lax.dot_general(jnp.where(keep, q, jnp.zeros_like(q)), k, _NT, preferred_element_type=F32)
        outs.append(_softmax_pv([s], [v]))
    o_ref[0] = jnp.where(lane < HEAD_DIM, outs[0], outs[1]).astype(BF16)


def _dense_attention(q, k, v):
    b, l, _ = q.shape
    spec = pl.BlockSpec((1, l, LANES), lambda bi, hp: (bi, 0, hp))
    return pl.pallas_call(
        _dense_attn_kernel,
        grid=(b, NA_W // LANES),
        in_specs=[spec, spec, spec],
        out_specs=spec,
        out_shape=jax.ShapeDtypeStruct((b, l, NA_W), BF16),
        compiler_params=_cparams(("arbitrary", "arbitrary")),
        name="context_attention",
    )(q, k, v)


POOL_HALO = max(POOL_SIZES) // 2


def _pool_kernel(x_ref, w_ref, sc_ref, o_ref, xpad, *, seq_len, chunk):
    zeros = jnp.zeros((POOL_HALO, POOL_W), F32)
    xpad[pl.ds(0, POOL_HALO), :] = zeros
    xpad[pl.ds(seq_len + POOL_HALO, POOL_HALO), :] = zeros
    xpad[pl.ds(POOL_HALO, seq_len), :] = x_ref[0]
    span = chunk + 2 * POOL_HALO
    w = w_ref[...]
    sc = sc_ref[...]

    def body(c, carry):
        base = pl.multiple_of(c * chunk, chunk)
        xs = xpad[pl.ds(base, span), :]
        s2 = xs + pltpu.roll(xs, 1, 0)
        s4 = pltpu.roll(s2, 1, 0) + pltpu.roll(s2, span - 1, 0)
        s8 = pltpu.roll(s4, 2, 0) + pltpu.roll(s4, span - 2, 0)
        s16 = pltpu.roll(s8, 4, 0) + pltpu.roll(s8, span - 4, 0)
        mid = slice(POOL_HALO, POOL_HALO + chunk)
        grp = lax.broadcasted_iota(jnp.int32, (chunk, POOL_W), 1) // POOL_GROUP
        tpos = base + lax.broadcasted_iota(jnp.int32, (chunk, POOL_W), 0)
        half = jnp.left_shift(1, grp)
        cnt = (jnp.minimum(tpos + half, seq_len) - jnp.maximum(tpos - half, 0)).astype(F32)
        ssum = jnp.where(grp == 0, s2[mid], jnp.where(grp == 1, s4[mid], jnp.where(grp == 2, s8[mid], s16[mid])))
        diff = ssum / cnt - xs[mid]
        y = jnp.dot(diff.astype(BF16), w, preferred_element_type=F32) * sc
        o_ref[0, pl.ds(base, chunk), :] = y.astype(BF16)
        return carry

    lax.fori_loop(0, seq_len // chunk, body, 0)


def _pool_mixer(u, w_blockdiag_bf, pool_scale):
    b, l, _ = u.shape
    chunk = min(l, 512)
    return pl.pallas_call(
        functools.partial(_pool_kernel, seq_len=l, chunk=chunk),
        grid=(b,),
        in_specs=[pl.BlockSpec((1, l, POOL_W), lambda bi: (bi, 0, 0)),
                  pl.BlockSpec((POOL_W, POOL_W), lambda bi: (0, 0)),
                  pl.BlockSpec((1, POOL_W), lambda bi: (0, 0))],
        out_specs=pl.BlockSpec((1, l, POOL_W), lambda bi: (bi, 0, 0)),
        out_shape=jax.ShapeDtypeStruct((b, l, POOL_W), BF16),
        scratch_shapes=[pltpu.VMEM((l + 2 * POOL_HALO, POOL_W), F32)],
        compiler_params=_cparams(("arbitrary",)),
        name="pool_mixer",
    )(u, w_blockdiag_bf, pool_scale.reshape(1, POOL_W))


def _outproj_kernel(na_ref, hy_ref, pl_ref, w_ref, x_ref, g1_ref, nw_ref, sh_ref, sc_ref, wr_ref,
                    xn_ref, h2_ref, aff_ref):
    w = w_ref
    acc = jnp.dot(na_ref[...], w[pl.ds(0, NA_W), :], preferred_element_type=F32)
    acc += jnp.dot(hy_ref[...], w[pl.ds(NA_W, HY_W), :], preferred_element_type=F32)
    acc += jnp.dot(pl_ref[...], w[pl.ds(NA_W + HY_W, POOL_W), :], preferred_element_type=F32)
    xn = x_ref[...] + g1_ref[0] * acc
    xn_ref[...] = xn
    ms = jnp.mean(xn * xn, axis=-1, keepdims=True)
    h2 = xn * lax.rsqrt(ms + EPS) * nw_ref[...]
    h2 = h2 * (1.0 + sc_ref[0]) + sh_ref[0]
    tm = h2.shape[0]
    for s in range(TOKEN_SUBLANES):
        h2_ref[pl.ds(s, tm, stride=TOKEN_SUBLANES), :] = h2[:, s * LANES:(s + 1) * LANES]
    logits = lax.dot_general(wr_ref[...], h2, _NT, preferred_element_type=F32,
                             precision=lax.Precision.HIGHEST)
    mx = jnp.max(logits, axis=0, keepdims=True)
    p = jnp.exp(logits - mx)
    aff_ref[0] = p / jnp.sum(p, axis=0, keepdims=True)


def _out_projection(na, hy, pool, w_out_bf, x2, g1, norm_w, shift, scale, w_router_t, *,
                    rows_per_mod, seq_len, tm):
    m, d = x2.shape
    tiles_per_mod = rows_per_mod // tm
    tiles_per_seq = seq_len // tm
    row = lambda i: (i, 0)
    const = lambda i: (0, 0)
    modi = lambda i: (i // tiles_per_mod, 0, 0)
    return pl.pallas_call(
        _outproj_kernel,
        grid=(m // tm,),
        in_specs=[pl.BlockSpec((tm, NA_W), row), pl.BlockSpec((tm, HY_W), row), pl.BlockSpec((tm, POOL_W), row),
                  pl.BlockSpec((MIX_W, d), const),
                  pl.BlockSpec((tm, d), row),
                  pl.BlockSpec((1, 1, d), modi),
                  pl.BlockSpec((1, d), const),
                  pl.BlockSpec((1, 1, d), modi), pl.BlockSpec((1, 1, d), modi),
                  pl.BlockSpec((N_EXPERTS, d), const)],
        out_specs=[pl.BlockSpec((tm, d), row), pl.BlockSpec((tm * TOKEN_SUBLANES, LANES), row),
                   pl.BlockSpec((1, N_EXPERTS, tm), lambda i: (i // tiles_per_seq, 0, i % tiles_per_seq))],
        out_shape=[jax.ShapeDtypeStruct((m, d), F32), jax.ShapeDtypeStruct((m * TOKEN_SUBLANES, LANES), F32),
                   jax.ShapeDtypeStruct((m // seq_len, N_EXPERTS, seq_len), F32)],
        compiler_params=_cparams(("arbitrary",)),
        name="out_projection_router",
    )(na, hy, pool, w_out_bf, x2, g1, norm_w.reshape(1, d), shift, scale, w_router_t)


FFN_TF = 256
FFN_ROW_CHUNKS = 2


def _ffn_kernel(*refs, n_experts, sources, n_rows):
    ns = len(sources)
    idx_cur = refs[0:ns]
    idx_nxt = refs[ns:2 * ns]
    src = refs[2 * ns:3 * ns]
    gate_ref, wg_ref, wu_ref, wd_ref, o_ref, stage, xb, acc, sem = refs[3 * ns:]
    e = pl.program_id(0)
    f = pl.program_id(1)
    n_f = pl.num_programs(1)

    ts = TOKEN_SUBLANES

    def row_copy(s, row, r):
        return pltpu.make_async_copy(src[s].at[pl.ds(pl.multiple_of(row * ts, ts), ts)],
                                     stage.at[pl.ds(pl.multiple_of((sources[s][0] + r) * ts, ts), ts)], sem.at[0])

    def gather(idx_refs):
        for s in range(ns):
            def body(r, carry, s=s):
                row_copy(s, idx_refs[s][0, 0, r], r).start()
                return carry
            lax.fori_loop(0, sources[s][1], body, 0, unroll=8)

    def gather_wait():
        for s in range(ns):
            off, cnt = sources[s]
            pltpu.make_async_copy(src[s].at[pl.ds(0, cnt * ts)], stage.at[pl.ds(off * ts, cnt * ts)],
                                  sem.at[0]).wait()

    @pl.when(f == 0)
    def _():
        @pl.when(e == 0)
        def _():
            gather(idx_cur)

        acc[...] = jnp.zeros_like(acc)
        gather_wait()
        for s in range(ts):
            xb[:, s * LANES:(s + 1) * LANES] = stage[pl.ds(s, n_rows, stride=ts), :].astype(BF16)

        @pl.when(e + 1 < n_experts)
        def _():
            gather(idx_nxt)

    wg = wg_ref[0, 0].astype(BF16)
    wu = wu_ref[0, 0].astype(BF16)
    wd = wd_ref[0, 0].astype(BF16)
    rc = n_rows // FFN_ROW_CHUNKS
    for r in range(FFN_ROW_CHUNKS):
        rs = pl.ds(r * rc, rc)
        x = xb[rs, :]
        a = jnp.dot(x, wg, preferred_element_type=F32)
        b = jnp.dot(x, wu, preferred_element_type=F32)
        h = (a * jax.nn.sigmoid(a) * b).astype(BF16)
        y = jnp.dot(h, wd, preferred_element_type=F32)
        acc[rs, :] += y

    @pl.when(f == n_f - 1)
    def _():
        o_ref[0] = (acc[...] * gate_ref[0]).astype(BF16)


def _expert_ffn(rows_idx, tokens, gate, w_gate, w_up, w_down, layer):
    ns = len(tokens)
    e, d = w_gate.shape[1], w_gate.shape[2]
    counts = [int(ri.shape[2]) for ri in rows_idx]
    offs = [int(sum(counts[:s])) for s in range(ns)]
    r_total = sum(counts)
    assert r_total % (FFN_ROW_CHUNKS * 16) == 0
    smem = lambda cnt, nxt: pl.BlockSpec(
        (1, 1, cnt), (lambda ei, f: (jnp.minimum(ei + 1, e - 1), 0, 0)) if nxt else (lambda ei, f: (ei, 0, 0)),
        memory_space=pltpu.SMEM)
    in_specs = ([smem(cnt, False) for cnt in counts] + [smem(cnt, True) for cnt in counts]
                + [pl.BlockSpec(memory_space=pl.ANY) for _ in range(ns)]
                + [pl.BlockSpec((1, r_total, 1), lambda ei, f: (ei, 0, 0)),
                   pl.BlockSpec((1, 1, d, FFN_TF), lambda ei, f: (layer, ei, 0, f)),
                   pl.BlockSpec((1, 1, d, FFN_TF), lambda ei, f: (layer, ei, 0, f)),
                   pl.BlockSpec((1, 1, FFN_TF, d), lambda ei, f: (layer, ei, f, 0))])
    return pl.pallas_call(
        functools.partial(_ffn_kernel, n_experts=e, sources=tuple(zip(offs, counts)), n_rows=r_total),
        grid=(e, D_EXPERT // FFN_TF),
        in_specs=in_specs,
        out_specs=pl.BlockSpec((1, r_total, d), lambda ei, f: (ei, 0, 0)),
        out_shape=jax.ShapeDtypeStruct((e, r_total, d), BF16),
        scratch_shapes=[pltpu.VMEM((r_total * TOKEN_SUBLANES, LANES), F32), pltpu.VMEM((r_total, d), BF16),
                        pltpu.VMEM((r_total, d), F32), pltpu.SemaphoreType.DMA((1,))],
        compiler_params=_cparams(("arbitrary", "arbitrary")),
        name="expert_ffn",
    )(*rows_idx, *rows_idx, *tokens, gate, w_gate, w_up, w_down)


def _prefix_count(m, n):
    lane = lax.broadcasted_iota(jnp.int32, m.shape, 1)
    c = m
    s = 1
    while s < n:
        c = c + jnp.where(lane >= s, pltpu.roll(c, s, 1), 0)
        s *= 2
    return c


COMBINE_TB = 256
COMBINE_CHUNK = 128


def _topk_kernel(aff_ref, idx_ref, gate_ref, off_ref, *, cap):
    a = aff_ref[...]
    rows, n = a.shape

    def search(i, t):
        cand = t | jnp.left_shift(jnp.int32(1), 30 - i)
        cnt = jnp.sum((a >= pltpu.bitcast(cand, F32)).astype(jnp.int32), axis=1, keepdims=True)
        return jnp.where(cnt >= cap, cand, t)

    thr_bits = lax.fori_loop(0, 31, search, jnp.zeros((rows, 1), jnp.int32))
    thr = pltpu.bitcast(thr_bits, F32)
    gt = a >= pltpu.bitcast(thr_bits + 1, F32)
    eq = (a >= thr) & jnp.logical_not(gt)
    need = cap - jnp.sum(gt.astype(jnp.int32), axis=1, keepdims=True)
    eq_i = eq.astype(jnp.int32)
    sel = gt | (eq & (_prefix_count(eq_i, n) - eq_i < need))
    sel_i = sel.astype(jnp.int32)
    lane = lax.broadcasted_iota(jnp.int32, a.shape, 1)
    dist = jnp.where(sel, lane + 1 - _prefix_count(sel_i, n), 0)
    tok = jnp.where(sel, lane, -1)
    val = a
    s = 1
    while s < n:
        tok_in = pltpu.roll(tok, n - s, 1)
        dist_in = pltpu.roll(dist, n - s, 1)
        val_in = pltpu.roll(val, n - s, 1)
        take = (lane < n - s) & (tok_in >= 0) & ((dist_in & s) != 0)
        stay = (tok >= 0) & ((dist & s) == 0)
        tok = jnp.where(take, tok_in, jnp.where(stay, tok, -1))
        dist = jnp.where(take, dist_in, jnp.where(stay, dist, 0))
        val = jnp.where(take, val_in, val)
        s *= 2
    idx_ref[...] = tok[:, :cap]
    gate_ref[...] = val[:, :cap]
    olane = lax.broadcasted_iota(jnp.int32, off_ref.shape, 1)
    off = jnp.zeros(off_ref.shape, jnp.int32)
    for j in range(n // COMBINE_TB + 1):
        below = jnp.sum(jnp.where(lane < j * COMBINE_TB, sel_i, 0), axis=1, keepdims=True)
        off = jnp.where(olane == j, below, off)
    off_ref[...] = off


def _topk_select(aff, cap):
    b, e, n = aff.shape
    assert n & (n - 1) == 0 and n // COMBINE_TB < LANES
    rows = b * e
    full = lambda w: pl.BlockSpec((rows, w), lambda i: (0, 0))
    idx, gate, off = pl.pallas_call(
        functools.partial(_topk_kernel, cap=cap),
        grid=(1,),
        in_specs=[full(n)],
        out_specs=[full(cap), full(cap), full(LANES)],
        out_shape=[jax.ShapeDtypeStruct((rows, cap), jnp.int32), jax.ShapeDtypeStruct((rows, cap), F32),
                   jax.ShapeDtypeStruct((rows, LANES), jnp.int32)],
        compiler_params=_cparams(("arbitrary",)),
        name="expert_choice_topk",
    )(aff.reshape(rows, n))
    return idx.reshape(b, e, cap), gate.reshape(b, e, cap), off


def _combine_kernel(off_ref, y_ref, idx_ref, x_ref, g_ref, o_ref, acc, ycat, *, chunk, n_chunks):
    bi = pl.program_id(0)
    j = pl.program_id(1)
    tb = x_ref.shape[0]
    tok = j * tb + lax.broadcasted_iota(jnp.int32, (tb, chunk), 0)

    def onehot(e, st):
        return (idx_ref[0, pl.ds(e, 1), pl.ds(st, chunk)] == tok).astype(BF16)

    if n_chunks == 1:
        acc[...] = jnp.zeros_like(acc)
        for e in range(N_EXPERTS):
            acc[...] += jnp.dot(onehot(e, 0), y_ref[e], preferred_element_type=F32)
    else:
        cap = n_chunks * chunk
        shift = chunk.bit_length() - 1
        col = lax.broadcasted_iota(jnp.int32, (1, chunk), 1)

        def slot_ids(e, st):
            base = jnp.minimum(lax.shift_left(lax.shift_right_logical(st, shift), shift), cap - 2 * chunk)
            wide = idx_ref[0, pl.ds(e, 1), pl.ds(pl.multiple_of(base, chunk), 2 * chunk)]
            wide = jnp.broadcast_to(wide, (8, 2 * chunk))
            return pltpu.roll(wide, (2 * chunk - (st - base)) & (2 * chunk - 1), 1)[:1, :chunk]

        starts, ends, hots = [], [], []
        for e in range(N_EXPERTS):
            lo = off_ref[bi * N_EXPERTS + e, j]
            hi = off_ref[bi * N_EXPERTS + e, j + 1]
            st = jnp.minimum(lax.shift_left(lax.shift_right_logical(lo, 4), 4), cap - chunk)
            starts.append(st)
            ends.append(hi)
            ycat[pl.ds(e * chunk, chunk), :] = y_ref[e, pl.ds(pl.multiple_of(st, 16), chunk), :]
            hots.append((slot_ids(e, st) == tok).astype(BF16))
        acc[...] = jnp.dot(jnp.concatenate(hots, axis=1), ycat[...], preferred_element_type=F32)
        for e in range(N_EXPERTS):
            def add_window(c, carry, e=e):
                first = starts[e] + c * chunk
                st = jnp.minimum(first, cap - chunk)
                hot = ((slot_ids(e, st) == tok) & (st + col >= first)).astype(BF16)
                acc[...] += jnp.dot(hot, y_ref[e, pl.ds(pl.multiple_of(st, 16), chunk), :],
                                    preferred_element_type=F32)
                return carry

            n_more = lax.shift_right_logical(jnp.maximum(ends[e] - starts[e] - 1, 0), shift)
            lax.fori_loop(1, n_more + 1, add_window, 0)
    o_ref[...] = x_ref[...] + g_ref[0] * acc[...]


def _moe_combine(off, y, idx, x2, g, *, row_block0, cap, seq_len):
    bsz = idx.shape[0]
    d = x2.shape[1]
    tb = min(seq_len, COMBINE_TB)
    nb = seq_len // tb
    chunk = min(cap, COMBINE_CHUNK)
    n_chunks = cap // chunk
    n_mod = g.shape[0]
    grid_spec = pltpu.PrefetchScalarGridSpec(
        num_scalar_prefetch=1,
        grid=(bsz, nb),
        in_specs=[pl.BlockSpec((N_EXPERTS, cap, d), lambda bi, j, off: (0, row_block0 + bi, 0),
                               pipeline_mode=pl.Buffered(1)),
                  pl.BlockSpec((1, N_EXPERTS, cap), lambda bi, j, off: (bi, 0, 0)),
                  pl.BlockSpec((tb, d), lambda bi, j, off: (bi * nb + j, 0)),
                  pl.BlockSpec((1, 1, d), lambda bi, j, off: (bi if n_mod > 1 else 0, 0, 0))],
        out_specs=pl.BlockSpec((tb, d), lambda bi, j, off: (bi * nb + j, 0)),
        scratch_shapes=[pltpu.VMEM((tb, d), F32), pltpu.VMEM((N_EXPERTS * chunk, d), BF16)])
    return pl.pallas_call(
        functools.partial(_combine_kernel, chunk=chunk, n_chunks=n_chunks),
        grid_spec=grid_spec,
        out_shape=jax.ShapeDtypeStruct(x2.shape, F32),
        compiler_params=_cparams(("arbitrary", "arbitrary")),
        name="moe_combine",
    )(off, y, idx, x2, g)


def _expert_choice_moe(streams, w_gate, w_up, w_down, layer):
    routed = []
    for h2, aff, x2, g in streams:
        bsz, _, l = aff.shape
        cap = EC_FACTOR * l // N_EXPERTS
        idx, gate, off = _topk_select(aff, cap)
        rows = (idx + (jnp.arange(bsz, dtype=jnp.int32) * l)[:, None, None]).transpose(1, 0, 2)
        routed.append((idx, off, rows.reshape(N_EXPERTS, 1, bsz * cap), gate.transpose(1, 0, 2).reshape(
            N_EXPERTS, bsz * cap, 1), cap))
    y = _expert_ffn([r[2] for r in routed], [s[0] for s in streams],
                    jnp.concatenate([r[3] for r in routed], axis=1), w_gate, w_up, w_down, layer)
    outs = []
    row0 = 0
    for (h2, aff, x2, g), (idx, off, rows, _, cap) in zip(streams, routed):
        assert row0 % cap == 0
        outs.append(_moe_combine(off, y, idx, x2, g, row_block0=row0 // cap, cap=cap, seq_len=aff.shape[2]))
        row0 += rows.shape[2]
    return outs


HY_R = 128
HY_COLS = 4096
HY_KB = 8
HY_UNROLL = 8


def _dft_consts(r):
    k = np.arange(r)
    ang = 2.0 * np.pi * np.outer(k, k) / r
    fre, fim = np.cos(ang), -np.sin(ang)
    k2 = np.arange(r)
    tang = 2.0 * np.pi * np.outer(k2, k2) / (r * r)
    tw = np.stack([np.cos(tang), -np.sin(tang)])
    fwd = np.block([[fre, -fim], [fim, fre]])
    inv = np.block([[fre, fim], [-fim, fre]])
    return fre, fim, tw, fwd, inv


def _hy_prep_kernel(x_ref, prev_ref, next_ref, w_ref, b_ref, x1_ref, x2_ref, v_ref, *, n_tiles):
    t = pl.program_id(1)
    x = x_ref[0]
    tm = x.shape[0]
    row = lax.broadcasted_iota(jnp.int32, x.shape, 0)
    prev = jnp.where(t == 0, 0.0, prev_ref[0, pl.ds(7, 1), :])
    nxt = jnp.where(t == n_tiles - 1, 0.0, next_ref[0, pl.ds(0, 1), :])
    up = jnp.where(row == 0, prev, pltpu.roll(x, 1, 0))
    dn = jnp.where(row == tm - 1, nxt, pltpu.roll(x, tm - 1, 0))
    y = b_ref[...] + up * w_ref[pl.ds(0, 1), :] + x * w_ref[pl.ds(1, 1), :] + dn * w_ref[pl.ds(2, 1), :]
    x1_ref[0] = y[:, :HY_W].astype(BF16)
    x2_ref[0] = y[:, HY_W:2 * HY_W].astype(BF16)
    v_ref[0] = y[:, 2 * HY_W:].astype(BF16)


def _hyena_prep(u, conv_w, conv_b):
    b, l, c = u.shape
    tm = min(l, 1024)
    n_tiles = l // tm
    g = tm // 8
    n_groups = l // 8
    out = jax.ShapeDtypeStruct((b, l, HY_W), BF16)
    ospec = pl.BlockSpec((1, tm, HY_W), lambda bi, t: (bi, t, 0))
    return pl.pallas_call(
        functools.partial(_hy_prep_kernel, n_tiles=n_tiles),
        grid=(b, n_tiles),
        in_specs=[pl.BlockSpec((1, tm, c), lambda bi, t: (bi, t, 0)),
                  pl.BlockSpec((1, 8, c), lambda bi, t: (bi, jnp.maximum(t * g - 1, 0), 0)),
                  pl.BlockSpec((1, 8, c), lambda bi, t: (bi, jnp.minimum((t + 1) * g, n_groups - 1), 0)),
                  pl.BlockSpec((HY_SHORT, c), lambda bi, t: (0, 0)),
                  pl.BlockSpec((1, c), lambda bi, t: (0, 0))],
        out_specs=[ospec, ospec, ospec],
        out_shape=[out, out, out],
        compiler_params=_cparams(("arbitrary", "arbitrary")),
        name="hyena_prep",
    )(u, u, u, conv_w, conv_b.reshape(1, c))


def _hy_filter_kernel(z_ref, w1_ref, b1_ref, w2_ref, b2_ref, w3_ref, fr_ref, dl_ref, k_ref, ss_ref, *, seq_len):
    i = pl.program_id(0)
    hp = lax.Precision.HIGHEST
    z = z_ref[...]
    tm = z.shape[0]
    half = tm // 2
    fr = fr_ref[...]
    zz = jnp.concatenate([z[:half], z[half:]], axis=1)
    h = jnp.sin(fr * (jnp.dot(zz, w1_ref[...], preferred_element_type=F32, precision=hp) + b1_ref[...]))
    h = jnp.sin(fr * (jnp.dot(h, w2_ref[...], preferred_element_type=F32, precision=hp) + b2_ref[...]))
    hb = h.astype(BF16)
    w3 = w3_ref[...].astype(BF16)
    fo = w3.shape[0]
    h = jnp.concatenate([jnp.dot(hb[:, :fo], w3, preferred_element_type=F32),
                         jnp.dot(hb[:, fo:], w3, preferred_element_type=F32)], axis=0)
    dec = jnp.exp(-z[:, 0:1] * dl_ref[...])
    r = i * tm + lax.broadcasted_iota(jnp.int32, (tm, HY_W), 0)
    parts = []
    for o in range(HY_ORDER):
        hf = h[:, (2 * o) * HY_W:(2 * o + 1) * HY_W]
        hb = h[:, (2 * o + 1) * HY_W:(2 * o + 2) * HY_W]
        parts.append(jnp.where(r < seq_len, hf, jnp.where(r > seq_len, hb, 0.0)) * dec)
    k = jnp.concatenate(parts, axis=1)
    k_ref[...] = k
    ss = jnp.sum(k * k, axis=0, keepdims=True)

    @pl.when(i == 0)
    def _():
        ss_ref[...] = ss

    @pl.when(i != 0)
    def _():
        ss_ref[...] += ss


def _hyena_filter_taps(l, w1, b1, w2, b2, w3, freq):
    t = jnp.linspace(0.0, 1.0, l, dtype=F32)[:, None]
    w = (2.0 * math.pi / l) * jnp.arange(l, dtype=F32)[:, None]
    f = jnp.linspace(1e-4, HY_BANDS - 1, HY_BANDS, dtype=F32)[None, :]
    z = jnp.concatenate([t, jnp.cos(f * w), -jnp.sin(f * w)], axis=-1)
    z2 = jnp.concatenate([z, z[:1], z[:0:-1]], axis=0)
    emb = z.shape[1]
    z2 = jnp.pad(z2, ((0, 0), (0, LANES - emb)))
    w1p = jnp.pad(w1, ((0, LANES - emb), (0, 0)))
    two = lambda m: jnp.kron(jnp.eye(2, dtype=m.dtype), m)
    pair = lambda v: jnp.tile(v, 2).reshape(1, 2 * v.shape[0])
    deltas = jnp.abs(jnp.linspace(math.log(HY_DECAY_TARGET) / HY_SLOW_DECAY,
                                  math.log(HY_DECAY_TARGET) / HY_FAST_DECAY, HY_W, dtype=F32))
    fo = w2.shape[0]
    nout = w3.shape[1]
    tm = min(2 * l, 1024)
    const = lambda i: (0, 0)
    return pl.pallas_call(
        functools.partial(_hy_filter_kernel, seq_len=l),
        grid=(2 * l // tm,),
        in_specs=[pl.BlockSpec((tm, LANES), lambda i: (i, 0)),
                  pl.BlockSpec((2 * LANES, 2 * fo), const), pl.BlockSpec((1, 2 * fo), const),
                  pl.BlockSpec((2 * fo, 2 * fo), const), pl.BlockSpec((1, 2 * fo), const),
                  pl.BlockSpec((fo, nout), const), pl.BlockSpec((1, 2 * fo), const),
                  pl.BlockSpec((1, HY_W), const)],
        out_specs=[pl.BlockSpec((tm, HY_ORDER * HY_W), lambda i: (i, 0)),
                   pl.BlockSpec((1, HY_ORDER * HY_W), const)],
        out_shape=[jax.ShapeDtypeStruct((2 * l, HY_ORDER * HY_W), F32),
                   jax.ShapeDtypeStruct((1, HY_ORDER * HY_W), F32)],
        compiler_params=_cparams(("arbitrary",)),
        name="hyena_filter_taps",
    )(z2, two(w1p), pair(b1), two(w2), pair(b2), w3, pair(freq), deltas.reshape(1, HY_W))


def _left_matmul_kernel(f_ref, x_ref, o_ref):
    o_ref[0] = jnp.dot(f_ref[...], x_ref[0].astype(BF16), preferred_element_type=F32).astype(o_ref.dtype)


def _left_matmul(f_bf, x3, name):
    b, k, c = x3.shape
    m = f_bf.shape[0]
    return pl.pallas_call(
        _left_matmul_kernel,
        grid=(b, c // HY_COLS),
        in_specs=[pl.BlockSpec((m, k), lambda bi, j: (0, 0)),
                  pl.BlockSpec((1, k, HY_COLS), lambda bi, j: (bi, 0, j))],
        out_specs=pl.BlockSpec((1, m, HY_COLS), lambda bi, j: (bi, 0, j)),
        out_shape=jax.ShapeDtypeStruct((b, m, c), BF16),
        compiler_params=_cparams(("arbitrary", "arbitrary")),
        name=name,
    )(f_bf, x3)


def _twiddle_cols(tw_ref, k1):
    lane = lax.broadcasted_iota(jnp.int32, (HY_R, HY_R), 1)
    pick = lane == k1
    twr = jnp.sum(jnp.where(pick, tw_ref[0], 0.0), axis=1, keepdims=True)
    twi = jnp.sum(jnp.where(pick, tw_ref[1], 0.0), axis=1, keepdims=True)
    return twr, twi


def _hy_spectrum_kernel(a_ref, tw_ref, fwd_ref, ss_ref, kf_ref):
    kb = pl.program_id(0)
    scale = lax.rsqrt(ss_ref[...] + EPS) * (1.0 / (HY_R * HY_R))

    def body(kk, carry):
        twr, twi = _twiddle_cols(tw_ref, kb * HY_KB + kk)
        are = a_ref[0, 0, kk].astype(F32)
        aim = a_ref[0, 1, kk].astype(F32)
        bst = jnp.concatenate([are * twr - aim * twi, are * twi + aim * twr], axis=0).astype(BF16)
        x = jnp.dot(fwd_ref[...], bst, preferred_element_type=F32)
        kf_ref[0, kk] = x[:HY_R] * scale
        kf_ref[1, kk] = x[HY_R:] * scale
        return carry

    lax.fori_loop(0, HY_KB, body, 0, unroll=HY_UNROLL)


def _hy_spectrum(a5, tw, fwd_bf, ss):
    nch = a5.shape[-1]
    return pl.pallas_call(
        _hy_spectrum_kernel,
        grid=(HY_R // HY_KB,),
        in_specs=[pl.BlockSpec((1, 2, HY_KB, HY_R, nch), lambda kb: (0, 0, kb, 0, 0)),
                  pl.BlockSpec((2, HY_R, HY_R), lambda kb: (0, 0, 0)),
                  pl.BlockSpec((2 * HY_R, 2 * HY_R), lambda kb: (0, 0)),
                  pl.BlockSpec((1, nch), lambda kb: (0, 0))],
        out_specs=pl.BlockSpec((2, HY_KB, HY_R, nch), lambda kb: (0, kb, 0, 0)),
        out_shape=jax.ShapeDtypeStruct((2, HY_R, HY_R, nch), F32),
        compiler_params=_cparams(("arbitrary",)),
        name="hyena_filter_spectrum",
    )(a5, tw, fwd_bf, ss)


def _hy_mid_kernel(a_ref, kf_ref, tw_ref, fwd_ref, inv_ref, z_ref):
    kb = pl.program_id(0)

    def body(kk, carry):
        twr, twi = _twiddle_cols(tw_ref, kb * HY_KB + kk)
        are = a_ref[0, 0, kk].astype(F32)
        aim = a_ref[0, 1, kk].astype(F32)
        bst = jnp.concatenate([are * twr - aim * twi, are * twi + aim * twr], axis=0).astype(BF16)
        x = jnp.dot(fwd_ref[...], bst, preferred_element_type=F32)
        xre, xim = x[:HY_R], x[HY_R:]
        kre, kim = kf_ref[0, kk], kf_ref[1, kk]
        yst = jnp.concatenate([xre * kre - xim * kim, xre * kim + xim * kre], axis=0).astype(BF16)
        zz = jnp.dot(inv_ref[...], yst, preferred_element_type=F32)
        zre, zim = zz[:HY_R], zz[HY_R:]
        z_ref[0, 0, kk] = (zre * twr + zim * twi).astype(BF16)
        z_ref[0, 1, kk] = (zim * twr - zre * twi).astype(BF16)
        return carry

    lax.fori_loop(0, HY_KB, body, 0, unroll=HY_UNROLL)


def _hy_mid(a5, kf, order, tw, fwd_bf, inv_bf):
    b = a5.shape[0]
    blk = (1, 2, HY_KB, HY_R, HY_W)
    return pl.pallas_call(
        _hy_mid_kernel,
        grid=(HY_R // HY_KB, b),
        in_specs=[pl.BlockSpec(blk, lambda kb, bi: (bi, 0, kb, 0, 0)),
                  pl.BlockSpec((2, HY_KB, HY_R, HY_W), lambda kb, bi: (0, kb, 0, order)),
                  pl.BlockSpec((2, HY_R, HY_R), lambda kb, bi: (0, 0, 0)),
                  pl.BlockSpec((2 * HY_R, 2 * HY_R), lambda kb, bi: (0, 0)),
                  pl.BlockSpec((2 * HY_R, 2 * HY_R), lambda kb, bi: (0, 0))],
        out_specs=pl.BlockSpec(blk, lambda kb, bi: (bi, 0, kb, 0, 0)),
        out_shape=jax.ShapeDtypeStruct(a5.shape, BF16),
        compiler_params=_cparams(("arbitrary", "arbitrary")),
        name="hyena_spectral_product",
    )(a5, kf, tw, fwd_bf, inv_bf)


def _hy_out_kernel(f_ref, z_ref, g_ref, w_ref, sk_ref, o_ref):
    y = jnp.dot(f_ref[...], z_ref[0], preferred_element_type=F32)
    w = w_ref[0].astype(F32)
    o_ref[0] = (g_ref[0].astype(F32) * (y + w * sk_ref[...])).astype(BF16)


def _hy_out(f_bf, z3, gate3, w3, skip_row):
    b, m2, c = z3.shape
    m = f_bf.shape[0]
    dspec = pl.BlockSpec((1, m, HY_COLS), lambda bi, j: (bi, 0, j))
    return pl.pallas_call(
        _hy_out_kernel,
        grid=(b, c // HY_COLS),
        in_specs=[pl.BlockSpec((m, m2), lambda bi, j: (0, 0)),
                  pl.BlockSpec((1, m2, HY_COLS), lambda bi, j: (bi, 0, j)),
                  dspec, dspec,
                  pl.BlockSpec((1, HY_COLS), lambda bi, j: (0, 0))],
        out_specs=dspec,
        out_shape=jax.ShapeDtypeStruct((b, m, c), BF16),
        compiler_params=_cparams(("arbitrary", "arbitrary")),
        name="hyena_inverse_gate",
    )(f_bf, z3, gate3, w3, skip_row)


def _hyena_long(u, conv_w, conv_b, skip, filt_args):
    b, l, _ = u.shape
    assert 2 * l == HY_R * HY_R
    fre, fim, tw, fwd, inv = _dft_consts(HY_R)
    half = HY_R // 2
    as_bf = lambda m: jnp.asarray(m, F32).astype(BF16)
    f_a_data = as_bf(np.concatenate([fre[:, :half], fim[:, :half]], axis=0))
    f_a_filt = as_bf(np.concatenate([fre, fim], axis=0))
    f_c = as_bf(np.concatenate([fre[:half], fim[:half]], axis=1))
    tw_j = jnp.asarray(tw, F32)
    fwd_bf = as_bf(fwd)
    inv_bf = as_bf(inv)

    taps, ss = _hyena_filter_taps(l, *filt_args)
    nch = HY_ORDER * HY_W
    ak = _left_matmul(f_a_filt, taps.reshape(1, HY_R, HY_R * nch), "hyena_filter_dft_a")
    kf = _hy_spectrum(ak.reshape(1, 2, HY_R, HY_R, nch), tw_j, fwd_bf, ss)

    x1, x2, v = _hyena_prep(u, conv_w, conv_b)
    z = v
    for o, gate in enumerate((x1, x2)):
        z3 = z.reshape(b, half, HY_R * HY_W)
        a = _left_matmul(f_a_data, z3, "hyena_dft_a")
        zs = _hy_mid(a.reshape(b, 2, HY_R, HY_R, HY_W), kf, o, tw_j, fwd_bf, inv_bf)
        skip_row = jnp.tile(skip[o], HY_COLS // HY_W).reshape(1, HY_COLS)
        z = _hy_out(f_c, zs.reshape(b, 2 * HY_R, HY_R * HY_W), gate.reshape(b, half, HY_R * HY_W), z3,
                    skip_row).reshape(b, l, HY_W)
    return z


def _hy_ctx_kernel(x1_ref, x2_ref, v_ref, taps_ref, ss_ref, fa_ref, fc_ref, sk_ref, o_ref, *, seq_len):
    n2 = 2 * seq_len
    fa = fa_ref[...]
    scale = lax.rsqrt(ss_ref[...] + EPS) * (1.0 / n2)
    kf = jnp.dot(fa, taps_ref[...].astype(BF16), preferred_element_type=F32) * scale
    z = v_ref[0].astype(F32)
    for o, g_ref in enumerate((x1_ref, x2_ref)):
        x = jnp.dot(fa[:, :seq_len], z.astype(BF16), preferred_element_type=F32)
        xre, xim = x[:n2], x[n2:]
        kre = kf[:n2, o * HY_W:(o + 1) * HY_W]
        kim = kf[n2:, o * HY_W:(o + 1) * HY_W]
        yst = jnp.concatenate([xre * kre - xim * kim, xre * kim + xim * kre], axis=0).astype(BF16)
        y = jnp.dot(fc_ref[...], yst, preferred_element_type=F32)
        z = g_ref[0].astype(F32) * (y + z * sk_ref[pl.ds(o, 1), :])
    o_ref[0] = z.astype(BF16)


def _hyena_short(u, conv_w, conv_b, skip, filt_args):
    b, l, _ = u.shape
    n2 = 2 * l
    k = np.arange(n2)
    ang = 2.0 * np.pi * np.outer(k, k) / n2
    fre, fim = np.cos(ang), -np.sin(ang)
    fa = jnp.asarray(np.concatenate([fre, fim], axis=0), F32).astype(BF16)
    fc = jnp.asarray(np.concatenate([fre[:l], fim[:l]], axis=1), F32).astype(BF16)
    taps, ss = _hyena_filter_taps(l, *filt_args)
    x1, x2, v = _hyena_prep(u, conv_w, conv_b)
    nch = HY_ORDER * HY_W
    dspec = pl.BlockSpec((1, l, HY_W), lambda bi: (bi, 0, 0))
    const = lambda bi: (0, 0)
    return pl.pallas_call(
        functools.partial(_hy_ctx_kernel, seq_len=l),
        grid=(b,),
        in_specs=[dspec, dspec, dspec,
                  pl.BlockSpec((n2, nch), const), pl.BlockSpec((1, nch), const),
                  pl.BlockSpec((2 * n2, n2), const), pl.BlockSpec((l, 2 * n2), const),
                  pl.BlockSpec((HY_ORDER, HY_W), const)],
        out_specs=dspec,
        out_shape=jax.ShapeDtypeStruct((b, l, HY_W), BF16),
        compiler_params=_cparams(("arbitrary",)),
        name="hyena_context",
    )(x1, x2, v, taps, ss, fa, fc, skip)


def _rope_tables(n):
    t = jnp.arange(n, dtype=jnp.int32)
    pos = jnp.stack([t // GRID_W, t % GRID_W], axis=-1).astype(F32)
    nf = HEAD_DIM // 4
    inv = ROPE_THETA ** (-jnp.arange(nf, dtype=F32) / nf)
    ang = pos[:, :, None] * inv
    cos, sin = jnp.cos(ang), jnp.sin(ang)
    c64 = jnp.concatenate([cos[:, 0], cos[:, 0], cos[:, 1], cos[:, 1]], axis=-1)
    s64 = jnp.concatenate([-sin[:, 0], sin[:, 0], -sin[:, 1], sin[:, 1]], axis=-1)
    return jnp.tile(c64, (1, 2)), jnp.tile(s64, (1, 2))


def _block_diag(pool_w):
    g, c, _ = pool_w.shape
    out = jnp.zeros((g * c, g * c), pool_w.dtype)
    for i in range(g):
        out = out.at[i * c:(i + 1) * c, i * c:(i + 1) * c].set(pool_w[i])
    return out


def kernel(x, c, ctx, c_ctx, w_mod, b_mod, norm1_w, norm2_w, w_in, w_out, q_norm_w, k_norm_w, na_rpb,
           hy_conv_w, hy_conv_b, hy_w1, hy_b1, hy_w2, hy_b2, hy_w3, hy_freq, hy_skip,
           pool_w, pool_scale, w_router, w_gate, w_up, w_down):
    b, n, d = x.shape
    lc = ctx.shape[1]
    depth = w_mod.shape[0]
    rows = n // GRID_W
    cos128, sin128 = _rope_tables(n)
    seg = jnp.asarray(np.kron(np.eye(2), np.full((HEAD_DIM, HEAD_DIM), 1.0 / HEAD_DIM)), BF16)
    cc = jnp.zeros((8, d), F32).at[:b].set(c).at[b].set(c_ctx)

    xl = x.reshape(b * n, d)
    xc = ctx.reshape(b * lc, d)
    for i in range(depth):
        last = i == depth - 1
        mod = _modulation(cc, w_mod, b_mod[i], i)
        ml = mod[:b].reshape(b, 1, 6, d)
        mc = mod[b].reshape(1, 1, 6, d)
        sh1, sc1, g1, sh2, sc2, g2 = [ml[:, :, j] for j in range(6)]
        csh1, csc1, cg1, csh2, csc2, cg2 = [mc[:, :, j] for j in range(6)]
        w_in_bf = w_in[i].astype(BF16)
        w_out_bf = w_out[i].astype(BF16)
        w_router_t = w_router[i].T
        pool_bd = _block_diag(pool_w[i]).astype(BF16)
        hy_args = (hy_w1[i], hy_b1[i], hy_w2[i], hy_b2[i], hy_w3[i], hy_freq[i])

        qc, kc, vc, hyc, poolc = _in_projection(
            xc, csh1, csc1, norm1_w[i], w_in_bf, q_norm_w[i], k_norm_w[i], cos128, sin128, seg,
            rows_per_mod=b * lc, seq_len=lc, tm=lc, rope=False)
        kc = kc.reshape(b, lc, NA_W)
        vc = vc.reshape(b, lc, NA_W)
        if not last:
            na_c = _dense_attention(qc.reshape(b, lc, NA_W), kc, vc).reshape(b * lc, NA_W)
            hy_c = _hyena_short(hyc.reshape(b, lc, 3 * HY_W), hy_conv_w[i], hy_conv_b[i], hy_skip[i], hy_args)
            pool_c = _pool_mixer(poolc.reshape(b, lc, POOL_W), pool_bd, pool_scale[i])
            xc1, h2c, affc = _out_projection(
                na_c, hy_c.reshape(b * lc, HY_W).astype(BF16), pool_c.reshape(b * lc, POOL_W), w_out_bf,
                xc, cg1, norm2_w[i], csh2, csc2, w_router_t, rows_per_mod=b * lc, seq_len=lc, tm=lc)

        qr, qp, kl, vl, hyl, pooll = _in_projection(
            xl, sh1, sc1, norm1_w[i], w_in_bf, q_norm_w[i], k_norm_w[i], cos128, sin128, seg,
            rows_per_mod=n, seq_len=n, tm=512, rope=True)
        bias = _na_bias_tables(na_rpb[i], rows)
        o_na = _neighborhood_attention(qr.reshape(b, n, NA_W), qp.reshape(b, n, NA_W), kl.reshape(b, n, NA_W),
                                       vl.reshape(b, n, NA_W), kc, vc, bias)
        hy_l = _hyena_long(hyl.reshape(b, n, 3 * HY_W), hy_conv_w[i], hy_conv_b[i], hy_skip[i], hy_args)
        pool_l = _pool_mixer(pooll.reshape(b, n, POOL_W), pool_bd, pool_scale[i])
        xl1, h2l, affl = _out_projection(
            o_na.reshape(b * n, NA_W), hy_l.reshape(b * n, HY_W).astype(BF16), pool_l.reshape(b * n, POOL_W),
            w_out_bf, xl, g1, norm2_w[i], sh2, sc2, w_router_t, rows_per_mod=n, seq_len=n, tm=512)
        streams = [(h2l, affl, xl1, g2)]
        if not last:
            streams.append((h2c, affc, xc1, cg2))
        outs = _expert_choice_moe(streams, w_gate, w_up, w_down, i)
        xl = outs[0]
        if not last:
            xc = outs[1]
    return xl.reshape(b, n, d)
```

```python
import functools
import math

import jax
import jax.numpy as jnp
import numpy as np
from jax import lax
from jax.experimental import pallas as pl
from jax.experimental.pallas import tpu as pltpu

F32 = jnp.float32
BF16 = jnp.bfloat16

D_MODEL = 1024
GRID_W = 64
NA_HEADS = 8
HEAD_DIM = 64
NA_W = NA_HEADS * HEAD_DIM
NA_WIN_R = 8
NA_WIN_C = 16
ROPE_THETA = 10000.0
HY_W = 256
HY_ORDER = 2
HY_SHORT = 3
HY_BANDS = 16
HY_DECAY_TARGET = 1e-2
HY_FAST_DECAY = 0.3
HY_SLOW_DECAY = 1.5
POOL_W = 256
POOL_SIZES = (2, 4, 8, 16)
POOL_GROUP = POOL_W // len(POOL_SIZES)
MIX_W = NA_W + HY_W + POOL_W
IN_W = 3 * NA_W + (HY_ORDER + 1) * HY_W + POOL_W
N_EXPERTS = 16
EC_FACTOR = 2
D_EXPERT = 2048
EPS = 1e-6

LANES = 128
TOKEN_SUBLANES = D_MODEL // 128
ROW_SUBTILES = 2
NEG_BIG = -1e30
VMEM_LIMIT = 56 * 1024 * 1024

NA_Q_ROWS = 4
NA_K_ROWS = NA_Q_ROWS + NA_WIN_R - 1
NA_TILES_PER_STEP = 4


def _cparams(sem):
    return pltpu.CompilerParams(dimension_semantics=sem, vmem_limit_bytes=VMEM_LIMIT)


def _mod_kernel(c_ref, w_ref, b_ref, o_ref):
    c = c_ref[...]
    s = c * jax.nn.sigmoid(c)
    o_ref[...] = jnp.dot(s, w_ref[0], preferred_element_type=F32,
                         precision=lax.Precision.HIGHEST) + b_ref[...]


def _modulation(cc, w_mod, b_mod, layer):
    rows, d = cc.shape
    n = w_mod.shape[2]
    tn = 1536
    return pl.pallas_call(
        _mod_kernel,
        grid=(n // tn,),
        in_specs=[pl.BlockSpec((rows, d), lambda j: (0, 0)),
                  pl.BlockSpec((1, d, tn), lambda j: (layer, 0, j)),
                  pl.BlockSpec((1, tn), lambda j: (0, j))],
        out_specs=pl.BlockSpec((rows, tn), lambda j: (0, j)),
        out_shape=jax.ShapeDtypeStruct((rows, n), F32),
        compiler_params=_cparams(("arbitrary",)),
        name="modulation",
    )(cc, w_mod, b_mod.reshape(1, n))


def _head_norm(xs, w128, seg):
    m = jnp.dot((xs * xs).astype(BF16), seg, preferred_element_type=F32)
    return xs * lax.rsqrt(m + EPS) * w128


def _rope(xs, c, s_signed):
    lane = lax.broadcasted_iota(jnp.int32, xs.shape, 1)
    partner = jnp.where((lane & 16) == 0, pltpu.roll(xs, LANES - 16, 1), pltpu.roll(xs, 16, 1))
    return xs * c + partner * s_signed


def _inproj_kernel(x_ref, sh_ref, sc_ref, nw_ref, w_ref, qw_ref, kw_ref, cos_ref, sin_ref, seg_ref,
                   *out_refs, rope):
    if rope:
        qr_ref, qp_ref, k_ref, v_ref, hy_ref, pool_ref = out_refs
    else:
        qp_ref, k_ref, v_ref, hy_ref, pool_ref = out_refs
    seg = seg_ref[...]
    qk_scale = HEAD_DIM ** -0.5
    sub = x_ref.shape[0] // ROW_SUBTILES
    for t in range(ROW_SUBTILES):
        rs = pl.ds(t * sub, sub)
        x = x_ref[rs, :]
        ms = jnp.mean(x * x, axis=-1, keepdims=True)
        h = x * lax.rsqrt(ms + EPS) * nw_ref[...]
        h = h * (1.0 + sc_ref[0]) + sh_ref[0]
        u = jnp.dot(h.astype(BF16), w_ref[...], preferred_element_type=F32)
        for ch in range(NA_W // LANES):
            sl = slice(ch * LANES, (ch + 1) * LANES)
            qn = _head_norm(u[:, sl], qw_ref[...], seg)
            kn = _head_norm(u[:, NA_W + ch * LANES:NA_W + (ch + 1) * LANES], kw_ref[...], seg)
            qp_ref[rs, sl] = (qn * qk_scale).astype(BF16)
            if rope:
                c = cos_ref[rs, :]
                s = sin_ref[rs, :]
                qr_ref[rs, sl] = (_rope(qn, c, s) * qk_scale).astype(BF16)
                k_ref[rs, sl] = _rope(kn, c, s).astype(BF16)
            else:
                k_ref[rs, sl] = kn.astype(BF16)
        v_ref[rs, :] = u[:, 2 * NA_W:3 * NA_W].astype(BF16)
        hy_ref[rs, :] = u[:, 3 * NA_W:3 * NA_W + 3 * HY_W]
        pool_ref[rs, :] = u[:, 3 * NA_W + 3 * HY_W:]


def _in_projection(x2, shift, scale, norm_w, w_in_bf, q_norm_w, k_norm_w, cos128, sin128, seg, *,
                   rows_per_mod, seq_len, tm, rope):
    m, d = x2.shape
    tiles_per_mod = rows_per_mod // tm
    tiles_per_seq = seq_len // tm
    row = lambda i: (i, 0)
    const = lambda i: (0, 0)
    modi = lambda i: (i // tiles_per_mod, 0, 0)
    pos = lambda i: (i % tiles_per_seq, 0)
    outs = []
    if rope:
        outs.append((NA_W, BF16))
    outs += [(NA_W, BF16), (NA_W, BF16), (NA_W, BF16), (3 * HY_W, F32), (POOL_W, F32)]
    return pl.pallas_call(
        functools.partial(_inproj_kernel, rope=rope),
        grid=(m // tm,),
        in_specs=[pl.BlockSpec((tm, d), row),
                  pl.BlockSpec((1, 1, d), modi), pl.BlockSpec((1, 1, d), modi),
                  pl.BlockSpec((1, d), const),
                  pl.BlockSpec((d, IN_W), const),
                  pl.BlockSpec((1, LANES), const), pl.BlockSpec((1, LANES), const),
                  pl.BlockSpec((tm, LANES), pos), pl.BlockSpec((tm, LANES), pos),
                  pl.BlockSpec((LANES, LANES), const)],
        out_specs=[pl.BlockSpec((tm, w), row) for w, _ in outs],
        out_shape=[jax.ShapeDtypeStruct((m, w), dt) for w, dt in outs],
        compiler_params=_cparams(("arbitrary",)),
        name="in_projection_rope" if rope else "in_projection_ctx",
    )(x2, shift, scale, norm_w.reshape(1, d), w_in_bf,
      jnp.tile(q_norm_w, 2).reshape(1, LANES), jnp.tile(k_norm_w, 2).reshape(1, LANES),
      cos128, sin128, seg)


def _softmax_pv(s_list, v_list):
    mx = functools.reduce(jnp.maximum, [jnp.max(s, axis=-1, keepdims=True) for s in s_list])
    ps = [jnp.exp(s - mx) for s in s_list]
    l = functools.reduce(jnp.add, [jnp.sum(p, axis=-1, keepdims=True) for p in ps])
    o = functools.reduce(jnp.add, [jnp.dot(p.astype(BF16), v, preferred_element_type=F32)
                                   for p, v in zip(ps, v_list)])
    return o / l


_NT = (((1,), (1,)), ((), ()))


def _na_kernel(qr_ref, qp_ref, k_ref, v_ref, kc_ref, vc_ref, bias_ref, o_ref, *, rows):
    n_tiles = rows // NA_Q_ROWS
    tq = NA_Q_ROWS * GRID_W
    kc = kc_ref[0]
    vc = vc_ref[0]
    lane = lax.broadcasted_iota(jnp.int32, (tq, LANES), 1)
    for sub in range(NA_TILES_PER_STEP):
        t = pl.program_id(2) * NA_TILES_PER_STEP + sub
        ks = jnp.clip(t * NA_Q_ROWS - NA_WIN_R // 2, 0, rows - NA_K_ROWS)
        start = pl.multiple_of(ks * GRID_W, GRID_W)
        k = k_ref[0, pl.ds(start, NA_K_ROWS * GRID_W), :]
        v = v_ref[0, pl.ds(start, NA_K_ROWS * GRID_W), :]
        q = qr_ref[0, pl.ds(sub * tq, tq), :]
        qp = qp_ref[0, pl.ds(sub * tq, tq), :]
        kind = jnp.where(t == 0, 0, jnp.where(t == n_tiles - 1, 2, 1))
        outs = []
        for hh in range(2):
            keep = (lane < HEAD_DIM) if hh == 0 else (lane >= HEAD_DIM)
            zero = jnp.zeros_like(q)
            s_lat = (lax.dot_general(jnp.where(keep, q, zero), k, _NT, preferred_element_type=F32)
                     + bias_ref[kind, hh])
            s_ctx = lax.dot_general(jnp.where(keep, qp, zero), kc, _NT, preferred_element_type=F32)
            outs.append(_softmax_pv([s_lat, s_ctx], [v, vc]))
        o_ref[0, pl.ds(sub * tq, tq), :] = jnp.where(lane < HEAD_DIM, outs[0], outs[1]).astype(BF16)


def _neighborhood_attention(qr, qp, k, v, kc, vc, bias):
    b, n, _ = qr.shape
    lc = kc.shape[1]
    rows = n // GRID_W
    n_steps = rows // (NA_Q_ROWS * NA_TILES_PER_STEP)
    tq = NA_Q_ROWS * GRID_W * NA_TILES_PER_STEP
    tk = NA_K_ROWS * GRID_W
    qmap = lambda bi, hp, t: (bi, t, hp)
    smap = lambda bi, hp, t: (bi, 0, hp)
    bmap = lambda bi, hp, t: (0, hp, 0, 0)
    return pl.pallas_call(
        functools.partial(_na_kernel, rows=rows),
        grid=(b, NA_W // LANES, n_steps),
        in_specs=[pl.BlockSpec((1, tq, LANES), qmap), pl.BlockSpec((1, tq, LANES), qmap),
                  pl.BlockSpec((1, n, LANES), smap), pl.BlockSpec((1, n, LANES), smap),
                  pl.BlockSpec((1, lc, LANES), smap), pl.BlockSpec((1, lc, LANES), smap),
                  pl.BlockSpec((3, 2, NA_Q_ROWS * GRID_W, tk), bmap)],
        out_specs=pl.BlockSpec((1, tq, LANES), qmap),
        out_shape=jax.ShapeDtypeStruct((b, n, NA_W), BF16),
        compiler_params=_cparams(("arbitrary", "arbitrary", "arbitrary")),
        name="neighborhood_attention",
    )(qr, qp, k, v, kc, vc, bias)


def _na_bias_tables(rpb, rows):
    n_tiles = rows // NA_Q_ROWS
    n_dr, n_dc = 2 * NA_WIN_R - 1, 2 * NA_WIN_C - 1
    qc = np.arange(GRID_W)
    cs = np.clip(qc - NA_WIN_C // 2, 0, GRID_W - NA_WIN_C)
    kcol = np.arange(GRID_W)
    ok_c = (kcol[None, :] >= cs[:, None]) & (kcol[None, :] < cs[:, None] + NA_WIN_C)
    dc = kcol[None, :] - qc[:, None] + NA_WIN_C - 1
    col_sel = np.concatenate([(dc[:, :, None] == np.arange(n_dc)) & ok_c[:, :, None],
                              ~ok_c[:, :, None]], axis=2)
    ext = jnp.pad(rpb.astype(F32), ((0, 0), (0, 1), (0, 1)), constant_values=NEG_BIG)
    blocks = jnp.einsum("hrd,qkd->hrqk", ext, jnp.asarray(col_sel, F32), precision=lax.Precision.HIGHEST)
    tabs = []
    for t in (0, 1, n_tiles - 1):
        r0 = t * NA_Q_ROWS
        ks = int(np.clip(r0 - NA_WIN_R // 2, 0, rows - NA_K_ROWS))
        r = r0 + np.arange(NA_Q_ROWS)
        rs = np.clip(r - NA_WIN_R // 2, 0, rows - NA_WIN_R)
        krow = ks + np.arange(NA_K_ROWS)
        ok_r = (krow[None, :] >= rs[:, None]) & (krow[None, :] < rs[:, None] + NA_WIN_R)
        dr = np.where(ok_r, krow[None, :] - r[:, None] + NA_WIN_R - 1, n_dr)
        tabs.append(jnp.concatenate(
            [jnp.concatenate([blocks[:, int(dr[i, j])] for j in range(NA_K_ROWS)], axis=-1)
             for i in range(NA_Q_ROWS)], axis=-2))
    return jnp.stack(tabs)


def _dense_attn_kernel(q_ref, k_ref, v_ref, o_ref):
    q = q_ref[0]
    k = k_ref[0]
    v = v_ref[0]
    lane = lax.broadcasted_iota(jnp.int32, q.shape, 1)
    outs = []
    for hh in range(2):
        keep = (lane < HEAD_DIM) if hh == 0 else (lane >= HEAD_DIM)
        s = lax.dot_general(jnp.where(keep, q, jnp.zeros_like(q)), k, _NT, preferred_element_type=F32)
        outs.append(_softmax_pv([s], [v]))
    o_ref[0] = jnp.where(lane < HEAD_DIM, outs[0], outs[1]).astype(BF16)


def _dense_attention(q, k, v):
    b, l, _ = q.shape
    spec = pl.BlockSpec((1, l, LANES), lambda bi, hp: (bi, 0, hp))
    return pl.pallas_call(
        _dense_attn_kernel,
        grid=(b, NA_W // LANES),
        in_specs=[spec, spec, spec],
        out_specs=spec,
        out_shape=jax.ShapeDtypeStruct((b, l, NA_W), BF16),
        compiler_params=_cparams(("arbitrary", "arbitrary")),
        name="context_attention",
    )(q, k, v)


POOL_HALO = max(POOL_SIZES) // 2


def _pool_kernel(x_ref, w_ref, sc_ref, o_ref, xpad, *, seq_len, chunk):
    zeros = jnp.zeros((POOL_HALO, POOL_W), F32)
    xpad[pl.ds(0, POOL_HALO), :] = zeros
    xpad[pl.ds(seq_len + POOL_HALO, POOL_HALO), :] = zeros
    xpad[pl.ds(POOL_HALO, seq_len), :] = x_ref[0]
    span = chunk + 2 * POOL_HALO
    w = w_ref[...]
    sc = sc_ref[...]

    def body(c, carry):
        base = pl.multiple_of(c * chunk, chunk)
        xs = xpad[pl.ds(base, span), :]
        s2 = xs + pltpu.roll(xs, 1, 0)
        s4 = pltpu.roll(s2, 1, 0) + pltpu.roll(s2, span - 1, 0)
        s8 = pltpu.roll(s4, 2, 0) + pltpu.roll(s4, span - 2, 0)
        s16 = pltpu.roll(s8, 4, 0) + pltpu.roll(s8, span - 4, 0)
        mid = slice(POOL_HALO, POOL_HALO + chunk)
        grp = lax.broadcasted_iota(jnp.int32, (chunk, POOL_W), 1) // POOL_GROUP
        tpos = base + lax.broadcasted_iota(jnp.int32, (chunk, POOL_W), 0)
        half = jnp.left_shift(1, grp)
        cnt = (jnp.minimum(tpos + half, seq_len) - jnp.maximum(tpos - half, 0)).astype(F32)
        ssum = jnp.where(grp == 0, s2[mid], jnp.where(grp == 1, s4[mid], jnp.where(grp == 2, s8[mid], s16[mid])))
        diff = ssum / cnt - xs[mid]
        y = jnp.dot(diff.astype(BF16), w, preferred_element_type=F32) * sc
        o_ref[0, pl.ds(base, chunk), :] = y.astype(BF16)
        return carry

    lax.fori_loop(0, seq_len // chunk, body, 0)


def _pool_mixer(u, w_blockdiag_bf, pool_scale):
    b, l, _ = u.shape
    chunk = min(l, 512)
    return pl.pallas_call(
        functools.partial(_pool_kernel, seq_len=l, chunk=chunk),
        grid=(b,),
        in_specs=[pl.BlockSpec((1, l, POOL_W), lambda bi: (bi, 0, 0)),
                  pl.BlockSpec((POOL_W, POOL_W), lambda bi: (0, 0)),
                  pl.BlockSpec((1, POOL_W), lambda bi: (0, 0))],
        out_specs=pl.BlockSpec((1, l, POOL_W), lambda bi: (bi, 0, 0)),
        out_shape=jax.ShapeDtypeStruct((b, l, POOL_W), BF16),
        scratch_shapes=[pltpu.VMEM((l + 2 * POOL_HALO, POOL_W), F32)],
        compiler_params=_cparams(("arbitrary",)),
        name="pool_mixer",
    )(u, w_blockdiag_bf, pool_scale.reshape(1, POOL_W))


def _outproj_kernel(na_ref, hy_ref, pl_ref, w_ref, x_ref, g1_ref, nw_ref, sh_ref, sc_ref, wr_ref,
                    xn_ref, h2_ref, aff_ref):
    w = w_ref
    acc = jnp.dot(na_ref[...], w[pl.ds(0, NA_W), :], preferred_element_type=F32)
    acc += jnp.dot(hy_ref[...], w[pl.ds(NA_W, HY_W), :], preferred_element_type=F32)
    acc += jnp.dot(pl_ref[...], w[pl.ds(NA_W + HY_W, POOL_W), :], preferred_element_type=F32)
    xn = x_ref[...] + g1_ref[0] * acc
    xn_ref[...] = xn
    ms = jnp.mean(xn * xn, axis=-1, keepdims=True)
    h2 = xn * lax.rsqrt(ms + EPS) * nw_ref[...]
    h2 = h2 * (1.0 + sc_ref[0]) + sh_ref[0]
    tm = h2.shape[0]
    for s in range(TOKEN_SUBLANES):
        h2_ref[pl.ds(s, tm, stride=TOKEN_SUBLANES), :] = h2[:, s * LANES:(s + 1) * LANES]
    logits = lax.dot_general(wr_ref[...], h2, _NT, preferred_element_type=F32,
                             precision=lax.Precision.HIGHEST)
    mx = jnp.max(logits, axis=0, keepdims=True)
    p = jnp.exp(logits - mx)
    aff_ref[0] = p / jnp.sum(p, axis=0, keepdims=True)


def _out_projection(na, hy, pool, w_out_bf, x2, g1, norm_w, shift, scale, w_router_t, *,
                    rows_per_mod, seq_len, tm):
    m, d = x2.shape
    tiles_per_mod = rows_per_mod // tm
    tiles_per_seq = seq_len // tm
    row = lambda i: (i, 0)
    const = lambda i: (0, 0)
    modi = lambda i: (i // tiles_per_mod, 0, 0)
    return pl.pallas_call(
        _outproj_kernel,
        grid=(m // tm,),
        in_specs=[pl.BlockSpec((tm, NA_W), row), pl.BlockSpec((tm, HY_W), row), pl.BlockSpec((tm, POOL_W), row),
                  pl.BlockSpec((MIX_W, d), const),
                  pl.BlockSpec((tm, d), row),
                  pl.BlockSpec((1, 1, d), modi),
                  pl.BlockSpec((1, d), const),
                  pl.BlockSpec((1, 1, d), modi), pl.BlockSpec((1, 1, d), modi),
                  pl.BlockSpec((N_EXPERTS, d), const)],
        out_specs=[pl.BlockSpec((tm, d), row), pl.BlockSpec((tm * TOKEN_SUBLANES, LANES), row),
                   pl.BlockSpec((1, N_EXPERTS, tm), lambda i: (i // tiles_per_seq, 0, i % tiles_per_seq))],
        out_shape=[jax.ShapeDtypeStruct((m, d), F32), jax.ShapeDtypeStruct((m * TOKEN_SUBLANES, LANES), F32),
                   jax.ShapeDtypeStruct((m // seq_len, N_EXPERTS, seq_len), F32)],
        compiler_params=_cparams(("arbitrary",)),
        name="out_projection_router",
    )(na, hy, pool, w_out_bf, x2, g1, norm_w.reshape(1, d), shift, scale, w_router_t)


FFN_TF = 256
FFN_ROW_CHUNKS = 2


def _ffn_kernel(*refs, n_experts, sources, n_rows):
    ns = len(sources)
    idx_cur = refs[0:ns]
    idx_nxt = refs[ns:2 * ns]
    src = refs[2 * ns:3 * ns]
    gate_ref, wg_ref, wu_ref, wd_ref, o_ref, stage, xb, acc, sem = refs[3 * ns:]
    e = pl.program_id(0)
    f = pl.program_id(1)
    n_f = pl.num_programs(1)

    ts = TOKEN_SUBLANES

    def row_copy(s, row, r):
        return pltpu.make_async_copy(src[s].at[pl.ds(pl.multiple_of(row * ts, ts), ts)],
                                     stage.at[pl.ds(pl.multiple_of((sources[s][0] + r) * ts, ts), ts)], sem.at[0])

    def gather(idx_refs):
        for s in range(ns):
            def body(r, carry, s=s):
                row_copy(s, idx_refs[s][0, 0, r], r).start()
                return carry
            lax.fori_loop(0, sources[s][1], body, 0, unroll=8)

    def gather_wait():
        for s in range(ns):
            off, cnt = sources[s]
            pltpu.make_async_copy(src[s].at[pl.ds(0, cnt * ts)], stage.at[pl.ds(off * ts, cnt * ts)],
                                  sem.at[0]).wait()

    @pl.when(f == 0)
    def _():
        @pl.when(e == 0)
        def _():
            gather(idx_cur)

        acc[...] = jnp.zeros_like(acc)
        gather_wait()
        for s in range(ts):
            xb[:, s * LANES:(s + 1) * LANES] = stage[pl.ds(s, n_rows, stride=ts), :].astype(BF16)

        @pl.when(e + 1 < n_experts)
        def _():
            gather(idx_nxt)

    wg = wg_ref[0, 0].astype(BF16)
    wu = wu_ref[0, 0].astype(BF16)
    wd = wd_ref[0, 0].astype(BF16)
    rc = n_rows // FFN_ROW_CHUNKS
    for r in range(FFN_ROW_CHUNKS):
        rs = pl.ds(r * rc, rc)
        x = xb[rs, :]
        a = jnp.dot(x, wg, preferred_element_type=F32)
        b = jnp.dot(x, wu, preferred_element_type=F32)
        h = (a * jax.nn.sigmoid(a) * b).astype(BF16)
        y = jnp.dot(h, wd, preferred_element_type=F32)
        acc[rs, :] += y

    @pl.when(f == n_f - 1)
    def _():
        o_ref[0] = (acc[...] * gate_ref[0]).astype(BF16)


def _expert_ffn(rows_idx, tokens, gate, w_gate, w_up, w_down, layer):
    ns = len(tokens)
    e, d = w_gate.shape[1], w_gate.shape[2]
    counts = [int(ri.shape[2]) for ri in rows_idx]
    offs = [int(sum(counts[:s])) for s in range(ns)]
    r_total = sum(counts)
    assert r_total % (FFN_ROW_CHUNKS * 16) == 0
    smem = lambda cnt, nxt: pl.BlockSpec(
        (1, 1, cnt), (lambda ei, f: (jnp.minimum(ei + 1, e - 1), 0, 0)) if nxt else (lambda ei, f: (ei, 0, 0)),
        memory_space=pltpu.SMEM)
    in_specs = ([smem(cnt, False) for cnt in counts] + [smem(cnt, True) for cnt in counts]
                + [pl.BlockSpec(memory_space=pl.ANY) for _ in range(ns)]
                + [pl.BlockSpec((1, r_total, 1), lambda ei, f: (ei, 0, 0)),
                   pl.BlockSpec((1, 1, d, FFN_TF), lambda ei, f: (layer, ei, 0, f)),
                   pl.BlockSpec((1, 1, d, FFN_TF), lambda ei, f: (layer, ei, 0, f)),
                   pl.BlockSpec((1, 1, FFN_TF, d), lambda ei, f: (layer, ei, f, 0))])
    return pl.pallas_call(
        functools.partial(_ffn_kernel, n_experts=e, sources=tuple(zip(offs, counts)), n_rows=r_total),
        grid=(e, D_EXPERT // FFN_TF),
        in_specs=in_specs,
        out_specs=pl.BlockSpec((1, r_total, d), lambda ei, f: (ei, 0, 0)),
        out_shape=jax.ShapeDtypeStruct((e, r_total, d), BF16),
        scratch_shapes=[pltpu.VMEM((r_total * TOKEN_SUBLANES, LANES), F32), pltpu.VMEM((r_total, d), BF16),
                        pltpu.VMEM((r_total, d), F32), pltpu.SemaphoreType.DMA((1,))],
        compiler_params=_cparams(("arbitrary", "arbitrary")),
        name="expert_ffn",
    )(*rows_idx, *rows_idx, *tokens, gate, w_gate, w_up, w_down)


def _prefix_count(m, n):
    lane = lax.broadcasted_iota(jnp.int32, m.shape, 1)
    c = m
    s = 1
    while s < n:
        c = c + jnp.where(lane >= s, pltpu.roll(c, s, 1), 0)
        s *= 2
    return c


COMBINE_TB = 256
COMBINE_CHUNK = 128


def _topk_kernel(aff_ref, idx_ref, gate_ref, off_ref, *, cap):
    a = aff_ref[...]
    rows, n = a.shape

    def search(i, t):
        cand = t | jnp.left_shift(jnp.int32(1), 30 - i)
        cnt = jnp.sum((a >= pltpu.bitcast(cand, F32)).astype(jnp.int32), axis=1, keepdims=True)
        return jnp.where(cnt >= cap, cand, t)

    thr_bits = lax.fori_loop(0, 31, search, jnp.zeros((rows, 1), jnp.int32))
    thr = pltpu.bitcast(thr_bits, F32)
    gt = a >= pltpu.bitcast(thr_bits + 1, F32)
    eq = (a >= thr) & jnp.logical_not(gt)
    need = cap - jnp.sum(gt.astype(jnp.int32), axis=1, keepdims=True)
    eq_i = eq.astype(jnp.int32)
    sel = gt | (eq & (_prefix_count(eq_i, n) - eq_i < need))
    sel_i = sel.astype(jnp.int32)
    lane = lax.broadcasted_iota(jnp.int32, a.shape, 1)
    dist = jnp.where(sel, lane + 1 - _prefix_count(sel_i, n), 0)
    tok = jnp.where(sel, lane, -1)
    val = a
    s = 1
    while s < n:
        tok_in = pltpu.roll(tok, n - s, 1)
        dist_in = pltpu.roll(dist, n - s, 1)
        val_in = pltpu.roll(val, n - s, 1)
        take = (lane < n - s) & (tok_in >= 0) & ((dist_in & s) != 0)
        stay = (tok >= 0) & ((dist & s) == 0)
        tok = jnp.where(take, tok_in, jnp.where(stay, tok, -1))
        dist = jnp.where(take, dist_in, jnp.where(stay, dist, 0))
        val = jnp.where(take, val_in, val)
        s *= 2
    idx_ref[...] = tok[:, :cap]
    gate_ref[...] = val[:, :cap]
    olane = lax.broadcasted_iota(jnp.int32, off_ref.shape, 1)
    off = jnp.zeros(off_ref.shape, jnp.int32)
    for j in range(n // COMBINE_TB + 1):
        below = jnp.sum(jnp.where(lane < j * COMBINE_TB, sel_i, 0), axis=1, keepdims=True)
        off = jnp.where(olane == j, below, off)
    off_ref[...] = off


def _topk_select(aff, cap):
    b, e, n = aff.shape
    assert n & (n - 1) == 0 and n // COMBINE_TB < LANES
    rows = b * e
    full = lambda w: pl.BlockSpec((rows, w), lambda i: (0, 0))
    idx, gate, off = pl.pallas_call(
        functools.partial(_topk_kernel, cap=cap),
        grid=(1,),
        in_specs=[full(n)],
        out_specs=[full(cap), full(cap), full(LANES)],
        out_shape=[jax.ShapeDtypeStruct((rows, cap), jnp.int32), jax.ShapeDtypeStruct((rows, cap), F32),
                   jax.ShapeDtypeStruct((rows, LANES), jnp.int32)],
        compiler_params=_cparams(("arbitrary",)),
        name="expert_choice_topk",
    )(aff.reshape(rows, n))
    return idx.reshape(b, e, cap), gate.reshape(b, e, cap), off


def _combine_kernel(off_ref, y_ref, idx_ref, x_ref, g_ref, o_ref, acc, ycat, *, chunk, n_chunks):
    bi = pl.program_id(0)
    j = pl.program_id(1)
    tb = x_ref.shape[0]
    tok = j * tb + lax.broadcasted_iota(jnp.int32, (tb, chunk), 0)

    def onehot(e, st):
        return (idx_ref[0, pl.ds(e, 1), pl.ds(st, chunk)] == tok).astype(BF16)

    if n_chunks == 1:
        acc[...] = jnp.zeros_like(acc)
        for e in range(N_EXPERTS):
            acc[...] += jnp.dot(onehot(e, 0), y_ref[e], preferred_element_type=F32)
    else:
        cap = n_chunks * chunk
        shift = chunk.bit_length() - 1
        col = lax.broadcasted_iota(jnp.int32, (1, chunk), 1)

        def slot_ids(e, st):
            base = jnp.minimum(lax.shift_left(lax.shift_right_logical(st, shift), shift), cap - 2 * chunk)
            wide = idx_ref[0, pl.ds(e, 1), pl.ds(pl.multiple_of(base, chunk), 2 * chunk)]
            wide = jnp.broadcast_to(wide, (8, 2 * chunk))
            return pltpu.roll(wide, (2 * chunk - (st - base)) & (2 * chunk - 1), 1)[:1, :chunk]

        starts, ends, hots = [], [], []
        for e in range(N_EXPERTS):
            lo = off_ref[bi * N_EXPERTS + e, j]
            hi = off_ref[bi * N_EXPERTS + e, j + 1]
            st = jnp.minimum(lax.shift_left(lax.shift_right_logical(lo, 4), 4), cap - chunk)
            starts.append(st)
            ends.append(hi)
            ycat[pl.ds(e * chunk, chunk), :] = y_ref[e, pl.ds(pl.multiple_of(st, 16), chunk), :]
            hots.append((slot_ids(e, st) == tok).astype(BF16))
        acc[...] = jnp.dot(jnp.concatenate(hots, axis=1), ycat[...], preferred_element_type=F32)
        for e in range(N_EXPERTS):
            def add_window(c, carry, e=e):
                first = starts[e] + c * chunk
                st = jnp.minimum(first, cap - chunk)
                hot = ((slot_ids(e, st) == tok) & (st + col >= first)).astype(BF16)
                acc[...] += jnp.dot(hot, y_ref[e, pl.ds(pl.multiple_of(st, 16), chunk), :],
                                    preferred_element_type=F32)
                return carry

            n_more = lax.shift_right_logical(jnp.maximum(ends[e] - starts[e] - 1, 0), shift)
            lax.fori_loop(1, n_more + 1, add_window, 0)
    o_ref[...] = x_ref[...] + g_ref[0] * acc[...]


def _moe_combine(off, y, idx, x2, g, *, row_block0, cap, seq_len):
    bsz = idx.shape[0]
    d = x2.shape[1]
    tb = min(seq_len, COMBINE_TB)
    nb = seq_len // tb
    chunk = min(cap, COMBINE_CHUNK)
    n_chunks = cap // chunk
    n_mod = g.shape[0]
    grid_spec = pltpu.PrefetchScalarGridSpec(
        num_scalar_prefetch=1,
        grid=(bsz, nb),
        in_specs=[pl.BlockSpec((N_EXPERTS, cap, d), lambda bi, j, off: (0, row_block0 + bi, 0),
                               pipeline_mode=pl.Buffered(1)),
                  pl.BlockSpec((1, N_EXPERTS, cap), lambda bi, j, off: (bi, 0, 0)),
                  pl.BlockSpec((tb, d), lambda bi, j, off: (bi * nb + j, 0)),
                  pl.BlockSpec((1, 1, d), lambda bi, j, off: (bi if n_mod > 1 else 0, 0, 0))],
        out_specs=pl.BlockSpec((tb, d), lambda bi, j, off: (bi * nb + j, 0)),
        scratch_shapes=[pltpu.VMEM((tb, d), F32), pltpu.VMEM((N_EXPERTS * chunk, d), BF16)])
    return pl.pallas_call(
        functools.partial(_combine_kernel, chunk=chunk, n_chunks=n_chunks),
        grid_spec=grid_spec,
        out_shape=jax.ShapeDtypeStruct(x2.shape, F32),
        compiler_params=_cparams(("arbitrary", "arbitrary")),
        name="moe_combine",
    )(off, y, idx, x2, g)


def _expert_choice_moe(streams, w_gate, w_up, w_down, layer):
    routed = []
    for h2, aff, x2, g in streams:
        bsz, _, l = aff.shape
        cap = EC_FACTOR * l // N_EXPERTS
        idx, gate, off = _topk_select(aff, cap)
        rows = (idx + (jnp.arange(bsz, dtype=jnp.int32) * l)[:, None, None]).transpose(1, 0, 2)
        routed.append((idx, off, rows.reshape(N_EXPERTS, 1, bsz * cap), gate.transpose(1, 0, 2).reshape(
            N_EXPERTS, bsz * cap, 1), cap))
    y = _expert_ffn([r[2] for r in routed], [s[0] for s in streams],
                    jnp.concatenate([r[3] for r in routed], axis=1), w_gate, w_up, w_down, layer)
    outs = []
    row0 = 0
    for (h2, aff, x2, g), (idx, off, rows, _, cap) in zip(streams, routed):
        assert row0 % cap == 0
        outs.append(_moe_combine(off, y, idx, x2, g, row_block0=row0 // cap, cap=cap, seq_len=aff.shape[2]))
        row0 += rows.shape[2]
    return outs


HY_R = 128
HY_COLS = 4096
HY_KB = 8
HY_UNROLL = 8


def _dft_consts(r):
    k = np.arange(r)
    ang = 2.0 * np.pi * np.outer(k, k) / r
    fre, fim = np.cos(ang), -np.sin(ang)
    k2 = np.arange(r)
    tang = 2.0 * np.pi * np.outer(k2, k2) / (r * r)
    tw = np.stack([np.cos(tang), -np.sin(tang)])
    fwd = np.block([[fre, -fim], [fim, fre]])
    inv = np.block([[fre, fim], [-fim, fre]])
    return fre, fim, tw, fwd, inv


def _hy_prep_kernel(x_ref, prev_ref, next_ref, w_ref, b_ref, x1_ref, x2_ref, v_ref, *, n_tiles):
    t = pl.program_id(1)
    x = x_ref[0]
    tm = x.shape[0]
    row = lax.broadcasted_iota(jnp.int32, x.shape, 0)
    prev = jnp.where(t == 0, 0.0, prev_ref[0, pl.ds(7, 1), :])
    nxt = jnp.where(t == n_tiles - 1, 0.0, next_ref[0, pl.ds(0, 1), :])
    up = jnp.where(row == 0, prev, pltpu.roll(x, 1, 0))
    dn = jnp.where(row == tm - 1, nxt, pltpu.roll(x, tm - 1, 0))
    y = b_ref[...] + up * w_ref[pl.ds(0, 1), :] + x * w_ref[pl.ds(1, 1), :] + dn * w_ref[pl.ds(2, 1), :]
    x1_ref[0] = y[:, :HY_W].astype(BF16)
    x2_ref[0] = y[:, HY_W:2 * HY_W].astype(BF16)
    v_ref[0] = y[:, 2 * HY_W:].astype(BF16)


def _hyena_prep(u, conv_w, conv_b):
    b, l, c = u.shape
    tm = min(l, 1024)
    n_tiles = l // tm
    g = tm // 8
    n_groups = l // 8
    out = jax.ShapeDtypeStruct((b, l, HY_W), BF16)
    ospec = pl.BlockSpec((1, tm, HY_W), lambda bi, t: (bi, t, 0))
    return pl.pallas_call(
        functools.partial(_hy_prep_kernel, n_tiles=n_tiles),
        grid=(b, n_tiles),
        in_specs=[pl.BlockSpec((1, tm, c), lambda bi, t: (bi, t, 0)),
                  pl.BlockSpec((1, 8, c), lambda bi, t: (bi, jnp.maximum(t * g - 1, 0), 0)),
                  pl.BlockSpec((1, 8, c), lambda bi, t: (bi, jnp.minimum((t + 1) * g, n_groups - 1), 0)),
                  pl.BlockSpec((HY_SHORT, c), lambda bi, t: (0, 0)),
                  pl.BlockSpec((1, c), lambda bi, t: (0, 0))],
        out_specs=[ospec, ospec, ospec],
        out_shape=[out, out, out],
        compiler_params=_cparams(("arbitrary", "arbitrary")),
        name="hyena_prep",
    )(u, u, u, conv_w, conv_b.reshape(1, c))


def _hy_filter_kernel(z_ref, w1_ref, b1_ref, w2_ref, b2_ref, w3_ref, fr_ref, dl_ref, k_ref, ss_ref, *, seq_len):
    i = pl.program_id(0)
    hp = lax.Precision.HIGHEST
    z = z_ref[...]
    tm = z.shape[0]
    half = tm // 2
    fr = fr_ref[...]
    zz = jnp.concatenate([z[:half], z[half:]], axis=1)
    h = jnp.sin(fr * (jnp.dot(zz, w1_ref[...], preferred_element_type=F32, precision=hp) + b1_ref[...]))
    h = jnp.sin(fr * (jnp.dot(h, w2_ref[...], preferred_element_type=F32, precision=hp) + b2_ref[...]))
    hb = h.astype(BF16)
    w3 = w3_ref[...].astype(BF16)
    fo = w3.shape[0]
    h = jnp.concatenate([jnp.dot(hb[:, :fo], w3, preferred_element_type=F32),
                         jnp.dot(hb[:, fo:], w3, preferred_element_type=F32)], axis=0)
    dec = jnp.exp(-z[:, 0:1] * dl_ref[...])
    r = i * tm + lax.broadcasted_iota(jnp.int32, (tm, HY_W), 0)
    parts = []
    for o in range(HY_ORDER):
        hf = h[:, (2 * o) * HY_W:(2 * o + 1) * HY_W]
        hb = h[:, (2 * o + 1) * HY_W:(2 * o + 2) * HY_W]
        parts.append(jnp.where(r < seq_len, hf, jnp.where(r > seq_len, hb, 0.0)) * dec)
    k = jnp.concatenate(parts, axis=1)
    k_ref[...] = k
    ss = jnp.sum(k * k, axis=0, keepdims=True)

    @pl.when(i == 0)
    def _():
        ss_ref[...] = ss

    @pl.when(i != 0)
    def _():
        ss_ref[...] += ss


def _hyena_filter_taps(l, w1, b1, w2, b2, w3, freq):
    t = jnp.linspace(0.0, 1.0, l, dtype=F32)[:, None]
    w = (2.0 * math.pi / l) * jnp.arange(l, dtype=F32)[:, None]
    f = jnp.linspace(1e-4, HY_BANDS - 1, HY_BANDS, dtype=F32)[None, :]
    z = jnp.concatenate([t, jnp.cos(f * w), -jnp.sin(f * w)], axis=-1)
    z2 = jnp.concatenate([z, z[:1], z[:0:-1]], axis=0)
    emb = z.shape[1]
    z2 = jnp.pad(z2, ((0, 0), (0, LANES - emb)))
    w1p = jnp.pad(w1, ((0, LANES - emb), (0, 0)))
    two = lambda m: jnp.kron(jnp.eye(2, dtype=m.dtype), m)
    pair = lambda v: jnp.tile(v, 2).reshape(1, 2 * v.shape[0])
    deltas = jnp.abs(jnp.linspace(math.log(HY_DECAY_TARGET) / HY_SLOW_DECAY,
                                  math.log(HY_DECAY_TARGET) / HY_FAST_DECAY, HY_W, dtype=F32))
    fo = w2.shape[0]
    nout = w3.shape[1]
    tm = min(2 * l, 1024)
    const = lambda i: (0, 0)
    return pl.pallas_call(
        functools.partial(_hy_filter_kernel, seq_len=l),
        grid=(2 * l // tm,),
        in_specs=[pl.BlockSpec((tm, LANES), lambda i: (i, 0)),
                  pl.BlockSpec((2 * LANES, 2 * fo), const), pl.BlockSpec((1, 2 * fo), const),
                  pl.BlockSpec((2 * fo, 2 * fo), const), pl.BlockSpec((1, 2 * fo), const),
                  pl.BlockSpec((fo, nout), const), pl.BlockSpec((1, 2 * fo), const),
                  pl.BlockSpec((1, HY_W), const)],
        out_specs=[pl.BlockSpec((tm, HY_ORDER * HY_W), lambda i: (i, 0)),
                   pl.BlockSpec((1, HY_ORDER * HY_W), const)],
        out_shape=[jax.ShapeDtypeStruct((2 * l, HY_ORDER * HY_W), F32),
                   jax.ShapeDtypeStruct((1, HY_ORDER * HY_W), F32)],
        compiler_params=_cparams(("arbitrary",)),
        name="hyena_filter_taps",
    )(z2, two(w1p), pair(b1), two(w2), pair(b2), w3, pair(freq), deltas.reshape(1, HY_W))


def _left_matmul_kernel(f_ref, x_ref, o_ref, *, width):
    res = jnp.dot(f_ref[...], x_ref[0].astype(BF16), preferred_element_type=F32)
    for q in range(HY_COLS // width):
        o_ref[0, :, q, :] = res[:, q * width:(q + 1) * width]


def _left_matmul(f_bf, x3, width, name):
    b, k, c = x3.shape
    m = f_bf.shape[0]
    nq = HY_COLS // width
    return pl.pallas_call(
        functools.partial(_left_matmul_kernel, width=width),
        grid=(b, c // HY_COLS),
        in_specs=[pl.BlockSpec((m, k), lambda bi, j: (0, 0)),
                  pl.BlockSpec((1, k, HY_COLS), lambda bi, j: (bi, 0, j))],
        out_specs=pl.BlockSpec((1, m, nq, width), lambda bi, j: (bi, 0, j, 0)),
        out_shape=jax.ShapeDtypeStruct((b, m, c // width, width), F32),
        compiler_params=_cparams(("arbitrary", "arbitrary")),
        name=name,
    )(f_bf, x3)


def _twiddle_cols(tw_ref, k1):
    lane = lax.broadcasted_iota(jnp.int32, (HY_R, HY_R), 1)
    pick = lane == k1
    twr = jnp.sum(jnp.where(pick, tw_ref[0], 0.0), axis=1, keepdims=True)
    twi = jnp.sum(jnp.where(pick, tw_ref[1], 0.0), axis=1, keepdims=True)
    return twr, twi


def _hy_spectrum_kernel(a_ref, tw_ref, fwd_ref, ss_ref, kf_ref):
    kb = pl.program_id(0)
    scale = lax.rsqrt(ss_ref[...] + EPS) * (1.0 / (HY_R * HY_R))

    def body(kk, carry):
        twr, twi = _twiddle_cols(tw_ref, kb * HY_KB + kk)
        are = a_ref[0, 0, kk].astype(F32)
        aim = a_ref[0, 1, kk].astype(F32)
        bst = jnp.concatenate([are * twr - aim * twi, are * twi + aim * twr], axis=0).astype(BF16)
        x = jnp.dot(fwd_ref[...], bst, preferred_element_type=F32)
        kf_ref[0, kk] = x[:HY_R] * scale
        kf_ref[1, kk] = x[HY_R:] * scale
        return carry

    lax.fori_loop(0, HY_KB, body, 0, unroll=HY_UNROLL)


def _hy_spectrum(a5, tw, fwd_bf, ss):
    nch = a5.shape[-1]
    return pl.pallas_call(
        _hy_spectrum_kernel,
        grid=(HY_R // HY_KB,),
        in_specs=[pl.BlockSpec((1, 2, HY_KB, HY_R, nch), lambda kb: (0, 0, kb, 0, 0)),
                  pl.BlockSpec((2, HY_R, HY_R), lambda kb: (0, 0, 0)),
                  pl.BlockSpec((2 * HY_R, 2 * HY_R), lambda kb: (0, 0)),
                  pl.BlockSpec((1, nch), lambda kb: (0, 0))],
        out_specs=pl.BlockSpec((2, HY_KB, HY_R, nch), lambda kb: (0, kb, 0, 0)),
        out_shape=jax.ShapeDtypeStruct((2, HY_R, HY_R, nch), F32),
        compiler_params=_cparams(("arbitrary",)),
        name="hyena_filter_spectrum",
    )(a5, tw, fwd_bf, ss)


def _hy_mid_kernel(a_ref, kf_ref, tw_ref, fwd_ref, inv_ref, z_ref):
    kb = pl.program_id(0)

    def body(kk, carry):
        twr, twi = _twiddle_cols(tw_ref, kb * HY_KB + kk)
        are = a_ref[0, 0, kk].astype(F32)
        aim = a_ref[0, 1, kk].astype(F32)
        bst = jnp.concatenate([are * twr - aim * twi, are * twi + aim * twr], axis=0).astype(BF16)
        x = jnp.dot(fwd_ref[...], bst, preferred_element_type=F32)
        xre, xim = x[:HY_R], x[HY_R:]
        kre, kim = kf_ref[0, kk], kf_ref[1, kk]
        yst = jnp.concatenate([xre * kre - xim * kim, xre * kim + xim * kre], axis=0).astype(BF16)
        zz = jnp.dot(inv_ref[...], yst, preferred_element_type=F32)
        zre, zim = zz[:HY_R], zz[HY_R:]
        z_ref[0, 0, kk] = (zre * twr + zim * twi).astype(BF16)
        z_ref[0, 1, kk] = (zim * twr - zre * twi).astype(BF16)
        return carry

    lax.fori_loop(0, HY_KB, body, 0, unroll=HY_UNROLL)


def _hy_mid(a5, kf, order, tw, fwd_bf, inv_bf):
    b = a5.shape[0]
    blk = (1, 2, HY_KB, HY_R, HY_W)
    return pl.pallas_call(
        _hy_mid_kernel,
        grid=(HY_R // HY_KB, b),
        in_specs=[pl.BlockSpec(blk, lambda kb, bi: (bi, 0, kb, 0, 0)),
                  pl.BlockSpec((2, HY_KB, HY_R, HY_W), lambda kb, bi: (0, kb, 0, order)),
                  pl.BlockSpec((2, HY_R, HY_R), lambda kb, bi: (0, 0, 0)),
                  pl.BlockSpec((2 * HY_R, 2 * HY_R), lambda kb, bi: (0, 0)),
                  pl.BlockSpec((2 * HY_R, 2 * HY_R), lambda kb, bi: (0, 0))],
        out_specs=pl.BlockSpec(blk, lambda kb, bi: (bi, 0, kb, 0, 0)),
        out_shape=jax.ShapeDtypeStruct(a5.shape, BF16),
        compiler_params=_cparams(("arbitrary", "arbitrary")),
        name="hyena_spectral_product",
    )(a5, kf, tw, fwd_bf, inv_bf)


def _hy_out_kernel(f_ref, z_ref, g_ref, w_ref, sk_ref, o_ref):
    y = jnp.dot(f_ref[...], z_ref[0], preferred_element_type=F32)
    w = w_ref[0].astype(F32)
    o_ref[0] = (g_ref[0].astype(F32) * (y + w * sk_ref[...])).astype(BF16)


def _hy_out(f_bf, z3, gate3, w3, skip_row):
    b, m2, c = z3.shape
    m = f_bf.shape[0]
    dspec = pl.BlockSpec((1, m, HY_COLS), lambda bi, j: (bi, 0, j))
    return pl.pallas_call(
        _hy_out_kernel,
        grid=(b, c // HY_COLS),
        in_specs=[pl.BlockSpec((m, m2), lambda bi, j: (0, 0)),
                  pl.BlockSpec((1, m2, HY_COLS), lambda bi, j: (bi, 0, j)),
                  dspec, dspec,
                  pl.BlockSpec((1, HY_COLS), lambda bi, j: (0, 0))],
        out_specs=dspec,
        out_shape=jax.ShapeDtypeStruct((b, m, c), BF16),
        compiler_params=_cparams(("arbitrary", "arbitrary")),
        name="hyena_inverse_gate",
    )(f_bf, z3, gate3, w3, skip_row)


def _hyena_long(u, conv_w, conv_b, skip, filt_args):
    b, l, _ = u.shape
    assert 2 * l == HY_R * HY_R
    fre, fim, tw, fwd, inv = _dft_consts(HY_R)
    half = HY_R // 2
    as_bf = lambda m: jnp.asarray(m, F32).astype(BF16)
    f_a_data = as_bf(np.concatenate([fre[:, :half], fim[:, :half]], axis=0))
    f_a_filt = as_bf(np.concatenate([fre, fim], axis=0))
    f_c = as_bf(np.concatenate([fre[:half], fim[:half]], axis=1))
    tw_j = jnp.asarray(tw, F32)
    fwd_bf = as_bf(fwd)
    inv_bf = as_bf(inv)

    taps, ss = _hyena_filter_taps(l, *filt_args)
    nch = HY_ORDER * HY_W
    ak = _left_matmul(f_a_filt, taps.reshape(1, HY_R, HY_R * nch), nch, "hyena_filter_dft_a")
    kf = _hy_spectrum(ak.reshape(1, 2, HY_R, HY_R, nch), tw_j, fwd_bf, ss)

    x1, x2, v = _hyena_prep(u, conv_w, conv_b)
    z = v
    for o, gate in enumerate((x1, x2)):
        z3 = z.reshape(b, half, HY_R * HY_W)
        a = _left_matmul(f_a_data, z3, HY_W, "hyena_dft_a")
        zs = _hy_mid(a.reshape(b, 2, HY_R, HY_R, HY_W), kf, o, tw_j, fwd_bf, inv_bf)
        skip_row = jnp.tile(skip[o], HY_COLS // HY_W).reshape(1, HY_COLS)
        z = _hy_out(f_c, zs.reshape(b, 2 * HY_R, HY_R * HY_W), gate.reshape(b, half, HY_R * HY_W), z3,
                    skip_row).reshape(b, l, HY_W)
    return z


def _hy_ctx_kernel(x1_ref, x2_ref, v_ref, taps_ref, ss_ref, fa_ref, fc_ref, sk_ref, o_ref, *, seq_len):
    n2 = 2 * seq_len
    fa = fa_ref[...]
    scale = lax.rsqrt(ss_ref[...] + EPS) * (1.0 / n2)
    kf = jnp.dot(fa, taps_ref[...].astype(BF16), preferred_element_type=F32) * scale
    z = v_ref[0].astype(F32)
    for o, g_ref in enumerate((x1_ref, x2_ref)):
        x = jnp.dot(fa[:, :seq_len], z.astype(BF16), preferred_element_type=F32)
        xre, xim = x[:n2], x[n2:]
        kre = kf[:n2, o * HY_W:(o + 1) * HY_W]
        kim = kf[n2:, o * HY_W:(o + 1) * HY_W]
        yst = jnp.concatenate([xre * kre - xim * kim, xre * kim + xim * kre], axis=0).astype(BF16)
        y = jnp.dot(fc_ref[...], yst, preferred_element_type=F32)
        z = g_ref[0].astype(F32) * (y + z * sk_ref[pl.ds(o, 1), :])
    o_ref[0] = z.astype(BF16)


def _hyena_short(u, conv_w, conv_b, skip, filt_args):
    b, l, _ = u.shape
    n2 = 2 * l
    k = np.arange(n2)
    ang = 2.0 * np.pi * np.outer(k, k) / n2
    fre, fim = np.cos(ang), -np.sin(ang)
    fa = jnp.asarray(np.concatenate([fre, fim], axis=0), F32).astype(BF16)
    fc = jnp.asarray(np.concatenate([fre[:l], fim[:l]], axis=1), F32).astype(BF16)
    taps, ss = _hyena_filter_taps(l, *filt_args)
    x1, x2, v = _hyena_prep(u, conv_w, conv_b)
    nch = HY_ORDER * HY_W
    dspec = pl.BlockSpec((1, l, HY_W), lambda bi: (bi, 0, 0))
    const = lambda bi: (0, 0)
    return pl.pallas_call(
        functools.partial(_hy_ctx_kernel, seq_len=l),
        grid=(b,),
        in_specs=[dspec, dspec, dspec,
                  pl.BlockSpec((n2, nch), const), pl.BlockSpec((1, nch), const),
                  pl.BlockSpec((2 * n2, n2), const), pl.BlockSpec((l, 2 * n2), const),
                  pl.BlockSpec((HY_ORDER, HY_W), const)],
        out_specs=dspec,
        out_shape=jax.ShapeDtypeStruct((b, l, HY_W), BF16),
        compiler_params=_cparams(("arbitrary",)),
        name="hyena_context",
    )(x1, x2, v, taps, ss, fa, fc, skip)


def _rope_tables(n):
    t = jnp.arange(n, dtype=jnp.int32)
    pos = jnp.stack([t // GRID_W, t % GRID_W], axis=-1).astype(F32)
    nf = HEAD_DIM // 4
    inv = ROPE_THETA ** (-jnp.arange(nf, dtype=F32) / nf)
    ang = pos[:, :, None] * inv
    cos, sin = jnp.cos(ang), jnp.sin(ang)
    c64 = jnp.concatenate([cos[:, 0], cos[:, 0], cos[:, 1], cos[:, 1]], axis=-1)
    s64 = jnp.concatenate([-sin[:, 0], sin[:, 0], -sin[:, 1], sin[:, 1]], axis=-1)
    return jnp.tile(c64, (1, 2)), jnp.tile(s64, (1, 2))


def _block_diag(pool_w):
    g, c, _ = pool_w.shape
    out = jnp.zeros((g * c, g * c), pool_w.dtype)
    for i in range(g):
        out = out.at[i * c:(i + 1) * c, i * c:(i + 1) * c].set(pool_w[i])
    return out


def kernel(x, c, ctx, c_ctx, w_mod, b_mod, norm1_w, norm2_w, w_in, w_out, q_norm_w, k_norm_w, na_rpb,
           hy_conv_w, hy_conv_b, hy_w1, hy_b1, hy_w2, hy_b2, hy_w3, hy_freq, hy_skip,
           pool_w, pool_scale, w_router, w_gate, w_up, w_down):
    b, n, d = x.shape
    lc = ctx.shape[1]
    depth = w_mod.shape[0]
    rows = n // GRID_W
    cos128, sin128 = _rope_tables(n)
    seg = jnp.asarray(np.kron(np.eye(2), np.full((HEAD_DIM, HEAD_DIM), 1.0 / HEAD_DIM)), BF16)
    cc = jnp.zeros((8, d), F32).at[:b].set(c).at[b].set(c_ctx)

    xl = x.reshape(b * n, d)
    xc = ctx.reshape(b * lc, d)
    for i in range(depth):
        last = i == depth - 1
        mod = _modulation(cc, w_mod, b_mod[i], i)
        ml = mod[:b].reshape(b, 1, 6, d)
        mc = mod[b].reshape(1, 1, 6, d)
        sh1, sc1, g1, sh2, sc2, g2 = [ml[:, :, j] for j in range(6)]
        csh1, csc1, cg1, csh2, csc2, cg2 = [mc[:, :, j] for j in range(6)]
        w_in_bf = w_in[i].astype(BF16)
        w_out_bf = w_out[i].astype(BF16)
        w_router_t = w_router[i].T
        pool_bd = _block_diag(pool_w[i]).astype(BF16)
        hy_args = (hy_w1[i], hy_b1[i], hy_w2[i], hy_b2[i], hy_w3[i], hy_freq[i])

        qc, kc, vc, hyc, poolc = _in_projection(
            xc, csh1, csc1, norm1_w[i], w_in_bf, q_norm_w[i], k_norm_w[i], cos128, sin128, seg,
            rows_per_mod=b * lc, seq_len=lc, tm=lc, rope=False)
        kc = kc.reshape(b, lc, NA_W)
        vc = vc.reshape(b, lc, NA_W)
        if not last:
            na_c = _dense_attention(qc.reshape(b, lc, NA_W), kc, vc).reshape(b * lc, NA_W)
            hy_c = _hyena_short(hyc.reshape(b, lc, 3 * HY_W), hy_conv_w[i], hy_conv_b[i], hy_skip[i], hy_args)
            pool_c = _pool_mixer(poolc.reshape(b, lc, POOL_W), pool_bd, pool_scale[i])
            xc1, h2c, affc = _out_projection(
                na_c, hy_c.reshape(b * lc, HY_W).astype(BF16), pool_c.reshape(b * lc, POOL_W), w_out_bf,
                xc, cg1, norm2_w[i], csh2, csc2, w_router_t, rows_per_mod=b * lc, seq_len=lc, tm=lc)

        qr, qp, kl, vl, hyl, pooll = _in_projection(
            xl, sh1, sc1, norm1_w[i], w_in_bf, q_norm_w[i], k_norm_w[i], cos128, sin128, seg,
            rows_per_mod=n, seq_len=n, tm=512, rope=True)
        bias = _na_bias_tables(na_rpb[i], rows)
        o_na = _neighborhood_attention(qr.reshape(b, n, NA_W), qp.reshape(b, n, NA_W), kl.reshape(b, n, NA_W),
                                       vl.reshape(b, n, NA_W), kc, vc, bias)
        hy_l = _hyena_long(hyl.reshape(b, n, 3 * HY_W), hy_conv_w[i], hy_conv_b[i], hy_skip[i], hy_args)
        pool_l = _pool_mixer(pooll.reshape(b, n, POOL_W), pool_bd, pool_scale[i])
        xl1, h2l, affl = _out_projection(
            o_na.reshape(b * n, NA_W), hy_l.reshape(b * n, HY_W).astype(BF16), pool_l.reshape(b * n, POOL_W),
            w_out_bf, xl, g1, norm2_w[i], sh2, sc2, w_router_t, rows_per_mod=n, seq_len=n, tm=512)
        streams = [(h2l, affl, xl1, g2)]
        if not last:
            streams.append((h2c, affc, xc1, cg2))
        outs = _expert_choice_moe(streams, w_gate, w_up, w_down, i)
        xl = outs[0]
        if not last:
            xc = outs[1]
    return xl.reshape(b, n, d)
```

```python
import functools
import math

import jax
import jax.numpy as jnp
import numpy as np
from jax import lax
from jax.experimental import pallas as pl
from jax.experimental.pallas import tpu as pltpu

F32 = jnp.float32
BF16 = jnp.bfloat16

D_MODEL = 1024
GRID_W = 64
NA_HEADS = 8
HEAD_DIM = 64
NA_W = NA_HEADS * HEAD_DIM
NA_WIN_R = 8
NA_WIN_C = 16
ROPE_THETA = 10000.0
HY_W = 256
HY_ORDER = 2
HY_SHORT = 3
HY_BANDS = 16
HY_DECAY_TARGET = 1e-2
HY_FAST_DECAY = 0.3
HY_SLOW_DECAY = 1.5
POOL_W = 256
POOL_SIZES = (2, 4, 8, 16)
POOL_GROUP = POOL_W // len(POOL_SIZES)
MIX_W = NA_W + HY_W + POOL_W
IN_W = 3 * NA_W + (HY_ORDER + 1) * HY_W + POOL_W
N_EXPERTS = 16
EC_FACTOR = 2
D_EXPERT = 2048
EPS = 1e-6

LANES = 128
TOKEN_SUBLANES = D_MODEL // 128
ROW_SUBTILES = 2
NEG_BIG = -1e30
VMEM_LIMIT = 56 * 1024 * 1024

NA_Q_ROWS = 4
NA_K_ROWS = NA_Q_ROWS + NA_WIN_R - 1
NA_TILES_PER_STEP = 4


def _cparams(sem):
    return pltpu.CompilerParams(dimension_semantics=sem, vmem_limit_bytes=VMEM_LIMIT)


def _mod_kernel(c_ref, w_ref, b_ref, o_ref):
    c = c_ref[...]
    s = c * jax.nn.sigmoid(c)
    o_ref[...] = jnp.dot(s, w_ref[0], preferred_element_type=F32,
                         precision=lax.Precision.HIGHEST) + b_ref[...]


def _modulation(cc, w_mod, b_mod, layer):
    rows, d = cc.shape
    n = w_mod.shape[2]
    tn = 1536
    return pl.pallas_call(
        _mod_kernel,
        grid=(n // tn,),
        in_specs=[pl.BlockSpec((rows, d), lambda j: (0, 0)),
                  pl.BlockSpec((1, d, tn), lambda j: (layer, 0, j)),
                  pl.BlockSpec((1, tn), lambda j: (0, j))],
        out_specs=pl.BlockSpec((rows, tn), lambda j: (0, j)),
        out_shape=jax.ShapeDtypeStruct((rows, n), F32),
        compiler_params=_cparams(("arbitrary",)),
        name="modulation",
    )(cc, w_mod, b_mod.reshape(1, n))


def _head_norm(xs, w128, seg):
    m = jnp.dot((xs * xs).astype(BF16), seg, preferred_element_type=F32)
    return xs * lax.rsqrt(m + EPS) * w128


def _rope(xs, c, s_signed):
    lane = lax.broadcasted_iota(jnp.int32, xs.shape, 1)
    partner = jnp.where((lane & 16) == 0, pltpu.roll(xs, LANES - 16, 1), pltpu.roll(xs, 16, 1))
    return xs * c + partner * s_signed


def _inproj_kernel(x_ref, sh_ref, sc_ref, nw_ref, w_ref, qw_ref, kw_ref, cos_ref, sin_ref, seg_ref,
                   *out_refs, rope):
    if rope:
        qr_ref, qp_ref, k_ref, v_ref, hy_ref, pool_ref = out_refs
    else:
        qp_ref, k_ref, v_ref, hy_ref, pool_ref = out_refs
    seg = seg_ref[...]
    qk_scale = HEAD_DIM ** -0.5
    sub = x_ref.shape[0] // ROW_SUBTILES
    for t in range(ROW_SUBTILES):
        rs = pl.ds(t * sub, sub)
        x = x_ref[rs, :]
        ms = jnp.mean(x * x, axis=-1, keepdims=True)
        h = x * lax.rsqrt(ms + EPS) * nw_ref[...]
        h = h * (1.0 + sc_ref[0]) + sh_ref[0]
        u = jnp.dot(h.astype(BF16), w_ref[...], preferred_element_type=F32)
        for ch in range(NA_W // LANES):
            sl = slice(ch * LANES, (ch + 1) * LANES)
            qn = _head_norm(u[:, sl], qw_ref[...], seg)
            kn = _head_norm(u[:, NA_W + ch * LANES:NA_W + (ch + 1) * LANES], kw_ref[...], seg)
            qp_ref[rs, sl] = (qn * qk_scale).astype(BF16)
            if rope:
                c = cos_ref[rs, :]
                s = sin_ref[rs, :]
                qr_ref[rs, sl] = (_rope(qn, c, s) * qk_scale).astype(BF16)
                k_ref[rs, sl] = _rope(kn, c, s).astype(BF16)
            else:
                k_ref[rs, sl] = kn.astype(BF16)
        v_ref[rs, :] = u[:, 2 * NA_W:3 * NA_W].astype(BF16)
        hy_ref[rs, :] = u[:, 3 * NA_W:3 * NA_W + 3 * HY_W].astype(BF16)
        pool_ref[rs, :] = u[:, 3 * NA_W + 3 * HY_W:].astype(BF16)


def _in_projection(x2, shift, scale, norm_w, w_in_bf, q_norm_w, k_norm_w, cos128, sin128, seg, *,
                   rows_per_mod, seq_len, tm, rope):
    m, d = x2.shape
    tiles_per_mod = rows_per_mod // tm
    tiles_per_seq = seq_len // tm
    row = lambda i: (i, 0)
    const = lambda i: (0, 0)
    modi = lambda i: (i // tiles_per_mod, 0, 0)
    pos = lambda i: (i % tiles_per_seq, 0)
    outs = []
    if rope:
        outs.append((NA_W, BF16))
    outs += [(NA_W, BF16), (NA_W, BF16), (NA_W, BF16), (3 * HY_W, BF16), (POOL_W, BF16)]
    return pl.pallas_call(
        functools.partial(_inproj_kernel, rope=rope),
        grid=(m // tm,),
        in_specs=[pl.BlockSpec((tm, d), row),
                  pl.BlockSpec((1, 1, d), modi), pl.BlockSpec((1, 1, d), modi),
                  pl.BlockSpec((1, d), const),
                  pl.BlockSpec((d, IN_W), const),
                  pl.BlockSpec((1, LANES), const), pl.BlockSpec((1, LANES), const),
                  pl.BlockSpec((tm, LANES), pos), pl.BlockSpec((tm, LANES), pos),
                  pl.BlockSpec((LANES, LANES), const)],
        out_specs=[pl.BlockSpec((tm, w), row) for w, _ in outs],
        out_shape=[jax.ShapeDtypeStruct((m, w), dt) for w, dt in outs],
        compiler_params=_cparams(("arbitrary",)),
        name="in_projection_rope" if rope else "in_projection_ctx",
    )(x2, shift, scale, norm_w.reshape(1, d), w_in_bf,
      jnp.tile(q_norm_w, 2).reshape(1, LANES), jnp.tile(k_norm_w, 2).reshape(1, LANES),
      cos128, sin128, seg)


def _softmax_pv(s_list, v_list):
    mx = functools.reduce(jnp.maximum, [jnp.max(s, axis=-1, keepdims=True) for s in s_list])
    ps = [jnp.exp(s - mx) for s in s_list]
    l = functools.reduce(jnp.add, [jnp.sum(p, axis=-1, keepdims=True) for p in ps])
    o = functools.reduce(jnp.add, [jnp.dot(p.astype(BF16), v, preferred_element_type=F32)
                                   for p, v in zip(ps, v_list)])
    return o / l


_NT = (((1,), (1,)), ((), ()))


def _na_kernel(qr_ref, qp_ref, k_ref, v_ref, kc_ref, vc_ref, bias_ref, o_ref, *, rows):
    n_tiles = rows // NA_Q_ROWS
    tq = NA_Q_ROWS * GRID_W
    kc = kc_ref[0]
    vc = vc_ref[0]
    lane = lax.broadcasted_iota(jnp.int32, (tq, LANES), 1)
    for sub in range(NA_TILES_PER_STEP):
        t = pl.program_id(2) * NA_TILES_PER_STEP + sub
        ks = jnp.clip(t * NA_Q_ROWS - NA_WIN_R // 2, 0, rows - NA_K_ROWS)
        start = pl.multiple_of(ks * GRID_W, GRID_W)
        k = k_ref[0, pl.ds(start, NA_K_ROWS * GRID_W), :]
        v = v_ref[0, pl.ds(start, NA_K_ROWS * GRID_W), :]
        q = qr_ref[0, pl.ds(sub * tq, tq), :]
        qp = qp_ref[0, pl.ds(sub * tq, tq), :]
        kind = jnp.where(t == 0, 0, jnp.where(t == n_tiles - 1, 2, 1))
        outs = []
        for hh in range(2):
            keep = (lane < HEAD_DIM) if hh == 0 else (lane >= HEAD_DIM)
            zero = jnp.zeros_like(q)
            s_lat = (lax.dot_general(jnp.where(keep, q, zero), k, _NT, preferred_element_type=F32)
                     + bias_ref[kind, hh])
            s_ctx = lax.dot_general(jnp.where(keep, qp, zero), kc, _NT, preferred_element_type=F32)
            outs.append(_softmax_pv([s_lat, s_ctx], [v, vc]))
        o_ref[0, pl.ds(sub * tq, tq), :] = jnp.where(lane < HEAD_DIM, outs[0], outs[1]).astype(BF16)


def _neighborhood_attention(qr, qp, k, v, kc, vc, bias):
    b, n, _ = qr.shape
    lc = kc.shape[1]
    rows = n // GRID_W
    n_steps = rows // (NA_Q_ROWS * NA_TILES_PER_STEP)
    tq = NA_Q_ROWS * GRID_W * NA_TILES_PER_STEP
    tk = NA_K_ROWS * GRID_W
    qmap = lambda bi, hp, t: (bi, t, hp)
    smap = lambda bi, hp, t: (bi, 0, hp)
    bmap = lambda bi, hp, t: (0, hp, 0, 0)
    return pl.pallas_call(
        functools.partial(_na_kernel, rows=rows),
        grid=(b, NA_W // LANES, n_steps),
        in_specs=[pl.BlockSpec((1, tq, LANES), qmap), pl.BlockSpec((1, tq, LANES), qmap),
                  pl.BlockSpec((1, n, LANES), smap), pl.BlockSpec((1, n, LANES), smap),
                  pl.BlockSpec((1, lc, LANES), smap), pl.BlockSpec((1, lc, LANES), smap),
                  pl.BlockSpec((3, 2, NA_Q_ROWS * GRID_W, tk), bmap)],
        out_specs=pl.BlockSpec((1, tq, LANES), qmap),
        out_shape=jax.ShapeDtypeStruct((b, n, NA_W), BF16),
        compiler_params=_cparams(("arbitrary", "arbitrary", "arbitrary")),
        name="neighborhood_attention",
    )(qr, qp, k, v, kc, vc, bias)


def _na_bias_tables(rpb, rows):
    n_tiles = rows // NA_Q_ROWS
    n_dr, n_dc = 2 * NA_WIN_R - 1, 2 * NA_WIN_C - 1
    qc = np.arange(GRID_W)
    cs = np.clip(qc - NA_WIN_C // 2, 0, GRID_W - NA_WIN_C)
    kcol = np.arange(GRID_W)
    ok_c = (kcol[None, :] >= cs[:, None]) & (kcol[None, :] < cs[:, None] + NA_WIN_C)
    dc = kcol[None, :] - qc[:, None] + NA_WIN_C - 1
    col_sel = np.concatenate([(dc[:, :, None] == np.arange(n_dc)) & ok_c[:, :, None],
                              ~ok_c[:, :, None]], axis=2)
    ext = jnp.pad(rpb.astype(F32), ((0, 0), (0, 1), (0, 1)), constant_values=NEG_BIG)
    blocks = jnp.einsum("hrd,qkd->hrqk", ext, jnp.asarray(col_sel, F32), precision=lax.Precision.HIGHEST)
    tabs = []
    for t in (0, 1, n_tiles - 1):
        r0 = t * NA_Q_ROWS
        ks = int(np.clip(r0 - NA_WIN_R // 2, 0, rows - NA_K_ROWS))
        r = r0 + np.arange(NA_Q_ROWS)
        rs = np.clip(r - NA_WIN_R // 2, 0, rows - NA_WIN_R)
        krow = ks + np.arange(NA_K_ROWS)
        ok_r = (krow[None, :] >= rs[:, None]) & (krow[None, :] < rs[:, None] + NA_WIN_R)
        dr = np.where(ok_r, krow[None, :] - r[:, None] + NA_WIN_R - 1, n_dr)
        tabs.append(jnp.concatenate(
            [jnp.concatenate([blocks[:, int(dr[i, j])] for j in range(NA_K_ROWS)], axis=-1)
             for i in range(NA_Q_ROWS)], axis=-2))
    return jnp.stack(tabs)


def _dense_attn_kernel(q_ref, k_ref, v_ref, o_ref):
    q = q_ref[0]
    k = k_ref[0]
    v = v_ref[0]
    lane = lax.broadcasted_iota(jnp.int32, q.shape, 1)
    outs = []
    for hh in range(2):
        keep = (lane < HEAD_DIM) if hh == 0 else (lane >= HEAD_DIM)
        s = lax.dot_general(jnp.where(keep, q, jnp.zeros_like(q)), k, _NT, preferred_element_type=F32)
        outs.append(_softmax_pv([s], [v]))
    o_ref[0] = jnp.where(lane < HEAD_DIM, outs[0], outs[1]).astype(BF16)


def _dense_attention(q, k, v):
    b, l, _ = q.shape
    spec = pl.BlockSpec((1, l, LANES), lambda bi, hp: (bi, 0, hp))
    return pl.pallas_call(
        _dense_attn_kernel,
        grid=(b, NA_W // LANES),
        in_specs=[spec, spec, spec],
        out_specs=spec,
        out_shape=jax.ShapeDtypeStruct((b, l, NA_W), BF16),
        compiler_params=_cparams(("arbitrary", "arbitrary")),
        name="context_attention",
    )(q, k, v)


POOL_HALO = max(POOL_SIZES) // 2


def _pool_kernel(x_ref, w_ref, sc_ref, o_ref, xpad, *, seq_len, chunk):
    zeros = jnp.zeros((POOL_HALO, POOL_W), F32)
    xpad[pl.ds(0, POOL_HALO), :] = zeros
    xpad[pl.ds(seq_len + POOL_HALO, POOL_HALO), :] = zeros
    xpad[pl.ds(POOL_HALO, seq_len), :] = x_ref[0].astype(F32)
    span = chunk + 2 * POOL_HALO
    w = w_ref[...]
    sc = sc_ref[...]

    def body(c, carry):
        base = pl.multiple_of(c * chunk, chunk)
        xs = xpad[pl.ds(base, span), :]
        s2 = xs + pltpu.roll(xs, 1, 0)
        s4 = pltpu.roll(s2, 1, 0) + pltpu.roll(s2, span - 1, 0)
        s8 = pltpu.roll(s4, 2, 0) + pltpu.roll(s4, span - 2, 0)
        s16 = pltpu.roll(s8, 4, 0) + pltpu.roll(s8, span - 4, 0)
        mid = slice(POOL_HALO, POOL_HALO + chunk)
        grp = lax.broadcasted_iota(jnp.int32, (chunk, POOL_W), 1) // POOL_GROUP
        tpos = base + lax.broadcasted_iota(jnp.int32, (chunk, POOL_W), 0)
        half = jnp.left_shift(1, grp)
        cnt = (jnp.minimum(tpos + half, seq_len) - jnp.maximum(tpos - half, 0)).astype(F32)
        ssum = jnp.where(grp == 0, s2[mid], jnp.where(grp == 1, s4[mid], jnp.where(grp == 2, s8[mid], s16[mid])))
        diff = ssum / cnt - xs[mid]
        y = jnp.dot(diff.astype(BF16), w, preferred_element_type=F32) * sc
        o_ref[0, pl.ds(base, chunk), :] = y.astype(BF16)
        return carry

    lax.fori_loop(0, seq_len // chunk, body, 0)


def _pool_mixer(u, w_blockdiag_bf, pool_scale):
    b, l, _ = u.shape
    chunk = min(l, 512)
    return pl.pallas_call(
        functools.partial(_pool_kernel, seq_len=l, chunk=chunk),
        grid=(b,),
        in_specs=[pl.BlockSpec((1, l, POOL_W), lambda bi: (bi, 0, 0)),
                  pl.BlockSpec((POOL_W, POOL_W), lambda bi: (0, 0)),
                  pl.BlockSpec((1, POOL_W), lambda bi: (0, 0))],
        out_specs=pl.BlockSpec((1, l, POOL_W), lambda bi: (bi, 0, 0)),
        out_shape=jax.ShapeDtypeStruct((b, l, POOL_W), BF16),
        scratch_shapes=[pltpu.VMEM((l + 2 * POOL_HALO, POOL_W), F32)],
        compiler_params=_cparams(("arbitrary",)),
        name="pool_mixer",
    )(u, w_blockdiag_bf, pool_scale.reshape(1, POOL_W))


def _outproj_kernel(na_ref, hy_ref, pl_ref, w_ref, x_ref, g1_ref, nw_ref, sh_ref, sc_ref, wr_ref,
                    xn_ref, h2_ref, aff_ref):
    w = w_ref
    acc = jnp.dot(na_ref[...], w[pl.ds(0, NA_W), :], preferred_element_type=F32)
    acc += jnp.dot(hy_ref[...], w[pl.ds(NA_W, HY_W), :], preferred_element_type=F32)
    acc += jnp.dot(pl_ref[...], w[pl.ds(NA_W + HY_W, POOL_W), :], preferred_element_type=F32)
    xn = x_ref[...] + g1_ref[0] * acc
    xn_ref[...] = xn
    ms = jnp.mean(xn * xn, axis=-1, keepdims=True)
    h2 = xn * lax.rsqrt(ms + EPS) * nw_ref[...]
    h2 = h2 * (1.0 + sc_ref[0]) + sh_ref[0]
    tm = h2.shape[0]
    for s in range(TOKEN_SUBLANES):
        h2_ref[pl.ds(s, tm, stride=TOKEN_SUBLANES), :] = h2[:, s * LANES:(s + 1) * LANES]
    logits = lax.dot_general(wr_ref[...], h2, _NT, preferred_element_type=F32,
                             precision=lax.Precision.HIGHEST)
    mx = jnp.max(logits, axis=0, keepdims=True)
    p = jnp.exp(logits - mx)
    aff_ref[0] = p / jnp.sum(p, axis=0, keepdims=True)


def _out_projection(na, hy, pool, w_out_bf, x2, g1, norm_w, shift, scale, w_router_t, *,
                    rows_per_mod, seq_len, tm):
    m, d = x2.shape
    tiles_per_mod = rows_per_mod // tm
    tiles_per_seq = seq_len // tm
    row = lambda i: (i, 0)
    const = lambda i: (0, 0)
    modi = lambda i: (i // tiles_per_mod, 0, 0)
    return pl.pallas_call(
        _outproj_kernel,
        grid=(m // tm,),
        in_specs=[pl.BlockSpec((tm, NA_W), row), pl.BlockSpec((tm, HY_W), row), pl.BlockSpec((tm, POOL_W), row),
                  pl.BlockSpec((MIX_W, d), const),
                  pl.BlockSpec((tm, d), row),
                  pl.BlockSpec((1, 1, d), modi),
                  pl.BlockSpec((1, d), const),
                  pl.BlockSpec((1, 1, d), modi), pl.BlockSpec((1, 1, d), modi),
                  pl.BlockSpec((N_EXPERTS, d), const)],
        out_specs=[pl.BlockSpec((tm, d), row), pl.BlockSpec((tm * TOKEN_SUBLANES, LANES), row),
                   pl.BlockSpec((1, N_EXPERTS, tm), lambda i: (i // tiles_per_seq, 0, i % tiles_per_seq))],
        out_shape=[jax.ShapeDtypeStruct((m, d), F32), jax.ShapeDtypeStruct((m * TOKEN_SUBLANES, LANES), F32),
                   jax.ShapeDtypeStruct((m // seq_len, N_EXPERTS, seq_len), F32)],
        compiler_params=_cparams(("arbitrary",)),
        name="out_projection_router",
    )(na, hy, pool, w_out_bf, x2, g1, norm_w.reshape(1, d), shift, scale, w_router_t)


FFN_TF = 256
FFN_ROW_CHUNKS = 2


def _ffn_kernel(*refs, n_experts, sources, n_rows):
    ns = len(sources)
    idx_cur = refs[0:ns]
    idx_nxt = refs[ns:2 * ns]
    src = refs[2 * ns:3 * ns]
    gate_ref, wg_ref, wu_ref, wd_ref, o_ref, stage, xb, acc, sem = refs[3 * ns:]
    e = pl.program_id(0)
    f = pl.program_id(1)
    n_f = pl.num_programs(1)

    ts = TOKEN_SUBLANES

    def row_copy(s, row, r):
        return pltpu.make_async_copy(src[s].at[pl.ds(pl.multiple_of(row * ts, ts), ts)],
                                     stage.at[pl.ds(pl.multiple_of((sources[s][0] + r) * ts, ts), ts)], sem.at[0])

    def gather(idx_refs):
        for s in range(ns):
            def body(r, carry, s=s):
                row_copy(s, idx_refs[s][0, 0, r], r).start()
                return carry
            lax.fori_loop(0, sources[s][1], body, 0, unroll=8)

    def gather_wait():
        for s in range(ns):
            off, cnt = sources[s]
            pltpu.make_async_copy(src[s].at[pl.ds(0, cnt * ts)], stage.at[pl.ds(off * ts, cnt * ts)],
                                  sem.at[0]).wait()

    @pl.when(f == 0)
    def _():
        @pl.when(e == 0)
        def _():
            gather(idx_cur)

        acc[...] = jnp.zeros_like(acc)
        gather_wait()
        for s in range(ts):
            xb[:, s * LANES:(s + 1) * LANES] = stage[pl.ds(s, n_rows, stride=ts), :].astype(BF16)

        @pl.when(e + 1 < n_experts)
        def _():
            gather(idx_nxt)

    wg = wg_ref[0, 0].astype(BF16)
    wu = wu_ref[0, 0].astype(BF16)
    wd = wd_ref[0, 0].astype(BF16)
    rc = n_rows // FFN_ROW_CHUNKS
    for r in range(FFN_ROW_CHUNKS):
        rs = pl.ds(r * rc, rc)
        x = xb[rs, :]
        a = jnp.dot(x, wg, preferred_element_type=F32)
        b = jnp.dot(x, wu, preferred_element_type=F32)
        h = (a * jax.nn.sigmoid(a) * b).astype(BF16)
        y = jnp.dot(h, wd, preferred_element_type=F32)
        acc[rs, :] += y

    @pl.when(f == n_f - 1)
    def _():
        o_ref[0] = (acc[...] * gate_ref[0]).astype(BF16)


def _expert_ffn(rows_idx, tokens, gate, w_gate, w_up, w_down, layer):
    ns = len(tokens)
    e, d = w_gate.shape[1], w_gate.shape[2]
    counts = [int(ri.shape[2]) for ri in rows_idx]
    offs = [int(sum(counts[:s])) for s in range(ns)]
    r_total = sum(counts)
    assert r_total % (FFN_ROW_CHUNKS * 16) == 0
    smem = lambda cnt, nxt: pl.BlockSpec(
        (1, 1, cnt), (lambda ei, f: (jnp.minimum(ei + 1, e - 1), 0, 0)) if nxt else (lambda ei, f: (ei, 0, 0)),
        memory_space=pltpu.SMEM)
    in_specs = ([smem(cnt, False) for cnt in counts] + [smem(cnt, True) for cnt in counts]
                + [pl.BlockSpec(memory_space=pl.ANY) for _ in range(ns)]
                + [pl.BlockSpec((1, r_total, 1), lambda ei, f: (ei, 0, 0)),
                   pl.BlockSpec((1, 1, d, FFN_TF), lambda ei, f: (layer, ei, 0, f)),
                   pl.BlockSpec((1, 1, d, FFN_TF), lambda ei, f: (layer, ei, 0, f)),
                   pl.BlockSpec((1, 1, FFN_TF, d), lambda ei, f: (layer, ei, f, 0))])
    return pl.pallas_call(
        functools.partial(_ffn_kernel, n_experts=e, sources=tuple(zip(offs, counts)), n_rows=r_total),
        grid=(e, D_EXPERT // FFN_TF),
        in_specs=in_specs,
        out_specs=pl.BlockSpec((1, r_total, d), lambda ei, f: (ei, 0, 0)),
        out_shape=jax.ShapeDtypeStruct((e, r_total, d), BF16),
        scratch_shapes=[pltpu.VMEM((r_total * TOKEN_SUBLANES, LANES), F32), pltpu.VMEM((r_total, d), BF16),
                        pltpu.VMEM((r_total, d), F32), pltpu.SemaphoreType.DMA((1,))],
        compiler_params=_cparams(("arbitrary", "arbitrary")),
        name="expert_ffn",
    )(*rows_idx, *rows_idx, *tokens, gate, w_gate, w_up, w_down)


def _prefix_count(m, n):
    lane = lax.broadcasted_iota(jnp.int32, m.shape, 1)
    c = m
    s = 1
    while s < n:
        c = c + jnp.where(lane >= s, pltpu.roll(c, s, 1), 0)
        s *= 2
    return c


COMBINE_TB = 256
COMBINE_CHUNK = 128


def _topk_kernel(aff_ref, idx_ref, gate_ref, off_ref, *, cap):
    a = aff_ref[...]
    rows, n = a.shape

    def search(i, t):
        cand = t | jnp.left_shift(jnp.int32(1), 30 - i)
        cnt = jnp.sum((a >= pltpu.bitcast(cand, F32)).astype(jnp.int32), axis=1, keepdims=True)
        return jnp.where(cnt >= cap, cand, t)

    thr_bits = lax.fori_loop(0, 31, search, jnp.zeros((rows, 1), jnp.int32))
    thr = pltpu.bitcast(thr_bits, F32)
    gt = a >= pltpu.bitcast(thr_bits + 1, F32)
    eq = (a >= thr) & jnp.logical_not(gt)
    need = cap - jnp.sum(gt.astype(jnp.int32), axis=1, keepdims=True)
    eq_i = eq.astype(jnp.int32)
    sel = gt | (eq & (_prefix_count(eq_i, n) - eq_i < need))
    sel_i = sel.astype(jnp.int32)
    lane = lax.broadcasted_iota(jnp.int32, a.shape, 1)
    dist = jnp.where(sel, lane + 1 - _prefix_count(sel_i, n), 0)
    tok = jnp.where(sel, lane, -1)
    val = a
    s = 1
    while s < n:
        tok_in = pltpu.roll(tok, n - s, 1)
        dist_in = pltpu.roll(dist, n - s, 1)
        val_in = pltpu.roll(val, n - s, 1)
        take = (lane < n - s) & (tok_in >= 0) & ((dist_in & s) != 0)
        stay = (tok >= 0) & ((dist & s) == 0)
        tok = jnp.where(take, tok_in, jnp.where(stay, tok, -1))
        dist = jnp.where(take, dist_in, jnp.where(stay, dist, 0))
        val = jnp.where(take, val_in, val)
        s *= 2
    idx_ref[...] = tok[:, :cap]
    gate_ref[...] = val[:, :cap]
    olane = lax.broadcasted_iota(jnp.int32, off_ref.shape, 1)
    off = jnp.zeros(off_ref.shape, jnp.int32)
    for j in range(n // COMBINE_TB + 1):
        below = jnp.sum(jnp.where(lane < j * COMBINE_TB, sel_i, 0), axis=1, keepdims=True)
        off = jnp.where(olane == j, below, off)
    off_ref[...] = off


def _topk_select(aff, cap):
    b, e, n = aff.shape
    assert n & (n - 1) == 0 and n // COMBINE_TB < LANES
    rows = b * e
    full = lambda w: pl.BlockSpec((rows, w), lambda i: (0, 0))
    idx, gate, off = pl.pallas_call(
        functools.partial(_topk_kernel, cap=cap),
        grid=(1,),
        in_specs=[full(n)],
        out_specs=[full(cap), full(cap), full(LANES)],
        out_shape=[jax.ShapeDtypeStruct((rows, cap), jnp.int32), jax.ShapeDtypeStruct((rows, cap), F32),
                   jax.ShapeDtypeStruct((rows, LANES), jnp.int32)],
        compiler_params=_cparams(("arbitrary",)),
        name="expert_choice_topk",
    )(aff.reshape(rows, n))
    return idx.reshape(b, e, cap), gate.reshape(b, e, cap), off


def _combine_kernel(off_ref, y_ref, idx_ref, x_ref, g_ref, o_ref, acc, ycat, *, chunk, n_chunks):
    bi = pl.program_id(0)
    j = pl.program_id(1)
    tb = x_ref.shape[0]
    tok = j * tb + lax.broadcasted_iota(jnp.int32, (tb, chunk), 0)

    def onehot(e, st):
        return (idx_ref[0, pl.ds(e, 1), pl.ds(st, chunk)] == tok).astype(BF16)

    if n_chunks == 1:
        acc[...] = jnp.zeros_like(acc)
        for e in range(N_EXPERTS):
            acc[...] += jnp.dot(onehot(e, 0), y_ref[e], preferred_element_type=F32)
    else:
        cap = n_chunks * chunk
        shift = chunk.bit_length() - 1
        col = lax.broadcasted_iota(jnp.int32, (1, chunk), 1)

        def slot_ids(e, st):
            base = jnp.minimum(lax.shift_left(lax.shift_right_logical(st, shift), shift), cap - 2 * chunk)
            wide = idx_ref[0, pl.ds(e, 1), pl.ds(pl.multiple_of(base, chunk), 2 * chunk)]
            wide = jnp.broadcast_to(wide, (8, 2 * chunk))
            return pltpu.roll(wide, (2 * chunk - (st - base)) & (2 * chunk - 1), 1)[:1, :chunk]

        starts, ends, hots = [], [], []
        for e in range(N_EXPERTS):
            lo = off_ref[bi * N_EXPERTS + e, j]
            hi = off_ref[bi * N_EXPERTS + e, j + 1]
            st = jnp.minimum(lax.shift_left(lax.shift_right_logical(lo, 4), 4), cap - chunk)
            starts.append(st)
            ends.append(hi)
            ycat[pl.ds(e * chunk, chunk), :] = y_ref[e, pl.ds(pl.multiple_of(st, 16), chunk), :]
            hots.append((slot_ids(e, st) == tok).astype(BF16))
        acc[...] = jnp.dot(jnp.concatenate(hots, axis=1), ycat[...], preferred_element_type=F32)
        for e in range(N_EXPERTS):
            def add_window(c, carry, e=e):
                first = starts[e] + c * chunk
                st = jnp.minimum(first, cap - chunk)
                hot = ((slot_ids(e, st) == tok) & (st + col >= first)).astype(BF16)
                acc[...] += jnp.dot(hot, y_ref[e, pl.ds(pl.multiple_of(st, 16), chunk), :],
                                    preferred_element_type=F32)
                return carry

            n_more = lax.shift_right_logical(jnp.maximum(ends[e] - starts[e] - 1, 0), shift)
            lax.fori_loop(1, n_more + 1, add_window, 0)
    o_ref[...] = x_ref[...] + g_ref[0] * acc[...]


def _moe_combine(off, y, idx, x2, g, *, row_block0, cap, seq_len):
    bsz = idx.shape[0]
    d = x2.shape[1]
    tb = min(seq_len, COMBINE_TB)
    nb = seq_len // tb
    chunk = min(cap, COMBINE_CHUNK)
    n_chunks = cap // chunk
    n_mod = g.shape[0]
    grid_spec = pltpu.PrefetchScalarGridSpec(
        num_scalar_prefetch=1,
        grid=(bsz, nb),
        in_specs=[pl.BlockSpec((N_EXPERTS, cap, d), lambda bi, j, off: (0, row_block0 + bi, 0),
                               pipeline_mode=pl.Buffered(1)),
                  pl.BlockSpec((1, N_EXPERTS, cap), lambda bi, j, off: (bi, 0, 0)),
                  pl.BlockSpec((tb, d), lambda bi, j, off: (bi * nb + j, 0)),
                  pl.BlockSpec((1, 1, d), lambda bi, j, off: (bi if n_mod > 1 else 0, 0, 0))],
        out_specs=pl.BlockSpec((tb, d), lambda bi, j, off: (bi * nb + j, 0)),
        scratch_shapes=[pltpu.VMEM((tb, d), F32), pltpu.VMEM((N_EXPERTS * chunk, d), BF16)])
    return pl.pallas_call(
        functools.partial(_combine_kernel, chunk=chunk, n_chunks=n_chunks),
        grid_spec=grid_spec,
        out_shape=jax.ShapeDtypeStruct(x2.shape, F32),
        compiler_params=_cparams(("arbitrary", "arbitrary")),
        name="moe_combine",
    )(off, y, idx, x2, g)


def _expert_choice_moe(streams, w_gate, w_up, w_down, layer):
    routed = []
    for h2, aff, x2, g in streams:
        bsz, _, l = aff.shape
        cap = EC_FACTOR * l // N_EXPERTS
        idx, gate, off = _topk_select(aff, cap)
        rows = (idx + (jnp.arange(bsz, dtype=jnp.int32) * l)[:, None, None]).transpose(1, 0, 2)
        routed.append((idx, off, rows.reshape(N_EXPERTS, 1, bsz * cap), gate.transpose(1, 0, 2).reshape(
            N_EXPERTS, bsz * cap, 1), cap))
    y = _expert_ffn([r[2] for r in routed], [s[0] for s in streams],
                    jnp.concatenate([r[3] for r in routed], axis=1), w_gate, w_up, w_down, layer)
    outs = []
    row0 = 0
    for (h2, aff, x2, g), (idx, off, rows, _, cap) in zip(streams, routed):
        assert row0 % cap == 0
        outs.append(_moe_combine(off, y, idx, x2, g, row_block0=row0 // cap, cap=cap, seq_len=aff.shape[2]))
        row0 += rows.shape[2]
    return outs


HY_R = 128
HY_COLS = 4096
HY_KB = 8
HY_UNROLL = 8


def _dft_consts(r):
    k = np.arange(r)
    ang = 2.0 * np.pi * np.outer(k, k) / r
    fre, fim = np.cos(ang), -np.sin(ang)
    k2 = np.arange(r)
    tang = 2.0 * np.pi * np.outer(k2, k2) / (r * r)
    tw = np.stack([np.cos(tang), -np.sin(tang)])
    fwd = np.block([[fre, -fim], [fim, fre]])
    inv = np.block([[fre, fim], [-fim, fre]])
    return fre, fim, tw, fwd, inv


def _hy_prep_kernel(x_ref, prev_ref, next_ref, w_ref, b_ref, x1_ref, x2_ref, v_ref, *, n_tiles):
    t = pl.program_id(1)
    x = x_ref[0].astype(F32)
    tm = x.shape[0]
    row = lax.broadcasted_iota(jnp.int32, x.shape, 0)
    halo = prev_ref.shape[1]
    prev = jnp.where(t == 0, 0.0, prev_ref[0].astype(F32)[halo - 1:halo, :])
    nxt = jnp.where(t == n_tiles - 1, 0.0, next_ref[0].astype(F32)[0:1, :])
    up = jnp.where(row == 0, prev, pltpu.roll(x, 1, 0))
    dn = jnp.where(row == tm - 1, nxt, pltpu.roll(x, tm - 1, 0))
    y = b_ref[...] + up * w_ref[pl.ds(0, 1), :] + x * w_ref[pl.ds(1, 1), :] + dn * w_ref[pl.ds(2, 1), :]
    x1_ref[0] = y[:, :HY_W].astype(BF16)
    x2_ref[0] = y[:, HY_W:2 * HY_W].astype(BF16)
    v_ref[0] = y[:, 2 * HY_W:].astype(BF16)


def _hyena_prep(u, conv_w, conv_b):
    b, l, c = u.shape
    tm = min(l, 1024)
    n_tiles = l // tm
    halo = 16
    g = tm // halo
    n_groups = l // halo
    out = jax.ShapeDtypeStruct((b, l, HY_W), BF16)
    ospec = pl.BlockSpec((1, tm, HY_W), lambda bi, t: (bi, t, 0))
    return pl.pallas_call(
        functools.partial(_hy_prep_kernel, n_tiles=n_tiles),
        grid=(b, n_tiles),
        in_specs=[pl.BlockSpec((1, tm, c), lambda bi, t: (bi, t, 0)),
                  pl.BlockSpec((1, halo, c), lambda bi, t: (bi, jnp.maximum(t * g - 1, 0), 0)),
                  pl.BlockSpec((1, halo, c), lambda bi, t: (bi, jnp.minimum((t + 1) * g, n_groups - 1), 0)),
                  pl.BlockSpec((HY_SHORT, c), lambda bi, t: (0, 0)),
                  pl.BlockSpec((1, c), lambda bi, t: (0, 0))],
        out_specs=[ospec, ospec, ospec],
        out_shape=[out, out, out],
        compiler_params=_cparams(("arbitrary", "arbitrary")),
        name="hyena_prep",
    )(u, u, u, conv_w, conv_b.reshape(1, c))


def _hy_filter_kernel(z_ref, w1_ref, b1_ref, w2_ref, b2_ref, w3_ref, fr_ref, dl_ref, k_ref, ss_ref, *, seq_len):
    i = pl.program_id(0)
    hp = lax.Precision.HIGHEST
    z = z_ref[...]
    tm = z.shape[0]
    half = tm // 2
    fr = fr_ref[...]
    zz = jnp.concatenate([z[:half], z[half:]], axis=1)
    h = jnp.sin(fr * (jnp.dot(zz, w1_ref[...], preferred_element_type=F32, precision=hp) + b1_ref[...]))
    h = jnp.sin(fr * (jnp.dot(h, w2_ref[...], preferred_element_type=F32, precision=hp) + b2_ref[...]))
    hb = h.astype(BF16)
    w3 = w3_ref[...].astype(BF16)
    fo = w3.shape[0]
    h = jnp.concatenate([jnp.dot(hb[:, :fo], w3, preferred_element_type=F32),
                         jnp.dot(hb[:, fo:], w3, preferred_element_type=F32)], axis=0)
    dec = jnp.exp(-z[:, 0:1] * dl_ref[...])
    r = i * tm + lax.broadcasted_iota(jnp.int32, (tm, HY_W), 0)
    parts = []
    for o in range(HY_ORDER):
        hf = h[:, (2 * o) * HY_W:(2 * o + 1) * HY_W]
        hb = h[:, (2 * o + 1) * HY_W:(2 * o + 2) * HY_W]
        parts.append(jnp.where(r < seq_len, hf, jnp.where(r > seq_len, hb, 0.0)) * dec)
    k = jnp.concatenate(parts, axis=1)
    k_ref[...] = k
    ss = jnp.sum(k * k, axis=0, keepdims=True)

    @pl.when(i == 0)
    def _():
        ss_ref[...] = ss

    @pl.when(i != 0)
    def _():
        ss_ref[...] += ss


def _hyena_filter_taps(l, w1, b1, w2, b2, w3, freq):
    t = jnp.linspace(0.0, 1.0, l, dtype=F32)[:, None]
    w = (2.0 * math.pi / l) * jnp.arange(l, dtype=F32)[:, None]
    f = jnp.linspace(1e-4, HY_BANDS - 1, HY_BANDS, dtype=F32)[None, :]
    z = jnp.concatenate([t, jnp.cos(f * w), -jnp.sin(f * w)], axis=-1)
    z2 = jnp.concatenate([z, z[:1], z[:0:-1]], axis=0)
    emb = z.shape[1]
    z2 = jnp.pad(z2, ((0, 0), (0, LANES - emb)))
    w1p = jnp.pad(w1, ((0, LANES - emb), (0, 0)))
    two = lambda m: jnp.kron(jnp.eye(2, dtype=m.dtype), m)
    pair = lambda v: jnp.tile(v, 2).reshape(1, 2 * v.shape[0])
    deltas = jnp.abs(jnp.linspace(math.log(HY_DECAY_TARGET) / HY_SLOW_DECAY,
                                  math.log(HY_DECAY_TARGET) / HY_FAST_DECAY, HY_W, dtype=F32))
    fo = w2.shape[0]
    nout = w3.shape[1]
    tm = min(2 * l, 1024)
    const = lambda i: (0, 0)
    return pl.pallas_call(
        functools.partial(_hy_filter_kernel, seq_len=l),
        grid=(2 * l // tm,),
        in_specs=[pl.BlockSpec((tm, LANES), lambda i: (i, 0)),
                  pl.BlockSpec((2 * LANES, 2 * fo), const), pl.BlockSpec((1, 2 * fo), const),
                  pl.BlockSpec((2 * fo, 2 * fo), const), pl.BlockSpec((1, 2 * fo), const),
                  pl.BlockSpec((fo, nout), const), pl.BlockSpec((1, 2 * fo), const),
                  pl.BlockSpec((1, HY_W), const)],
        out_specs=[pl.BlockSpec((tm, HY_ORDER * HY_W), lambda i: (i, 0)),
                   pl.BlockSpec((1, HY_ORDER * HY_W), const)],
        out_shape=[jax.ShapeDtypeStruct((2 * l, HY_ORDER * HY_W), F32),
                   jax.ShapeDtypeStruct((1, HY_ORDER * HY_W), F32)],
        compiler_params=_cparams(("arbitrary",)),
        name="hyena_filter_taps",
    )(z2, two(w1p), pair(b1), two(w2), pair(b2), w3, pair(freq), deltas.reshape(1, HY_W))


def _left_matmul_kernel(f_ref, x_ref, o_ref):
    o_ref[0] = jnp.dot(f_ref[...], x_ref[0].astype(BF16), preferred_element_type=F32).astype(o_ref.dtype)


def _left_matmul(f_bf, x3, name):
    b, k, c = x3.shape
    m = f_bf.shape[0]
    return pl.pallas_call(
        _left_matmul_kernel,
        grid=(b, c // HY_COLS),
        in_specs=[pl.BlockSpec((m, k), lambda bi, j: (0, 0)),
                  pl.BlockSpec((1, k, HY_COLS), lambda bi, j: (bi, 0, j))],
        out_specs=pl.BlockSpec((1, m, HY_COLS), lambda bi, j: (bi, 0, j)),
        out_shape=jax.ShapeDtypeStruct((b, m, c), BF16),
        compiler_params=_cparams(("arbitrary", "arbitrary")),
        name=name,
    )(f_bf, x3)


def _twiddle_cols(tw_ref, k1):
    lane = lax.broadcasted_iota(jnp.int32, (HY_R, HY_R), 1)
    pick = lane == k1
    twr = jnp.sum(jnp.where(pick, tw_ref[0], 0.0), axis=1, keepdims=True)
    twi = jnp.sum(jnp.where(pick, tw_ref[1], 0.0), axis=1, keepdims=True)
    return twr, twi


def _hy_spectrum_kernel(a_ref, tw_ref, fwd_ref, ss_ref, kf_ref):
    kb = pl.program_id(0)
    scale = lax.rsqrt(ss_ref[...] + EPS) * (1.0 / (HY_R * HY_R))

    def body(kk, carry):
        twr, twi = _twiddle_cols(tw_ref, kb * HY_KB + kk)
        are = a_ref[0, 0, kk].astype(F32)
        aim = a_ref[0, 1, kk].astype(F32)
        bst = jnp.concatenate([are * twr - aim * twi, are * twi + aim * twr], axis=0).astype(BF16)
        x = jnp.dot(fwd_ref[...], bst, preferred_element_type=F32)
        kf_ref[0, kk] = x[:HY_R] * scale
        kf_ref[1, kk] = x[HY_R:] * scale
        return carry

    lax.fori_loop(0, HY_KB, body, 0, unroll=HY_UNROLL)


def _hy_spectrum(a5, tw, fwd_bf, ss):
    nch = a5.shape[-1]
    return pl.pallas_call(
        _hy_spectrum_kernel,
        grid=(HY_R // HY_KB,),
        in_specs=[pl.BlockSpec((1, 2, HY_KB, HY_R, nch), lambda kb: (0, 0, kb, 0, 0)),
                  pl.BlockSpec((2, HY_R, HY_R), lambda kb: (0, 0, 0)),
                  pl.BlockSpec((2 * HY_R, 2 * HY_R), lambda kb: (0, 0)),
                  pl.BlockSpec((1, nch), lambda kb: (0, 0))],
        out_specs=pl.BlockSpec((2, HY_KB, HY_R, nch), lambda kb: (0, kb, 0, 0)),
        out_shape=jax.ShapeDtypeStruct((2, HY_R, HY_R, nch), F32),
        compiler_params=_cparams(("arbitrary",)),
        name="hyena_filter_spectrum",
    )(a5, tw, fwd_bf, ss)


def _hy_mid_kernel(a_ref, kf_ref, tw_ref, fwd_ref, inv_ref, z_ref):
    kb = pl.program_id(0)

    def body(kk, carry):
        twr, twi = _twiddle_cols(tw_ref, kb * HY_KB + kk)
        are = a_ref[0, 0, kk].astype(F32)
        aim = a_ref[0, 1, kk].astype(F32)
        bst = jnp.concatenate([are * twr - aim * twi, are * twi + aim * twr], axis=0).astype(BF16)
        x = jnp.dot(fwd_ref[...], bst, preferred_element_type=F32)
        xre, xim = x[:HY_R], x[HY_R:]
        kre, kim = kf_ref[0, kk], kf_ref[1, kk]
        yst = jnp.concatenate([xre * kre - xim * kim, xre * kim + xim * kre], axis=0).astype(BF16)
        zz = jnp.dot(inv_ref[...], yst, preferred_element_type=F32)
        zre, zim = zz[:HY_R], zz[HY_R:]
        z_ref[0, 0, kk] = (zre * twr + zim * twi).astype(BF16)
        z_ref[0, 1, kk] = (zim * twr - zre * twi).astype(BF16)
        return carry

    lax.fori_loop(0, HY_KB, body, 0, unroll=HY_UNROLL)


def _hy_mid(a5, kf, order, tw, fwd_bf, inv_bf):
    b = a5.shape[0]
    blk = (1, 2, HY_KB, HY_R, HY_W)
    return pl.pallas_call(
        _hy_mid_kernel,
        grid=(HY_R // HY_KB, b),
        in_specs=[pl.BlockSpec(blk, lambda kb, bi: (bi, 0, kb, 0, 0)),
                  pl.BlockSpec((2, HY_KB, HY_R, HY_W), lambda kb, bi: (0, kb, 0, order)),
                  pl.BlockSpec((2, HY_R, HY_R), lambda kb, bi: (0, 0, 0)),
                  pl.BlockSpec((2 * HY_R, 2 * HY_R), lambda kb, bi: (0, 0)),
                  pl.BlockSpec((2 * HY_R, 2 * HY_R), lambda kb, bi: (0, 0))],
        out_specs=pl.BlockSpec(blk, lambda kb, bi: (bi, 0, kb, 0, 0)),
        out_shape=jax.ShapeDtypeStruct(a5.shape, BF16),
        compiler_params=_cparams(("arbitrary", "arbitrary")),
        name="hyena_spectral_product",
    )(a5, kf, tw, fwd_bf, inv_bf)


def _hy_out_kernel(f_ref, z_ref, g_ref, w_ref, sk_ref, o_ref):
    y = jnp.dot(f_ref[...], z_ref[0], preferred_element_type=F32)
    w = w_ref[0].astype(F32)
    o_ref[0] = (g_ref[0].astype(F32) * (y + w * sk_ref[...])).astype(BF16)


def _hy_out(f_bf, z3, gate3, w3, skip_row):
    b, m2, c = z3.shape
    m = f_bf.shape[0]
    dspec = pl.BlockSpec((1, m, HY_COLS), lambda bi, j: (bi, 0, j))
    return pl.pallas_call(
        _hy_out_kernel,
        grid=(b, c // HY_COLS),
        in_specs=[pl.BlockSpec((m, m2), lambda bi, j: (0, 0)),
                  pl.BlockSpec((1, m2, HY_COLS), lambda bi, j: (bi, 0, j)),
                  dspec, dspec,
                  pl.BlockSpec((1, HY_COLS), lambda bi, j: (0, 0))],
        out_specs=dspec,
        out_shape=jax.ShapeDtypeStruct((b, m, c), BF16),
        compiler_params=_cparams(("arbitrary", "arbitrary")),
        name="hyena_inverse_gate",
    )(f_bf, z3, gate3, w3, skip_row)


def _hyena_long(u, conv_w, conv_b, skip, filt_args):
    b, l, _ = u.shape
    assert 2 * l == HY_R * HY_R
    fre, fim, tw, fwd, inv = _dft_consts(HY_R)
    half = HY_R // 2
    as_bf = lambda m: jnp.asarray(m, F32).astype(BF16)
    f_a_data = as_bf(np.concatenate([fre[:, :half], fim[:, :half]], axis=0))
    f_a_filt = as_bf(np.concatenate([fre, fim], axis=0))
    f_c = as_bf(np.concatenate([fre[:half], fim[:half]], axis=1))
    tw_j = jnp.asarray(tw, F32)
    fwd_bf = as_bf(fwd)
    inv_bf = as_bf(inv)

    taps, ss = _hyena_filter_taps(l, *filt_args)
    nch = HY_ORDER * HY_W
    ak = _left_matmul(f_a_filt, taps.reshape(1, HY_R, HY_R * nch), "hyena_filter_dft_a")
    kf = _hy_spectrum(ak.reshape(1, 2, HY_R, HY_R, nch), tw_j, fwd_bf, ss)

    x1, x2, v = _hyena_prep(u, conv_w, conv_b)
    z = v
    for o, gate in enumerate((x1, x2)):
        z3 = z.reshape(b, half, HY_R * HY_W)
        a = _left_matmul(f_a_data, z3, "hyena_dft_a")
        zs = _hy_mid(a.reshape(b, 2, HY_R, HY_R, HY_W), kf, o, tw_j, fwd_bf, inv_bf)
        skip_row = jnp.tile(skip[o], HY_COLS // HY_W).reshape(1, HY_COLS)
        z = _hy_out(f_c, zs.reshape(b, 2 * HY_R, HY_R * HY_W), gate.reshape(b, half, HY_R * HY_W), z3,
                    skip_row).reshape(b, l, HY_W)
    return z


def _hy_ctx_kernel(x1_ref, x2_ref, v_ref, taps_ref, ss_ref, fa_ref, fc_ref, sk_ref, o_ref, *, seq_len):
    n2 = 2 * seq_len
    fa = fa_ref[...]
    scale = lax.rsqrt(ss_ref[...] + EPS) * (1.0 / n2)
    kf = jnp.dot(fa, taps_ref[...].astype(BF16), preferred_element_type=F32) * scale
    z = v_ref[0].astype(F32)
    for o, g_ref in enumerate((x1_ref, x2_ref)):
        x = jnp.dot(fa[:, :seq_len], z.astype(BF16), preferred_element_type=F32)
        xre, xim = x[:n2], x[n2:]
        kre = kf[:n2, o * HY_W:(o + 1) * HY_W]
        kim = kf[n2:, o * HY_W:(o + 1) * HY_W]
        yst = jnp.concatenate([xre * kre - xim * kim, xre * kim + xim * kre], axis=0).astype(BF16)
        y = jnp.dot(fc_ref[...], yst, preferred_element_type=F32)
        z = g_ref[0].astype(F32) * (y + z * sk_ref[pl.ds(o, 1), :])
    o_ref[0] = z.astype(BF16)


def _hyena_short(u, conv_w, conv_b, skip, filt_args):
    b, l, _ = u.shape
    n2 = 2 * l
    k = np.arange(n2)
    ang = 2.0 * np.pi * np.outer(k, k) / n2
    fre, fim = np.cos(ang), -np.sin(ang)
    fa = jnp.asarray(np.concatenate([fre, fim], axis=0), F32).astype(BF16)
    fc = jnp.asarray(np.concatenate([fre[:l], fim[:l]], axis=1), F32).astype(BF16)
    taps, ss = _hyena_filter_taps(l, *filt_args)
    x1, x2, v = _hyena_prep(u, conv_w, conv_b)
    nch = HY_ORDER * HY_W
    dspec = pl.BlockSpec((1, l, HY_W), lambda bi: (bi, 0, 0))
    const = lambda bi: (0, 0)
    return pl.pallas_call(
        functools.partial(_hy_ctx_kernel, seq_len=l),
        grid=(b,),
        in_specs=[dspec, dspec, dspec,
                  pl.BlockSpec((n2, nch), const), pl.BlockSpec((1, nch), const),
                  pl.BlockSpec((2 * n2, n2), const), pl.BlockSpec((l, 2 * n2), const),
                  pl.BlockSpec((HY_ORDER, HY_W), const)],
        out_specs=dspec,
        out_shape=jax.ShapeDtypeStruct((b, l, HY_W), BF16),
        compiler_params=_cparams(("arbitrary",)),
        name="hyena_context",
    )(x1, x2, v, taps, ss, fa, fc, skip)


def _rope_tables(n):
    t = jnp.arange(n, dtype=jnp.int32)
    pos = jnp.stack([t // GRID_W, t % GRID_W], axis=-1).astype(F32)
    nf = HEAD_DIM // 4
    inv = ROPE_THETA ** (-jnp.arange(nf, dtype=F32) / nf)
    ang = pos[:, :, None] * inv
    cos, sin = jnp.cos(ang), jnp.sin(ang)
    c64 = jnp.concatenate([cos[:, 0], cos[:, 0], cos[:, 1], cos[:, 1]], axis=-1)
    s64 = jnp.concatenate([-sin[:, 0], sin[:, 0], -sin[:, 1], sin[:, 1]], axis=-1)
    return jnp.tile(c64, (1, 2)), jnp.tile(s64, (1, 2))


def _block_diag(pool_w):
    g, c, _ = pool_w.shape
    out = jnp.zeros((g * c, g * c), pool_w.dtype)
    for i in range(g):
        out = out.at[i * c:(i + 1) * c, i * c:(i + 1) * c].set(pool_w[i])
    return out


def kernel(x, c, ctx, c_ctx, w_mod, b_mod, norm1_w, norm2_w, w_in, w_out, q_norm_w, k_norm_w, na_rpb,
           hy_conv_w, hy_conv_b, hy_w1, hy_b1, hy_w2, hy_b2, hy_w3, hy_freq, hy_skip,
           pool_w, pool_scale, w_router, w_gate, w_up, w_down):
    b, n, d = x.shape
    lc = ctx.shape[1]
    depth = w_mod.shape[0]
    rows = n // GRID_W
    cos128, sin128 = _rope_tables(n)
    seg = jnp.asarray(np.kron(np.eye(2), np.full((HEAD_DIM, HEAD_DIM), 1.0 / HEAD_DIM)), BF16)
    cc = jnp.zeros((8, d), F32).at[:b].set(c).at[b].set(c_ctx)

    xl = x.reshape(b * n, d)
    xc = ctx.reshape(b * lc, d)
    for i in range(depth):
        last = i == depth - 1
        mod = _modulation(cc, w_mod, b_mod[i], i)
        ml = mod[:b].reshape(b, 1, 6, d)
        mc = mod[b].reshape(1, 1, 6, d)
        sh1, sc1, g1, sh2, sc2, g2 = [ml[:, :, j] for j in range(6)]
        csh1, csc1, cg1, csh2, csc2, cg2 = [mc[:, :, j] for j in range(6)]
        w_in_bf = w_in[i].astype(BF16)
        w_out_bf = w_out[i].astype(BF16)
        w_router_t = w_router[i].T
        pool_bd = _block_diag(pool_w[i]).astype(BF16)
        hy_args = (hy_w1[i], hy_b1[i], hy_w2[i], hy_b2[i], hy_w3[i], hy_freq[i])

        qc, kc, vc, hyc, poolc = _in_projection(
            xc, csh1, csc1, norm1_w[i], w_in_bf, q_norm_w[i], k_norm_w[i], cos128, sin128, seg,
            rows_per_mod=b * lc, seq_len=lc, tm=lc, rope=False)
        kc = kc.reshape(b, lc, NA_W)
        vc = vc.reshape(b, lc, NA_W)
        if not last:
            na_c = _dense_attention(qc.reshape(b, lc, NA_W), kc, vc).reshape(b * lc, NA_W)
            hy_c = _hyena_short(hyc.reshape(b, lc, 3 * HY_W), hy_conv_w[i], hy_conv_b[i], hy_skip[i], hy_args)
            pool_c = _pool_mixer(poolc.reshape(b, lc, POOL_W), pool_bd, pool_scale[i])
            xc1, h2c, affc = _out_projection(
                na_c, hy_c.reshape(b * lc, HY_W).astype(BF16), pool_c.reshape(b * lc, POOL_W), w_out_bf,
                xc, cg1, norm2_w[i], csh2, csc2, w_router_t, rows_per_mod=b * lc, seq_len=lc, tm=lc)

        qr, qp, kl, vl, hyl, pooll = _in_projection(
            xl, sh1, sc1, norm1_w[i], w_in_bf, q_norm_w[i], k_norm_w[i], cos128, sin128, seg,
            rows_per_mod=n, seq_len=n, tm=512, rope=True)
        bias = _na_bias_tables(na_rpb[i], rows)
        o_na = _neighborhood_attention(qr.reshape(b, n, NA_W), qp.reshape(b, n, NA_W), kl.reshape(b, n, NA_W),
                                       vl.reshape(b, n, NA_W), kc, vc, bias)
        hy_l = _hyena_long(hyl.reshape(b, n, 3 * HY_W), hy_conv_w[i], hy_conv_b[i], hy_skip[i], hy_args)
        pool_l = _pool_mixer(pooll.reshape(b, n, POOL_W), pool_bd, pool_scale[i])
        xl1, h2l, affl = _out_projection(
            o_na.reshape(b * n, NA_W), hy_l.reshape(b * n, HY_W).astype(BF16), pool_l.reshape(b * n, POOL_W),
            w_out_bf, xl, g1, norm2_w[i], sh2, sc2, w_router_t, rows_per_mod=n, seq_len=n, tm=512)
        streams = [(h2l, affl, xl1, g2)]
        if not last:
            streams.append((h2c, affc, xc1, cg2))
        outs = _expert_choice_moe(streams, w_gate, w_up, w_down, i)
        xl = outs[0]
        if not last:
            xc = outs[1]
    return xl.reshape(b, n, d)
```

```python
import functools
import math

import jax
import jax.numpy as jnp
import numpy as np
from jax import lax
from jax.experimental import pallas as pl
from jax.experimental.pallas import tpu as pltpu

F32 = jnp.float32
BF16 = jnp.bfloat16

D_MODEL = 1024
GRID_W = 64
NA_HEADS = 8
HEAD_DIM = 64
NA_W = NA_HEADS * HEAD_DIM
NA_WIN_R = 8
NA_WIN_C = 16
ROPE_THETA = 10000.0
HY_W = 256
HY_ORDER = 2
HY_SHORT = 3
HY_BANDS = 16
HY_DECAY_TARGET = 1e-2
HY_FAST_DECAY = 0.3
HY_SLOW_DECAY = 1.5
POOL_W = 256
POOL_SIZES = (2, 4, 8, 16)
POOL_GROUP = POOL_W // len(POOL_SIZES)
MIX_W = NA_W + HY_W + POOL_W
IN_W = 3 * NA_W + (HY_ORDER + 1) * HY_W + POOL_W
N_EXPERTS = 16
EC_FACTOR = 2
D_EXPERT = 2048
EPS = 1e-6

LANES = 128
TOKEN_SUBLANES = D_MODEL // 128
ROW_SUBTILES = 2
NEG_BIG = -1e30
VMEM_LIMIT = 56 * 1024 * 1024

NA_Q_ROWS = 4
NA_K_ROWS = NA_Q_ROWS + NA_WIN_R - 1
NA_TILES_PER_STEP = 4


def _cparams(sem):
    return pltpu.CompilerParams(dimension_semantics=sem, vmem_limit_bytes=VMEM_LIMIT)


def _mod_kernel(c_ref, w_ref, b_ref, o_ref):
    c = c_ref[...]
    s = c * jax.nn.sigmoid(c)
    o_ref[...] = jnp.dot(s, w_ref[0], preferred_element_type=F32,
                         precision=lax.Precision.HIGHEST) + b_ref[...]


def _modulation(cc, w_mod, b_mod, layer):
    rows, d = cc.shape
    n = w_mod.shape[2]
    tn = 1536
    return pl.pallas_call(
        _mod_kernel,
        grid=(n // tn,),
        in_specs=[pl.BlockSpec((rows, d), lambda j: (0, 0)),
                  pl.BlockSpec((1, d, tn), lambda j: (layer, 0, j)),
                  pl.BlockSpec((1, tn), lambda j: (0, j))],
        out_specs=pl.BlockSpec((rows, tn), lambda j: (0, j)),
        out_shape=jax.ShapeDtypeStruct((rows, n), F32),
        compiler_params=_cparams(("arbitrary",)),
        name="modulation",
    )(cc, w_mod, b_mod.reshape(1, n))


def _head_norm(xs, w128, seg):
    m = jnp.dot((xs * xs).astype(BF16), seg, preferred_element_type=F32)
    return xs * lax.rsqrt(m + EPS) * w128


def _rope(xs, c, s_signed):
    lane = lax.broadcasted_iota(jnp.int32, xs.shape, 1)
    partner = jnp.where((lane & 16) == 0, pltpu.roll(xs, LANES - 16, 1), pltpu.roll(xs, 16, 1))
    return xs * c + partner * s_signed


def _inproj_kernel(x_ref, sh_ref, sc_ref, nw_ref, w_ref, qw_ref, kw_ref, cos_ref, sin_ref, seg_ref,
                   *out_refs, rope):
    if rope:
        qr_ref, qp_ref, k_ref, v_ref, hy_ref, pool_ref = out_refs
    else:
        qp_ref, k_ref, v_ref, hy_ref, pool_ref = out_refs
    seg = seg_ref[...]
    qk_scale = HEAD_DIM ** -0.5
    sub = x_ref.shape[0] // ROW_SUBTILES
    for t in range(ROW_SUBTILES):
        rs = pl.ds(t * sub, sub)
        x = x_ref[rs, :]
        ms = jnp.mean(x * x, axis=-1, keepdims=True)
        h = x * lax.rsqrt(ms + EPS) * nw_ref[...]
        h = h * (1.0 + sc_ref[0]) + sh_ref[0]
        u = jnp.dot(h.astype(BF16), w_ref[...], preferred_element_type=F32)
        for ch in range(NA_W // LANES):
            sl = slice(ch * LANES, (ch + 1) * LANES)
            qn = _head_norm(u[:, sl], qw_ref[...], seg)
            kn = _head_norm(u[:, NA_W + ch * LANES:NA_W + (ch + 1) * LANES], kw_ref[...], seg)
            qp_ref[rs, sl] = (qn * qk_scale).astype(BF16)
            if rope:
                c = cos_ref[rs, :]
                s = sin_ref[rs, :]
                qr_ref[rs, sl] = (_rope(qn, c, s) * qk_scale).astype(BF16)
                k_ref[rs, sl] = _rope(kn, c, s).astype(BF16)
            else:
                k_ref[rs, sl] = kn.astype(BF16)
        v_ref[rs, :] = u[:, 2 * NA_W:3 * NA_W].astype(BF16)
        hy_ref[rs, :] = u[:, 3 * NA_W:3 * NA_W + 3 * HY_W].astype(BF16)
        pool_ref[rs, :] = u[:, 3 * NA_W + 3 * HY_W:].astype(BF16)


def _in_projection(x2, shift, scale, norm_w, w_in_bf, q_norm_w, k_norm_w, cos128, sin128, seg, *,
                   rows_per_mod, seq_len, tm, rope):
    m, d = x2.shape
    tiles_per_mod = rows_per_mod // tm
    tiles_per_seq = seq_len // tm
    row = lambda i: (i, 0)
    const = lambda i: (0, 0)
    modi = lambda i: (i // tiles_per_mod, 0, 0)
    pos = lambda i: (i % tiles_per_seq, 0)
    outs = []
    if rope:
        outs.append((NA_W, BF16))
    outs += [(NA_W, BF16), (NA_W, BF16), (NA_W, BF16), (3 * HY_W, BF16), (POOL_W, BF16)]
    return pl.pallas_call(
        functools.partial(_inproj_kernel, rope=rope),
        grid=(m // tm,),
        in_specs=[pl.BlockSpec((tm, d), row),
                  pl.BlockSpec((1, 1, d), modi), pl.BlockSpec((1, 1, d), modi),
                  pl.BlockSpec((1, d), const),
                  pl.BlockSpec((d, IN_W), const),
                  pl.BlockSpec((1, LANES), const), pl.BlockSpec((1, LANES), const),
                  pl.BlockSpec((tm, LANES), pos), pl.BlockSpec((tm, LANES), pos),
                  pl.BlockSpec((LANES, LANES), const)],
        out_specs=[pl.BlockSpec((tm, w), row) for w, _ in outs],
        out_shape=[jax.ShapeDtypeStruct((m, w), dt) for w, dt in outs],
        compiler_params=_cparams(("arbitrary",)),
        name="in_projection_rope" if rope else "in_projection_ctx",
    )(x2, shift, scale, norm_w.reshape(1, d), w_in_bf,
      jnp.tile(q_norm_w, 2).reshape(1, LANES), jnp.tile(k_norm_w, 2).reshape(1, LANES),
      cos128, sin128, seg)


def _softmax_pv(s_list, v_list):
    mx = functools.reduce(jnp.maximum, [jnp.max(s, axis=-1, keepdims=True) for s in s_list])
    ps = [jnp.exp(s - mx) for s in s_list]
    l = functools.reduce(jnp.add, [jnp.sum(p, axis=-1, keepdims=True) for p in ps])
    o = functools.reduce(jnp.add, [jnp.dot(p.astype(BF16), v, preferred_element_type=F32)
                                   for p, v in zip(ps, v_list)])
    return o / l


_NT = (((1,), (1,)), ((), ()))


def _na_kernel(qr_ref, qp_ref, k_ref, v_ref, kc_ref, vc_ref, bias_ref, o_ref, *, rows):
    n_tiles = rows // NA_Q_ROWS
    tq = NA_Q_ROWS * GRID_W
    kc = kc_ref[0]
    vc = vc_ref[0]
    lane = lax.broadcasted_iota(jnp.int32, (tq, LANES), 1)
    for sub in range(NA_TILES_PER_STEP):
        t = pl.program_id(2) * NA_TILES_PER_STEP + sub
        ks = jnp.clip(t * NA_Q_ROWS - NA_WIN_R // 2, 0, rows - NA_K_ROWS)
        start = pl.multiple_of(ks * GRID_W, GRID_W)
        k = k_ref[0, pl.ds(start, NA_K_ROWS * GRID_W), :]
        v = v_ref[0, pl.ds(start, NA_K_ROWS * GRID_W), :]
        q = qr_ref[0, pl.ds(sub * tq, tq), :]
        qp = qp_ref[0, pl.ds(sub * tq, tq), :]
        kind = jnp.where(t == 0, 0, jnp.where(t == n_tiles - 1, 2, 1))
        outs = []
        for hh in range(2):
            keep = (lane < HEAD_DIM) if hh == 0 else (lane >= HEAD_DIM)
            zero = jnp.zeros_like(q)
            s_lat = (lax.dot_general(jnp.where(keep, q, zero), k, _NT, preferred_element_type=F32)
                     + bias_ref[kind, hh])
            s_ctx = lax.dot_general(jnp.where(keep, qp, zero), kc, _NT, preferred_element_type=F32)
            outs.append(_softmax_pv([s_lat, s_ctx], [v, vc]))
        o_ref[0, pl.ds(sub * tq, tq), :] = jnp.where(lane < HEAD_DIM, outs[0], outs[1]).astype(BF16)


def _neighborhood_attention(qr, qp, k, v, kc, vc, bias):
    b, n, _ = qr.shape
    lc = kc.shape[1]
    rows = n // GRID_W
    n_steps = rows // (NA_Q_ROWS * NA_TILES_PER_STEP)
    tq = NA_Q_ROWS * GRID_W * NA_TILES_PER_STEP
    tk = NA_K_ROWS * GRID_W
    qmap = lambda bi, hp, t: (bi, t, hp)
    smap = lambda bi, hp, t: (bi, 0, hp)
    bmap = lambda bi, hp, t: (0, hp, 0, 0)
    return pl.pallas_call(
        functools.partial(_na_kernel, rows=rows),
        grid=(b, NA_W // LANES, n_steps),
        in_specs=[pl.BlockSpec((1, tq, LANES), qmap), pl.BlockSpec((1, tq, LANES), qmap),
                  pl.BlockSpec((1, n, LANES), smap), pl.BlockSpec((1, n, LANES), smap),
                  pl.BlockSpec((1, lc, LANES), smap), pl.BlockSpec((1, lc, LANES), smap),
                  pl.BlockSpec((3, 2, NA_Q_ROWS * GRID_W, tk), bmap)],
        out_specs=pl.BlockSpec((1, tq, LANES), qmap),
        out_shape=jax.ShapeDtypeStruct((b, n, NA_W), BF16),
        compiler_params=_cparams(("arbitrary", "arbitrary", "arbitrary")),
        name="neighborhood_attention",
    )(qr, qp, k, v, kc, vc, bias)


def _na_bias_tables(rpb, rows):
    n_tiles = rows // NA_Q_ROWS
    n_dr, n_dc = 2 * NA_WIN_R - 1, 2 * NA_WIN_C - 1
    qc = np.arange(GRID_W)
    cs = np.clip(qc - NA_WIN_C // 2, 0, GRID_W - NA_WIN_C)
    kcol = np.arange(GRID_W)
    ok_c = (kcol[None, :] >= cs[:, None]) & (kcol[None, :] < cs[:, None] + NA_WIN_C)
    dc = kcol[None, :] - qc[:, None] + NA_WIN_C - 1
    col_sel = np.concatenate([(dc[:, :, None] == np.arange(n_dc)) & ok_c[:, :, None],
                              ~ok_c[:, :, None]], axis=2)
    ext = jnp.pad(rpb.astype(F32), ((0, 0), (0, 1), (0, 1)), constant_values=NEG_BIG)
    blocks = jnp.einsum("hrd,qkd->hrqk", ext, jnp.asarray(col_sel, F32), precision=lax.Precision.HIGHEST)
    tabs = []
    for t in (0, 1, n_tiles - 1):
        r0 = t * NA_Q_ROWS
        ks = int(np.clip(r0 - NA_WIN_R // 2, 0, rows - NA_K_ROWS))
        r = r0 + np.arange(NA_Q_ROWS)
        rs = np.clip(r - NA_WIN_R // 2, 0, rows - NA_WIN_R)
        krow = ks + np.arange(NA_K_ROWS)
        ok_r = (krow[None, :] >= rs[:, None]) & (krow[None, :] < rs[:, None] + NA_WIN_R)
        dr = np.where(ok_r, krow[None, :] - r[:, None] + NA_WIN_R - 1, n_dr)
        tabs.append(jnp.concatenate(
            [jnp.concatenate([blocks[:, int(dr[i, j])] for j in range(NA_K_ROWS)], axis=-1)
             for i in range(NA_Q_ROWS)], axis=-2))
    return jnp.stack(tabs)


def _dense_attn_kernel(q_ref, k_ref, v_ref, o_ref):
    q = q_ref[0]
    k = k_ref[0]
    v = v_ref[0]
    lane = lax.broadcasted_iota(jnp.int32, q.shape, 1)
    outs = []
    for hh in range(2):
        keep = (lane < HEAD_DIM) if hh == 0 else (lane >= HEAD_DIM)
        s = lax.dot_general(jnp.where(keep, q, jnp.zeros_like(q)), k, _NT, preferred_element_type=F32)
        outs.append(_softmax_pv([s], [v]))
    o_ref[0] = jnp.where(lane < HEAD_DIM, outs[0], outs[1]).astype(BF16)


def _dense_attention(q, k, v):
    b, l, _ = q.shape
    spec = pl.BlockSpec((1, l, LANES), lambda bi, hp: (bi, 0, hp))
    return pl.pallas_call(
        _dense_attn_kernel,
        grid=(b, NA_W // LANES),
        in_specs=[spec, spec, spec],
        out_specs=spec,
        out_shape=jax.ShapeDtypeStruct((b, l, NA_W), BF16),
        compiler_params=_cparams(("arbitrary", "arbitrary")),
        name="context_attention",
    )(q, k, v)


POOL_HALO = max(POOL_SIZES) // 2


def _pool_kernel(x_ref, w_ref, sc_ref, o_ref, xpad, *, seq_len, chunk):
    zeros = jnp.zeros((POOL_HALO, POOL_W), F32)
    xpad[pl.ds(0, POOL_HALO), :] = zeros
    xpad[pl.ds(seq_len + POOL_HALO, POOL_HALO), :] = zeros
    xpad[pl.ds(POOL_HALO, seq_len), :] = x_ref[0].astype(F32)
    span = chunk + 2 * POOL_HALO
    w = w_ref[...]
    sc = sc_ref[...]

    def body(c, carry):
        base = pl.multiple_of(c * chunk, chunk)
        xs = xpad[pl.ds(base, span), :]
        s2 = xs + pltpu.roll(xs, 1, 0)
        s4 = pltpu.roll(s2, 1, 0) + pltpu.roll(s2, span - 1, 0)
        s8 = pltpu.roll(s4, 2, 0) + pltpu.roll(s4, span - 2, 0)
        s16 = pltpu.roll(s8, 4, 0) + pltpu.roll(s8, span - 4, 0)
        mid = slice(POOL_HALO, POOL_HALO + chunk)
        grp = lax.broadcasted_iota(jnp.int32, (chunk, POOL_W), 1) // POOL_GROUP
        tpos = base + lax.broadcasted_iota(jnp.int32, (chunk, POOL_W), 0)
        half = jnp.left_shift(1, grp)
        cnt = (jnp.minimum(tpos + half, seq_len) - jnp.maximum(tpos - half, 0)).astype(F32)
        ssum = jnp.where(grp == 0, s2[mid], jnp.where(grp == 1, s4[mid], jnp.where(grp == 2, s8[mid], s16[mid])))
        diff = ssum / cnt - xs[mid]
        y = jnp.dot(diff.astype(BF16), w, preferred_element_type=F32) * sc
        o_ref[0, pl.ds(base, chunk), :] = y.astype(BF16)
        return carry

    lax.fori_loop(0, seq_len // chunk, body, 0)


def _pool_mixer(u, w_blockdiag_bf, pool_scale):
    b, l, _ = u.shape
    chunk = min(l, 512)
    return pl.pallas_call(
        functools.partial(_pool_kernel, seq_len=l, chunk=chunk),
        grid=(b,),
        in_specs=[pl.BlockSpec((1, l, POOL_W), lambda bi: (bi, 0, 0)),
                  pl.BlockSpec((POOL_W, POOL_W), lambda bi: (0, 0)),
                  pl.BlockSpec((1, POOL_W), lambda bi: (0, 0))],
        out_specs=pl.BlockSpec((1, l, POOL_W), lambda bi: (bi, 0, 0)),
        out_shape=jax.ShapeDtypeStruct((b, l, POOL_W), BF16),
        scratch_shapes=[pltpu.VMEM((l + 2 * POOL_HALO, POOL_W), F32)],
        compiler_params=_cparams(("arbitrary",)),
        name="pool_mixer",
    )(u, w_blockdiag_bf, pool_scale.reshape(1, POOL_W))


def _outproj_kernel(na_ref, hy_ref, pl_ref, w_ref, x_ref, g1_ref, nw_ref, sh_ref, sc_ref, wr_ref,
                    xn_ref, h2_ref, aff_ref):
    w = w_ref
    acc = jnp.dot(na_ref[...], w[pl.ds(0, NA_W), :], preferred_element_type=F32)
    acc += jnp.dot(hy_ref[...], w[pl.ds(NA_W, HY_W), :], preferred_element_type=F32)
    acc += jnp.dot(pl_ref[...], w[pl.ds(NA_W + HY_W, POOL_W), :], preferred_element_type=F32)
    xn = x_ref[...] + g1_ref[0] * acc
    xn_ref[...] = xn
    ms = jnp.mean(xn * xn, axis=-1, keepdims=True)
    h2 = xn * lax.rsqrt(ms + EPS) * nw_ref[...]
    h2 = h2 * (1.0 + sc_ref[0]) + sh_ref[0]
    tm = h2.shape[0]
    for s in range(TOKEN_SUBLANES):
        h2_ref[pl.ds(s, tm, stride=TOKEN_SUBLANES), :] = h2[:, s * LANES:(s + 1) * LANES]
    logits = lax.dot_general(wr_ref[...], h2, _NT, preferred_element_type=F32,
                             precision=lax.Precision.HIGHEST)
    mx = jnp.max(logits, axis=0, keepdims=True)
    p = jnp.exp(logits - mx)
    aff_ref[0] = p / jnp.sum(p, axis=0, keepdims=True)


def _out_projection(na, hy, pool, w_out_bf, x2, g1, norm_w, shift, scale, w_router_t, *,
                    rows_per_mod, seq_len, tm):
    m, d = x2.shape
    tiles_per_mod = rows_per_mod // tm
    tiles_per_seq = seq_len // tm
    row = lambda i: (i, 0)
    const = lambda i: (0, 0)
    modi = lambda i: (i // tiles_per_mod, 0, 0)
    return pl.pallas_call(
        _outproj_kernel,
        grid=(m // tm,),
        in_specs=[pl.BlockSpec((tm, NA_W), row), pl.BlockSpec((tm, HY_W), row), pl.BlockSpec((tm, POOL_W), row),
                  pl.BlockSpec((MIX_W, d), const),
                  pl.BlockSpec((tm, d), row),
                  pl.BlockSpec((1, 1, d), modi),
                  pl.BlockSpec((1, d), const),
                  pl.BlockSpec((1, 1, d), modi), pl.BlockSpec((1, 1, d), modi),
                  pl.BlockSpec((N_EXPERTS, d), const)],
        out_specs=[pl.BlockSpec((tm, d), row), pl.BlockSpec((tm * TOKEN_SUBLANES, LANES), row),
                   pl.BlockSpec((1, N_EXPERTS, tm), lambda i: (i // tiles_per_seq, 0, i % tiles_per_seq))],
        out_shape=[jax.ShapeDtypeStruct((m, d), F32), jax.ShapeDtypeStruct((m * TOKEN_SUBLANES, LANES), F32),
                   jax.ShapeDtypeStruct((m // seq_len, N_EXPERTS, seq_len), F32)],
        compiler_params=_cparams(("arbitrary",)),
        name="out_projection_router",
    )(na, hy, pool, w_out_bf, x2, g1, norm_w.reshape(1, d), shift, scale, w_router_t)


FFN_TF = 256
FFN_ROW_CHUNKS = 2


def _ffn_kernel(*refs, n_experts, sources, n_rows):
    ns = len(sources)
    idx_cur = refs[0:ns]
    idx_nxt = refs[ns:2 * ns]
    src = refs[2 * ns:3 * ns]
    gate_ref, wg_ref, wu_ref, wd_ref, o_ref, stage, xb, acc, sem = refs[3 * ns:]
    e = pl.program_id(0)
    f = pl.program_id(1)
    n_f = pl.num_programs(1)

    ts = TOKEN_SUBLANES

    def row_copy(s, row, r):
        return pltpu.make_async_copy(src[s].at[pl.ds(pl.multiple_of(row * ts, ts), ts)],
                                     stage.at[pl.ds(pl.multiple_of((sources[s][0] + r) * ts, ts), ts)], sem.at[0])

    def gather(idx_refs):
        for s in range(ns):
            def body(r, carry, s=s):
                row_copy(s, idx_refs[s][0, 0, r], r).start()
                return carry
            lax.fori_loop(0, sources[s][1], body, 0, unroll=8)

    def gather_wait():
        for s in range(ns):
            off, cnt = sources[s]
            pltpu.make_async_copy(src[s].at[pl.ds(0, cnt * ts)], stage.at[pl.ds(off * ts, cnt * ts)],
                                  sem.at[0]).wait()

    @pl.when(f == 0)
    def _():
        @pl.when(e == 0)
        def _():
            gather(idx_cur)

        gather_wait()

        def unpack(base, n):
            for s in range(ts):
                xb[pl.ds(base, n), s * LANES:(s + 1) * LANES] = (
                    stage[pl.ds(base * ts + s, n, stride=ts), :].astype(BF16))
            acc[pl.ds(base, n), :] = jnp.zeros((n, xb.shape[1]), F32)

        @pl.when(e + 1 < n_experts)
        def _():
            group = 16
            for s in range(ns):
                off, cnt = sources[s]

                def body(i, carry, s=s, off=off):
                    unpack(pl.multiple_of(off + i * group, group), group)
                    for k in range(group):
                        row_copy(s, idx_nxt[s][0, 0, i * group + k], i * group + k).start()
                    return carry

                lax.fori_loop(0, cnt // group, body, 0)

        @pl.when(e + 1 >= n_experts)
        def _():
            unpack(0, n_rows)

    wg = wg_ref[0, 0].astype(BF16)
    wu = wu_ref[0, 0].astype(BF16)
    wd = wd_ref[0, 0].astype(BF16)
    rc = n_rows // FFN_ROW_CHUNKS
    for r in range(FFN_ROW_CHUNKS):
        rs = pl.ds(r * rc, rc)
        x = xb[rs, :]
        a = jnp.dot(x, wg, preferred_element_type=F32)
        b = jnp.dot(x, wu, preferred_element_type=F32)
        h = (a * jax.nn.sigmoid(a) * b).astype(BF16)
        y = jnp.dot(h, wd, preferred_element_type=F32)
        acc[rs, :] += y

    @pl.when(f == n_f - 1)
    def _():
        o_ref[0] = (acc[...] * gate_ref[0]).astype(BF16)


def _expert_ffn(rows_idx, tokens, gate, w_gate, w_up, w_down, layer):
    ns = len(tokens)
    e, d = w_gate.shape[1], w_gate.shape[2]
    counts = [int(ri.shape[2]) for ri in rows_idx]
    offs = [int(sum(counts[:s])) for s in range(ns)]
    r_total = sum(counts)
    assert r_total % (FFN_ROW_CHUNKS * 16) == 0
    smem = lambda cnt, nxt: pl.BlockSpec(
        (1, 1, cnt), (lambda ei, f: (jnp.minimum(ei + 1, e - 1), 0, 0)) if nxt else (lambda ei, f: (ei, 0, 0)),
        memory_space=pltpu.SMEM)
    in_specs = ([smem(cnt, False) for cnt in counts] + [smem(cnt, True) for cnt in counts]
                + [pl.BlockSpec(memory_space=pl.ANY) for _ in range(ns)]
                + [pl.BlockSpec((1, r_total, 1), lambda ei, f: (ei, 0, 0)),
                   pl.BlockSpec((1, 1, d, FFN_TF), lambda ei, f: (layer, ei, 0, f)),
                   pl.BlockSpec((1, 1, d, FFN_TF), lambda ei, f: (layer, ei, 0, f)),
                   pl.BlockSpec((1, 1, FFN_TF, d), lambda ei, f: (layer, ei, f, 0))])
    return pl.pallas_call(
        functools.partial(_ffn_kernel, n_experts=e, sources=tuple(zip(offs, counts)), n_rows=r_total),
        grid=(e, D_EXPERT // FFN_TF),
        in_specs=in_specs,
        out_specs=pl.BlockSpec((1, r_total, d), lambda ei, f: (ei, 0, 0)),
        out_shape=jax.ShapeDtypeStruct((e, r_total, d), BF16),
        scratch_shapes=[pltpu.VMEM((r_total * TOKEN_SUBLANES, LANES), F32), pltpu.VMEM((r_total, d), BF16),
                        pltpu.VMEM((r_total, d), F32), pltpu.SemaphoreType.DMA((1,))],
        compiler_params=_cparams(("arbitrary", "arbitrary")),
        name="expert_ffn",
    )(*rows_idx, *rows_idx, *tokens, gate, w_gate, w_up, w_down)


def _prefix_count(m, n):
    lane = lax.broadcasted_iota(jnp.int32, m.shape, 1)
    c = m
    s = 1
    while s < n:
        c = c + jnp.where(lane >= s, pltpu.roll(c, s, 1), 0)
        s *= 2
    return c


COMBINE_TB = 256
COMBINE_CHUNK = 128


def _topk_kernel(aff_ref, idx_ref, gate_ref, off_ref, *, cap):
    a = aff_ref[...]
    rows, n = a.shape

    def search(i, t):
        cand = t | jnp.left_shift(jnp.int32(1), 30 - i)
        cnt = jnp.sum((a >= pltpu.bitcast(cand, F32)).astype(jnp.int32), axis=1, keepdims=True)
        return jnp.where(cnt >= cap, cand, t)

    thr_bits = lax.fori_loop(0, 31, search, jnp.zeros((rows, 1), jnp.int32))
    thr = pltpu.bitcast(thr_bits, F32)
    gt = a >= pltpu.bitcast(thr_bits + 1, F32)
    eq = (a >= thr) & jnp.logical_not(gt)
    need = cap - jnp.sum(gt.astype(jnp.int32), axis=1, keepdims=True)
    eq_i = eq.astype(jnp.int32)
    sel = gt | (eq & (_prefix_count(eq_i, n) - eq_i < need))
    sel_i = sel.astype(jnp.int32)
    lane = lax.broadcasted_iota(jnp.int32, a.shape, 1)
    dist = jnp.where(sel, lane + 1 - _prefix_count(sel_i, n), 0)
    tok = jnp.where(sel, lane, -1)
    val = a
    s = 1
    while s < n:
        tok_in = pltpu.roll(tok, n - s, 1)
        dist_in = pltpu.roll(dist, n - s, 1)
        val_in = pltpu.roll(val, n - s, 1)
        take = (lane < n - s) & (tok_in >= 0) & ((dist_in & s) != 0)
        stay = (tok >= 0) & ((dist & s) == 0)
        tok = jnp.where(take, tok_in, jnp.where(stay, tok, -1))
        dist = jnp.where(take, dist_in, jnp.where(stay, dist, 0))
        val = jnp.where(take, val_in, val)
        s *= 2
    idx_ref[...] = tok[:, :cap]
    gate_ref[...] = val[:, :cap]
    olane = lax.broadcasted_iota(jnp.int32, off_ref.shape, 1)
    off = jnp.zeros(off_ref.shape, jnp.int32)
    for j in range(n // COMBINE_TB + 1):
        below = jnp.sum(jnp.where(lane < j * COMBINE_TB, sel_i, 0), axis=1, keepdims=True)
        off = jnp.where(olane == j, below, off)
    off_ref[...] = off


def _topk_select(aff, cap):
    b, e, n = aff.shape
    assert n & (n - 1) == 0 and n // COMBINE_TB < LANES
    rows = b * e
    full = lambda w: pl.BlockSpec((rows, w), lambda i: (0, 0))
    idx, gate, off = pl.pallas_call(
        functools.partial(_topk_kernel, cap=cap),
        grid=(1,),
        in_specs=[full(n)],
        out_specs=[full(cap), full(cap), full(LANES)],
        out_shape=[jax.ShapeDtypeStruct((rows, cap), jnp.int32), jax.ShapeDtypeStruct((rows, cap), F32),
                   jax.ShapeDtypeStruct((rows, LANES), jnp.int32)],
        compiler_params=_cparams(("arbitrary",)),
        name="expert_choice_topk",
    )(aff.reshape(rows, n))
    return idx.reshape(b, e, cap), gate.reshape(b, e, cap), off


def _combine_kernel(off_ref, y_ref, idx_ref, x_ref, g_ref, o_ref, acc, ycat, *, chunk, n_chunks):
    bi = pl.program_id(0)
    j = pl.program_id(1)
    tb = x_ref.shape[0]
    tok = j * tb + lax.broadcasted_iota(jnp.int32, (tb, chunk), 0)

    def onehot(e, st):
        return (idx_ref[0, pl.ds(e, 1), pl.ds(st, chunk)] == tok).astype(BF16)

    if n_chunks == 1:
        acc[...] = jnp.zeros_like(acc)
        for e in range(N_EXPERTS):
            acc[...] += jnp.dot(onehot(e, 0), y_ref[e], preferred_element_type=F32)
    else:
        cap = n_chunks * chunk
        shift = chunk.bit_length() - 1
        col = lax.broadcasted_iota(jnp.int32, (1, chunk), 1)

        def slot_ids(e, st):
            base = jnp.minimum(lax.shift_left(lax.shift_right_logical(st, shift), shift), cap - 2 * chunk)
            wide = idx_ref[0, pl.ds(e, 1), pl.ds(pl.multiple_of(base, chunk), 2 * chunk)]
            wide = jnp.broadcast_to(wide, (8, 2 * chunk))
            return pltpu.roll(wide, (2 * chunk - (st - base)) & (2 * chunk - 1), 1)[:1, :chunk]

        starts, ends, hots = [], [], []
        for e in range(N_EXPERTS):
            lo = off_ref[bi * N_EXPERTS + e, j]
            hi = off_ref[bi * N_EXPERTS + e, j + 1]
            st = jnp.minimum(lax.shift_left(lax.shift_right_logical(lo, 4), 4), cap - chunk)
            starts.append(st)
            ends.append(hi)
            ycat[pl.ds(e * chunk, chunk), :] = y_ref[e, pl.ds(pl.multiple_of(st, 16), chunk), :]
            hots.append((slot_ids(e, st) == tok).astype(BF16))
        acc[...] = jnp.dot(jnp.concatenate(hots, axis=1), ycat[...], preferred_element_type=F32)
        for e in range(N_EXPERTS):
            def add_window(c, carry, e=e):
                first = starts[e] + c * chunk
                st = jnp.minimum(first, cap - chunk)
                hot = ((slot_ids(e, st) == tok) & (st + col >= first)).astype(BF16)
                acc[...] += jnp.dot(hot, y_ref[e, pl.ds(pl.multiple_of(st, 16), chunk), :],
                                    preferred_element_type=F32)
                return carry

            n_more = lax.shift_right_logical(jnp.maximum(ends[e] - starts[e] - 1, 0), shift)
            lax.fori_loop(1, n_more + 1, add_window, 0)
    o_ref[...] = x_ref[...] + g_ref[0] * acc[...]


def _moe_combine(off, y, idx, x2, g, *, row_block0, cap, seq_len):
    bsz = idx.shape[0]
    d = x2.shape[1]
    tb = min(seq_len, COMBINE_TB)
    nb = seq_len // tb
    chunk = min(cap, COMBINE_CHUNK)
    n_chunks = cap // chunk
    n_mod = g.shape[0]
    grid_spec = pltpu.PrefetchScalarGridSpec(
        num_scalar_prefetch=1,
        grid=(bsz, nb),
        in_specs=[pl.BlockSpec((N_EXPERTS, cap, d), lambda bi, j, off: (0, row_block0 + bi, 0),
                               pipeline_mode=pl.Buffered(1)),
                  pl.BlockSpec((1, N_EXPERTS, cap), lambda bi, j, off: (bi, 0, 0)),
                  pl.BlockSpec((tb, d), lambda bi, j, off: (bi * nb + j, 0)),
                  pl.BlockSpec((1, 1, d), lambda bi, j, off: (bi if n_mod > 1 else 0, 0, 0))],
        out_specs=pl.BlockSpec((tb, d), lambda bi, j, off: (bi * nb + j, 0)),
        scratch_shapes=[pltpu.VMEM((tb, d), F32), pltpu.VMEM((N_EXPERTS * chunk, d), BF16)])
    return pl.pallas_call(
        functools.partial(_combine_kernel, chunk=chunk, n_chunks=n_chunks),
        grid_spec=grid_spec,
        out_shape=jax.ShapeDtypeStruct(x2.shape, F32),
        compiler_params=_cparams(("arbitrary", "arbitrary")),
        name="moe_combine",
    )(off, y, idx, x2, g)


def _expert_choice_moe(streams, w_gate, w_up, w_down, layer):
    routed = []
    for h2, aff, x2, g in streams:
        bsz, _, l = aff.shape
        cap = EC_FACTOR * l // N_EXPERTS
        idx, gate, off = _topk_select(aff, cap)
        rows = (idx + (jnp.arange(bsz, dtype=jnp.int32) * l)[:, None, None]).transpose(1, 0, 2)
        routed.append((idx, off, rows.reshape(N_EXPERTS, 1, bsz * cap), gate.transpose(1, 0, 2).reshape(
            N_EXPERTS, bsz * cap, 1), cap))
    y = _expert_ffn([r[2] for r in routed], [s[0] for s in streams],
                    jnp.concatenate([r[3] for r in routed], axis=1), w_gate, w_up, w_down, layer)
    outs = []
    row0 = 0
    for (h2, aff, x2, g), (idx, off, rows, _, cap) in zip(streams, routed):
        assert row0 % cap == 0
        outs.append(_moe_combine(off, y, idx, x2, g, row_block0=row0 // cap, cap=cap, seq_len=aff.shape[2]))
        row0 += rows.shape[2]
    return outs


HY_R = 128
HY_COLS = 4096
HY_KB = 8
HY_UNROLL = 8


def _dft_consts(r):
    k = np.arange(r)
    ang = 2.0 * np.pi * np.outer(k, k) / r
    fre, fim = np.cos(ang), -np.sin(ang)
    k2 = np.arange(r)
    tang = 2.0 * np.pi * np.outer(k2, k2) / (r * r)
    tw = np.stack([np.cos(tang), -np.sin(tang)])
    fwd = np.block([[fre, -fim], [fim, fre]])
    inv = np.block([[fre, fim], [-fim, fre]])
    return fre, fim, tw, fwd, inv


def _hy_prep_kernel(x_ref, prev_ref, next_ref, w_ref, b_ref, x1_ref, x2_ref, v_ref, *, n_tiles):
    t = pl.program_id(1)
    x = x_ref[0].astype(F32)
    tm = x.shape[0]
    row = lax.broadcasted_iota(jnp.int32, x.shape, 0)
    halo = prev_ref.shape[1]
    prev = jnp.where(t == 0, 0.0, prev_ref[0].astype(F32)[halo - 1:halo, :])
    nxt = jnp.where(t == n_tiles - 1, 0.0, next_ref[0].astype(F32)[0:1, :])
    up = jnp.where(row == 0, prev, pltpu.roll(x, 1, 0))
    dn = jnp.where(row == tm - 1, nxt, pltpu.roll(x, tm - 1, 0))
    y = b_ref[...] + up * w_ref[pl.ds(0, 1), :] + x * w_ref[pl.ds(1, 1), :] + dn * w_ref[pl.ds(2, 1), :]
    x1_ref[0] = y[:, :HY_W].astype(BF16)
    x2_ref[0] = y[:, HY_W:2 * HY_W].astype(BF16)
    v_ref[0] = y[:, 2 * HY_W:].astype(BF16)


def _hyena_prep(u, conv_w, conv_b):
    b, l, c = u.shape
    tm = min(l, 1024)
    n_tiles = l // tm
    halo = 16
    g = tm // halo
    n_groups = l // halo
    out = jax.ShapeDtypeStruct((b, l, HY_W), BF16)
    ospec = pl.BlockSpec((1, tm, HY_W), lambda bi, t: (bi, t, 0))
    return pl.pallas_call(
        functools.partial(_hy_prep_kernel, n_tiles=n_tiles),
        grid=(b, n_tiles),
        in_specs=[pl.BlockSpec((1, tm, c), lambda bi, t: (bi, t, 0)),
                  pl.BlockSpec((1, halo, c), lambda bi, t: (bi, jnp.maximum(t * g - 1, 0), 0)),
                  pl.BlockSpec((1, halo, c), lambda bi, t: (bi, jnp.minimum((t + 1) * g, n_groups - 1), 0)),
                  pl.BlockSpec((HY_SHORT, c), lambda bi, t: (0, 0)),
                  pl.BlockSpec((1, c), lambda bi, t: (0, 0))],
        out_specs=[ospec, ospec, ospec],
        out_shape=[out, out, out],
        compiler_params=_cparams(("arbitrary", "arbitrary")),
        name="hyena_prep",
    )(u, u, u, conv_w, conv_b.reshape(1, c))


def _hy_filter_kernel(z_ref, w1_ref, b1_ref, w2_ref, b2_ref, w3_ref, fr_ref, dl_ref, k_ref, ss_ref, *, seq_len):
    i = pl.program_id(0)
    hp = lax.Precision.HIGHEST
    z = z_ref[...]
    tm = z.shape[0]
    half = tm // 2
    fr = fr_ref[...]
    zz = jnp.concatenate([z[:half], z[half:]], axis=1)
    h = jnp.sin(fr * (jnp.dot(zz, w1_ref[...], preferred_element_type=F32, precision=hp) + b1_ref[...]))
    h = jnp.sin(fr * (jnp.dot(h, w2_ref[...], preferred_element_type=F32, precision=hp) + b2_ref[...]))
    hb = h.astype(BF16)
    w3 = w3_ref[...].astype(BF16)
    fo = w3.shape[0]
    h = jnp.concatenate([jnp.dot(hb[:, :fo], w3, preferred_element_type=F32),
                         jnp.dot(hb[:, fo:], w3, preferred_element_type=F32)], axis=0)
    dec = jnp.exp(-z[:, 0:1] * dl_ref[...])
    r = i * tm + lax.broadcasted_iota(jnp.int32, (tm, HY_W), 0)
    parts = []
    for o in range(HY_ORDER):
        hf = h[:, (2 * o) * HY_W:(2 * o + 1) * HY_W]
        hb = h[:, (2 * o + 1) * HY_W:(2 * o + 2) * HY_W]
        parts.append(jnp.where(r < seq_len, hf, jnp.where(r > seq_len, hb, 0.0)) * dec)
    k = jnp.concatenate(parts, axis=1)
    k_ref[...] = k
    ss = jnp.sum(k * k, axis=0, keepdims=True)

    @pl.when(i == 0)
    def _():
        ss_ref[...] = ss

    @pl.when(i != 0)
    def _():
        ss_ref[...] += ss


def _hyena_filter_taps(l, w1, b1, w2, b2, w3, freq):
    t = jnp.linspace(0.0, 1.0, l, dtype=F32)[:, None]
    w = (2.0 * math.pi / l) * jnp.arange(l, dtype=F32)[:, None]
    f = jnp.linspace(1e-4, HY_BANDS - 1, HY_BANDS, dtype=F32)[None, :]
    z = jnp.concatenate([t, jnp.cos(f * w), -jnp.sin(f * w)], axis=-1)
    z2 = jnp.concatenate([z, z[:1], z[:0:-1]], axis=0)
    emb = z.shape[1]
    z2 = jnp.pad(z2, ((0, 0), (0, LANES - emb)))
    w1p = jnp.pad(w1, ((0, LANES - emb), (0, 0)))
    two = lambda m: jnp.kron(jnp.eye(2, dtype=m.dtype), m)
    pair = lambda v: jnp.tile(v, 2).reshape(1, 2 * v.shape[0])
    deltas = jnp.abs(jnp.linspace(math.log(HY_DECAY_TARGET) / HY_SLOW_DECAY,
                                  math.log(HY_DECAY_TARGET) / HY_FAST_DECAY, HY_W, dtype=F32))
    fo = w2.shape[0]
    nout = w3.shape[1]
    tm = min(2 * l, 1024)
    const = lambda i: (0, 0)
    return pl.pallas_call(
        functools.partial(_hy_filter_kernel, seq_len=l),
        grid=(2 * l // tm,),
        in_specs=[pl.BlockSpec((tm, LANES), lambda i: (i, 0)),
                  pl.BlockSpec((2 * LANES, 2 * fo), const), pl.BlockSpec((1, 2 * fo), const),
                  pl.BlockSpec((2 * fo, 2 * fo), const), pl.BlockSpec((1, 2 * fo), const),
                  pl.BlockSpec((fo, nout), const), pl.BlockSpec((1, 2 * fo), const),
                  pl.BlockSpec((1, HY_W), const)],
        out_specs=[pl.BlockSpec((tm, HY_ORDER * HY_W), lambda i: (i, 0)),
                   pl.BlockSpec((1, HY_ORDER * HY_W), const)],
        out_shape=[jax.ShapeDtypeStruct((2 * l, HY_ORDER * HY_W), F32),
                   jax.ShapeDtypeStruct((1, HY_ORDER * HY_W), F32)],
        compiler_params=_cparams(("arbitrary",)),
        name="hyena_filter_taps",
    )(z2, two(w1p), pair(b1), two(w2), pair(b2), w3, pair(freq), deltas.reshape(1, HY_W))


def _left_matmul_kernel(f_ref, x_ref, o_ref):
    o_ref[0] = jnp.dot(f_ref[...], x_ref[0].astype(BF16), preferred_element_type=F32).astype(o_ref.dtype)


def _left_matmul(f_bf, x3, name):
    b, k, c = x3.shape
    m = f_bf.shape[0]
    return pl.pallas_call(
        _left_matmul_kernel,
        grid=(b, c // HY_COLS),
        in_specs=[pl.BlockSpec((m, k), lambda bi, j: (0, 0)),
                  pl.BlockSpec((1, k, HY_COLS), lambda bi, j: (bi, 0, j))],
        out_specs=pl.BlockSpec((1, m, HY_COLS), lambda bi, j: (bi, 0, j)),
        out_shape=jax.ShapeDtypeStruct((b, m, c), BF16),
        compiler_params=_cparams(("arbitrary", "arbitrary")),
        name=name,
    )(f_bf, x3)


def _twiddle_cols(tw_ref, k1):
    lane = lax.broadcasted_iota(jnp.int32, (HY_R, HY_R), 1)
    pick = lane == k1
    twr = jnp.sum(jnp.where(pick, tw_ref[0], 0.0), axis=1, keepdims=True)
    twi = jnp.sum(jnp.where(pick, tw_ref[1], 0.0), axis=1, keepdims=True)
    return twr, twi


def _hy_spectrum_kernel(a_ref, tw_ref, fwd_ref, ss_ref, kf_ref):
    kb = pl.program_id(0)
    scale = lax.rsqrt(ss_ref[...] + EPS) * (1.0 / (HY_R * HY_R))

    def body(kk, carry):
        twr, twi = _twiddle_cols(tw_ref, kb * HY_KB + kk)
        are = a_ref[0, 0, kk].astype(F32)
        aim = a_ref[0, 1, kk].astype(F32)
        bst = jnp.concatenate([are * twr - aim * twi, are * twi + aim * twr], axis=0).astype(BF16)
        x = jnp.dot(fwd_ref[...], bst, preferred_element_type=F32)
        kf_ref[0, kk] = x[:HY_R] * scale
        kf_ref[1, kk] = x[HY_R:] * scale
        return carry

    lax.fori_loop(0, HY_KB, body, 0, unroll=HY_UNROLL)


def _hy_spectrum(a5, tw, fwd_bf, ss):
    nch = a5.shape[-1]
    return pl.pallas_call(
        _hy_spectrum_kernel,
        grid=(HY_R // HY_KB,),
        in_specs=[pl.BlockSpec((1, 2, HY_KB, HY_R, nch), lambda kb: (0, 0, kb, 0, 0)),
                  pl.BlockSpec((2, HY_R, HY_R), lambda kb: (0, 0, 0)),
                  pl.BlockSpec((2 * HY_R, 2 * HY_R), lambda kb: (0, 0)),
                  pl.BlockSpec((1, nch), lambda kb: (0, 0))],
        out_specs=pl.BlockSpec((2, HY_KB, HY_R, nch), lambda kb: (0, kb, 0, 0)),
        out_shape=jax.ShapeDtypeStruct((2, HY_R, HY_R, nch), F32),
        compiler_params=_cparams(("arbitrary",)),
        name="hyena_filter_spectrum",
    )(a5, tw, fwd_bf, ss)


def _hy_mid_kernel(a_ref, kf_ref, tw_ref, fwd_ref, inv_ref, z_ref):
    kb = pl.program_id(0)

    def body(kk, carry):
        twr, twi = _twiddle_cols(tw_ref, kb * HY_KB + kk)
        are = a_ref[0, 0, kk].astype(F32)
        aim = a_ref[0, 1, kk].astype(F32)
        bst = jnp.concatenate([are * twr - aim * twi, are * twi + aim * twr], axis=0).astype(BF16)
        x = jnp.dot(fwd_ref[...], bst, preferred_element_type=F32)
        xre, xim = x[:HY_R], x[HY_R:]
        kre, kim = kf_ref[0, kk], kf_ref[1, kk]
        yst = jnp.concatenate([xre * kre - xim * kim, xre * kim + xim * kre], axis=0).astype(BF16)
        zz = jnp.dot(inv_ref[...], yst, preferred_element_type=F32)
        zre, zim = zz[:HY_R], zz[HY_R:]
        z_ref[0, 0, kk] = (zre * twr + zim * twi).astype(BF16)
        z_ref[0, 1, kk] = (zim * twr - zre * twi).astype(BF16)
        return carry

    lax.fori_loop(0, HY_KB, body, 0, unroll=HY_UNROLL)


def _hy_mid(a5, kf, order, tw, fwd_bf, inv_bf):
    b = a5.shape[0]
    blk = (1, 2, HY_KB, HY_R, HY_W)
    return pl.pallas_call(
        _hy_mid_kernel,
        grid=(HY_R // HY_KB, b),
        in_specs=[pl.BlockSpec(blk, lambda kb, bi: (bi, 0, kb, 0, 0)),
                  pl.BlockSpec((2, HY_KB, HY_R, HY_W), lambda kb, bi: (0, kb, 0, order)),
                  pl.BlockSpec((2, HY_R, HY_R), lambda kb, bi: (0, 0, 0)),
                  pl.BlockSpec((2 * HY_R, 2 * HY_R), lambda kb, bi: (0, 0)),
                  pl.BlockSpec((2 * HY_R, 2 * HY_R), lambda kb, bi: (0, 0))],
        out_specs=pl.BlockSpec(blk, lambda kb, bi: (bi, 0, kb, 0, 0)),
        out_shape=jax.ShapeDtypeStruct(a5.shape, BF16),
        compiler_params=_cparams(("arbitrary", "arbitrary")),
        name="hyena_spectral_product",
    )(a5, kf, tw, fwd_bf, inv_bf)


def _hy_out_kernel(f_ref, z_ref, g_ref, w_ref, sk_ref, o_ref):
    y = jnp.dot(f_ref[...], z_ref[0], preferred_element_type=F32)
    w = w_ref[0].astype(F32)
    o_ref[0] = (g_ref[0].astype(F32) * (y + w * sk_ref[...])).astype(BF16)


def _hy_out(f_bf, z3, gate3, w3, skip_row):
    b, m2, c = z3.shape
    m = f_bf.shape[0]
    dspec = pl.BlockSpec((1, m, HY_COLS), lambda bi, j: (bi, 0, j))
    return pl.pallas_call(
        _hy_out_kernel,
        grid=(b, c // HY_COLS),
        in_specs=[pl.BlockSpec((m, m2), lambda bi, j: (0, 0)),
                  pl.BlockSpec((1, m2, HY_COLS), lambda bi, j: (bi, 0, j)),
                  dspec, dspec,
                  pl.BlockSpec((1, HY_COLS), lambda bi, j: (0, 0))],
        out_specs=dspec,
        out_shape=jax.ShapeDtypeStruct((b, m, c), BF16),
        compiler_params=_cparams(("arbitrary", "arbitrary")),
        name="hyena_inverse_gate",
    )(f_bf, z3, gate3, w3, skip_row)


def _hyena_long(u, conv_w, conv_b, skip, filt_args):
    b, l, _ = u.shape
    assert 2 * l == HY_R * HY_R
    fre, fim, tw, fwd, inv = _dft_consts(HY_R)
    half = HY_R // 2
    as_bf = lambda m: jnp.asarray(m, F32).astype(BF16)
    f_a_data = as_bf(np.concatenate([fre[:, :half], fim[:, :half]], axis=0))
    f_a_filt = as_bf(np.concatenate([fre, fim], axis=0))
    f_c = as_bf(np.concatenate([fre[:half], fim[:half]], axis=1))
    tw_j = jnp.asarray(tw, F32)
    fwd_bf = as_bf(fwd)
    inv_bf = as_bf(inv)

    taps, ss = _hyena_filter_taps(l, *filt_args)
    nch = HY_ORDER * HY_W
    ak = _left_matmul(f_a_filt, taps.reshape(1, HY_R, HY_R * nch), "hyena_filter_dft_a")
    kf = _hy_spectrum(ak.reshape(1, 2, HY_R, HY_R, nch), tw_j, fwd_bf, ss)

    x1, x2, v = _hyena_prep(u, conv_w, conv_b)
    z = v
    for o, gate in enumerate((x1, x2)):
        z3 = z.reshape(b, half, HY_R * HY_W)
        a = _left_matmul(f_a_data, z3, "hyena_dft_a")
        zs = _hy_mid(a.reshape(b, 2, HY_R, HY_R, HY_W), kf, o, tw_j, fwd_bf, inv_bf)
        skip_row = jnp.tile(skip[o], HY_COLS // HY_W).reshape(1, HY_COLS)
        z = _hy_out(f_c, zs.reshape(b, 2 * HY_R, HY_R * HY_W), gate.reshape(b, half, HY_R * HY_W), z3,
                    skip_row).reshape(b, l, HY_W)
    return z


def _hy_ctx_kernel(x1_ref, x2_ref, v_ref, taps_ref, ss_ref, fa_ref, fc_ref, sk_ref, o_ref, *, seq_len):
    n2 = 2 * seq_len
    fa = fa_ref[...]
    scale = lax.rsqrt(ss_ref[...] + EPS) * (1.0 / n2)
    kf = jnp.dot(fa, taps_ref[...].astype(BF16), preferred_element_type=F32) * scale
    z = v_ref[0].astype(F32)
    for o, g_ref in enumerate((x1_ref, x2_ref)):
        x = jnp.dot(fa[:, :seq_len], z.astype(BF16), preferred_element_type=F32)
        xre, xim = x[:n2], x[n2:]
        kre = kf[:n2, o * HY_W:(o + 1) * HY_W]
        kim = kf[n2:, o * HY_W:(o + 1) * HY_W]
        yst = jnp.concatenate([xre * kre - xim * kim, xre * kim + xim * kre], axis=0).astype(BF16)
        y = jnp.dot(fc_ref[...], yst, preferred_element_type=F32)
        z = g_ref[0].astype(F32) * (y + z * sk_ref[pl.ds(o, 1), :])
    o_ref[0] = z.astype(BF16)


def _hyena_short(u, conv_w, conv_b, skip, filt_args):
    b, l, _ = u.shape
    n2 = 2 * l
    k = np.arange(n2)
    ang = 2.0 * np.pi * np.outer(k, k) / n2
    fre, fim = np.cos(ang), -np.sin(ang)
    fa = jnp.asarray(np.concatenate([fre, fim], axis=0), F32).astype(BF16)
    fc = jnp.asarray(np.concatenate([fre[:l], fim[:l]], axis=1), F32).astype(BF16)
    taps, ss = _hyena_filter_taps(l, *filt_args)
    x1, x2, v = _hyena_prep(u, conv_w, conv_b)
    nch = HY_ORDER * HY_W
    dspec = pl.BlockSpec((1, l, HY_W), lambda bi: (bi, 0, 0))
    const = lambda bi: (0, 0)
    return pl.pallas_call(
        functools.partial(_hy_ctx_kernel, seq_len=l),
        grid=(b,),
        in_specs=[dspec, dspec, dspec,
                  pl.BlockSpec((n2, nch), const), pl.BlockSpec((1, nch), const),
                  pl.BlockSpec((2 * n2, n2), const), pl.BlockSpec((l, 2 * n2), const),
                  pl.BlockSpec((HY_ORDER, HY_W), const)],
        out_specs=dspec,
        out_shape=jax.ShapeDtypeStruct((b, l, HY_W), BF16),
        compiler_params=_cparams(("arbitrary",)),
        name="hyena_context",
    )(x1, x2, v, taps, ss, fa, fc, skip)


def _rope_tables(n):
    t = jnp.arange(n, dtype=jnp.int32)
    pos = jnp.stack([t // GRID_W, t % GRID_W], axis=-1).astype(F32)
    nf = HEAD_DIM // 4
    inv = ROPE_THETA ** (-jnp.arange(nf, dtype=F32) / nf)
    ang = pos[:, :, None] * inv
    cos, sin = jnp.cos(ang), jnp.sin(ang)
    c64 = jnp.concatenate([cos[:, 0], cos[:, 0], cos[:, 1], cos[:, 1]], axis=-1)
    s64 = jnp.concatenate([-sin[:, 0], sin[:, 0], -sin[:, 1], sin[:, 1]], axis=-1)
    return jnp.tile(c64, (1, 2)), jnp.tile(s64, (1, 2))


def _block_diag(pool_w):
    g, c, _ = pool_w.shape
    out = jnp.zeros((g * c, g * c), pool_w.dtype)
    for i in range(g):
        out = out.at[i * c:(i + 1) * c, i * c:(i + 1) * c].set(pool_w[i])
    return out


def kernel(x, c, ctx, c_ctx, w_mod, b_mod, norm1_w, norm2_w, w_in, w_out, q_norm_w, k_norm_w, na_rpb,
           hy_conv_w, hy_conv_b, hy_w1, hy_b1, hy_w2, hy_b2, hy_w3, hy_freq, hy_skip,
           pool_w, pool_scale, w_router, w_gate, w_up, w_down):
    b, n, d = x.shape
    lc = ctx.shape[1]
    depth = w_mod.shape[0]
    rows = n // GRID_W
    cos128, sin128 = _rope_tables(n)
    seg = jnp.asarray(np.kron(np.eye(2), np.full((HEAD_DIM, HEAD_DIM), 1.0 / HEAD_DIM)), BF16)
    cc = jnp.zeros((8, d), F32).at[:b].set(c).at[b].set(c_ctx)

    xl = x.reshape(b * n, d)
    xc = ctx.reshape(b * lc, d)
    for i in range(depth):
        last = i == depth - 1
        mod = _modulation(cc, w_mod, b_mod[i], i)
        ml = mod[:b].reshape(b, 1, 6, d)
        mc = mod[b].reshape(1, 1, 6, d)
        sh1, sc1, g1, sh2, sc2, g2 = [ml[:, :, j] for j in range(6)]
        csh1, csc1, cg1, csh2, csc2, cg2 = [mc[:, :, j] for j in range(6)]
        w_in_bf = w_in[i].astype(BF16)
        w_out_bf = w_out[i].astype(BF16)
        w_router_t = w_router[i].T
        pool_bd = _block_diag(pool_w[i]).astype(BF16)
        hy_args = (hy_w1[i], hy_b1[i], hy_w2[i], hy_b2[i], hy_w3[i], hy_freq[i])

        qc, kc, vc, hyc, poolc = _in_projection(
            xc, csh1, csc1, norm1_w[i], w_in_bf, q_norm_w[i], k_norm_w[i], cos128, sin128, seg,
            rows_per_mod=b * lc, seq_len=lc, tm=lc, rope=False)
        kc = kc.reshape(b, lc, NA_W)
        vc = vc.reshape(b, lc, NA_W)
        if not last:
            na_c = _dense_attention(qc.reshape(b, lc, NA_W), kc, vc).reshape(b * lc, NA_W)
            hy_c = _hyena_short(hyc.reshape(b, lc, 3 * HY_W), hy_conv_w[i], hy_conv_b[i], hy_skip[i], hy_args)
            pool_c = _pool_mixer(poolc.reshape(b, lc, POOL_W), pool_bd, pool_scale[i])
            xc1, h2c, affc = _out_projection(
                na_c, hy_c.reshape(b * lc, HY_W).astype(BF16), pool_c.reshape(b * lc, POOL_W), w_out_bf,
                xc, cg1, norm2_w[i], csh2, csc2, w_router_t, rows_per_mod=b * lc, seq_len=lc, tm=lc)

        qr, qp, kl, vl, hyl, pooll = _in_projection(
            xl, sh1, sc1, norm1_w[i], w_in_bf, q_norm_w[i], k_norm_w[i], cos128, sin128, seg,
            rows_per_mod=n, seq_len=n, tm=512, rope=True)
        bias = _na_bias_tables(na_rpb[i], rows)
        o_na = _neighborhood_attention(qr.reshape(b, n, NA_W), qp.reshape(b, n, NA_W), kl.reshape(b, n, NA_W),
                                       vl.reshape(b, n, NA_W), kc, vc, bias)
        hy_l = _hyena_long(hyl.reshape(b, n, 3 * HY_W), hy_conv_w[i], hy_conv_b[i], hy_skip[i], hy_args)
        pool_l = _pool_mixer(pooll.reshape(b, n, POOL_W), pool_bd, pool_scale[i])
        xl1, h2l, affl = _out_projection(
            o_na.reshape(b * n, NA_W), hy_l.reshape(b * n, HY_W).astype(BF16), pool_l.reshape(b * n, POOL_W),
            w_out_bf, xl, g1, norm2_w[i], sh2, sc2, w_router_t, rows_per_mod=n, seq_len=n, tm=512)
        streams = [(h2l, affl, xl1, g2)]
        if not last:
            streams.append((h2c, affc, xc1, cg2))
        outs = _expert_choice_moe(streams, w_gate, w_up, w_down, i)
        xl = outs[0]
        if not last:
            xc = outs[1]
    return xl.reshape(b, n, d)
```

```python
import functools
import math

import jax
import jax.numpy as jnp
import numpy as np
from jax import lax
from jax.experimental import pallas as pl
from jax.experimental.pallas import tpu as pltpu

F32 = jnp.float32
BF16 = jnp.bfloat16

D_MODEL = 1024
GRID_W = 64
NA_HEADS = 8
HEAD_DIM = 64
NA_W = NA_HEADS * HEAD_DIM
NA_WIN_R = 8
NA_WIN_C = 16
ROPE_THETA = 10000.0
HY_W = 256
HY_ORDER = 2
HY_SHORT = 3
HY_BANDS = 16
HY_DECAY_TARGET = 1e-2
HY_FAST_DECAY = 0.3
HY_SLOW_DECAY = 1.5
POOL_W = 256
POOL_SIZES = (2, 4, 8, 16)
POOL_GROUP = POOL_W // len(POOL_SIZES)
MIX_W = NA_W + HY_W + POOL_W
IN_W = 3 * NA_W + (HY_ORDER + 1) * HY_W + POOL_W
N_EXPERTS = 16
EC_FACTOR = 2
D_EXPERT = 2048
EPS = 1e-6

LANES = 128
TOKEN_SUBLANES = D_MODEL // 128
ROW_SUBTILES = 2
NEG_BIG = -1e30
VMEM_LIMIT = 56 * 1024 * 1024

NA_Q_ROWS = 4
NA_K_ROWS = NA_Q_ROWS + NA_WIN_R - 1
NA_TILES_PER_STEP = 4


def _cparams(sem):
    return pltpu.CompilerParams(dimension_semantics=sem, vmem_limit_bytes=VMEM_LIMIT)


def _mod_kernel(c_ref, w_ref, b_ref, o_ref):
    c = c_ref[...]
    s = c * jax.nn.sigmoid(c)
    o_ref[...] = jnp.dot(s.astype(BF16), w_ref[0].astype(BF16), preferred_element_type=F32) + b_ref[...]


def _modulation(cc, w_mod, b_mod, layer):
    rows, d = cc.shape
    n = w_mod.shape[2]
    tn = 1536
    return pl.pallas_call(
        _mod_kernel,
        grid=(n // tn,),
        in_specs=[pl.BlockSpec((rows, d), lambda j: (0, 0)),
                  pl.BlockSpec((1, d, tn), lambda j: (layer, 0, j)),
                  pl.BlockSpec((1, tn), lambda j: (0, j))],
        out_specs=pl.BlockSpec((rows, tn), lambda j: (0, j)),
        out_shape=jax.ShapeDtypeStruct((rows, n), F32),
        compiler_params=_cparams(("arbitrary",)),
        name="modulation",
    )(cc, w_mod, b_mod.reshape(1, n))


def _head_norm(xs, w128, seg):
    m = jnp.dot((xs * xs).astype(BF16), seg, preferred_element_type=F32)
    return xs * lax.rsqrt(m + EPS) * w128


def _rope(xs, c, s_signed):
    lane = lax.broadcasted_iota(jnp.int32, xs.shape, 1)
    partner = jnp.where((lane & 16) == 0, pltpu.roll(xs, LANES - 16, 1), pltpu.roll(xs, 16, 1))
    return xs * c + partner * s_signed


def _inproj_kernel(x_ref, sh_ref, sc_ref, nw_ref, w_ref, qw_ref, kw_ref, cos_ref, sin_ref, seg_ref,
                   *out_refs, rope):
    if rope:
        qr_ref, qp_ref, k_ref, v_ref, hy_ref, pool_ref = out_refs
    else:
        qp_ref, k_ref, v_ref, hy_ref, pool_ref = out_refs
    seg = seg_ref[...]
    qk_scale = HEAD_DIM ** -0.5
    sub = x_ref.shape[0] // ROW_SUBTILES
    for t in range(ROW_SUBTILES):
        rs = pl.ds(t * sub, sub)
        x = x_ref[rs, :]
        ms = jnp.mean(x * x, axis=-1, keepdims=True)
        h = x * lax.rsqrt(ms + EPS) * nw_ref[...]
        h = h * (1.0 + sc_ref[0]) + sh_ref[0]
        u = jnp.dot(h.astype(BF16), w_ref[...], preferred_element_type=F32)
        for ch in range(NA_W // LANES):
            sl = slice(ch * LANES, (ch + 1) * LANES)
            qn = _head_norm(u[:, sl], qw_ref[...], seg)
            kn = _head_norm(u[:, NA_W + ch * LANES:NA_W + (ch + 1) * LANES], kw_ref[...], seg)
            qp_ref[rs, sl] = (qn * qk_scale).astype(BF16)
            if rope:
                c = cos_ref[rs, :]
                s = sin_ref[rs, :]
                qr_ref[rs, sl] = (_rope(qn, c, s) * qk_scale).astype(BF16)
                k_ref[rs, sl] = _rope(kn, c, s).astype(BF16)
            else:
                k_ref[rs, sl] = kn.astype(BF16)
        v_ref[rs, :] = u[:, 2 * NA_W:3 * NA_W].astype(BF16)
        hy_ref[rs, :] = u[:, 3 * NA_W:3 * NA_W + 3 * HY_W].astype(BF16)
        pool_ref[rs, :] = u[:, 3 * NA_W + 3 * HY_W:].astype(BF16)


def _in_projection(x2, shift, scale, norm_w, w_in_bf, q_norm_w, k_norm_w, cos128, sin128, seg, *,
                   rows_per_mod, seq_len, tm, rope):
    m, d = x2.shape
    tiles_per_mod = rows_per_mod // tm
    tiles_per_seq = seq_len // tm
    row = lambda i: (i, 0)
    const = lambda i: (0, 0)
    modi = lambda i: (i // tiles_per_mod, 0, 0)
    pos = lambda i: (i % tiles_per_seq, 0)
    outs = []
    if rope:
        outs.append((NA_W, BF16))
    outs += [(NA_W, BF16), (NA_W, BF16), (NA_W, BF16), (3 * HY_W, BF16), (POOL_W, BF16)]
    return pl.pallas_call(
        functools.partial(_inproj_kernel, rope=rope),
        grid=(m // tm,),
        in_specs=[pl.BlockSpec((tm, d), row),
                  pl.BlockSpec((1, 1, d), modi), pl.BlockSpec((1, 1, d), modi),
                  pl.BlockSpec((1, d), const),
                  pl.BlockSpec((d, IN_W), const),
                  pl.BlockSpec((1, LANES), const), pl.BlockSpec((1, LANES), const),
                  pl.BlockSpec((tm, LANES), pos), pl.BlockSpec((tm, LANES), pos),
                  pl.BlockSpec((LANES, LANES), const)],
        out_specs=[pl.BlockSpec((tm, w), row) for w, _ in outs],
        out_shape=[jax.ShapeDtypeStruct((m, w), dt) for w, dt in outs],
        compiler_params=_cparams(("arbitrary",)),
        name="in_projection_rope" if rope else "in_projection_ctx",
    )(x2, shift, scale, norm_w.reshape(1, d), w_in_bf,
      jnp.tile(q_norm_w, 2).reshape(1, LANES), jnp.tile(k_norm_w, 2).reshape(1, LANES),
      cos128, sin128, seg)


def _softmax_pv(s_list, v_list):
    mx = functools.reduce(jnp.maximum, [jnp.max(s, axis=-1, keepdims=True) for s in s_list])
    ps = [jnp.exp(s - mx) for s in s_list]
    l = functools.reduce(jnp.add, [jnp.sum(p, axis=-1, keepdims=True) for p in ps])
    o = functools.reduce(jnp.add, [jnp.dot(p.astype(BF16), v, preferred_element_type=F32)
                                   for p, v in zip(ps, v_list)])
    return o / l


_NT = (((1,), (1,)), ((), ()))


def _na_kernel(qr_ref, qp_ref, k_ref, v_ref, kc_ref, vc_ref, bias_ref, o_ref, *, rows):
    n_tiles = rows // NA_Q_ROWS
    tq = NA_Q_ROWS * GRID_W
    kc = kc_ref[0]
    vc = vc_ref[0]
    lane = lax.broadcasted_iota(jnp.int32, (tq, LANES), 1)
    for sub in range(NA_TILES_PER_STEP):
        t = pl.program_id(2) * NA_TILES_PER_STEP + sub
        ks = jnp.clip(t * NA_Q_ROWS - NA_WIN_R // 2, 0, rows - NA_K_ROWS)
        start = pl.multiple_of(ks * GRID_W, GRID_W)
        k = k_ref[0, pl.ds(start, NA_K_ROWS * GRID_W), :]
        v = v_ref[0, pl.ds(start, NA_K_ROWS * GRID_W), :]
        q = qr_ref[0, pl.ds(sub * tq, tq), :]
        qp = qp_ref[0, pl.ds(sub * tq, tq), :]
        kind = jnp.where(t == 0, 0, jnp.where(t == n_tiles - 1, 2, 1))
        outs = []
        for hh in range(2):
            keep = (lane < HEAD_DIM) if hh == 0 else (lane >= HEAD_DIM)
            zero = jnp.zeros_like(q)
            s_lat = (lax.dot_general(jnp.where(keep, q, zero), k, _NT, preferred_element_type=F32)
                     + bias_ref[kind, hh])
            s_ctx = lax.dot_general(jnp.where(keep, qp, zero), kc, _NT, preferred_element_type=F32)
            outs.append(_softmax_pv([s_lat, s_ctx], [v, vc]))
        o_ref[0, pl.ds(sub * tq, tq), :] = jnp.where(lane < HEAD_DIM, outs[0], outs[1]).astype(BF16)


def _neighborhood_attention(qr, qp, k, v, kc, vc, bias):
    b, n, _ = qr.shape
    lc = kc.shape[1]
    rows = n // GRID_W
    n_steps = rows // (NA_Q_ROWS * NA_TILES_PER_STEP)
    tq = NA_Q_ROWS * GRID_W * NA_TILES_PER_STEP
    tk = NA_K_ROWS * GRID_W
    qmap = lambda bi, hp, t: (bi, t, hp)
    smap = lambda bi, hp, t: (bi, 0, hp)
    bmap = lambda bi, hp, t: (0, hp, 0, 0)
    return pl.pallas_call(
        functools.partial(_na_kernel, rows=rows),
        grid=(b, NA_W // LANES, n_steps),
        in_specs=[pl.BlockSpec((1, tq, LANES), qmap), pl.BlockSpec((1, tq, LANES), qmap),
                  pl.BlockSpec((1, n, LANES), smap), pl.BlockSpec((1, n, LANES), smap),
                  pl.BlockSpec((1, lc, LANES), smap), pl.BlockSpec((1, lc, LANES), smap),
                  pl.BlockSpec((3, 2, NA_Q_ROWS * GRID_W, tk), bmap)],
        out_specs=pl.BlockSpec((1, tq, LANES), qmap),
        out_shape=jax.ShapeDtypeStruct((b, n, NA_W), BF16),
        compiler_params=_cparams(("arbitrary", "arbitrary", "arbitrary")),
        name="neighborhood_attention",
    )(qr, qp, k, v, kc, vc, bias)


def _na_bias_tables(rpb, rows):
    n_tiles = rows // NA_Q_ROWS
    n_dr, n_dc = 2 * NA_WIN_R - 1, 2 * NA_WIN_C - 1
    qc = np.arange(GRID_W)
    cs = np.clip(qc - NA_WIN_C // 2, 0, GRID_W - NA_WIN_C)
    kcol = np.arange(GRID_W)
    ok_c = (kcol[None, :] >= cs[:, None]) & (kcol[None, :] < cs[:, None] + NA_WIN_C)
    dc = kcol[None, :] - qc[:, None] + NA_WIN_C - 1
    col_sel = np.concatenate([(dc[:, :, None] == np.arange(n_dc)) & ok_c[:, :, None],
                              ~ok_c[:, :, None]], axis=2)
    ext = jnp.pad(rpb.astype(F32), ((0, 0), (0, 1), (0, 1)), constant_values=NEG_BIG)
    blocks = jnp.einsum("hrd,qkd->hrqk", ext, jnp.asarray(col_sel, F32), precision=lax.Precision.HIGHEST)
    tabs = []
    for t in (0, 1, n_tiles - 1):
        r0 = t * NA_Q_ROWS
        ks = int(np.clip(r0 - NA_WIN_R // 2, 0, rows - NA_K_ROWS))
        r = r0 + np.arange(NA_Q_ROWS)
        rs = np.clip(r - NA_WIN_R // 2, 0, rows - NA_WIN_R)
        krow = ks + np.arange(NA_K_ROWS)
        ok_r = (krow[None, :] >= rs[:, None]) & (krow[None, :] < rs[:, None] + NA_WIN_R)
        dr = np.where(ok_r, krow[None, :] - r[:, None] + NA_WIN_R - 1, n_dr)
        tabs.append(jnp.concatenate(
            [jnp.concatenate([blocks[:, int(dr[i, j])] for j in range(NA_K_ROWS)], axis=-1)
             for i in range(NA_Q_ROWS)], axis=-2))
    return jnp.stack(tabs)


def _dense_attn_kernel(q_ref, k_ref, v_ref, o_ref):
    q = q_ref[0]
    k = k_ref[0]
    v = v_ref[0]
    lane = lax.broadcasted_iota(jnp.int32, q.shape, 1)
    outs = []
    for hh in range(2):
        keep = (lane < HEAD_DIM) if hh == 0 else (lane >= HEAD_DIM)
        s = lax.dot_general(jnp.where(keep, q, jnp.zeros_like(q)), k, _NT, preferred_element_type=F32)
        outs.append(_softmax_pv([s], [v]))
    o_ref[0] = jnp.where(lane < HEAD_DIM, outs[0], outs[1]).astype(BF16)


def _dense_attention(q, k, v):
    b, l, _ = q.shape
    spec = pl.BlockSpec((1, l, LANES), lambda bi, hp: (bi, 0, hp))
    return pl.pallas_call(
        _dense_attn_kernel,
        grid=(b, NA_W // LANES),
        in_specs=[spec, spec, spec],
        out_specs=spec,
        out_shape=jax.ShapeDtypeStruct((b, l, NA_W), BF16),
        compiler_params=_cparams(("arbitrary", "arbitrary")),
        name="context_attention",
    )(q, k, v)


POOL_HALO = max(POOL_SIZES) // 2


def _pool_kernel(x_ref, w_ref, sc_ref, o_ref, xpad, *, seq_len, chunk):
    zeros = jnp.zeros((POOL_HALO, POOL_W), F32)
    xpad[pl.ds(0, POOL_HALO), :] = zeros
    xpad[pl.ds(seq_len + POOL_HALO, POOL_HALO), :] = zeros
    xpad[pl.ds(POOL_HALO, seq_len), :] = x_ref[0].astype(F32)
    span = chunk + 2 * POOL_HALO
    w = w_ref[...]
    sc = sc_ref[...]

    def body(c, carry):
        base = pl.multiple_of(c * chunk, chunk)
        xs = xpad[pl.ds(base, span), :]
        s2 = xs + pltpu.roll(xs, 1, 0)
        s4 = pltpu.roll(s2, 1, 0) + pltpu.roll(s2, span - 1, 0)
        s8 = pltpu.roll(s4, 2, 0) + pltpu.roll(s4, span - 2, 0)
        s16 = pltpu.roll(s8, 4, 0) + pltpu.roll(s8, span - 4, 0)
        mid = slice(POOL_HALO, POOL_HALO + chunk)
        grp = lax.broadcasted_iota(jnp.int32, (chunk, POOL_W), 1) // POOL_GROUP
        tpos = base + lax.broadcasted_iota(jnp.int32, (chunk, POOL_W), 0)
        half = jnp.left_shift(1, grp)
        cnt = (jnp.minimum(tpos + half, seq_len) - jnp.maximum(tpos - half, 0)).astype(F32)
        ssum = jnp.where(grp == 0, s2[mid], jnp.where(grp == 1, s4[mid], jnp.where(grp == 2, s8[mid], s16[mid])))
        diff = ssum / cnt - xs[mid]
        y = jnp.dot(diff.astype(BF16), w, preferred_element_type=F32) * sc
        o_ref[0, pl.ds(base, chunk), :] = y.astype(BF16)
        return carry

    lax.fori_loop(0, seq_len // chunk, body, 0)


def _pool_mixer(u, w_blockdiag_bf, pool_scale):
    b, l, _ = u.shape
    chunk = min(l, 512)
    return pl.pallas_call(
        functools.partial(_pool_kernel, seq_len=l, chunk=chunk),
        grid=(b,),
        in_specs=[pl.BlockSpec((1, l, POOL_W), lambda bi: (bi, 0, 0)),
                  pl.BlockSpec((POOL_W, POOL_W), lambda bi: (0, 0)),
                  pl.BlockSpec((1, POOL_W), lambda bi: (0, 0))],
        out_specs=pl.BlockSpec((1, l, POOL_W), lambda bi: (bi, 0, 0)),
        out_shape=jax.ShapeDtypeStruct((b, l, POOL_W), BF16),
        scratch_shapes=[pltpu.VMEM((l + 2 * POOL_HALO, POOL_W), F32)],
        compiler_params=_cparams(("arbitrary",)),
        name="pool_mixer",
    )(u, w_blockdiag_bf, pool_scale.reshape(1, POOL_W))


def _outproj_kernel(na_ref, hy_ref, pl_ref, w_ref, x_ref, g1_ref, nw_ref, sh_ref, sc_ref, wr_ref,
                    xn_ref, h2_ref, aff_ref):
    w = w_ref
    acc = jnp.dot(na_ref[...], w[pl.ds(0, NA_W), :], preferred_element_type=F32)
    acc += jnp.dot(hy_ref[...], w[pl.ds(NA_W, HY_W), :], preferred_element_type=F32)
    acc += jnp.dot(pl_ref[...], w[pl.ds(NA_W + HY_W, POOL_W), :], preferred_element_type=F32)
    xn = x_ref[...] + g1_ref[0] * acc
    xn_ref[...] = xn
    ms = jnp.mean(xn * xn, axis=-1, keepdims=True)
    h2 = xn * lax.rsqrt(ms + EPS) * nw_ref[...]
    h2 = h2 * (1.0 + sc_ref[0]) + sh_ref[0]
    tm = h2.shape[0]
    for s in range(TOKEN_SUBLANES):
        h2_ref[pl.ds(s, tm, stride=TOKEN_SUBLANES), :] = h2[:, s * LANES:(s + 1) * LANES]
    logits = lax.dot_general(wr_ref[...], h2, _NT, preferred_element_type=F32,
                             precision=lax.Precision.HIGHEST)
    mx = jnp.max(logits, axis=0, keepdims=True)
    p = jnp.exp(logits - mx)
    aff_ref[0] = p / jnp.sum(p, axis=0, keepdims=True)


def _out_projection(na, hy, pool, w_out_bf, x2, g1, norm_w, shift, scale, w_router_t, *,
                    rows_per_mod, seq_len, tm):
    m, d = x2.shape
    tiles_per_mod = rows_per_mod // tm
    tiles_per_seq = seq_len // tm
    row = lambda i: (i, 0)
    const = lambda i: (0, 0)
    modi = lambda i: (i // tiles_per_mod, 0, 0)
    return pl.pallas_call(
        _outproj_kernel,
        grid=(m // tm,),
        in_specs=[pl.BlockSpec((tm, NA_W), row), pl.BlockSpec((tm, HY_W), row), pl.BlockSpec((tm, POOL_W), row),
                  pl.BlockSpec((MIX_W, d), const),
                  pl.BlockSpec((tm, d), row),
                  pl.BlockSpec((1, 1, d), modi),
                  pl.BlockSpec((1, d), const),
                  pl.BlockSpec((1, 1, d), modi), pl.BlockSpec((1, 1, d), modi),
                  pl.BlockSpec((N_EXPERTS, d), const)],
        out_specs=[pl.BlockSpec((tm, d), row), pl.BlockSpec((tm * TOKEN_SUBLANES, LANES), row),
                   pl.BlockSpec((1, N_EXPERTS, tm), lambda i: (i // tiles_per_seq, 0, i % tiles_per_seq))],
        out_shape=[jax.ShapeDtypeStruct((m, d), F32), jax.ShapeDtypeStruct((m * TOKEN_SUBLANES, LANES), F32),
                   jax.ShapeDtypeStruct((m // seq_len, N_EXPERTS, seq_len), F32)],
        compiler_params=_cparams(("arbitrary",)),
        name="out_projection_router",
    )(na, hy, pool, w_out_bf, x2, g1, norm_w.reshape(1, d), shift, scale, w_router_t)


FFN_TF = 256
FFN_ROW_CHUNKS = 2


def _ffn_kernel(*refs, n_experts, sources, n_rows):
    ns = len(sources)
    idx_cur = refs[0:ns]
    idx_nxt = refs[ns:2 * ns]
    src = refs[2 * ns:3 * ns]
    gate_ref, wg_ref, wu_ref, wd_ref, o_ref, stage, xb, acc, sem = refs[3 * ns:]
    e = pl.program_id(0)
    f = pl.program_id(1)
    n_f = pl.num_programs(1)

    ts = TOKEN_SUBLANES

    def row_copy(s, row, r):
        return pltpu.make_async_copy(src[s].at[pl.ds(pl.multiple_of(row * ts, ts), ts)],
                                     stage.at[pl.ds(pl.multiple_of((sources[s][0] + r) * ts, ts), ts)], sem.at[0])

    def gather(idx_refs):
        for s in range(ns):
            def body(r, carry, s=s):
                row_copy(s, idx_refs[s][0, 0, r], r).start()
                return carry
            lax.fori_loop(0, sources[s][1], body, 0, unroll=8)

    def gather_wait():
        for s in range(ns):
            off, cnt = sources[s]
            pltpu.make_async_copy(src[s].at[pl.ds(0, cnt * ts)], stage.at[pl.ds(off * ts, cnt * ts)],
                                  sem.at[0]).wait()

    @pl.when(f == 0)
    def _():
        @pl.when(e == 0)
        def _():
            gather(idx_cur)

        gather_wait()

        def unpack(base, n):
            for s in range(ts):
                xb[pl.ds(base, n), s * LANES:(s + 1) * LANES] = (
                    stage[pl.ds(base * ts + s, n, stride=ts), :].astype(BF16))
            acc[pl.ds(base, n), :] = jnp.zeros((n, xb.shape[1]), F32)

        @pl.when(e + 1 < n_experts)
        def _():
            group = 16
            for s in range(ns):
                off, cnt = sources[s]

                def body(i, carry, s=s, off=off):
                    unpack(pl.multiple_of(off + i * group, group), group)
                    for k in range(group):
                        row_copy(s, idx_nxt[s][0, 0, i * group + k], i * group + k).start()
                    return carry

                lax.fori_loop(0, cnt // group, body, 0)

        @pl.when(e + 1 >= n_experts)
        def _():
            unpack(0, n_rows)

    wg = wg_ref[0, 0].astype(BF16)
    wu = wu_ref[0, 0].astype(BF16)
    wd = wd_ref[0, 0].astype(BF16)
    rc = n_rows // FFN_ROW_CHUNKS
    for r in range(FFN_ROW_CHUNKS):
        rs = pl.ds(r * rc, rc)
        x = xb[rs, :]
        a = jnp.dot(x, wg, preferred_element_type=F32)
        b = jnp.dot(x, wu, preferred_element_type=F32)
        h = (a * jax.nn.sigmoid(a) * b).astype(BF16)
        y = jnp.dot(h, wd, preferred_element_type=F32)
        acc[rs, :] += y

    @pl.when(f == n_f - 1)
    def _():
        o_ref[0] = (acc[...] * gate_ref[0]).astype(BF16)


def _expert_ffn(rows_idx, tokens, gate, w_gate, w_up, w_down, layer):
    ns = len(tokens)
    e, d = w_gate.shape[1], w_gate.shape[2]
    counts = [int(ri.shape[2]) for ri in rows_idx]
    offs = [int(sum(counts[:s])) for s in range(ns)]
    r_total = sum(counts)
    assert r_total % (FFN_ROW_CHUNKS * 16) == 0
    smem = lambda cnt, nxt: pl.BlockSpec(
        (1, 1, cnt), (lambda ei, f: (jnp.minimum(ei + 1, e - 1), 0, 0)) if nxt else (lambda ei, f: (ei, 0, 0)),
        memory_space=pltpu.SMEM)
    in_specs = ([smem(cnt, False) for cnt in counts] + [smem(cnt, True) for cnt in counts]
                + [pl.BlockSpec(memory_space=pl.ANY) for _ in range(ns)]
                + [pl.BlockSpec((1, r_total, 1), lambda ei, f: (ei, 0, 0)),
                   pl.BlockSpec((1, 1, d, FFN_TF), lambda ei, f: (layer, ei, 0, f)),
                   pl.BlockSpec((1, 1, d, FFN_TF), lambda ei, f: (layer, ei, 0, f)),
                   pl.BlockSpec((1, 1, FFN_TF, d), lambda ei, f: (layer, ei, f, 0))])
    return pl.pallas_call(
        functools.partial(_ffn_kernel, n_experts=e, sources=tuple(zip(offs, counts)), n_rows=r_total),
        grid=(e, D_EXPERT // FFN_TF),
        in_specs=in_specs,
        out_specs=pl.BlockSpec((1, r_total, d), lambda ei, f: (ei, 0, 0)),
        out_shape=jax.ShapeDtypeStruct((e, r_total, d), BF16),
        scratch_shapes=[pltpu.VMEM((r_total * TOKEN_SUBLANES, LANES), F32), pltpu.VMEM((r_total, d), BF16),
                        pltpu.VMEM((r_total, d), F32), pltpu.SemaphoreType.DMA((1,))],
        compiler_params=_cparams(("arbitrary", "arbitrary")),
        name="expert_ffn",
    )(*rows_idx, *rows_idx, *tokens, gate, w_gate, w_up, w_down)


def _prefix_count(m, n):
    lane = lax.broadcasted_iota(jnp.int32, m.shape, 1)
    c = m
    s = 1
    while s < n:
        c = c + jnp.where(lane >= s, pltpu.roll(c, s, 1), 0)
        s *= 2
    return c


COMBINE_TB = 256
COMBINE_CHUNK = 128


def _topk_kernel(aff_ref, idx_ref, gate_ref, off_ref, *, cap):
    a = aff_ref[...]
    rows, n = a.shape

    def search(i, t):
        cand = t | jnp.left_shift(jnp.int32(1), 30 - i)
        cnt = jnp.sum((a >= pltpu.bitcast(cand, F32)).astype(jnp.int32), axis=1, keepdims=True)
        return jnp.where(cnt >= cap, cand, t)

    thr_bits = lax.fori_loop(0, 31, search, jnp.zeros((rows, 1), jnp.int32))
    thr = pltpu.bitcast(thr_bits, F32)
    gt = a >= pltpu.bitcast(thr_bits + 1, F32)
    eq = (a >= thr) & jnp.logical_not(gt)
    need = cap - jnp.sum(gt.astype(jnp.int32), axis=1, keepdims=True)
    eq_i = eq.astype(jnp.int32)
    sel = gt | (eq & (_prefix_count(eq_i, n) - eq_i < need))
    sel_i = sel.astype(jnp.int32)
    lane = lax.broadcasted_iota(jnp.int32, a.shape, 1)
    dist = jnp.where(sel, lane + 1 - _prefix_count(sel_i, n), 0)
    tok = jnp.where(sel, lane, -1)
    val = a
    s = 1
    while s < n:
        tok_in = pltpu.roll(tok, n - s, 1)
        dist_in = pltpu.roll(dist, n - s, 1)
        val_in = pltpu.roll(val, n - s, 1)
        take = (lane < n - s) & (tok_in >= 0) & ((dist_in & s) != 0)
        stay = (tok >= 0) & ((dist & s) == 0)
        tok = jnp.where(take, tok_in, jnp.where(stay, tok, -1))
        dist = jnp.where(take, dist_in, jnp.where(stay, dist, 0))
        val = jnp.where(take, val_in, val)
        s *= 2
    idx_ref[...] = tok[:, :cap]
    gate_ref[...] = val[:, :cap]
    olane = lax.broadcasted_iota(jnp.int32, off_ref.shape, 1)
    off = jnp.zeros(off_ref.shape, jnp.int32)
    for j in range(n // COMBINE_TB + 1):
        below = jnp.sum(jnp.where(lane < j * COMBINE_TB, sel_i, 0), axis=1, keepdims=True)
        off = jnp.where(olane == j, below, off)
    off_ref[...] = off


def _topk_select(aff, cap):
    b, e, n = aff.shape
    assert n & (n - 1) == 0 and n // COMBINE_TB < LANES
    rows = b * e
    full = lambda w: pl.BlockSpec((rows, w), lambda i: (0, 0))
    idx, gate, off = pl.pallas_call(
        functools.partial(_topk_kernel, cap=cap),
        grid=(1,),
        in_specs=[full(n)],
        out_specs=[full(cap), full(cap), full(LANES)],
        out_shape=[jax.ShapeDtypeStruct((rows, cap), jnp.int32), jax.ShapeDtypeStruct((rows, cap), F32),
                   jax.ShapeDtypeStruct((rows, LANES), jnp.int32)],
        compiler_params=_cparams(("arbitrary",)),
        name="expert_choice_topk",
    )(aff.reshape(rows, n))
    return idx.reshape(b, e, cap), gate.reshape(b, e, cap), off


def _combine_kernel(off_ref, y_ref, idx_ref, x_ref, g_ref, o_ref, acc, ycat, *, chunk, n_chunks):
    bi = pl.program_id(0)
    j = pl.program_id(1)
    tb = x_ref.shape[0]
    tok = j * tb + lax.broadcasted_iota(jnp.int32, (tb, chunk), 0)

    def onehot(e, st):
        return (idx_ref[0, pl.ds(e, 1), pl.ds(st, chunk)] == tok).astype(BF16)

    if n_chunks == 1:
        acc[...] = jnp.zeros_like(acc)
        for e in range(N_EXPERTS):
            acc[...] += jnp.dot(onehot(e, 0), y_ref[e], preferred_element_type=F32)
    else:
        cap = n_chunks * chunk
        shift = chunk.bit_length() - 1
        col = lax.broadcasted_iota(jnp.int32, (1, chunk), 1)

        def slot_ids(e, st):
            base = jnp.minimum(lax.shift_left(lax.shift_right_logical(st, shift), shift), cap - 2 * chunk)
            wide = idx_ref[0, pl.ds(e, 1), pl.ds(pl.multiple_of(base, chunk), 2 * chunk)]
            wide = jnp.broadcast_to(wide, (8, 2 * chunk))
            return pltpu.roll(wide, (2 * chunk - (st - base)) & (2 * chunk - 1), 1)[:1, :chunk]

        starts, ends, hots = [], [], []
        for e in range(N_EXPERTS):
            lo = off_ref[bi * N_EXPERTS + e, j]
            hi = off_ref[bi * N_EXPERTS + e, j + 1]
            st = jnp.minimum(lax.shift_left(lax.shift_right_logical(lo, 4), 4), cap - chunk)
            starts.append(st)
            ends.append(hi)
            ycat[pl.ds(e * chunk, chunk), :] = y_ref[e, pl.ds(pl.multiple_of(st, 16), chunk), :]
            hots.append((slot_ids(e, st) == tok).astype(BF16))
        acc[...] = jnp.dot(jnp.concatenate(hots, axis=1), ycat[...], preferred_element_type=F32)
        for e in range(N_EXPERTS):
            def add_window(c, carry, e=e):
                first = starts[e] + c * chunk
                st = jnp.minimum(first, cap - chunk)
                hot = ((slot_ids(e, st) == tok) & (st + col >= first)).astype(BF16)
                acc[...] += jnp.dot(hot, y_ref[e, pl.ds(pl.multiple_of(st, 16), chunk), :],
                                    preferred_element_type=F32)
                return carry

            n_more = lax.shift_right_logical(jnp.maximum(ends[e] - starts[e] - 1, 0), shift)
            lax.fori_loop(1, n_more + 1, add_window, 0)
    o_ref[...] = x_ref[...] + g_ref[0] * acc[...]


def _moe_combine(off, y, idx, x2, g, *, row_block0, cap, seq_len):
    bsz = idx.shape[0]
    d = x2.shape[1]
    tb = min(seq_len, COMBINE_TB)
    nb = seq_len // tb
    chunk = min(cap, COMBINE_CHUNK)
    n_chunks = cap // chunk
    n_mod = g.shape[0]
    grid_spec = pltpu.PrefetchScalarGridSpec(
        num_scalar_prefetch=1,
        grid=(bsz, nb),
        in_specs=[pl.BlockSpec((N_EXPERTS, cap, d), lambda bi, j, off: (0, row_block0 + bi, 0),
                               pipeline_mode=pl.Buffered(1)),
                  pl.BlockSpec((1, N_EXPERTS, cap), lambda bi, j, off: (bi, 0, 0)),
                  pl.BlockSpec((tb, d), lambda bi, j, off: (bi * nb + j, 0)),
                  pl.BlockSpec((1, 1, d), lambda bi, j, off: (bi if n_mod > 1 else 0, 0, 0))],
        out_specs=pl.BlockSpec((tb, d), lambda bi, j, off: (bi * nb + j, 0)),
        scratch_shapes=[pltpu.VMEM((tb, d), F32), pltpu.VMEM((N_EXPERTS * chunk, d), BF16)])
    return pl.pallas_call(
        functools.partial(_combine_kernel, chunk=chunk, n_chunks=n_chunks),
        grid_spec=grid_spec,
        out_shape=jax.ShapeDtypeStruct(x2.shape, F32),
        compiler_params=_cparams(("arbitrary", "arbitrary")),
        name="moe_combine",
    )(off, y, idx, x2, g)


def _expert_choice_moe(streams, w_gate, w_up, w_down, layer):
    routed = []
    for h2, aff, x2, g in streams:
        bsz, _, l = aff.shape
        cap = EC_FACTOR * l // N_EXPERTS
        idx, gate, off = _topk_select(aff, cap)
        rows = (idx + (jnp.arange(bsz, dtype=jnp.int32) * l)[:, None, None]).transpose(1, 0, 2)
        routed.append((idx, off, rows.reshape(N_EXPERTS, 1, bsz * cap), gate.transpose(1, 0, 2).reshape(
            N_EXPERTS, bsz * cap, 1), cap))
    y = _expert_ffn([r[2] for r in routed], [s[0] for s in streams],
                    jnp.concatenate([r[3] for r in routed], axis=1), w_gate, w_up, w_down, layer)
    outs = []
    row0 = 0
    for (h2, aff, x2, g), (idx, off, rows, _, cap) in zip(streams, routed):
        assert row0 % cap == 0
        outs.append(_moe_combine(off, y, idx, x2, g, row_block0=row0 // cap, cap=cap, seq_len=aff.shape[2]))
        row0 += rows.shape[2]
    return outs


HY_R = 128
HY_COLS = 4096
HY_KB = 8
HY_UNROLL = 8


def _dft_consts(r):
    k = np.arange(r)
    ang = 2.0 * np.pi * np.outer(k, k) / r
    fre, fim = np.cos(ang), -np.sin(ang)
    k2 = np.arange(r)
    tang = 2.0 * np.pi * np.outer(k2, k2) / (r * r)
    tw = np.stack([np.cos(tang), -np.sin(tang)])
    fwd = np.block([[fre, -fim], [fim, fre]])
    inv = np.block([[fre, fim], [-fim, fre]])
    return fre, fim, tw, fwd, inv


def _hy_prep_kernel(x_ref, prev_ref, next_ref, w_ref, b_ref, x1_ref, x2_ref, v_ref, *, n_tiles):
    t = pl.program_id(1)
    x = x_ref[0].astype(F32)
    tm = x.shape[0]
    row = lax.broadcasted_iota(jnp.int32, x.shape, 0)
    halo = prev_ref.shape[1]
    prev = jnp.where(t == 0, 0.0, prev_ref[0].astype(F32)[halo - 1:halo, :])
    nxt = jnp.where(t == n_tiles - 1, 0.0, next_ref[0].astype(F32)[0:1, :])
    up = jnp.where(row == 0, prev, pltpu.roll(x, 1, 0))
    dn = jnp.where(row == tm - 1, nxt, pltpu.roll(x, tm - 1, 0))
    y = b_ref[...] + up * w_ref[pl.ds(0, 1), :] + x * w_ref[pl.ds(1, 1), :] + dn * w_ref[pl.ds(2, 1), :]
    x1_ref[0] = y[:, :HY_W].astype(BF16)
    x2_ref[0] = y[:, HY_W:2 * HY_W].astype(BF16)
    v_ref[0] = y[:, 2 * HY_W:].astype(BF16)


def _hyena_prep(u, conv_w, conv_b):
    b, l, c = u.shape
    tm = min(l, 1024)
    n_tiles = l // tm
    halo = 16
    g = tm // halo
    n_groups = l // halo
    out = jax.ShapeDtypeStruct((b, l, HY_W), BF16)
    ospec = pl.BlockSpec((1, tm, HY_W), lambda bi, t: (bi, t, 0))
    return pl.pallas_call(
        functools.partial(_hy_prep_kernel, n_tiles=n_tiles),
        grid=(b, n_tiles),
        in_specs=[pl.BlockSpec((1, tm, c), lambda bi, t: (bi, t, 0)),
                  pl.BlockSpec((1, halo, c), lambda bi, t: (bi, jnp.maximum(t * g - 1, 0), 0)),
                  pl.BlockSpec((1, halo, c), lambda bi, t: (bi, jnp.minimum((t + 1) * g, n_groups - 1), 0)),
                  pl.BlockSpec((HY_SHORT, c), lambda bi, t: (0, 0)),
                  pl.BlockSpec((1, c), lambda bi, t: (0, 0))],
        out_specs=[ospec, ospec, ospec],
        out_shape=[out, out, out],
        compiler_params=_cparams(("arbitrary", "arbitrary")),
        name="hyena_prep",
    )(u, u, u, conv_w, conv_b.reshape(1, c))


def _hy_filter_kernel(z_ref, w1_ref, b1_ref, w2_ref, b2_ref, w3_ref, fr_ref, dl_ref, k_ref, ss_ref, *, seq_len):
    i = pl.program_id(0)
    hp = lax.Precision.HIGHEST
    z = z_ref[...]
    tm = z.shape[0]
    half = tm // 2
    fr = fr_ref[...]
    zz = jnp.concatenate([z[:half], z[half:]], axis=1)
    h = jnp.sin(fr * (jnp.dot(zz, w1_ref[...], preferred_element_type=F32, precision=hp) + b1_ref[...]))
    h = jnp.sin(fr * (jnp.dot(h, w2_ref[...], preferred_element_type=F32, precision=hp) + b2_ref[...]))
    hb = h.astype(BF16)
    w3 = w3_ref[...].astype(BF16)
    fo = w3.shape[0]
    h = jnp.concatenate([jnp.dot(hb[:, :fo], w3, preferred_element_type=F32),
                         jnp.dot(hb[:, fo:], w3, preferred_element_type=F32)], axis=0)
    dec = jnp.exp(-z[:, 0:1] * dl_ref[...])
    r = i * tm + lax.broadcasted_iota(jnp.int32, (tm, HY_W), 0)
    parts = []
    for o in range(HY_ORDER):
        hf = h[:, (2 * o) * HY_W:(2 * o + 1) * HY_W]
        hb = h[:, (2 * o + 1) * HY_W:(2 * o + 2) * HY_W]
        parts.append(jnp.where(r < seq_len, hf, jnp.where(r > seq_len, hb, 0.0)) * dec)
    k = jnp.concatenate(parts, axis=1)
    k_ref[...] = k
    ss = jnp.sum(k * k, axis=0, keepdims=True)

    @pl.when(i == 0)
    def _():
        ss_ref[...] = ss

    @pl.when(i != 0)
    def _():
        ss_ref[...] += ss


def _hyena_filter_taps(l, w1, b1, w2, b2, w3, freq):
    t = jnp.linspace(0.0, 1.0, l, dtype=F32)[:, None]
    w = (2.0 * math.pi / l) * jnp.arange(l, dtype=F32)[:, None]
    f = jnp.linspace(1e-4, HY_BANDS - 1, HY_BANDS, dtype=F32)[None, :]
    z = jnp.concatenate([t, jnp.cos(f * w), -jnp.sin(f * w)], axis=-1)
    z2 = jnp.concatenate([z, z[:1], z[:0:-1]], axis=0)
    emb = z.shape[1]
    z2 = jnp.pad(z2, ((0, 0), (0, LANES - emb)))
    w1p = jnp.pad(w1, ((0, LANES - emb), (0, 0)))
    two = lambda m: jnp.kron(jnp.eye(2, dtype=m.dtype), m)
    pair = lambda v: jnp.tile(v, 2).reshape(1, 2 * v.shape[0])
    deltas = jnp.abs(jnp.linspace(math.log(HY_DECAY_TARGET) / HY_SLOW_DECAY,
                                  math.log(HY_DECAY_TARGET) / HY_FAST_DECAY, HY_W, dtype=F32))
    fo = w2.shape[0]
    nout = w3.shape[1]
    tm = min(2 * l, 1024)
    const = lambda i: (0, 0)
    return pl.pallas_call(
        functools.partial(_hy_filter_kernel, seq_len=l),
        grid=(2 * l // tm,),
        in_specs=[pl.BlockSpec((tm, LANES), lambda i: (i, 0)),
                  pl.BlockSpec((2 * LANES, 2 * fo), const), pl.BlockSpec((1, 2 * fo), const),
                  pl.BlockSpec((2 * fo, 2 * fo), const), pl.BlockSpec((1, 2 * fo), const),
                  pl.BlockSpec((fo, nout), const), pl.BlockSpec((1, 2 * fo), const),
                  pl.BlockSpec((1, HY_W), const)],
        out_specs=[pl.BlockSpec((tm, HY_ORDER * HY_W), lambda i: (i, 0)),
                   pl.BlockSpec((1, HY_ORDER * HY_W), const)],
        out_shape=[jax.ShapeDtypeStruct((2 * l, HY_ORDER * HY_W), F32),
                   jax.ShapeDtypeStruct((1, HY_ORDER * HY_W), F32)],
        compiler_params=_cparams(("arbitrary",)),
        name="hyena_filter_taps",
    )(z2, two(w1p), pair(b1), two(w2), pair(b2), w3, pair(freq), deltas.reshape(1, HY_W))


def _left_matmul_kernel(f_ref, x_ref, o_ref):
    o_ref[0] = jnp.dot(f_ref[...], x_ref[0].astype(BF16), preferred_element_type=F32).astype(o_ref.dtype)


def _left_matmul(f_bf, x3, name):
    b, k, c = x3.shape
    m = f_bf.shape[0]
    return pl.pallas_call(
        _left_matmul_kernel,
        grid=(b, c // HY_COLS),
        in_specs=[pl.BlockSpec((m, k), lambda bi, j: (0, 0)),
                  pl.BlockSpec((1, k, HY_COLS), lambda bi, j: (bi, 0, j))],
        out_specs=pl.BlockSpec((1, m, HY_COLS), lambda bi, j: (bi, 0, j)),
        out_shape=jax.ShapeDtypeStruct((b, m, c), BF16),
        compiler_params=_cparams(("arbitrary", "arbitrary")),
        name=name,
    )(f_bf, x3)


def _twiddle_cols(tw_ref, k1):
    lane = lax.broadcasted_iota(jnp.int32, (HY_R, HY_R), 1)
    pick = lane == k1
    twr = jnp.sum(jnp.where(pick, tw_ref[0], 0.0), axis=1, keepdims=True)
    twi = jnp.sum(jnp.where(pick, tw_ref[1], 0.0), axis=1, keepdims=True)
    return twr, twi


def _hy_spectrum_kernel(a_ref, tw_ref, fwd_ref, ss_ref, kf_ref):
    kb = pl.program_id(0)
    scale = lax.rsqrt(ss_ref[...] + EPS) * (1.0 / (HY_R * HY_R))

    def body(kk, carry):
        twr, twi = _twiddle_cols(tw_ref, kb * HY_KB + kk)
        are = a_ref[0, 0, kk].astype(F32)
        aim = a_ref[0, 1, kk].astype(F32)
        bst = jnp.concatenate([are * twr - aim * twi, are * twi + aim * twr], axis=0).astype(BF16)
        x = jnp.dot(fwd_ref[...], bst, preferred_element_type=F32)
        kf_ref[0, kk] = x[:HY_R] * scale
        kf_ref[1, kk] = x[HY_R:] * scale
        return carry

    lax.fori_loop(0, HY_KB, body, 0, unroll=HY_UNROLL)


def _hy_spectrum(a5, tw, fwd_bf, ss):
    nch = a5.shape[-1]
    return pl.pallas_call(
        _hy_spectrum_kernel,
        grid=(HY_R // HY_KB,),
        in_specs=[pl.BlockSpec((1, 2, HY_KB, HY_R, nch), lambda kb: (0, 0, kb, 0, 0)),
                  pl.BlockSpec((2, HY_R, HY_R), lambda kb: (0, 0, 0)),
                  pl.BlockSpec((2 * HY_R, 2 * HY_R), lambda kb: (0, 0)),
                  pl.BlockSpec((1, nch), lambda kb: (0, 0))],
        out_specs=pl.BlockSpec((2, HY_KB, HY_R, nch), lambda kb: (0, kb, 0, 0)),
        out_shape=jax.ShapeDtypeStruct((2, HY_R, HY_R, nch), F32),
        compiler_params=_cparams(("arbitrary",)),
        name="hyena_filter_spectrum",
    )(a5, tw, fwd_bf, ss)


def _hy_mid_kernel(a_ref, kf_ref, tw_ref, fwd_ref, inv_ref, z_ref):
    kb = pl.program_id(0)

    def body(kk, carry):
        twr, twi = _twiddle_cols(tw_ref, kb * HY_KB + kk)
        are = a_ref[0, 0, kk].astype(F32)
        aim = a_ref[0, 1, kk].astype(F32)
        bst = jnp.concatenate([are * twr - aim * twi, are * twi + aim * twr], axis=0).astype(BF16)
        x = jnp.dot(fwd_ref[...], bst, preferred_element_type=F32)
        xre, xim = x[:HY_R], x[HY_R:]
        kre, kim = kf_ref[0, kk], kf_ref[1, kk]
        yst = jnp.concatenate([xre * kre - xim * kim, xre * kim + xim * kre], axis=0).astype(BF16)
        zz = jnp.dot(inv_ref[...], yst, preferred_element_type=F32)
        zre, zim = zz[:HY_R], zz[HY_R:]
        z_ref[0, 0, kk] = (zre * twr + zim * twi).astype(BF16)
        z_ref[0, 1, kk] = (zim * twr - zre * twi).astype(BF16)
        return carry

    lax.fori_loop(0, HY_KB, body, 0, unroll=HY_UNROLL)


def _hy_mid(a5, kf, order, tw, fwd_bf, inv_bf):
    b = a5.shape[0]
    blk = (1, 2, HY_KB, HY_R, HY_W)
    return pl.pallas_call(
        _hy_mid_kernel,
        grid=(HY_R // HY_KB, b),
        in_specs=[pl.BlockSpec(blk, lambda kb, bi: (bi, 0, kb, 0, 0)),
                  pl.BlockSpec((2, HY_KB, HY_R, HY_W), lambda kb, bi: (0, kb, 0, order)),
                  pl.BlockSpec((2, HY_R, HY_R), lambda kb, bi: (0, 0, 0)),
                  pl.BlockSpec((2 * HY_R, 2 * HY_R), lambda kb, bi: (0, 0)),
                  pl.BlockSpec((2 * HY_R, 2 * HY_R), lambda kb, bi: (0, 0))],
        out_specs=pl.BlockSpec(blk, lambda kb, bi: (bi, 0, kb, 0, 0)),
        out_shape=jax.ShapeDtypeStruct(a5.shape, BF16),
        compiler_params=_cparams(("arbitrary", "arbitrary")),
        name="hyena_spectral_product",
    )(a5, kf, tw, fwd_bf, inv_bf)


def _hy_out_kernel(f_ref, z_ref, g_ref, w_ref, sk_ref, o_ref):
    y = jnp.dot(f_ref[...], z_ref[0], preferred_element_type=F32)
    w = w_ref[0].astype(F32)
    o_ref[0] = (g_ref[0].astype(F32) * (y + w * sk_ref[...])).astype(BF16)


def _hy_out(f_bf, z3, gate3, w3, skip_row):
    b, m2, c = z3.shape
    m = f_bf.shape[0]
    dspec = pl.BlockSpec((1, m, HY_COLS), lambda bi, j: (bi, 0, j))
    return pl.pallas_call(
        _hy_out_kernel,
        grid=(b, c // HY_COLS),
        in_specs=[pl.BlockSpec((m, m2), lambda bi, j: (0, 0)),
                  pl.BlockSpec((1, m2, HY_COLS), lambda bi, j: (bi, 0, j)),
                  dspec, dspec,
                  pl.BlockSpec((1, HY_COLS), lambda bi, j: (0, 0))],
        out_specs=dspec,
        out_shape=jax.ShapeDtypeStruct((b, m, c), BF16),
        compiler_params=_cparams(("arbitrary", "arbitrary")),
        name="hyena_inverse_gate",
    )(f_bf, z3, gate3, w3, skip_row)


def _hyena_long(u, conv_w, conv_b, skip, filt_args):
    b, l, _ = u.shape
    assert 2 * l == HY_R * HY_R
    fre, fim, tw, fwd, inv = _dft_consts(HY_R)
    half = HY_R // 2
    as_bf = lambda m: jnp.asarray(m, F32).astype(BF16)
    f_a_data = as_bf(np.concatenate([fre[:, :half], fim[:, :half]], axis=0))
    f_a_filt = as_bf(np.concatenate([fre, fim], axis=0))
    f_c = as_bf(np.concatenate([fre[:half], fim[:half]], axis=1))
    tw_j = jnp.asarray(tw, F32)
    fwd_bf = as_bf(fwd)
    inv_bf = as_bf(inv)

    taps, ss = _hyena_filter_taps(l, *filt_args)
    nch = HY_ORDER * HY_W
    ak = _left_matmul(f_a_filt, taps.reshape(1, HY_R, HY_R * nch), "hyena_filter_dft_a")
    kf = _hy_spectrum(ak.reshape(1, 2, HY_R, HY_R, nch), tw_j, fwd_bf, ss)

    x1, x2, v = _hyena_prep(u, conv_w, conv_b)
    z = v
    for o, gate in enumerate((x1, x2)):
        z3 = z.reshape(b, half, HY_R * HY_W)
        a = _left_matmul(f_a_data, z3, "hyena_dft_a")
        zs = _hy_mid(a.reshape(b, 2, HY_R, HY_R, HY_W), kf, o, tw_j, fwd_bf, inv_bf)
        skip_row = jnp.tile(skip[o], HY_COLS // HY_W).reshape(1, HY_COLS)
        z = _hy_out(f_c, zs.reshape(b, 2 * HY_R, HY_R * HY_W), gate.reshape(b, half, HY_R * HY_W), z3,
                    skip_row).reshape(b, l, HY_W)
    return z


def _hy_ctx_kernel(x1_ref, x2_ref, v_ref, taps_ref, ss_ref, fa_ref, fc_ref, sk_ref, o_ref, *, seq_len):
    n2 = 2 * seq_len
    fa = fa_ref[...]
    scale = lax.rsqrt(ss_ref[...] + EPS) * (1.0 / n2)
    kf = jnp.dot(fa, taps_ref[...].astype(BF16), preferred_element_type=F32) * scale
    z = v_ref[0].astype(F32)
    for o, g_ref in enumerate((x1_ref, x2_ref)):
        x = jnp.dot(fa[:, :seq_len], z.astype(BF16), preferred_element_type=F32)
        xre, xim = x[:n2], x[n2:]
        kre = kf[:n2, o * HY_W:(o + 1) * HY_W]
        kim = kf[n2:, o * HY_W:(o + 1) * HY_W]
        yst = jnp.concatenate([xre * kre - xim * kim, xre * kim + xim * kre], axis=0).astype(BF16)
        y = jnp.dot(fc_ref[...], yst, preferred_element_type=F32)
        z = g_ref[0].astype(F32) * (y + z * sk_ref[pl.ds(o, 1), :])
    o_ref[0] = z.astype(BF16)


def _hyena_short(u, conv_w, conv_b, skip, filt_args):
    b, l, _ = u.shape
    n2 = 2 * l
    k = np.arange(n2)
    ang = 2.0 * np.pi * np.outer(k, k) / n2
    fre, fim = np.cos(ang), -np.sin(ang)
    fa = jnp.asarray(np.concatenate([fre, fim], axis=0), F32).astype(BF16)
    fc = jnp.asarray(np.concatenate([fre[:l], fim[:l]], axis=1), F32).astype(BF16)
    taps, ss = _hyena_filter_taps(l, *filt_args)
    x1, x2, v = _hyena_prep(u, conv_w, conv_b)
    nch = HY_ORDER * HY_W
    dspec = pl.BlockSpec((1, l, HY_W), lambda bi: (bi, 0, 0))
    const = lambda bi: (0, 0)
    return pl.pallas_call(
        functools.partial(_hy_ctx_kernel, seq_len=l),
        grid=(b,),
        in_specs=[dspec, dspec, dspec,
                  pl.BlockSpec((n2, nch), const), pl.BlockSpec((1, nch), const),
                  pl.BlockSpec((2 * n2, n2), const), pl.BlockSpec((l, 2 * n2), const),
                  pl.BlockSpec((HY_ORDER, HY_W), const)],
        out_specs=dspec,
        out_shape=jax.ShapeDtypeStruct((b, l, HY_W), BF16),
        compiler_params=_cparams(("arbitrary",)),
        name="hyena_context",
    )(x1, x2, v, taps, ss, fa, fc, skip)


def _rope_tables(n):
    t = jnp.arange(n, dtype=jnp.int32)
    pos = jnp.stack([t // GRID_W, t % GRID_W], axis=-1).astype(F32)
    nf = HEAD_DIM // 4
    inv = ROPE_THETA ** (-jnp.arange(nf, dtype=F32) / nf)
    ang = pos[:, :, None] * inv
    cos, sin = jnp.cos(ang), jnp.sin(ang)
    c64 = jnp.concatenate([cos[:, 0], cos[:, 0], cos[:, 1], cos[:, 1]], axis=-1)
    s64 = jnp.concatenate([-sin[:, 0], sin[:, 0], -sin[:, 1], sin[:, 1]], axis=-1)
    return jnp.tile(c64, (1, 2)), jnp.tile(s64, (1, 2))


def _block_diag(pool_w):
    g, c, _ = pool_w.shape
    out = jnp.zeros((g * c, g * c), pool_w.dtype)
    for i in range(g):
        out = out.at[i * c:(i + 1) * c, i * c:(i + 1) * c].set(pool_w[i])
    return out


def kernel(x, c, ctx, c_ctx, w_mod, b_mod, norm1_w, norm2_w, w_in, w_out, q_norm_w, k_norm_w, na_rpb,
           hy_conv_w, hy_conv_b, hy_w1, hy_b1, hy_w2, hy_b2, hy_w3, hy_freq, hy_skip,
           pool_w, pool_scale, w_router, w_gate, w_up, w_down):
    b, n, d = x.shape
    lc = ctx.shape[1]
    depth = w_mod.shape[0]
    rows = n // GRID_W
    cos128, sin128 = _rope_tables(n)
    seg = jnp.asarray(np.kron(np.eye(2), np.full((HEAD_DIM, HEAD_DIM), 1.0 / HEAD_DIM)), BF16)
    cc = jnp.zeros((8, d), F32).at[:b].set(c).at[b].set(c_ctx)

    xl = x.reshape(b * n, d)
    xc = ctx.reshape(b * lc, d)
    for i in range(depth):
        last = i == depth - 1
        mod = _modulation(cc, w_mod, b_mod[i], i)
        ml = mod[:b].reshape(b, 1, 6, d)
        mc = mod[b].reshape(1, 1, 6, d)
        sh1, sc1, g1, sh2, sc2, g2 = [ml[:, :, j] for j in range(6)]
        csh1, csc1, cg1, csh2, csc2, cg2 = [mc[:, :, j] for j in range(6)]
        w_in_bf = w_in[i].astype(BF16)
        w_out_bf = w_out[i].astype(BF16)
        w_router_t = w_router[i].T
        pool_bd = _block_diag(pool_w[i]).astype(BF16)
        hy_args = (hy_w1[i], hy_b1[i], hy_w2[i], hy_b2[i], hy_w3[i], hy_freq[i])

        qc, kc, vc, hyc, poolc = _in_projection(
            xc, csh1, csc1, norm1_w[i], w_in_bf, q_norm_w[i], k_norm_w[i], cos128, sin128, seg,
            rows_per_mod=b * lc, seq_len=lc, tm=lc, rope=False)
        kc = kc.reshape(b, lc, NA_W)
        vc = vc.reshape(b, lc, NA_W)
        if not last:
            na_c = _dense_attention(qc.reshape(b, lc, NA_W), kc, vc).reshape(b * lc, NA_W)
            hy_c = _hyena_short(hyc.reshape(b, lc, 3 * HY_W), hy_conv_w[i], hy_conv_b[i], hy_skip[i], hy_args)
            pool_c = _pool_mixer(poolc.reshape(b, lc, POOL_W), pool_bd, pool_scale[i])
            xc1, h2c, affc = _out_projection(
                na_c, hy_c.reshape(b * lc, HY_W).astype(BF16), pool_c.reshape(b * lc, POOL_W), w_out_bf,
                xc, cg1, norm2_w[i], csh2, csc2, w_router_t, rows_per_mod=b * lc, seq_len=lc, tm=lc)

        qr, qp, kl, vl, hyl, pooll = _in_projection(
            xl, sh1, sc1, norm1_w[i], w_in_bf, q_norm_w[i], k_norm_w[i], cos128, sin128, seg,
            rows_per_mod=n, seq_len=n, tm=512, rope=True)
        bias = _na_bias_tables(na_rpb[i], rows)
        o_na = _neighborhood_attention(qr.reshape(b, n, NA_W), qp.reshape(b, n, NA_W), kl.reshape(b, n, NA_W),
                                       vl.reshape(b, n, NA_W), kc, vc, bias)
        hy_l = _hyena_long(hyl.reshape(b, n, 3 * HY_W), hy_conv_w[i], hy_conv_b[i], hy_skip[i], hy_args)
        pool_l = _pool_mixer(pooll.reshape(b, n, POOL_W), pool_bd, pool_scale[i])
        xl1, h2l, affl = _out_projection(
            o_na.reshape(b * n, NA_W), hy_l.reshape(b * n, HY_W).astype(BF16), pool_l.reshape(b * n, POOL_W),
            w_out_bf, xl, g1, norm2_w[i], sh2, sc2, w_router_t, rows_per_mod=n, seq_len=n, tm=512)
        streams = [(h2l, affl, xl1, g2)]
        if not last:
            streams.append((h2c, affc, xc1, cg2))
        outs = _expert_choice_moe(streams, w_gate, w_up, w_down, i)
        xl = outs[0]
        if not last:
            xc = outs[1]
    return xl.reshape(b, n, d)
```

```python
import functools
import math

import jax
import jax.numpy as jnp
import numpy as np
from jax import lax
from jax.experimental import pallas as pl
from jax.experimental.pallas import tpu as pltpu

F32 = jnp.float32
BF16 = jnp.bfloat16

D_MODEL = 1024
GRID_W = 64
NA_HEADS = 8
HEAD_DIM = 64
NA_W = NA_HEADS * HEAD_DIM
NA_WIN_R = 8
NA_WIN_C = 16
ROPE_THETA = 10000.0
HY_W = 256
HY_ORDER = 2
HY_SHORT = 3
HY_BANDS = 16
HY_DECAY_TARGET = 1e-2
HY_FAST_DECAY = 0.3
HY_SLOW_DECAY = 1.5
POOL_W = 256
POOL_SIZES = (2, 4, 8, 16)
POOL_GROUP = POOL_W // len(POOL_SIZES)
MIX_W = NA_W + HY_W + POOL_W
IN_W = 3 * NA_W + (HY_ORDER + 1) * HY_W + POOL_W
N_EXPERTS = 16
EC_FACTOR = 2
D_EXPERT = 2048
EPS = 1e-6

LANES = 128
TOKEN_SUBLANES = D_MODEL // 128
ROW_SUBTILES = 2
NEG_BIG = -1e30
VMEM_LIMIT = 56 * 1024 * 1024

NA_Q_ROWS = 4
NA_K_ROWS = NA_Q_ROWS + NA_WIN_R - 1
NA_TILES_PER_STEP = 4


def _cparams(sem):
    return pltpu.CompilerParams(dimension_semantics=sem, vmem_limit_bytes=VMEM_LIMIT)


def _mod_kernel(c_ref, w_ref, b_ref, o_ref):
    c = c_ref[...]
    s = c * jax.nn.sigmoid(c)
    o_ref[...] = jnp.dot(s, w_ref[0], preferred_element_type=F32,
                         precision=lax.Precision.HIGHEST) + b_ref[...]


def _modulation(cc, w_mod, b_mod, layer):
    rows, d = cc.shape
    n = w_mod.shape[2]
    tn = 1536
    return pl.pallas_call(
        _mod_kernel,
        grid=(n // tn,),
        in_specs=[pl.BlockSpec((rows, d), lambda j: (0, 0)),
                  pl.BlockSpec((1, d, tn), lambda j: (layer, 0, j)),
                  pl.BlockSpec((1, tn), lambda j: (0, j))],
        out_specs=pl.BlockSpec((rows, tn), lambda j: (0, j)),
        out_shape=jax.ShapeDtypeStruct((rows, n), F32),
        compiler_params=_cparams(("arbitrary",)),
        name="modulation",
    )(cc, w_mod, b_mod.reshape(1, n))


def _head_norm(xs, w128, seg):
    m = jnp.dot((xs * xs).astype(BF16), seg, preferred_element_type=F32)
    return xs * lax.rsqrt(m + EPS) * w128


def _rope(xs, c, s_signed):
    lane = lax.broadcasted_iota(jnp.int32, xs.shape, 1)
    partner = jnp.where((lane & 16) == 0, pltpu.roll(xs, LANES - 16, 1), pltpu.roll(xs, 16, 1))
    return xs * c + partner * s_signed


def _inproj_kernel(x_ref, sh_ref, sc_ref, nw_ref, w_ref, qw_ref, kw_ref, cos_ref, sin_ref, seg_ref,
                   *out_refs, rope):
    if rope:
        qr_ref, qp_ref, k_ref, v_ref, hy_ref, pool_ref = out_refs
    else:
        qp_ref, k_ref, v_ref, hy_ref, pool_ref = out_refs
    seg = seg_ref[...]
    qk_scale = HEAD_DIM ** -0.5
    sub = x_ref.shape[0] // ROW_SUBTILES
    for t in range(ROW_SUBTILES):
        rs = pl.ds(t * sub, sub)
        x = x_ref[rs, :]
        ms = jnp.mean(x * x, axis=-1, keepdims=True)
        h = x * lax.rsqrt(ms + EPS) * nw_ref[...]
        h = h * (1.0 + sc_ref[0]) + sh_ref[0]
        u = jnp.dot(h.astype(BF16), w_ref[...], preferred_element_type=F32)
        for ch in range(NA_W // LANES):
            sl = slice(ch * LANES, (ch + 1) * LANES)
            qn = _head_norm(u[:, sl], qw_ref[...], seg)
            kn = _head_norm(u[:, NA_W + ch * LANES:NA_W + (ch + 1) * LANES], kw_ref[...], seg)
            qp_ref[rs, sl] = (qn * qk_scale).astype(BF16)
            if rope:
                c = cos_ref[rs, :]
                s = sin_ref[rs, :]
                qr_ref[rs, sl] = (_rope(qn, c, s) * qk_scale).astype(BF16)
                k_ref[rs, sl] = _rope(kn, c, s).astype(BF16)
            else:
                k_ref[rs, sl] = kn.astype(BF16)
        v_ref[rs, :] = u[:, 2 * NA_W:3 * NA_W].astype(BF16)
        hy_ref[rs, :] = u[:, 3 * NA_W:3 * NA_W + 3 * HY_W].astype(BF16)
        pool_ref[rs, :] = u[:, 3 * NA_W + 3 * HY_W:].astype(BF16)


def _in_projection(x2, shift, scale, norm_w, w_in_bf, q_norm_w, k_norm_w, cos128, sin128, seg, *,
                   rows_per_mod, seq_len, tm, rope):
    m, d = x2.shape
    tiles_per_mod = rows_per_mod // tm
    tiles_per_seq = seq_len // tm
    row = lambda i: (i, 0)
    const = lambda i: (0, 0)
    modi = lambda i: (i // tiles_per_mod, 0, 0)
    pos = lambda i: (i % tiles_per_seq, 0)
    outs = []
    if rope:
        outs.append((NA_W, BF16))
    outs += [(NA_W, BF16), (NA_W, BF16), (NA_W, BF16), (3 * HY_W, BF16), (POOL_W, BF16)]
    return pl.pallas_call(
        functools.partial(_inproj_kernel, rope=rope),
        grid=(m // tm,),
        in_specs=[pl.BlockSpec((tm, d), row),
                  pl.BlockSpec((1, 1, d), modi), pl.BlockSpec((1, 1, d), modi),
                  pl.BlockSpec((1, d), const),
                  pl.BlockSpec((d, IN_W), const),
                  pl.BlockSpec((1, LANES), const), pl.BlockSpec((1, LANES), const),
                  pl.BlockSpec((tm, LANES), pos), pl.BlockSpec((tm, LANES), pos),
                  pl.BlockSpec((LANES, LANES), const)],
        out_specs=[pl.BlockSpec((tm, w), row) for w, _ in outs],
        out_shape=[jax.ShapeDtypeStruct((m, w), dt) for w, dt in outs],
        compiler_params=_cparams(("arbitrary",)),
        name="in_projection_rope" if rope else "in_projection_ctx",
    )(x2, shift, scale, norm_w.reshape(1, d), w_in_bf,
      jnp.tile(q_norm_w, 2).reshape(1, LANES), jnp.tile(k_norm_w, 2).reshape(1, LANES),
      cos128, sin128, seg)


def _softmax_pv(s_list, v_list):
    mx = functools.reduce(jnp.maximum, [jnp.max(s, axis=-1, keepdims=True) for s in s_list])
    ps = [jnp.exp(s - mx) for s in s_list]
    l = functools.reduce(jnp.add, [jnp.sum(p, axis=-1, keepdims=True) for p in ps])
    o = functools.reduce(jnp.add, [jnp.dot(p.astype(BF16), v, preferred_element_type=F32)
                                   for p, v in zip(ps, v_list)])
    return o / l


_NT = (((1,), (1,)), ((), ()))


def _na_kernel(qr_ref, qp_ref, k_ref, v_ref, kc_ref, vc_ref, bias_ref, o_ref, *, rows):
    n_tiles = rows // NA_Q_ROWS
    tq = NA_Q_ROWS * GRID_W
    kc = kc_ref[0]
    vc = vc_ref[0]
    lane = lax.broadcasted_iota(jnp.int32, (tq, LANES), 1)
    for sub in range(NA_TILES_PER_STEP):
        t = pl.program_id(2) * NA_TILES_PER_STEP + sub
        ks = jnp.clip(t * NA_Q_ROWS - NA_WIN_R // 2, 0, rows - NA_K_ROWS)
        start = pl.multiple_of(ks * GRID_W, GRID_W)
        k = k_ref[0, pl.ds(start, NA_K_ROWS * GRID_W), :]
        v = v_ref[0, pl.ds(start, NA_K_ROWS * GRID_W), :]
        q = qr_ref[0, pl.ds(sub * tq, tq), :]
        qp = qp_ref[0, pl.ds(sub * tq, tq), :]
        kind = jnp.where(t == 0, 0, jnp.where(t == n_tiles - 1, 2, 1))
        outs = []
        for hh in range(2):
            keep = (lane < HEAD_DIM) if hh == 0 else (lane >= HEAD_DIM)
            zero = jnp.zeros_like(q)
            s_lat = (lax.dot_general(jnp.where(keep, q, zero), k, _NT, preferred_element_type=F32)
                     + bias_ref[kind, hh])
            s_ctx = lax.dot_general(jnp.where(keep, qp, zero), kc, _NT, preferred_element_type=F32)
            outs.append(_softmax_pv([s_lat, s_ctx], [v, vc]))
        o_ref[0, pl.ds(sub * tq, tq), :] = jnp.where(lane < HEAD_DIM, outs[0], outs[1]).astype(BF16)


def _neighborhood_attention(qr, qp, k, v, kc, vc, bias):
    b, n, _ = qr.shape
    lc = kc.shape[1]
    rows = n // GRID_W
    n_steps = rows // (NA_Q_ROWS * NA_TILES_PER_STEP)
    tq = NA_Q_ROWS * GRID_W * NA_TILES_PER_STEP
    tk = NA_K_ROWS * GRID_W
    qmap = lambda bi, hp, t: (bi, t, hp)
    smap = lambda bi, hp, t: (bi, 0, hp)
    bmap = lambda bi, hp, t: (0, hp, 0, 0)
    return pl.pallas_call(
        functools.partial(_na_kernel, rows=rows),
        grid=(b, NA_W // LANES, n_steps),
        in_specs=[pl.BlockSpec((1, tq, LANES), qmap), pl.BlockSpec((1, tq, LANES), qmap),
                  pl.BlockSpec((1, n, LANES), smap), pl.BlockSpec((1, n, LANES), smap),
                  pl.BlockSpec((1, lc, LANES), smap), pl.BlockSpec((1, lc, LANES), smap),
                  pl.BlockSpec((3, 2, NA_Q_ROWS * GRID_W, tk), bmap)],
        out_specs=pl.BlockSpec((1, tq, LANES), qmap),
        out_shape=jax.ShapeDtypeStruct((b, n, NA_W), BF16),
        compiler_params=_cparams(("arbitrary", "arbitrary", "arbitrary")),
        name="neighborhood_attention",
    )(qr, qp, k, v, kc, vc, bias)


def _na_bias_tables(rpb, rows):
    n_tiles = rows // NA_Q_ROWS
    n_dr, n_dc = 2 * NA_WIN_R - 1, 2 * NA_WIN_C - 1
    qc = np.arange(GRID_W)
    cs = np.clip(qc - NA_WIN_C // 2, 0, GRID_W - NA_WIN_C)
    kcol = np.arange(GRID_W)
    ok_c = (kcol[None, :] >= cs[:, None]) & (kcol[None, :] < cs[:, None] + NA_WIN_C)
    dc = kcol[None, :] - qc[:, None] + NA_WIN_C - 1
    col_sel = np.concatenate([(dc[:, :, None] == np.arange(n_dc)) & ok_c[:, :, None],
                              ~ok_c[:, :, None]], axis=2)
    ext = jnp.pad(rpb.astype(F32), ((0, 0), (0, 1), (0, 1)), constant_values=NEG_BIG)
    blocks = jnp.einsum("hrd,qkd->hrqk", ext, jnp.asarray(col_sel, F32), precision=lax.Precision.HIGHEST)
    tabs = []
    for t in (0, 1, n_tiles - 1):
        r0 = t * NA_Q_ROWS
        ks = int(np.clip(r0 - NA_WIN_R // 2, 0, rows - NA_K_ROWS))
        r = r0 + np.arange(NA_Q_ROWS)
        rs = np.clip(r - NA_WIN_R // 2, 0, rows - NA_WIN_R)
        krow = ks + np.arange(NA_K_ROWS)
        ok_r = (krow[None, :] >= rs[:, None]) & (krow[None, :] < rs[:, None] + NA_WIN_R)
        dr = np.where(ok_r, krow[None, :] - r[:, None] + NA_WIN_R - 1, n_dr)
        tabs.append(jnp.concatenate(
            [jnp.concatenate([blocks[:, int(dr[i, j])] for j in range(NA_K_ROWS)], axis=-1)
             for i in range(NA_Q_ROWS)], axis=-2))
    return jnp.stack(tabs)


def _dense_attn_kernel(q_ref, k_ref, v_ref, o_ref):
    q = q_ref[0]
    k = k_ref[0]
    v = v_ref[0]
    lane = lax.broadcasted_iota(jnp.int32, q.shape, 1)
    outs = []
    for hh in range(2):
        keep = (lane < HEAD_DIM) if hh == 0 else (lane >= HEAD_DIM)
        s = lax.dot_general(jnp.where(keep, q, jnp.zeros_like(q)), k, _NT, preferred_element_type=F32)
        outs.append(_softmax_pv([s], [v]))
    o_ref[0] = jnp.where(lane < HEAD_DIM, outs[0], outs[1]).astype(BF16)


def _dense_attention(q, k, v):
    b, l, _ = q.shape
    spec = pl.BlockSpec((1, l, LANES), lambda bi, hp: (bi, 0, hp))
    return pl.pallas_call(
        _dense_attn_kernel,
        grid=(b, NA_W // LANES),
        in_specs=[spec, spec, spec],
        out_specs=spec,
        out_shape=jax.ShapeDtypeStruct((b, l, NA_W), BF16),
        compiler_params=_cparams(("arbitrary", "arbitrary")),
        name="context_attention",
    )(q, k, v)


POOL_HALO = max(POOL_SIZES) // 2


def _pool_kernel(x_ref, w_ref, sc_ref, o_ref, xpad, *, seq_len, chunk):
    zeros = jnp.zeros((POOL_HALO, POOL_W), F32)
    xpad[pl.ds(0, POOL_HALO), :] = zeros
    xpad[pl.ds(seq_len + POOL_HALO, POOL_HALO), :] = zeros
    xpad[pl.ds(POOL_HALO, seq_len), :] = x_ref[0].astype(F32)
    span = chunk + 2 * POOL_HALO
    w = w_ref[...]
    sc = sc_ref[...]

    def body(c, carry):
        base = pl.multiple_of(c * chunk, chunk)
        xs = xpad[pl.ds(base, span), :]
        s2 = xs + pltpu.roll(xs, 1, 0)
        s4 = pltpu.roll(s2, 1, 0) + pltpu.roll(s2, span - 1, 0)
        s8 = pltpu.roll(s4, 2, 0) + pltpu.roll(s4, span - 2, 0)
        s16 = pltpu.roll(s8, 4, 0) + pltpu.roll(s8, span - 4, 0)
        mid = slice(POOL_HALO, POOL_HALO + chunk)
        grp = lax.broadcasted_iota(jnp.int32, (chunk, POOL_W), 1) // POOL_GROUP
        tpos = base + lax.broadcasted_iota(jnp.int32, (chunk, POOL_W), 0)
        half = jnp.left_shift(1, grp)
        cnt = (jnp.minimum(tpos + half, seq_len) - jnp.maximum(tpos - half, 0)).astype(F32)
        ssum = jnp.where(grp == 0, s2[mid], jnp.where(grp == 1, s4[mid], jnp.where(grp == 2, s8[mid], s16[mid])))
        diff = ssum / cnt - xs[mid]
        y = jnp.dot(diff.astype(BF16), w, preferred_element_type=F32) * sc
        o_ref[0, pl.ds(base, chunk), :] = y.astype(BF16)
        return carry

    lax.fori_loop(0, seq_len // chunk, body, 0)


def _pool_mixer(u, w_blockdiag_bf, pool_scale):
    b, l, _ = u.shape
    chunk = min(l, 512)
    return pl.pallas_call(
        functools.partial(_pool_kernel, seq_len=l, chunk=chunk),
        grid=(b,),
        in_specs=[pl.BlockSpec((1, l, POOL_W), lambda bi: (bi, 0, 0)),
                  pl.BlockSpec((POOL_W, POOL_W), lambda bi: (0, 0)),
                  pl.BlockSpec((1, POOL_W), lambda bi: (0, 0))],
        out_specs=pl.BlockSpec((1, l, POOL_W), lambda bi: (bi, 0, 0)),
        out_shape=jax.ShapeDtypeStruct((b, l, POOL_W), BF16),
        scratch_shapes=[pltpu.VMEM((l + 2 * POOL_HALO, POOL_W), F32)],
        compiler_params=_cparams(("arbitrary",)),
        name="pool_mixer",
    )(u, w_blockdiag_bf, pool_scale.reshape(1, POOL_W))


def _outproj_kernel(na_ref, hy_ref, pl_ref, w_ref, x_ref, g1_ref, nw_ref, sh_ref, sc_ref, wr_ref,
                    xn_ref, h2_ref, aff_ref):
    w = w_ref
    acc = jnp.dot(na_ref[...], w[pl.ds(0, NA_W), :], preferred_element_type=F32)
    acc += jnp.dot(hy_ref[...], w[pl.ds(NA_W, HY_W), :], preferred_element_type=F32)
    acc += jnp.dot(pl_ref[...], w[pl.ds(NA_W + HY_W, POOL_W), :], preferred_element_type=F32)
    xn = x_ref[...] + g1_ref[0] * acc
    xn_ref[...] = xn
    ms = jnp.mean(xn * xn, axis=-1, keepdims=True)
    h2 = xn * lax.rsqrt(ms + EPS) * nw_ref[...]
    h2 = h2 * (1.0 + sc_ref[0]) + sh_ref[0]
    tm = h2.shape[0]
    for s in range(TOKEN_SUBLANES):
        h2_ref[pl.ds(s, tm, stride=TOKEN_SUBLANES), :] = h2[:, s * LANES:(s + 1) * LANES]
    logits = lax.dot_general(wr_ref[...], h2, _NT, preferred_element_type=F32,
                             precision=lax.Precision.HIGHEST)
    mx = jnp.max(logits, axis=0, keepdims=True)
    p = jnp.exp(logits - mx)
    aff_ref[0] = p / jnp.sum(p, axis=0, keepdims=True)


def _out_projection(na, hy, pool, w_out_bf, x2, g1, norm_w, shift, scale, w_router_t, *,
                    rows_per_mod, seq_len, tm):
    m, d = x2.shape
    tiles_per_mod = rows_per_mod // tm
    tiles_per_seq = seq_len // tm
    row = lambda i: (i, 0)
    const = lambda i: (0, 0)
    modi = lambda i: (i // tiles_per_mod, 0, 0)
    return pl.pallas_call(
        _outproj_kernel,
        grid=(m // tm,),
        in_specs=[pl.BlockSpec((tm, NA_W), row), pl.BlockSpec((tm, HY_W), row), pl.BlockSpec((tm, POOL_W), row),
                  pl.BlockSpec((MIX_W, d), const),
                  pl.BlockSpec((tm, d), row),
                  pl.BlockSpec((1, 1, d), modi),
                  pl.BlockSpec((1, d), const),
                  pl.BlockSpec((1, 1, d), modi), pl.BlockSpec((1, 1, d), modi),
                  pl.BlockSpec((N_EXPERTS, d), const)],
        out_specs=[pl.BlockSpec((tm, d), row), pl.BlockSpec((tm * TOKEN_SUBLANES, LANES), row),
                   pl.BlockSpec((1, N_EXPERTS, tm), lambda i: (i // tiles_per_seq, 0, i % tiles_per_seq))],
        out_shape=[jax.ShapeDtypeStruct((m, d), F32), jax.ShapeDtypeStruct((m * TOKEN_SUBLANES, LANES), F32),
                   jax.ShapeDtypeStruct((m // seq_len, N_EXPERTS, seq_len), F32)],
        compiler_params=_cparams(("arbitrary",)),
        name="out_projection_router",
    )(na, hy, pool, w_out_bf, x2, g1, norm_w.reshape(1, d), shift, scale, w_router_t)


FFN_TF = 256
FFN_ROW_CHUNKS = 2


def _ffn_kernel(*refs, n_experts, sources, n_rows):
    ns = len(sources)
    idx_cur = refs[0:ns]
    idx_nxt = refs[ns:2 * ns]
    src = refs[2 * ns:3 * ns]
    gate_ref, wg_ref, wu_ref, wd_ref, o_ref, stage, xb, acc, sem = refs[3 * ns:]
    e = pl.program_id(0)
    f = pl.program_id(1)
    n_f = pl.num_programs(1)

    ts = TOKEN_SUBLANES

    def row_copy(s, row, r):
        return pltpu.make_async_copy(src[s].at[pl.ds(pl.multiple_of(row * ts, ts), ts)],
                                     stage.at[pl.ds(pl.multiple_of((sources[s][0] + r) * ts, ts), ts)], sem.at[0])

    def gather(idx_refs):
        for s in range(ns):
            def body(r, carry, s=s):
                row_copy(s, idx_refs[s][0, 0, r], r).start()
                return carry
            lax.fori_loop(0, sources[s][1], body, 0, unroll=8)

    def gather_wait():
        for s in range(ns):
            off, cnt = sources[s]
            pltpu.make_async_copy(src[s].at[pl.ds(0, cnt * ts)], stage.at[pl.ds(off * ts, cnt * ts)],
                                  sem.at[0]).wait()

    @pl.when(f == 0)
    def _():
        @pl.when(e == 0)
        def _():
            gather(idx_cur)

        gather_wait()

        def unpack(base, n):
            for s in range(ts):
                xb[pl.ds(base, n), s * LANES:(s + 1) * LANES] = (
                    stage[pl.ds(base * ts + s, n, stride=ts), :].astype(BF16))
            acc[pl.ds(base, n), :] = jnp.zeros((n, xb.shape[1]), F32)

        @pl.when(e + 1 < n_experts)
        def _():
            group = 16
            for s in range(ns):
                off, cnt = sources[s]

                def body(i, carry, s=s, off=off):
                    unpack(pl.multiple_of(off + i * group, group), group)
                    for k in range(group):
                        row_copy(s, idx_nxt[s][0, 0, i * group + k], i * group + k).start(priority=k % 2)
                    return carry

                lax.fori_loop(0, cnt // group, body, 0)

        @pl.when(e + 1 >= n_experts)
        def _():
            unpack(0, n_rows)

    wg = wg_ref[0, 0].astype(BF16)
    wu = wu_ref[0, 0].astype(BF16)
    wd = wd_ref[0, 0].astype(BF16)
    rc = n_rows // FFN_ROW_CHUNKS
    for r in range(FFN_ROW_CHUNKS):
        rs = pl.ds(r * rc, rc)
        x = xb[rs, :]
        a = jnp.dot(x, wg, preferred_element_type=F32)
        b = jnp.dot(x, wu, preferred_element_type=F32)
        h = (a * jax.nn.sigmoid(a) * b).astype(BF16)
        y = jnp.dot(h, wd, preferred_element_type=F32)
        acc[rs, :] += y

    @pl.when(f == n_f - 1)
    def _():
        o_ref[0] = (acc[...] * gate_ref[0]).astype(BF16)


def _expert_ffn(rows_idx, tokens, gate, w_gate, w_up, w_down, layer):
    ns = len(tokens)
    e, d = w_gate.shape[1], w_gate.shape[2]
    counts = [int(ri.shape[2]) for ri in rows_idx]
    offs = [int(sum(counts[:s])) for s in range(ns)]
    r_total = sum(counts)
    assert r_total % (FFN_ROW_CHUNKS * 16) == 0
    smem = lambda cnt, nxt: pl.BlockSpec(
        (1, 1, cnt), (lambda ei, f: (jnp.minimum(ei + 1, e - 1), 0, 0)) if nxt else (lambda ei, f: (ei, 0, 0)),
        memory_space=pltpu.SMEM)
    in_specs = ([smem(cnt, False) for cnt in counts] + [smem(cnt, True) for cnt in counts]
                + [pl.BlockSpec(memory_space=pl.ANY) for _ in range(ns)]
                + [pl.BlockSpec((1, r_total, 1), lambda ei, f: (ei, 0, 0)),
                   pl.BlockSpec((1, 1, d, FFN_TF), lambda ei, f: (layer, ei, 0, f)),
                   pl.BlockSpec((1, 1, d, FFN_TF), lambda ei, f: (layer, ei, 0, f)),
                   pl.BlockSpec((1, 1, FFN_TF, d), lambda ei, f: (layer, ei, f, 0))])
    return pl.pallas_call(
        functools.partial(_ffn_kernel, n_experts=e, sources=tuple(zip(offs, counts)), n_rows=r_total),
        grid=(e, D_EXPERT // FFN_TF),
        in_specs=in_specs,
        out_specs=pl.BlockSpec((1, r_total, d), lambda ei, f: (ei, 0, 0)),
        out_shape=jax.ShapeDtypeStruct((e, r_total, d), BF16),
        scratch_shapes=[pltpu.VMEM((r_total * TOKEN_SUBLANES, LANES), F32), pltpu.VMEM((r_total, d), BF16),
                        pltpu.VMEM((r_total, d), F32), pltpu.SemaphoreType.DMA((1,))],
        compiler_params=_cparams(("arbitrary", "arbitrary")),
        name="expert_ffn",
    )(*rows_idx, *rows_idx, *tokens, gate, w_gate, w_up, w_down)


def _prefix_count(m, n):
    lane = lax.broadcasted_iota(jnp.int32, m.shape, 1)
    c = m
    s = 1
    while s < n:
        c = c + jnp.where(lane >= s, pltpu.roll(c, s, 1), 0)
        s *= 2
    return c


COMBINE_TB = 256
COMBINE_CHUNK = 128


def _topk_kernel(aff_ref, idx_ref, gate_ref, off_ref, *, cap):
    a = aff_ref[...]
    rows, n = a.shape

    def search(i, t):
        cand = t | jnp.left_shift(jnp.int32(1), 30 - i)
        cnt = jnp.sum((a >= pltpu.bitcast(cand, F32)).astype(jnp.int32), axis=1, keepdims=True)
        return jnp.where(cnt >= cap, cand, t)

    thr_bits = lax.fori_loop(0, 31, search, jnp.zeros((rows, 1), jnp.int32))
    thr = pltpu.bitcast(thr_bits, F32)
    gt = a >= pltpu.bitcast(thr_bits + 1, F32)
    eq = (a >= thr) & jnp.logical_not(gt)
    need = cap - jnp.sum(gt.astype(jnp.int32), axis=1, keepdims=True)
    eq_i = eq.astype(jnp.int32)
    sel = gt | (eq & (_prefix_count(eq_i, n) - eq_i < need))
    sel_i = sel.astype(jnp.int32)
    lane = lax.broadcasted_iota(jnp.int32, a.shape, 1)
    dist = jnp.where(sel, lane + 1 - _prefix_count(sel_i, n), 0)
    tok = jnp.where(sel, lane, -1)
    val = a
    s = 1
    while s < n:
        tok_in = pltpu.roll(tok, n - s, 1)
        dist_in = pltpu.roll(dist, n - s, 1)
        val_in = pltpu.roll(val, n - s, 1)
        take = (lane < n - s) & (tok_in >= 0) & ((dist_in & s) != 0)
        stay = (tok >= 0) & ((dist & s) == 0)
        tok = jnp.where(take, tok_in, jnp.where(stay, tok, -1))
        dist = jnp.where(take, dist_in, jnp.where(stay, dist, 0))
        val = jnp.where(take, val_in, val)
        s *= 2
    idx_ref[...] = tok[:, :cap]
    gate_ref[...] = val[:, :cap]
    olane = lax.broadcasted_iota(jnp.int32, off_ref.shape, 1)
    off = jnp.zeros(off_ref.shape, jnp.int32)
    for j in range(n // COMBINE_TB + 1):
        below = jnp.sum(jnp.where(lane < j * COMBINE_TB, sel_i, 0), axis=1, keepdims=True)
        off = jnp.where(olane == j, below, off)
    off_ref[...] = off


def _topk_select(aff, cap):
    b, e, n = aff.shape
    assert n & (n - 1) == 0 and n // COMBINE_TB < LANES
    rows = b * e
    full = lambda w: pl.BlockSpec((rows, w), lambda i: (0, 0))
    idx, gate, off = pl.pallas_call(
        functools.partial(_topk_kernel, cap=cap),
        grid=(1,),
        in_specs=[full(n)],
        out_specs=[full(cap), full(cap), full(LANES)],
        out_shape=[jax.ShapeDtypeStruct((rows, cap), jnp.int32), jax.ShapeDtypeStruct((rows, cap), F32),
                   jax.ShapeDtypeStruct((rows, LANES), jnp.int32)],
        compiler_params=_cparams(("arbitrary",)),
        name="expert_choice_topk",
    )(aff.reshape(rows, n))
    return idx.reshape(b, e, cap), gate.reshape(b, e, cap), off


def _combine_kernel(off_ref, y_ref, idx_ref, x_ref, g_ref, o_ref, acc, ycat, *, chunk, n_chunks):
    bi = pl.program_id(0)
    j = pl.program_id(1)
    tb = x_ref.shape[0]
    tok = j * tb + lax.broadcasted_iota(jnp.int32, (tb, chunk), 0)

    def onehot(e, st):
        return (idx_ref[0, pl.ds(e, 1), pl.ds(st, chunk)] == tok).astype(BF16)

    if n_chunks == 1:
        acc[...] = jnp.zeros_like(acc)
        for e in range(N_EXPERTS):
            acc[...] += jnp.dot(onehot(e, 0), y_ref[e], preferred_element_type=F32)
    else:
        cap = n_chunks * chunk
        shift = chunk.bit_length() - 1
        col = lax.broadcasted_iota(jnp.int32, (1, chunk), 1)

        def slot_ids(e, st):
            base = jnp.minimum(lax.shift_left(lax.shift_right_logical(st, shift), shift), cap - 2 * chunk)
            wide = idx_ref[0, pl.ds(e, 1), pl.ds(pl.multiple_of(base, chunk), 2 * chunk)]
            wide = jnp.broadcast_to(wide, (8, 2 * chunk))
            return pltpu.roll(wide, (2 * chunk - (st - base)) & (2 * chunk - 1), 1)[:1, :chunk]

        starts, ends, hots = [], [], []
        for e in range(N_EXPERTS):
            lo = off_ref[bi * N_EXPERTS + e, j]
            hi = off_ref[bi * N_EXPERTS + e, j + 1]
            st = jnp.minimum(lax.shift_left(lax.shift_right_logical(lo, 4), 4), cap - chunk)
            starts.append(st)
            ends.append(hi)
            ycat[pl.ds(e * chunk, chunk), :] = y_ref[e, pl.ds(pl.multiple_of(st, 16), chunk), :]
            hots.append((slot_ids(e, st) == tok).astype(BF16))
        acc[...] = jnp.dot(jnp.concatenate(hots, axis=1), ycat[...], preferred_element_type=F32)
        for e in range(N_EXPERTS):
            def add_window(c, carry, e=e):
                first = starts[e] + c * chunk
                st = jnp.minimum(first, cap - chunk)
                hot = ((slot_ids(e, st) == tok) & (st + col >= first)).astype(BF16)
                acc[...] += jnp.dot(hot, y_ref[e, pl.ds(pl.multiple_of(st, 16), chunk), :],
                                    preferred_element_type=F32)
                return carry

            n_more = lax.shift_right_logical(jnp.maximum(ends[e] - starts[e] - 1, 0), shift)
            lax.fori_loop(1, n_more + 1, add_window, 0)
    o_ref[...] = x_ref[...] + g_ref[0] * acc[...]


def _moe_combine(off, y, idx, x2, g, *, row_block0, cap, seq_len):
    bsz = idx.shape[0]
    d = x2.shape[1]
    tb = min(seq_len, COMBINE_TB)
    nb = seq_len // tb
    chunk = min(cap, COMBINE_CHUNK)
    n_chunks = cap // chunk
    n_mod = g.shape[0]
    grid_spec = pltpu.PrefetchScalarGridSpec(
        num_scalar_prefetch=1,
        grid=(bsz, nb),
        in_specs=[pl.BlockSpec((N_EXPERTS, cap, d), lambda bi, j, off: (0, row_block0 + bi, 0),
                               pipeline_mode=pl.Buffered(1)),
                  pl.BlockSpec((1, N_EXPERTS, cap), lambda bi, j, off: (bi, 0, 0)),
                  pl.BlockSpec((tb, d), lambda bi, j, off: (bi * nb + j, 0)),
                  pl.BlockSpec((1, 1, d), lambda bi, j, off: (bi if n_mod > 1 else 0, 0, 0))],
        out_specs=pl.BlockSpec((tb, d), lambda bi, j, off: (bi * nb + j, 0)),
        scratch_shapes=[pltpu.VMEM((tb, d), F32), pltpu.VMEM((N_EXPERTS * chunk, d), BF16)])
    return pl.pallas_call(
        functools.partial(_combine_kernel, chunk=chunk, n_chunks=n_chunks),
        grid_spec=grid_spec,
        out_shape=jax.ShapeDtypeStruct(x2.shape, F32),
        compiler_params=_cparams(("arbitrary", "arbitrary")),
        name="moe_combine",
    )(off, y, idx, x2, g)


def _expert_choice_moe(streams, w_gate, w_up, w_down, layer):
    routed = []
    for h2, aff, x2, g in streams:
        bsz, _, l = aff.shape
        cap = EC_FACTOR * l // N_EXPERTS
        idx, gate, off = _topk_select(aff, cap)
        rows = (idx + (jnp.arange(bsz, dtype=jnp.int32) * l)[:, None, None]).transpose(1, 0, 2)
        routed.append((idx, off, rows.reshape(N_EXPERTS, 1, bsz * cap), gate.transpose(1, 0, 2).reshape(
            N_EXPERTS, bsz * cap, 1), cap))
    y = _expert_ffn([r[2] for r in routed], [s[0] for s in streams],
                    jnp.concatenate([r[3] for r in routed], axis=1), w_gate, w_up, w_down, layer)
    outs = []
    row0 = 0
    for (h2, aff, x2, g), (idx, off, rows, _, cap) in zip(streams, routed):
        assert row0 % cap == 0
        outs.append(_moe_combine(off, y, idx, x2, g, row_block0=row0 // cap, cap=cap, seq_len=aff.shape[2]))
        row0 += rows.shape[2]
    return outs


HY_R = 128
HY_COLS = 4096
HY_KB = 8
HY_UNROLL = 8


def _dft_consts(r):
    k = np.arange(r)
    ang = 2.0 * np.pi * np.outer(k, k) / r
    fre, fim = np.cos(ang), -np.sin(ang)
    k2 = np.arange(r)
    tang = 2.0 * np.pi * np.outer(k2, k2) / (r * r)
    tw = np.stack([np.cos(tang), -np.sin(tang)])
    fwd = np.block([[fre, -fim], [fim, fre]])
    inv = np.block([[fre, fim], [-fim, fre]])
    return fre, fim, tw, fwd, inv


def _hy_prep_kernel(x_ref, prev_ref, next_ref, w_ref, b_ref, x1_ref, x2_ref, v_ref, *, n_tiles):
    t = pl.program_id(1)
    x = x_ref[0].astype(F32)
    tm = x.shape[0]
    row = lax.broadcasted_iota(jnp.int32, x.shape, 0)
    halo = prev_ref.shape[1]
    prev = jnp.where(t == 0, 0.0, prev_ref[0].astype(F32)[halo - 1:halo, :])
    nxt = jnp.where(t == n_tiles - 1, 0.0, next_ref[0].astype(F32)[0:1, :])
    up = jnp.where(row == 0, prev, pltpu.roll(x, 1, 0))
    dn = jnp.where(row == tm - 1, nxt, pltpu.roll(x, tm - 1, 0))
    y = b_ref[...] + up * w_ref[pl.ds(0, 1), :] + x * w_ref[pl.ds(1, 1), :] + dn * w_ref[pl.ds(2, 1), :]
    x1_ref[0] = y[:, :HY_W].astype(BF16)
    x2_ref[0] = y[:, HY_W:2 * HY_W].astype(BF16)
    v_ref[0] = y[:, 2 * HY_W:].astype(BF16)


def _hyena_prep(u, conv_w, conv_b):
    b, l, c = u.shape
    tm = min(l, 1024)
    n_tiles = l // tm
    halo = 16
    g = tm // halo
    n_groups = l // halo
    out = jax.ShapeDtypeStruct((b, l, HY_W), BF16)
    ospec = pl.BlockSpec((1, tm, HY_W), lambda bi, t: (bi, t, 0))
    return pl.pallas_call(
        functools.partial(_hy_prep_kernel, n_tiles=n_tiles),
        grid=(b, n_tiles),
        in_specs=[pl.BlockSpec((1, tm, c), lambda bi, t: (bi, t, 0)),
                  pl.BlockSpec((1, halo, c), lambda bi, t: (bi, jnp.maximum(t * g - 1, 0), 0)),
                  pl.BlockSpec((1, halo, c), lambda bi, t: (bi, jnp.minimum((t + 1) * g, n_groups - 1), 0)),
                  pl.BlockSpec((HY_SHORT, c), lambda bi, t: (0, 0)),
                  pl.BlockSpec((1, c), lambda bi, t: (0, 0))],
        out_specs=[ospec, ospec, ospec],
        out_shape=[out, out, out],
        compiler_params=_cparams(("arbitrary", "arbitrary")),
        name="hyena_prep",
    )(u, u, u, conv_w, conv_b.reshape(1, c))


def _hy_filter_kernel(z_ref, w1_ref, b1_ref, w2_ref, b2_ref, w3_ref, fr_ref, dl_ref, k_ref, ss_ref, *, seq_len):
    i = pl.program_id(0)
    hp = lax.Precision.HIGHEST
    z = z_ref[...]
    tm = z.shape[0]
    half = tm // 2
    fr = fr_ref[...]
    zz = jnp.concatenate([z[:half], z[half:]], axis=1)
    h = jnp.sin(fr * (jnp.dot(zz, w1_ref[...], preferred_element_type=F32, precision=hp) + b1_ref[...]))
    h = jnp.sin(fr * (jnp.dot(h, w2_ref[...], preferred_element_type=F32, precision=hp) + b2_ref[...]))
    hb = h.astype(BF16)
    w3 = w3_ref[...].astype(BF16)
    fo = w3.shape[0]
    h = jnp.concatenate([jnp.dot(hb[:, :fo], w3, preferred_element_type=F32),
                         jnp.dot(hb[:, fo:], w3, preferred_element_type=F32)], axis=0)
    dec = jnp.exp(-z[:, 0:1] * dl_ref[...])
    r = i * tm + lax.broadcasted_iota(jnp.int32, (tm, HY_W), 0)
    parts = []
    for o in range(HY_ORDER):
        hf = h[:, (2 * o) * HY_W:(2 * o + 1) * HY_W]
        hb = h[:, (2 * o + 1) * HY_W:(2 * o + 2) * HY_W]
        parts.append(jnp.where(r < seq_len, hf, jnp.where(r > seq_len, hb, 0.0)) * dec)
    k = jnp.concatenate(parts, axis=1)
    k_ref[...] = k
    ss = jnp.sum(k * k, axis=0, keepdims=True)

    @pl.when(i == 0)
    def _():
        ss_ref[...] = ss

    @pl.when(i != 0)
    def _():
        ss_ref[...] += ss


def _hyena_filter_taps(l, w1, b1, w2, b2, w3, freq):
    t = jnp.linspace(0.0, 1.0, l, dtype=F32)[:, None]
    w = (2.0 * math.pi / l) * jnp.arange(l, dtype=F32)[:, None]
    f = jnp.linspace(1e-4, HY_BANDS - 1, HY_BANDS, dtype=F32)[None, :]
    z = jnp.concatenate([t, jnp.cos(f * w), -jnp.sin(f * w)], axis=-1)
    z2 = jnp.concatenate([z, z[:1], z[:0:-1]], axis=0)
    emb = z.shape[1]
    z2 = jnp.pad(z2, ((0, 0), (0, LANES - emb)))
    w1p = jnp.pad(w1, ((0, LANES - emb), (0, 0)))
    two = lambda m: jnp.kron(jnp.eye(2, dtype=m.dtype), m)
    pair = lambda v: jnp.tile(v, 2).reshape(1, 2 * v.shape[0])
    deltas = jnp.abs(jnp.linspace(math.log(HY_DECAY_TARGET) / HY_SLOW_DECAY,
                                  math.log(HY_DECAY_TARGET) / HY_FAST_DECAY, HY_W, dtype=F32))
    fo = w2.shape[0]
    nout = w3.shape[1]
    tm = min(2 * l, 1024)
    const = lambda i: (0, 0)
    return pl.pallas_call(
        functools.partial(_hy_filter_kernel, seq_len=l),
        grid=(2 * l // tm,),
        in_specs=[pl.BlockSpec((tm, LANES), lambda i: (i, 0)),
                  pl.BlockSpec((2 * LANES, 2 * fo), const), pl.BlockSpec((1, 2 * fo), const),
                  pl.BlockSpec((2 * fo, 2 * fo), const), pl.BlockSpec((1, 2 * fo), const),
                  pl.BlockSpec((fo, nout), const), pl.BlockSpec((1, 2 * fo), const),
                  pl.BlockSpec((1, HY_W), const)],
        out_specs=[pl.BlockSpec((tm, HY_ORDER * HY_W), lambda i: (i, 0)),
                   pl.BlockSpec((1, HY_ORDER * HY_W), const)],
        out_shape=[jax.ShapeDtypeStruct((2 * l, HY_ORDER * HY_W), F32),
                   jax.ShapeDtypeStruct((1, HY_ORDER * HY_W), F32)],
        compiler_params=_cparams(("arbitrary",)),
        name="hyena_filter_taps",
    )(z2, two(w1p), pair(b1), two(w2), pair(b2), w3, pair(freq), deltas.reshape(1, HY_W))


def _left_matmul_kernel(f_ref, x_ref, o_ref):
    o_ref[0] = jnp.dot(f_ref[...], x_ref[0].astype(BF16), preferred_element_type=F32).astype(o_ref.dtype)


def _left_matmul(f_bf, x3, name):
    b, k, c = x3.shape
    m = f_bf.shape[0]
    return pl.pallas_call(
        _left_matmul_kernel,
        grid=(b, c // HY_COLS),
        in_specs=[pl.BlockSpec((m, k), lambda bi, j: (0, 0)),
                  pl.BlockSpec((1, k, HY_COLS), lambda bi, j: (bi, 0, j))],
        out_specs=pl.BlockSpec((1, m, HY_COLS), lambda bi, j: (bi, 0, j)),
        out_shape=jax.ShapeDtypeStruct((b, m, c), BF16),
        compiler_params=_cparams(("arbitrary", "arbitrary")),
        name=name,
    )(f_bf, x3)


def _twiddle_cols(tw_ref, k1):
    lane = lax.broadcasted_iota(jnp.int32, (HY_R, HY_R), 1)
    pick = lane == k1
    twr = jnp.sum(jnp.where(pick, tw_ref[0], 0.0), axis=1, keepdims=True)
    twi = jnp.sum(jnp.where(pick, tw_ref[1], 0.0), axis=1, keepdims=True)
    return twr, twi


def _hy_spectrum_kernel(a_ref, tw_ref, fwd_ref, ss_ref, kf_ref):
    kb = pl.program_id(0)
    scale = lax.rsqrt(ss_ref[...] + EPS) * (1.0 / (HY_R * HY_R))

    def body(kk, carry):
        twr, twi = _twiddle_cols(tw_ref, kb * HY_KB + kk)
        are = a_ref[0, 0, kk].astype(F32)
        aim = a_ref[0, 1, kk].astype(F32)
        bst = jnp.concatenate([are * twr - aim * twi, are * twi + aim * twr], axis=0).astype(BF16)
        x = jnp.dot(fwd_ref[...], bst, preferred_element_type=F32)
        kf_ref[0, kk] = x[:HY_R] * scale
        kf_ref[1, kk] = x[HY_R:] * scale
        return carry

    lax.fori_loop(0, HY_KB, body, 0, unroll=HY_UNROLL)


def _hy_spectrum(a5, tw, fwd_bf, ss):
    nch = a5.shape[-1]
    return pl.pallas_call(
        _hy_spectrum_kernel,
        grid=(HY_R // HY_KB,),
        in_specs=[pl.BlockSpec((1, 2, HY_KB, HY_R, nch), lambda kb: (0, 0, kb, 0, 0)),
                  pl.BlockSpec((2, HY_R, HY_R), lambda kb: (0, 0, 0)),
                  pl.BlockSpec((2 * HY_R, 2 * HY_R), lambda kb: (0, 0)),
                  pl.BlockSpec((1, nch), lambda kb: (0, 0))],
        out_specs=pl.BlockSpec((2, HY_KB, HY_R, nch), lambda kb: (0, kb, 0, 0)),
        out_shape=jax.ShapeDtypeStruct((2, HY_R, HY_R, nch), F32),
        compiler_params=_cparams(("arbitrary",)),
        name="hyena_filter_spectrum",
    )(a5, tw, fwd_bf, ss)


def _hy_mid_kernel(a_ref, kf_ref, tw_ref, fwd_ref, inv_ref, z_ref):
    kb = pl.program_id(0)

    def body(kk, carry):
        twr, twi = _twiddle_cols(tw_ref, kb * HY_KB + kk)
        are = a_ref[0, 0, kk].astype(F32)
        aim = a_ref[0, 1, kk].astype(F32)
        bst = jnp.concatenate([are * twr - aim * twi, are * twi + aim * twr], axis=0).astype(BF16)
        x = jnp.dot(fwd_ref[...], bst, preferred_element_type=F32)
        xre, xim = x[:HY_R], x[HY_R:]
        kre, kim = kf_ref[0, kk], kf_ref[1, kk]
        yst = jnp.concatenate([xre * kre - xim * kim, xre * kim + xim * kre], axis=0).astype(BF16)
        zz = jnp.dot(inv_ref[...], yst, preferred_element_type=F32)
        zre, zim = zz[:HY_R], zz[HY_R:]
        z_ref[0, 0, kk] = (zre * twr + zim * twi).astype(BF16)
        z_ref[0, 1, kk] = (zim * twr - zre * twi).astype(BF16)
        return carry

    lax.fori_loop(0, HY_KB, body, 0, unroll=HY_UNROLL)


def _hy_mid(a5, kf, order, tw, fwd_bf, inv_bf):
    b = a5.shape[0]
    blk = (1, 2, HY_KB, HY_R, HY_W)
    return pl.pallas_call(
        _hy_mid_kernel,
        grid=(HY_R // HY_KB, b),
        in_specs=[pl.BlockSpec(blk, lambda kb, bi: (bi, 0, kb, 0, 0)),
                  pl.BlockSpec((2, HY_KB, HY_R, HY_W), lambda kb, bi: (0, kb, 0, order)),
                  pl.BlockSpec((2, HY_R, HY_R), lambda kb, bi: (0, 0, 0)),
                  pl.BlockSpec((2 * HY_R, 2 * HY_R), lambda kb, bi: (0, 0)),
                  pl.BlockSpec((2 * HY_R, 2 * HY_R), lambda kb, bi: (0, 0))],
        out_specs=pl.BlockSpec(blk, lambda kb, bi: (bi, 0, kb, 0, 0)),
        out_shape=jax.ShapeDtypeStruct(a5.shape, BF16),
        compiler_params=_cparams(("arbitrary", "arbitrary")),
        name="hyena_spectral_product",
    )(a5, kf, tw, fwd_bf, inv_bf)


def _hy_out_kernel(f_ref, z_ref, g_ref, w_ref, sk_ref, o_ref):
    y = jnp.dot(f_ref[...], z_ref[0], preferred_element_type=F32)
    w = w_ref[0].astype(F32)
    o_ref[0] = (g_ref[0].astype(F32) * (y + w * sk_ref[...])).astype(BF16)


def _hy_out(f_bf, z3, gate3, w3, skip_row):
    b, m2, c = z3.shape
    m = f_bf.shape[0]
    dspec = pl.BlockSpec((1, m, HY_COLS), lambda bi, j: (bi, 0, j))
    return pl.pallas_call(
        _hy_out_kernel,
        grid=(b, c // HY_COLS),
        in_specs=[pl.BlockSpec((m, m2), lambda bi, j: (0, 0)),
                  pl.BlockSpec((1, m2, HY_COLS), lambda bi, j: (bi, 0, j)),
                  dspec, dspec,
                  pl.BlockSpec((1, HY_COLS), lambda bi, j: (0, 0))],
        out_specs=dspec,
        out_shape=jax.ShapeDtypeStruct((b, m, c), BF16),
        compiler_params=_cparams(("arbitrary", "arbitrary")),
        name="hyena_inverse_gate",
    )(f_bf, z3, gate3, w3, skip_row)


def _hyena_long(u, conv_w, conv_b, skip, filt_args):
    b, l, _ = u.shape
    assert 2 * l == HY_R * HY_R
    fre, fim, tw, fwd, inv = _dft_consts(HY_R)
    half = HY_R // 2
    as_bf = lambda m: jnp.asarray(m, F32).astype(BF16)
    f_a_data = as_bf(np.concatenate([fre[:, :half], fim[:, :half]], axis=0))
    f_a_filt = as_bf(np.concatenate([fre, fim], axis=0))
    f_c = as_bf(np.concatenate([fre[:half], fim[:half]], axis=1))
    tw_j = jnp.asarray(tw, F32)
    fwd_bf = as_bf(fwd)
    inv_bf = as_bf(inv)

    taps, ss = _hyena_filter_taps(l, *filt_args)
    nch = HY_ORDER * HY_W
    ak = _left_matmul(f_a_filt, taps.reshape(1, HY_R, HY_R * nch), "hyena_filter_dft_a")
    kf = _hy_spectrum(ak.reshape(1, 2, HY_R, HY_R, nch), tw_j, fwd_bf, ss)

    x1, x2, v = _hyena_prep(u, conv_w, conv_b)
    z = v
    for o, gate in enumerate((x1, x2)):
        z3 = z.reshape(b, half, HY_R * HY_W)
        a = _left_matmul(f_a_data, z3, "hyena_dft_a")
        zs = _hy_mid(a.reshape(b, 2, HY_R, HY_R, HY_W), kf, o, tw_j, fwd_bf, inv_bf)
        skip_row = jnp.tile(skip[o], HY_COLS // HY_W).reshape(1, HY_COLS)
        z = _hy_out(f_c, zs.reshape(b, 2 * HY_R, HY_R * HY_W), gate.reshape(b, half, HY_R * HY_W), z3,
                    skip_row).reshape(b, l, HY_W)
    return z


def _hy_ctx_kernel(x1_ref, x2_ref, v_ref, taps_ref, ss_ref, fa_ref, fc_ref, sk_ref, o_ref, *, seq_len):
    n2 = 2 * seq_len
    fa = fa_ref[...]
    scale = lax.rsqrt(ss_ref[...] + EPS) * (1.0 / n2)
    kf = jnp.dot(fa, taps_ref[...].astype(BF16), preferred_element_type=F32) * scale
    z = v_ref[0].astype(F32)
    for o, g_ref in enumerate((x1_ref, x2_ref)):
        x = jnp.dot(fa[:, :seq_len], z.astype(BF16), preferred_element_type=F32)
        xre, xim = x[:n2], x[n2:]
        kre = kf[:n2, o * HY_W:(o + 1) * HY_W]
        kim = kf[n2:, o * HY_W:(o + 1) * HY_W]
        yst = jnp.concatenate([xre * kre - xim * kim, xre * kim + xim * kre], axis=0).astype(BF16)
        y = jnp.dot(fc_ref[...], yst, preferred_element_type=F32)
        z = g_ref[0].astype(F32) * (y + z * sk_ref[pl.ds(o, 1), :])
    o_ref[0] = z.astype(BF16)


def _hyena_short(u, conv_w, conv_b, skip, filt_args):
    b, l, _ = u.shape
    n2 = 2 * l
    k = np.arange(n2)
    ang = 2.0 * np.pi * np.outer(k, k) / n2
    fre, fim = np.cos(ang), -np.sin(ang)
    fa = jnp.asarray(np.concatenate([fre, fim], axis=0), F32).astype(BF16)
    fc = jnp.asarray(np.concatenate([fre[:l], fim[:l]], axis=1), F32).astype(BF16)
    taps, ss = _hyena_filter_taps(l, *filt_args)
    x1, x2, v = _hyena_prep(u, conv_w, conv_b)
    nch = HY_ORDER * HY_W
    dspec = pl.BlockSpec((1, l, HY_W), lambda bi: (bi, 0, 0))
    const = lambda bi: (0, 0)
    return pl.pallas_call(
        functools.partial(_hy_ctx_kernel, seq_len=l),
        grid=(b,),
        in_specs=[dspec, dspec, dspec,
                  pl.BlockSpec((n2, nch), const), pl.BlockSpec((1, nch), const),
                  pl.BlockSpec((2 * n2, n2), const), pl.BlockSpec((l, 2 * n2), const),
                  pl.BlockSpec((HY_ORDER, HY_W), const)],
        out_specs=dspec,
        out_shape=jax.ShapeDtypeStruct((b, l, HY_W), BF16),
        compiler_params=_cparams(("arbitrary",)),
        name="hyena_context",
    )(x1, x2, v, taps, ss, fa, fc, skip)


def _rope_tables(n):
    t = jnp.arange(n, dtype=jnp.int32)
    pos = jnp.stack([t // GRID_W, t % GRID_W], axis=-1).astype(F32)
    nf = HEAD_DIM // 4
    inv = ROPE_THETA ** (-jnp.arange(nf, dtype=F32) / nf)
    ang = pos[:, :, None] * inv
    cos, sin = jnp.cos(ang), jnp.sin(ang)
    c64 = jnp.concatenate([cos[:, 0], cos[:, 0], cos[:, 1], cos[:, 1]], axis=-1)
    s64 = jnp.concatenate([-sin[:, 0], sin[:, 0], -sin[:, 1], sin[:, 1]], axis=-1)
    return jnp.tile(c64, (1, 2)), jnp.tile(s64, (1, 2))


def _block_diag(pool_w):
    g, c, _ = pool_w.shape
    out = jnp.zeros((g * c, g * c), pool_w.dtype)
    for i in range(g):
        out = out.at[i * c:(i + 1) * c, i * c:(i + 1) * c].set(pool_w[i])
    return out


def kernel(x, c, ctx, c_ctx, w_mod, b_mod, norm1_w, norm2_w, w_in, w_out, q_norm_w, k_norm_w, na_rpb,
           hy_conv_w, hy_conv_b, hy_w1, hy_b1, hy_w2, hy_b2, hy_w3, hy_freq, hy_skip,
           pool_w, pool_scale, w_router, w_gate, w_up, w_down):
    b, n, d = x.shape
    lc = ctx.shape[1]
    depth = w_mod.shape[0]
    rows = n // GRID_W
    cos128, sin128 = _rope_tables(n)
    seg = jnp.asarray(np.kron(np.eye(2), np.full((HEAD_DIM, HEAD_DIM), 1.0 / HEAD_DIM)), BF16)
    cc = jnp.zeros((8, d), F32).at[:b].set(c).at[b].set(c_ctx)

    xl = x.reshape(b * n, d)
    xc = ctx.reshape(b * lc, d)
    for i in range(depth):
        last = i == depth - 1
        mod = _modulation(cc, w_mod, b_mod[i], i)
        ml = mod[:b].reshape(b, 1, 6, d)
        mc = mod[b].reshape(1, 1, 6, d)
        sh1, sc1, g1, sh2, sc2, g2 = [ml[:, :, j] for j in range(6)]
        csh1, csc1, cg1, csh2, csc2, cg2 = [mc[:, :, j] for j in range(6)]
        w_in_bf = w_in[i].astype(BF16)
        w_out_bf = w_out[i].astype(BF16)
        w_router_t = w_router[i].T
        pool_bd = _block_diag(pool_w[i]).astype(BF16)
        hy_args = (hy_w1[i], hy_b1[i], hy_w2[i], hy_b2[i], hy_w3[i], hy_freq[i])

        qc, kc, vc, hyc, poolc = _in_projection(
            xc, csh1, csc1, norm1_w[i], w_in_bf, q_norm_w[i], k_norm_w[i], cos128, sin128, seg,
            rows_per_mod=b * lc, seq_len=lc, tm=lc, rope=False)
        kc = kc.reshape(b, lc, NA_W)
        vc = vc.reshape(b, lc, NA_W)
        if not last:
            na_c = _dense_attention(qc.reshape(b, lc, NA_W), kc, vc).reshape(b * lc, NA_W)
            hy_c = _hyena_short(hyc.reshape(b, lc, 3 * HY_W), hy_conv_w[i], hy_conv_b[i], hy_skip[i], hy_args)
            pool_c = _pool_mixer(poolc.reshape(b, lc, POOL_W), pool_bd, pool_scale[i])
            xc1, h2c, affc = _out_projection(
                na_c, hy_c.reshape(b * lc, HY_W).astype(BF16), pool_c.reshape(b * lc, POOL_W), w_out_bf,
                xc, cg1, norm2_w[i], csh2, csc2, w_router_t, rows_per_mod=b * lc, seq_len=lc, tm=lc)

        qr, qp, kl, vl, hyl, pooll = _in_projection(
            xl, sh1, sc1, norm1_w[i], w_in_bf, q_norm_w[i], k_norm_w[i], cos128, sin128, seg,
            rows_per_mod=n, seq_len=n, tm=512, rope=True)
        bias = _na_bias_tables(na_rpb[i], rows)
        o_na = _neighborhood_attention(qr.reshape(b, n, NA_W), qp.reshape(b, n, NA_W), kl.reshape(b, n, NA_W),
                                       vl.reshape(b, n, NA_W), kc, vc, bias)
        hy_l = _hyena_long(hyl.reshape(b, n, 3 * HY_W), hy_conv_w[i], hy_conv_b[i], hy_skip[i], hy_args)
        pool_l = _pool_mixer(pooll.reshape(b, n, POOL_W), pool_bd, pool_scale[i])
        xl1, h2l, affl = _out_projection(
            o_na.reshape(b * n, NA_W), hy_l.reshape(b * n, HY_W).astype(BF16), pool_l.reshape(b * n, POOL_W),
            w_out_bf, xl, g1, norm2_w[i], sh2, sc2, w_router_t, rows_per_mod=n, seq_len=n, tm=512)
        streams = [(h2l, affl, xl1, g2)]
        if not last:
            streams.append((h2c, affc, xc1, cg2))
        outs = _expert_choice_moe(streams, w_gate, w_up, w_down, i)
        xl = outs[0]
        if not last:
            xc = outs[1]
    return xl.reshape(b, n, d)
```
